```python
import math
import jax, jax.numpy as jnp
from jax import lax
import numpy as np

D_MODEL = 1024
BATCH = 8
SEQ = 2048
DEPTH = 2
DEC_BATCH = 128
DEC_SEQ = 8
PAST_LEN = 16384
PAGE_SIZE = 128

N_MIXERS = 2
N_SSD_LAYERS = (DEPTH + N_MIXERS - 1) // N_MIXERS
N_SG_LAYERS = DEPTH // N_MIXERS

SSD_EXPAND = 2
SSD_D_INNER = SSD_EXPAND * D_MODEL
SSD_HEAD_DIM = 64
SSD_HEADS = SSD_D_INNER // SSD_HEAD_DIM
SSD_GROUPS = 4
SSD_D_STATE = 128
SSD_CONV_W = 4
SSD_CONV_DIM = SSD_D_INNER + 2 * SSD_GROUPS * SSD_D_STATE
SSD_IN_DIM = SSD_D_INNER + SSD_CONV_DIM + SSD_HEADS
SSD_CHUNK = 128

SG_CHUNK = 128
SG_WIDTH = 2 * D_MODEL
SG_GROUPS = 8
SG_GROUP_DIM = SG_WIDTH // SG_GROUPS

MOE_GROUPS = 4
MOE_EXPERTS_PER_GROUP = 8
MOE_EXPERTS = MOE_GROUPS * MOE_EXPERTS_PER_GROUP
MOE_TOP_K = 2
MOE_D_FF = 256

NORM_EPS = 1e-6
LN_EPS = 1e-5

kernel_name = "hybrid_ssd_chunkmlp_hmoe_step"


def rms_norm(x, g):
    xf = x.astype(jnp.float32)
    y = xf * lax.rsqrt(jnp.mean(xf * xf, -1, keepdims=True) + NORM_EPS)
    return (y * g.astype(jnp.float32)).astype(x.dtype)


def _chunk_len(l, chunk):
    if l % chunk == 0:
        return chunk
    if l < chunk:
        return l
    return math.gcd(l, chunk)


def segsum(a):
    t = a.shape[-1]
    cs = jnp.cumsum(a, -1)
    diff = cs[..., :, None] - cs[..., None, :]
    mask = jnp.tril(jnp.ones((t, t), bool))
    return jnp.where(mask, diff, -jnp.inf)


def ssd_scan(x, dt, a, b, c, h0):
    f32 = jnp.float32
    n, l, nh, p = x.shape
    g, ns = b.shape[2], b.shape[3]
    r = nh // g
    q = _chunk_len(l, SSD_CHUNK)
    nc = l // q
    xdt = (x.astype(f32) * dt[..., None]).reshape(n, nc, q, g, r, p)
    da = (dt * a).reshape(n, nc, q, g, r).transpose(0, 3, 4, 1, 2)
    bq = b.astype(f32).reshape(n, nc, q, g, ns)
    cq = c.astype(f32).reshape(n, nc, q, g, ns)
    a_cs = jnp.cumsum(da, -1)
    decay_in = jnp.exp(segsum(da))
    scores = jnp.einsum("bclgn,bcsgn->bcgls", cq, bq)
    y_diag = jnp.einsum("bcgls,bgrcls,bcsgrp->bclgrp", scores, decay_in, xdt)
    decay_to_end = jnp.exp(a_cs[..., -1:] - a_cs)
    chunk_states = jnp.einsum("bcsgn,bgrcs,bcsgrp->bcgrpn", bq, decay_to_end, xdt)
    h0g = h0.astype(f32).reshape(n, g, r, p, ns)[:, None]
    states = jnp.concatenate([h0g, chunk_states], 1)
    chunk_tot = jnp.pad(a_cs[..., -1], ((0, 0), (0, 0), (0, 0), (1, 0)))
    decay_chunk = jnp.exp(segsum(chunk_tot))
    states = jnp.einsum("bgrzc,bcgrpn->bzgrpn", decay_chunk, states)
    h_final = states[:, -1].reshape(n, nh, p, ns)
    states = states[:, :-1]
    y_off = jnp.einsum("bclgn,bcgrpn,bgrcl->bclgrp", cq, states, jnp.exp(a_cs))
    y = (y_diag + y_off).reshape(n, l, nh, p)
    return y, h_final


def causal_dwconv(xbc, prev, w, bias):
    xp = jnp.concatenate([prev.astype(xbc.dtype), xbc], 1)
    l = xbc.shape[1]
    y = sum(xp[:, k:k + l] * w[k] for k in range(SSD_CONV_W)) + bias
    return y, xp[:, -(SSD_CONV_W - 1):]


def ssd_mixer(h, conv_prev, ssm_prev, w_in, conv_w, conv_b, dt_bias, a_log, d_skip, norm_g, w_out):
    f32 = jnp.float32
    n, l, _ = h.shape
    zxbcdt = h @ w_in
    z, xbc, dt_raw = jnp.split(zxbcdt, [SSD_D_INNER, SSD_D_INNER + SSD_CONV_DIM], -1)
    xbc_c, conv_new = causal_dwconv(xbc, conv_prev, conv_w, conv_b)
    xbc_c = jax.nn.silu(xbc_c)
    xs, bs, cs = jnp.split(xbc_c, [SSD_D_INNER, SSD_D_INNER + SSD_GROUPS * SSD_D_STATE], -1)
    xs = xs.reshape(n, l, SSD_HEADS, SSD_HEAD_DIM)
    bs = bs.reshape(n, l, SSD_GROUPS, SSD_D_STATE)
    cs = cs.reshape(n, l, SSD_GROUPS, SSD_D_STATE)
    dt = jax.nn.softplus(dt_raw.astype(f32) + dt_bias.astype(f32))
    a = -jnp.exp(a_log.astype(f32))
    y, ssm_new = ssd_scan(xs, dt, a, bs, cs, ssm_prev)
    y = y + xs.astype(f32) * d_skip.astype(f32)[:, None]
    y = y.reshape(n, l, SSD_D_INNER) * jax.nn.silu(z.astype(f32))
    yg = y.reshape(n, l, SSD_GROUPS, SSD_D_INNER // SSD_GROUPS)
    yg = yg * lax.rsqrt(jnp.mean(yg * yg, -1, keepdims=True) + NORM_EPS)
    y = yg.reshape(n, l, SSD_D_INNER) * norm_g.astype(f32)
    out = y.astype(h.dtype) @ w_out
    return out, conv_new, ssm_new.astype(ssm_prev.dtype)


def sg_mixer(h, w_in, b_in, ln_g, ln_b, w_s, b_s, w_out):
    f32 = jnp.float32
    n, l, _ = h.shape
    uv = jax.nn.gelu(h @ w_in + b_in)
    u, v = jnp.split(uv, 2, -1)
    vf = v.astype(f32)
    mu = jnp.mean(vf, -1, keepdims=True)
    var = jnp.mean(jnp.square(vf - mu), -1, keepdims=True)
    v = ((vf - mu) * lax.rsqrt(var + LN_EPS) * ln_g.astype(f32) + ln_b.astype(f32)).astype(h.dtype)
    pad = (-l) % SG_CHUNK
    lp = l + pad
    vc = jnp.pad(v, ((0, 0), (0, pad), (0, 0))).reshape(n, lp // SG_CHUNK, SG_CHUNK, SG_GROUPS, SG_GROUP_DIM)
    mask = jnp.tril(jnp.ones((SG_CHUNK, SG_CHUNK), bool))
    ws = jnp.where(mask, w_s, 0.0)
    mixed = jnp.einsum("gts,bcsgk->bctgk", ws, vc) + b_s.T[None, None, :, :, None]
    mixed = mixed.reshape(n, lp, SG_WIDTH)[:, :l]
    out = (u * mixed) @ w_out
    return out, v


def hier_moe(h, w_grp, b_grp, w_rt, b_rt, w_gate, w_up, w_down):
    f32 = jnp.float32
    shp = h.shape
    t = h.reshape(-1, D_MODEL)
    g_logits = (t @ w_grp).astype(f32) + b_grp.astype(f32)
    g_prob = jax.nn.softmax(g_logits, -1)
    g_top = jnp.argmax(g_logits, -1)
    p_g = jnp.take_along_axis(g_prob, g_top[:, None], -1)[:, 0]
    e_logits = ((t @ w_rt).astype(f32) + b_rt.astype(f32)).reshape(-1, MOE_GROUPS, MOE_EXPERTS_PER_GROUP)
    e_logits = jnp.take_along_axis(e_logits, g_top[:, None, None], 1)[:, 0]
    e_prob = jax.nn.softmax(e_logits, -1)
    top_p, top_i = lax.top_k(e_prob, MOE_TOP_K)
    top_p = top_p / jnp.sum(top_p, -1, keepdims=True)
    expert_id = g_top[:, None] * MOE_EXPERTS_PER_GROUP + top_i
    gates = jnp.sum(jax.nn.one_hot(expert_id, MOE_EXPERTS, dtype=f32) * (p_g[:, None] * top_p)[..., None], 1)
    hg = jnp.einsum("td,edf->tef", t, w_gate)
    hu = jnp.einsum("td,edf->tef", t, w_up)
    act = jax.nn.silu(hg) * hu * gates[..., None].astype(t.dtype)
    y = jnp.einsum("tef,efd->td", act, w_down)
    return y.reshape(shp)


def _trunk(x, conv_prev, ssm_prev, norm_mix, norm_ffn, norm_final,
           ssd_w_in, ssd_conv_w, ssd_conv_b, ssd_dt_bias, ssd_a_log, ssd_d, ssd_norm, ssd_w_out,
           sg_w_in, sg_b_in, sg_ln_g, sg_ln_b, sg_w_s, sg_b_s, sg_w_out,
           moe_w_group, moe_b_group, moe_w_router, moe_b_router, moe_w_gate, moe_w_up, moe_w_down):
    conv_out, ssm_out, v_out = [], [], []
    for i in range(DEPTH):
        hn = rms_norm(x, norm_mix[i])
        j = i // N_MIXERS
        if i % N_MIXERS == 0:
            mix, c_new, s_new = ssd_mixer(hn, conv_prev[j], ssm_prev[j], ssd_w_in[j], ssd_conv_w[j], ssd_conv_b[j],
                                          ssd_dt_bias[j], ssd_a_log[j], ssd_d[j], ssd_norm[j], ssd_w_out[j])
            conv_out.append(c_new)
            ssm_out.append(s_new)
        else:
            mix, v_new = sg_mixer(hn, sg_w_in[j], sg_b_in[j], sg_ln_g[j], sg_ln_b[j], sg_w_s[j], sg_b_s[j], sg_w_out[j])
            v_out.append(v_new)
        x = x + mix
        x = x + hier_moe(rms_norm(x, norm_ffn[i]), moe_w_group[i], moe_b_group[i], moe_w_router[i],
                         moe_b_router[i], moe_w_gate[i], moe_w_up[i], moe_w_down[i])
    y = rms_norm(x, norm_final)
    return y, jnp.stack(conv_out), jnp.stack(ssm_out), jnp.stack(v_out)


def setup_inputs(seed: int = 0) -> dict:
    key = jax.random.key(seed)
    ks = iter(jax.random.split(key, 40))
    f32 = jnp.float32

    def nrm(shape, scale):
        return scale * jax.random.normal(next(ks), shape, f32)

    u = jax.random.uniform(next(ks), (N_SSD_LAYERS, SSD_HEADS), f32)
    dt0 = jnp.exp(u * (math.log(0.1) - math.log(0.001)) + math.log(0.001))
    dt_bias = dt0 + jnp.log(-jnp.expm1(-dt0))
    a_log = jnp.log(jax.random.uniform(next(ks), (N_SSD_LAYERS, SSD_HEADS), f32, 1.0, 16.0))
    return {
        "x_prompt": nrm((BATCH, SEQ, D_MODEL), 1.0),
        "x_sample": nrm((DEC_BATCH, DEC_SEQ, D_MODEL), 1.0),
        "state_ssm": nrm((N_SSD_LAYERS, DEC_BATCH, SSD_HEADS, SSD_HEAD_DIM, SSD_D_STATE), 0.5),
        "state_conv": nrm((N_SSD_LAYERS, DEC_BATCH, SSD_CONV_W - 1, SSD_CONV_DIM), 1.0),
        "norm_mix": 1.0 + nrm((DEPTH, D_MODEL), 0.02),
        "norm_ffn": 1.0 + nrm((DEPTH, D_MODEL), 0.02),
        "norm_final": 1.0 + nrm((D_MODEL,), 0.02),
        "ssd_w_in": nrm((N_SSD_LAYERS, D_MODEL, SSD_IN_DIM), D_MODEL ** -0.5),
        "ssd_conv_w": nrm((N_SSD_LAYERS, SSD_CONV_W, SSD_CONV_DIM), SSD_CONV_W ** -0.5),
        "ssd_conv_b": nrm((N_SSD_LAYERS, SSD_CONV_DIM), 0.02),
        "ssd_dt_bias": dt_bias,
        "ssd_a_log": a_log,
        "ssd_d": 1.0 + nrm((N_SSD_LAYERS, SSD_HEADS), 0.1),
        "ssd_norm": 1.0 + nrm((N_SSD_LAYERS, SSD_D_INNER), 0.02),
        "ssd_w_out": nrm((N_SSD_LAYERS, SSD_D_INNER, D_MODEL), SSD_D_INNER ** -0.5),
        "sg_w_in": nrm((N_SG_LAYERS, D_MODEL, 2 * SG_WIDTH), D_MODEL ** -0.5),
        "sg_b_in": nrm((N_SG_LAYERS, 2 * SG_WIDTH), 0.02),
        "sg_ln_g": 1.0 + nrm((N_SG_LAYERS, SG_WIDTH), 0.02),
        "sg_ln_b": nrm((N_SG_LAYERS, SG_WIDTH), 0.02),
        "sg_w_s": nrm((N_SG_LAYERS, SG_GROUPS, SG_CHUNK, SG_CHUNK), SG_CHUNK ** -0.5),
        "sg_b_s": 1.0 + nrm((N_SG_LAYERS, SG_GROUPS, SG_CHUNK), 0.1),
        "sg_w_out": nrm((N_SG_LAYERS, SG_WIDTH, D_MODEL), SG_WIDTH ** -0.5),
        "moe_w_group": nrm((DEPTH, D_MODEL, MOE_GROUPS), D_MODEL ** -0.5),
        "moe_b_group": nrm((DEPTH, MOE_GROUPS), 0.01),
        "moe_w_router": nrm((DEPTH, D_MODEL, MOE_EXPERTS), D_MODEL ** -0.5),
        "moe_b_router": nrm((DEPTH, MOE_EXPERTS), 0.01),
        "moe_w_gate": nrm((DEPTH, MOE_EXPERTS, D_MODEL, MOE_D_FF), D_MODEL ** -0.5),
        "moe_w_up": nrm((DEPTH, MOE_EXPERTS, D_MODEL, MOE_D_FF), D_MODEL ** -0.5),
        "moe_w_down": nrm((DEPTH, MOE_EXPERTS, MOE_D_FF, D_MODEL), MOE_D_FF ** -0.5),
    }


def reference(x_prompt, x_sample, state_ssm, state_conv, norm_mix, norm_ffn, norm_final,
              ssd_w_in, ssd_conv_w, ssd_conv_b, ssd_dt_bias, ssd_a_log, ssd_d, ssd_norm, ssd_w_out,
              sg_w_in, sg_b_in, sg_ln_g, sg_ln_b, sg_w_s, sg_b_s, sg_w_out,
              moe_w_group, moe_b_group, moe_w_router, moe_b_router, moe_w_gate, moe_w_up, moe_w_down):
    params = (norm_mix, norm_ffn, norm_final,
              ssd_w_in, ssd_conv_w, ssd_conv_b, ssd_dt_bias, ssd_a_log, ssd_d, ssd_norm, ssd_w_out,
              sg_w_in, sg_b_in, sg_ln_g, sg_ln_b, sg_w_s, sg_b_s, sg_w_out,
              moe_w_group, moe_b_group, moe_w_router, moe_b_router, moe_w_gate, moe_w_up, moe_w_down)
    nb = x_prompt.shape[0]
    conv0 = jnp.zeros((N_SSD_LAYERS, nb, SSD_CONV_W - 1, SSD_CONV_DIM), x_prompt.dtype)
    ssm0 = jnp.zeros((N_SSD_LAYERS, nb, SSD_HEADS, SSD_HEAD_DIM, SSD_D_STATE), jnp.float32)
    y_prompt, conv_p, ssm_p, _ = _trunk(x_prompt, conv0, ssm0, *params)
    y_sample, conv_s, ssm_s, v_s = _trunk(x_sample, state_conv, state_ssm, *params)
    return (y_prompt, y_sample, ssm_p, conv_p, ssm_s, conv_s, v_s)
```

```python
import functools
import math

import jax
import jax.numpy as jnp
import numpy as np
from jax import lax
from jax.experimental import pallas as pl
from jax.experimental.pallas import tpu as pltpu

F32 = jnp.float32
BF16 = jnp.bfloat16
I32 = jnp.int32

D_MODEL = 1024
N_HEADS = 32
HEAD_DIM = 64
N_GROUPS = 4
D_STATE = 128
D_INNER = N_HEADS * HEAD_DIM
GROUP_W = D_INNER // N_GROUPS
CONV_W = 4
CONV_DIM = D_INNER + 2 * N_GROUPS * D_STATE
DT_PAD = 128
SSD_CHUNK = 128
SG_WIDTH = 2 * D_MODEL
SG_GROUPS = 8
SG_GROUP_DIM = SG_WIDTH // SG_GROUPS
SG_CHUNK = 128
MOE_GROUPS = 4
MOE_EPG = 8
MOE_EXPERTS = MOE_GROUPS * MOE_EPG
MOE_D_FF = 256
NORM_EPS = 1e-6
LN_EPS = 1e-5

ROUTER_TILE = 256
EXPERT_BLOCK = 256
COMBINE_TILE = 256
VMEM_LIMIT = 56 * 1024 * 1024


def _sigmoid(x):
    return 1.0 / (1.0 + jnp.exp(-x))


def _silu(x):
    return x * _sigmoid(x)


def _softplus(x):
    return jnp.maximum(x, 0.0) + jnp.log(1.0 + jnp.exp(-jnp.abs(x)))


def _gelu_tanh(x):
    c = math.sqrt(2.0 / math.pi)
    return x * (0.5 * (1.0 + jnp.tanh(c * (x + 0.044715 * (x * x * x)))))


def _rms(x, g):
    return x * lax.rsqrt(jnp.mean(x * x, axis=-1, keepdims=True) + NORM_EPS) * g


def _split3(x):
    a = x.astype(BF16)
    r = x - a.astype(F32)
    b = r.astype(BF16)
    c = (r - b.astype(F32)).astype(BF16)
    return a, b, c


def _dot(a, b):
    return jnp.dot(a, b, preferred_element_type=F32)


def _dot_nt(a, b):
    return lax.dot_general(a, b, (((1,), (1,)), ((), ())), preferred_element_type=F32)


def _dot_tn(a, b):
    return lax.dot_general(a, b, (((0,), (0,)), ((), ())), preferred_element_type=F32)


def _const_spec(shape):
    nd = len(shape)
    return pl.BlockSpec(shape, lambda *_: (0,) * nd)


def _ssd_kernel(x_ref, g_ref, win_ref, cw_ref, cb_ref, dtb_ref, alog_ref, dsk_ref, ng_ref,
                wout_ref, tri_ref, ones_ref, cin_ref, sin_ref,
                xo_ref, cout_ref, sout_ref,
                cscr, yoff_scr, y_scr, xw_scr, *, ns, q):
    c = pl.program_id(1)
    r = ns * q
    seg = 8 + q

    x = x_ref[...]
    hn = _rms(x, g_ref[...]).astype(BF16)
    z = _dot(hn, win_ref[:, 0:D_INNER])
    xbc = _dot(hn, win_ref[:, D_INNER:D_INNER + CONV_DIM])
    dtr = _dot(hn, win_ref[:, D_INNER + CONV_DIM:D_INNER + CONV_DIM + DT_PAD])

    @pl.when(c == 0)
    def _():
        sout_ref[...] = sin_ref[...]
        for s in range(ns):
            for i in range(CONV_W - 1):
                cscr[pl.ds(s * seg + 5 + i, 1), :] = cin_ref[s, pl.ds(i, 1), :]

    for s in range(ns):
        cscr[pl.ds(s * seg + 8, q), :] = xbc[s * q:(s + 1) * q, :]
    parts = []
    for s in range(ns):
        acc = cb_ref[...]
        for k in range(CONV_W):
            acc = acc + cw_ref[pl.ds(k, 1), :] * cscr[pl.ds(s * seg + 5 + k, q), :]
        parts.append(acc)
    xc = parts[0] if ns == 1 else jnp.concatenate(parts, axis=0)
    xc = _silu(xc)
    for s in range(ns):
        for i in range(CONV_W - 1):
            row = cscr[pl.ds(s * seg + 5 + q + i, 1), :]
            cscr[pl.ds(s * seg + 5 + i, 1), :] = row
            cout_ref[s, pl.ds(i, 1), :] = row

    xs = xc[:, 0:D_INNER]
    bb = xc[:, D_INNER:D_INNER + N_GROUPS * D_STATE].astype(BF16)
    cc = xc[:, D_INNER + N_GROUPS * D_STATE:CONV_DIM].astype(BF16)

    dt = _softplus(dtr + dtb_ref[...])
    da = dt * (-jnp.exp(alog_ref[...]))
    tri = tri_ref[...]
    trib = tri.astype(BF16)
    onesb = ones_ref[...].astype(BF16)
    d1, d2, d3 = _split3(da)
    cs = _dot(trib, d1) + _dot(trib, d2) + _dot(trib, d3)
    cl = _dot(onesb, d1) + _dot(onesb, d2) + _dot(onesb, d3)
    ecs = jnp.exp(cs)
    wgt = dt * jnp.exp(cl - cs)
    ecl = jnp.exp(cl)
    cs_t = cs.T
    dt_t = dt.T

    rowseq = lax.shift_right_logical(lax.broadcasted_iota(I32, (r, 1), 0), int(math.log2(q)))
    lane = lax.broadcasted_iota(I32, (r, 128), 1)
    lo = lane < HEAD_DIM
    mask = tri > 0.5

    for g in range(N_GROUPS):
        cg = cc[:, g * D_STATE:(g + 1) * D_STATE]
        acc = None
        for s in range(ns):
            st = sout_ref[s, g * GROUP_W:(g + 1) * GROUP_W, :].astype(BF16)
            yo = _dot_nt(cg, st)
            if ns > 1:
                yo = jnp.where(rowseq == s, yo, 0.0)
            acc = yo if acc is None else acc + yo
        yoff_scr[:, g * GROUP_W:(g + 1) * GROUP_W] = acc

    for g in range(N_GROUPS):
        cg = cc[:, g * D_STATE:(g + 1) * D_STATE]
        bg = bb[:, g * D_STATE:(g + 1) * D_STATE]
        sc = _dot_nt(cg, bg)
        for jj in range(N_HEADS // N_GROUPS // 2):
            j = g * (N_HEADS // N_GROUPS // 2) + jj
            ms = []
            for h in (2 * j, 2 * j + 1):
                diff = cs[:, h:h + 1] - cs_t[h:h + 1, :]
                dec = jnp.exp(jnp.where(mask, diff, -jnp.inf))
                ms.append((sc * dec * dt_t[h:h + 1, :]).astype(BF16))
            lhs = jnp.concatenate(ms, axis=1)
            xp = xs[:, j * 128:(j + 1) * 128]
            rhs = jnp.concatenate([jnp.where(lo, xp, 0.0).astype(BF16),
                                   jnp.where(lo, 0.0, xp).astype(BF16)], axis=0)
            yd = _dot(lhs, rhs)
            ecs_p = jnp.where(lo, ecs[:, 2 * j:2 * j + 1], ecs[:, 2 * j + 1:2 * j + 2])
            wgt_p = jnp.where(lo, wgt[:, 2 * j:2 * j + 1], wgt[:, 2 * j + 1:2 * j + 2])
            cols = slice(j * 128, (j + 1) * 128)
            y_scr[:, cols] = yd + yoff_scr[:, cols] * ecs_p + xp * dsk_ref[:, cols]
            xw_scr[:, cols] = (xp * wgt_p).astype(BF16)

    bf = xc[:, D_INNER:D_INNER + N_GROUPS * D_STATE]
    for g in range(N_GROUPS):
        xwg = xw_scr[:, g * GROUP_W:(g + 1) * GROUP_W]
        for s in range(ns):
            bg = bf[:, g * D_STATE:(g + 1) * D_STATE]
            if ns > 1:
                bg = jnp.where(rowseq == s, bg, 0.0)
            upd = _dot_tn(xwg, bg.astype(BF16))
            dec = jnp.concatenate(
                [jnp.broadcast_to(ecl[s * q:s * q + 1, h:h + 1], (HEAD_DIM, D_STATE))
                 for h in range(g * 8, g * 8 + 8)], axis=0)
            rows = slice(g * GROUP_W, (g + 1) * GROUP_W)
            sout_ref[s, rows, :] = sout_ref[s, rows, :] * dec + upd

    y = y_scr[...] * _silu(z)
    parts = []
    for g in range(N_GROUPS):
        yg = y[:, g * GROUP_W:(g + 1) * GROUP_W]
        parts.append(yg * lax.rsqrt(jnp.mean(yg * yg, axis=-1, keepdims=True) + NORM_EPS))
    yn = jnp.concatenate(parts, axis=1) * ng_ref[...]
    xo_ref[...] = x + _dot(yn.astype(BF16), wout_ref[...])


def _ssd_layer(x2d, conv_prev, ssm_prev, norm_g, w_in, conv_w, conv_b, dt_bias, a_log, d_skip,
               norm_y, w_out, *, n_seq, seq_len, ns, q):
    r = ns * q
    n_chunks = seq_len // q
    n_sb = n_seq // ns
    pad = DT_PAD - N_HEADS
    win = jnp.concatenate([w_in, jnp.zeros((D_MODEL, pad), F32)], axis=1).astype(BF16)
    dtb = jnp.pad(dt_bias, (0, pad)).reshape(1, DT_PAD)
    alog = jnp.pad(a_log, (0, pad)).reshape(1, DT_PAD)
    dsk = jnp.repeat(d_skip, HEAD_DIM).reshape(1, D_INNER)
    blk = np.kron(np.eye(ns), np.ones((q, q)))
    tri = jnp.asarray(blk * np.tril(np.ones((r, r))), F32)
    ones = jnp.asarray(blk, F32)
    state = ssm_prev.reshape(n_seq, D_INNER, D_STATE)

    kern = functools.partial(_ssd_kernel, ns=ns, q=q)
    out_shape = (jax.ShapeDtypeStruct(x2d.shape, F32),
                 jax.ShapeDtypeStruct((n_seq, CONV_W - 1, CONV_DIM), F32),
                 jax.ShapeDtypeStruct((n_seq, D_INNER, D_STATE), F32))
    row_spec = pl.BlockSpec((r, D_MODEL), lambda sb, c: (sb * n_chunks + c, 0))
    conv_spec = pl.BlockSpec((ns, CONV_W - 1, CONV_DIM), lambda sb, c: (sb, 0, 0))
    state_spec = pl.BlockSpec((ns, D_INNER, D_STATE), lambda sb, c: (sb, 0, 0))
    in_specs = [row_spec,
                _const_spec((1, D_MODEL)),
                _const_spec(win.shape),
                _const_spec((CONV_W, CONV_DIM)),
                _const_spec((1, CONV_DIM)),
                _const_spec((1, DT_PAD)),
                _const_spec((1, DT_PAD)),
                _const_spec((1, D_INNER)),
                _const_spec((1, D_INNER)),
                _const_spec((D_INNER, D_MODEL)),
                _const_spec((r, r)),
                _const_spec((r, r)),
                conv_spec, state_spec]
    x_new, conv_new, state_new = pl.pallas_call(
        kern,
        grid=(n_sb, n_chunks),
        in_specs=in_specs,
        out_specs=(row_spec, conv_spec, state_spec),
        out_shape=out_shape,
        scratch_shapes=[pltpu.VMEM((ns * (8 + q), CONV_DIM), F32),
                        pltpu.VMEM((r, D_INNER), F32),
                        pltpu.VMEM((r, D_INNER), F32),
                        pltpu.VMEM((r, D_INNER), BF16)],
        compiler_params=pltpu.CompilerParams(
            dimension_semantics=("arbitrary", "arbitrary"), vmem_limit_bytes=VMEM_LIMIT),
        name="ssd_layer",
    )(x2d, norm_g.reshape(1, D_MODEL), win, conv_w, conv_b.reshape(1, CONV_DIM), dtb, alog, dsk,
      norm_y.reshape(1, D_INNER), w_out.astype(BF16), tri, ones, conv_prev, state)
    return x_new, conv_new, state_new.reshape(n_seq, N_HEADS, HEAD_DIM, D_STATE)


def _sg_kernel(x_ref, g_ref, win_ref, bin_ref, lng_ref, lnb_ref, wmix_ref, bmix_ref, wout_ref,
               xo_ref, *v_out, r):
    x = x_ref[...]
    hn = _rms(x, g_ref[...]).astype(BF16)
    u = _gelu_tanh(_dot(hn, win_ref[:, 0:SG_WIDTH]) + bin_ref[:, 0:SG_WIDTH])
    v = _gelu_tanh(_dot(hn, win_ref[:, SG_WIDTH:2 * SG_WIDTH]) + bin_ref[:, SG_WIDTH:2 * SG_WIDTH])
    mu = jnp.mean(v, axis=-1, keepdims=True)
    vc = v - mu
    var = jnp.mean(vc * vc, axis=-1, keepdims=True)
    vn = vc * lax.rsqrt(var + LN_EPS) * lng_ref[...] + lnb_ref[...]
    if v_out:
        v_out[0][...] = vn
    vb = vn.astype(BF16)
    parts = []
    for g in range(SG_GROUPS):
        cols = slice(g * SG_GROUP_DIM, (g + 1) * SG_GROUP_DIM)
        parts.append(_dot(wmix_ref[g], vb[:, cols]) + bmix_ref[:, cols])
    mixed = jnp.concatenate(parts, axis=1)
    xo_ref[...] = x + _dot((u * mixed).astype(BF16), wout_ref[...])


def _sg_layer(x2d, norm_g, w_in, b_in, ln_g, ln_b, w_s, b_s, w_out, *, seq_len, want_v):
    r = SG_CHUNK
    t = x2d.shape[0]
    q = min(seq_len, SG_CHUNK)
    reps = r // q
    ws = jnp.tril(w_s)[:, :q, :q]
    wmix = jnp.einsum("ab,gts->gatbs", jnp.eye(reps, dtype=F32), ws).reshape(SG_GROUPS, r, r)
    bmix = jnp.tile(jnp.repeat(b_s.T[:q], SG_GROUP_DIM, axis=1), (reps, 1))
    row_spec = pl.BlockSpec((r, D_MODEL), lambda i: (i, 0))
    v_spec = pl.BlockSpec((r, SG_WIDTH), lambda i: (i, 0))
    out_shape = [jax.ShapeDtypeStruct(x2d.shape, F32)]
    out_specs = [row_spec]
    if want_v:
        out_shape.append(jax.ShapeDtypeStruct((t, SG_WIDTH), F32))
        out_specs.append(v_spec)
    outs = pl.pallas_call(
        functools.partial(_sg_kernel, r=r),
        grid=(t // r,),
        in_specs=[row_spec,
                  _const_spec((1, D_MODEL)),
                  _const_spec((D_MODEL, 2 * SG_WIDTH)),
                  _const_spec((1, 2 * SG_WIDTH)),
                  _const_spec((1, SG_WIDTH)),
                  _const_spec((1, SG_WIDTH)),
                  _const_spec((SG_GROUPS, r, r)),
                  _const_spec((r, SG_WIDTH)),
                  _const_spec((SG_WIDTH, D_MODEL))],
        out_specs=out_specs,
        out_shape=out_shape,
        compiler_params=pltpu.CompilerParams(
            dimension_semantics=("arbitrary",), vmem_limit_bytes=VMEM_LIMIT),
        name="sg_layer",
    )(x2d, norm_g.reshape(1, D_MODEL), w_in.astype(BF16), b_in.reshape(1, 2 * SG_WIDTH),
      ln_g.reshape(1, SG_WIDTH), ln_b.reshape(1, SG_WIDTH), wmix.astype(BF16), bmix,
      w_out.astype(BF16))
    return outs if want_v else (outs[0], None)


_LANE_E0 = MOE_GROUPS


def _router_kernel(x_ref, g_ref, wr_ref, br_ref, ltri_ref, hn_ref, meta_ref, cnt_ref, cnt_scr):
    i = pl.program_id(0)
    tt = x_ref.shape[0]

    @pl.when(i == 0)
    def _():
        cnt_scr[...] = jnp.zeros_like(cnt_scr)

    hn = _rms(x_ref[...], g_ref[...])
    hn_ref[...] = hn
    h1 = hn.astype(BF16)
    h2 = (hn - h1.astype(F32)).astype(BF16)
    wr = wr_ref[...]
    w1 = wr.astype(BF16)
    w2 = (wr - w1.astype(F32)).astype(BF16)
    logits = _dot(h1, w1) + _dot(h1, w2) + _dot(h2, w1) + br_ref[...]

    lane = lax.broadcasted_iota(I32, (tt, 128), 1).astype(F32)
    neg = -jnp.inf
    gl = jnp.where(lane < MOE_GROUPS, logits, neg)
    gmax = jnp.max(gl, axis=-1, keepdims=True)
    g_top = jnp.min(jnp.where(gl == gmax, lane, 128.0), axis=-1, keepdims=True)
    p_g = 1.0 / jnp.sum(jnp.exp(gl - gmax), axis=-1, keepdims=True)

    first = _LANE_E0 + MOE_EPG * g_top
    in_grp = (lane >= first) & (lane < first + MOE_EPG)
    el = jnp.where(in_grp, logits, neg)
    emax = jnp.max(el, axis=-1, keepdims=True)
    ee = jnp.exp(el - emax)
    prob = jnp.where(in_grp, ee / jnp.sum(ee, axis=-1, keepdims=True), -1.0)
    p1 = jnp.max(prob, axis=-1, keepdims=True)
    i1 = jnp.min(jnp.where(prob == p1, lane, 128.0), axis=-1, keepdims=True)
    prob2 = jnp.where(lane == i1, -1.0, prob)
    p2 = jnp.max(prob2, axis=-1, keepdims=True)
    i2 = jnp.min(jnp.where(prob2 == p2, lane, 128.0), axis=-1, keepdims=True)
    psum = p1 + p2
    gate1 = p_g * (p1 / psum)
    gate2 = p_g * (p2 / psum)

    sel1 = lane == i1
    sel2 = lane == i2
    onehot = jnp.where(sel1 | sel2, 1.0, 0.0)
    before = _dot(ltri_ref[...], onehot.astype(BF16)) + cnt_scr[...]
    rank1 = jnp.sum(jnp.where(sel1, before, 0.0), axis=-1, keepdims=True)
    rank2 = jnp.sum(jnp.where(sel2, before, 0.0), axis=-1, keepdims=True)
    cnt_scr[...] = cnt_scr[...] + jnp.sum(onehot, axis=0, keepdims=True)
    cnt_ref[...] = cnt_scr[...]

    e1 = i1 - _LANE_E0
    e2 = i2 - _LANE_E0
    meta = jnp.where(lane == 0, e1, 0.0)
    meta = jnp.where(lane == 1, e2, meta)
    meta = jnp.where(lane == 2, gate1, meta)
    meta = jnp.where(lane == 3, gate2, meta)
    meta = jnp.where(lane == 4, rank1, meta)
    meta = jnp.where(lane == 5, rank2, meta)
    meta_ref[...] = meta


def _expert_kernel(bexp_ref, nv_ref, tok_ref, tokn_ref, hn_hbm, wg_ref, wu_ref, wd_ref, ys_ref,
                   buf, sem):
    b = pl.program_id(0)
    nv = nv_ref[0]
    slot = b % 2
    bm = ys_ref.shape[0]

    def gather(idx_ref, sl):
        def body(rr, carry):
            tok = idx_ref[0, 0, rr]
            pltpu.make_async_copy(hn_hbm.at[pl.ds(tok, 1), :], buf.at[sl, pl.ds(rr, 1), :],
                                  sem.at[sl]).start()
            return carry
        lax.fori_loop(0, bm, body, 0)

    @pl.when(b == 0)
    def _():
        gather(tok_ref, 0)

    @pl.when(b >= nv)
    def _():
        ys_ref[...] = jnp.zeros_like(ys_ref)

    @pl.when(b < nv)
    def _():
        @pl.when(b + 1 < nv)
        def _():
            gather(tokn_ref, 1 - slot)

        pltpu.make_async_copy(hn_hbm.at[pl.ds(0, bm), :], buf.at[slot], sem.at[slot]).wait()
        xg = buf[slot].astype(BF16)
        hg = _dot(xg, wg_ref[...].astype(BF16))
        hu = _dot(xg, wu_ref[...].astype(BF16))
        act = (_silu(hg) * hu).astype(BF16)
        ys_ref[...] = _dot(act, wd_ref[...].astype(BF16))


def _combine_kernel(pos_ref, posn_ref, x_ref, meta_ref, gf_ref, ys_hbm, o_ref, buf, sem, *, final):
    i = pl.program_id(0)
    n = pl.num_programs(0)
    slot = i % 2
    tc = x_ref.shape[0]

    def gather(idx_ref, sl):
        def body(rr, carry):
            p = idx_ref[0, 0, rr]
            pltpu.make_async_copy(ys_hbm.at[pl.ds(p, 1), :], buf.at[sl, pl.ds(rr, 1), :],
                                  sem.at[sl]).start()
            return carry
        lax.fori_loop(0, 2 * tc, body, 0)

    @pl.when(i == 0)
    def _():
        gather(pos_ref, 0)

    @pl.when(i + 1 < n)
    def _():
        gather(posn_ref, 1 - slot)

    pltpu.make_async_copy(ys_hbm.at[pl.ds(0, 2 * tc), :], buf.at[slot], sem.at[slot]).wait()
    meta = meta_ref[...]
    y = x_ref[...] + meta[:, 2:3] * buf[slot, 0:tc, :] + meta[:, 3:4] * buf[slot, tc:2 * tc, :]
    if final:
        y = _rms(y, gf_ref[...])
    o_ref[...] = y


def _moe(x2d, norm_g, w_grp, b_grp, w_rt, b_rt, w_gate, w_up, w_down, norm_final):
    t = x2d.shape[0]
    tt = ROUTER_TILE
    bm = EXPERT_BLOCK
    tc = COMBINE_TILE
    lane_pad = 128 - MOE_GROUPS - MOE_EXPERTS
    wr = jnp.concatenate([w_grp, w_rt, jnp.zeros((D_MODEL, lane_pad), F32)], axis=1)
    br = jnp.concatenate([b_grp, b_rt, jnp.zeros((lane_pad,), F32)]).reshape(1, 128)
    ltri = jnp.asarray(np.tril(np.ones((tt, tt)), -1), BF16)

    hn, meta, cnt = pl.pallas_call(
        _router_kernel,
        grid=(t // tt,),
        in_specs=[pl.BlockSpec((tt, D_MODEL), lambda i: (i, 0)),
                  _const_spec((1, D_MODEL)),
                  _const_spec((D_MODEL, 128)),
                  _const_spec((1, 128)),
                  _const_spec((tt, tt))],
        out_specs=(pl.BlockSpec((tt, D_MODEL), lambda i: (i, 0)),
                   pl.BlockSpec((tt, 128), lambda i: (i, 0)),
                   _const_spec((1, 128))),
        out_shape=(jax.ShapeDtypeStruct((t, D_MODEL), F32),
                   jax.ShapeDtypeStruct((t, 128), F32),
                   jax.ShapeDtypeStruct((1, 128), F32)),
        scratch_shapes=[pltpu.VMEM((1, 128), F32)],
        compiler_params=pltpu.CompilerParams(
            dimension_semantics=("arbitrary",), vmem_limit_bytes=VMEM_LIMIT),
        name="moe_router",
    )(x2d, norm_g.reshape(1, D_MODEL), wr, br, ltri)

    eid = meta[:, 0:2].astype(I32)
    rank = meta[:, 4:6].astype(I32)
    counts = cnt[0, _LANE_E0:_LANE_E0 + MOE_EXPERTS].astype(I32)
    nblk = (counts + bm - 1) // bm
    blk_end = jnp.cumsum(nblk)
    seg_start = (blk_end - nblk) * bm
    pos = jnp.sum(jnp.where(eid[..., None] == jnp.arange(MOE_EXPERTS, dtype=I32), seg_start, 0),
                  axis=-1) + rank
    n_blocks = (2 * t) // bm + MOE_EXPERTS
    n_valid = blk_end[-1:].astype(I32)
    block_expert = jnp.minimum(
        jnp.sum((jnp.arange(n_blocks, dtype=I32)[:, None] >= blk_end[None, :]).astype(I32), axis=1),
        MOE_EXPERTS - 1).astype(I32)
    tok = jnp.zeros((n_blocks * bm,), I32).at[pos.reshape(-1)].set(
        jnp.repeat(jnp.arange(t, dtype=I32), 2), unique_indices=True)
    tok = tok.reshape(n_blocks, 1, bm)

    w_spec = lambda shape: pl.BlockSpec((None,) + shape, lambda b, be, nv: (be[b], 0, 0))
    idx_spec = lambda nxt: pl.BlockSpec(
        (1, 1, bm), lambda b, be, nv: (jnp.minimum(b + nxt, n_blocks - 1), 0, 0),
        memory_space=pltpu.SMEM)
    ys = pl.pallas_call(
        _expert_kernel,
        grid_spec=pltpu.PrefetchScalarGridSpec(
            num_scalar_prefetch=2,
            grid=(n_blocks,),
            in_specs=[idx_spec(0), idx_spec(1),
                      pl.BlockSpec(memory_space=pl.ANY),
                      w_spec((D_MODEL, MOE_D_FF)),
                      w_spec((D_MODEL, MOE_D_FF)),
                      w_spec((MOE_D_FF, D_MODEL))],
            out_specs=pl.BlockSpec((bm, D_MODEL), lambda b, be, nv: (b, 0)),
            scratch_shapes=[pltpu.VMEM((2, bm, D_MODEL), F32),
                            pltpu.SemaphoreType.DMA((2,))]),
        out_shape=jax.ShapeDtypeStruct((n_blocks * bm, D_MODEL), F32),
        compiler_params=pltpu.CompilerParams(
            dimension_semantics=("arbitrary",), vmem_limit_bytes=VMEM_LIMIT),
        name="moe_experts",
    )(block_expert, n_valid, tok, tok, hn, w_gate, w_up, w_down)

    n_tiles = t // tc
    pos_t = pos.reshape(n_tiles, tc, 2).transpose(0, 2, 1).reshape(n_tiles, 1, 2 * tc)
    final = norm_final is not None
    gf = (norm_final if final else jnp.ones((D_MODEL,), F32)).reshape(1, D_MODEL)
    pos_spec = lambda nxt: pl.BlockSpec(
        (1, 1, 2 * tc), lambda i: (jnp.minimum(i + nxt, n_tiles - 1), 0, 0),
        memory_space=pltpu.SMEM)
    out = pl.pallas_call(
        functools.partial(_combine_kernel, final=final),
        grid=(n_tiles,),
        in_specs=[pos_spec(0), pos_spec(1),
                  pl.BlockSpec((tc, D_MODEL), lambda i: (i, 0)),
                  pl.BlockSpec((tc, 128), lambda i: (i, 0)),
                  _const_spec((1, D_MODEL)),
                  pl.BlockSpec(memory_space=pl.ANY)],
        out_specs=pl.BlockSpec((tc, D_MODEL), lambda i: (i, 0)),
        out_shape=jax.ShapeDtypeStruct((t, D_MODEL), F32),
        scratch_shapes=[pltpu.VMEM((2, 2 * tc, D_MODEL), F32),
                        pltpu.SemaphoreType.DMA((2,))],
        compiler_params=pltpu.CompilerParams(
            dimension_semantics=("arbitrary",), vmem_limit_bytes=VMEM_LIMIT),
        name="moe_combine",
    )(pos_t, pos_t, x2d, meta, gf, ys)
    return out


def _trunk(x, conv_prev, ssm_prev, p, *, ssd_ns, want_v):
    n_seq, seq_len, _ = x.shape
    q = seq_len if seq_len < SSD_CHUNK else SSD_CHUNK
    x2d = x.reshape(n_seq * seq_len, D_MODEL)
    x2d, conv_new, ssm_new = _ssd_layer(
        x2d, conv_prev[0], ssm_prev[0], p["norm_mix"][0], p["ssd_w_in"][0], p["ssd_conv_w"][0],
        p["ssd_conv_b"][0], p["ssd_dt_bias"][0], p["ssd_a_log"][0], p["ssd_d"][0], p["ssd_norm"][0],
        p["ssd_w_out"][0], n_seq=n_seq, seq_len=seq_len, ns=ssd_ns, q=q)
    x2d = _moe(x2d, p["norm_ffn"][0], p["moe_w_group"][0], p["moe_b_group"][0], p["moe_w_router"][0],
               p["moe_b_router"][0], p["moe_w_gate"][0], p["moe_w_up"][0], p["moe_w_down"][0], None)
    x2d, v = _sg_layer(x2d, p["norm_mix"][1], p["sg_w_in"][0], p["sg_b_in"][0], p["sg_ln_g"][0],
                       p["sg_ln_b"][0], p["sg_w_s"][0], p["sg_b_s"][0], p["sg_w_out"][0],
                       seq_len=seq_len, want_v=want_v)
    y2d = _moe(x2d, p["norm_ffn"][1], p["moe_w_group"][1], p["moe_b_group"][1], p["moe_w_router"][1],
               p["moe_b_router"][1], p["moe_w_gate"][1], p["moe_w_up"][1], p["moe_w_down"][1],
               p["norm_final"])
    y = y2d.reshape(n_seq, seq_len, D_MODEL)
    if want_v:
        v = v.reshape(1, n_seq, seq_len, SG_WIDTH)
    return y, conv_new[None], ssm_new[None], v


def kernel(x_prompt, x_sample, state_ssm, state_conv, norm_mix, norm_ffn, norm_final, ssd_w_in, ssd_conv_w, ssd_conv_b, ssd_dt_bias, ssd_a_log, ssd_d, ssd_norm, ssd_w_out, sg_w_in, sg_b_in, sg_ln_g, sg_ln_b, sg_w_s, sg_b_s, sg_w_out, moe_w_group, moe_b_group, moe_w_router, moe_b_router, moe_w_gate, moe_w_up, moe_w_down):
    p = dict(norm_mix=norm_mix, norm_ffn=norm_ffn, norm_final=norm_final, ssd_w_in=ssd_w_in,
             ssd_conv_w=ssd_conv_w, ssd_conv_b=ssd_conv_b, ssd_dt_bias=ssd_dt_bias,
             ssd_a_log=ssd_a_log, ssd_d=ssd_d, ssd_norm=ssd_norm, ssd_w_out=ssd_w_out,
             sg_w_in=sg_w_in, sg_b_in=sg_b_in, sg_ln_g=sg_ln_g, sg_ln_b=sg_ln_b, sg_w_s=sg_w_s,
             sg_b_s=sg_b_s, sg_w_out=sg_w_out, moe_w_group=moe_w_group, moe_b_group=moe_b_group,
             moe_w_router=moe_w_router, moe_b_router=moe_b_router, moe_w_gate=moe_w_gate,
             moe_w_up=moe_w_up, moe_w_down=moe_w_down)
    nb = x_prompt.shape[0]
    conv0 = jnp.zeros((1, nb, CONV_W - 1, CONV_DIM), F32)
    ssm0 = jnp.zeros((1, nb, N_HEADS, HEAD_DIM, D_STATE), F32)
    y_p, conv_p, ssm_p, _ = _trunk(x_prompt, conv0, ssm0, p, ssd_ns=1, want_v=False)
    y_s, conv_s, ssm_s, v_s = _trunk(x_sample, state_conv, state_ssm, p, ssd_ns=4, want_v=True)
    return (y_p, y_s, ssm_p, conv_p, ssm_s, conv_s, v_s)
```

```python
import functools
import math

import jax
import jax.numpy as jnp
import numpy as np
from jax import lax
from jax.experimental import pallas as pl
from jax.experimental.pallas import tpu as pltpu

F32 = jnp.float32
BF16 = jnp.bfloat16
I32 = jnp.int32

D_MODEL = 1024
N_HEADS = 32
HEAD_DIM = 64
N_GROUPS = 4
D_STATE = 128
D_INNER = N_HEADS * HEAD_DIM
GROUP_W = D_INNER // N_GROUPS
CONV_W = 4
CONV_DIM = D_INNER + 2 * N_GROUPS * D_STATE
DT_PAD = 128
SSD_CHUNK = 128
SG_WIDTH = 2 * D_MODEL
SG_GROUPS = 8
SG_GROUP_DIM = SG_WIDTH // SG_GROUPS
SG_CHUNK = 128
MOE_GROUPS = 4
MOE_EPG = 8
MOE_EXPERTS = MOE_GROUPS * MOE_EPG
MOE_D_FF = 256
NORM_EPS = 1e-6
LN_EPS = 1e-5

MOE_TILE = 256
EXPERT_BLOCK = 256
GRANULE = 8
LOCAL_ROWS = 2 * MOE_TILE + MOE_EXPERTS * GRANULE
N_LOCAL_GRAN = LOCAL_ROWS // GRANULE
VMEM_LIMIT = 56 * 1024 * 1024


def _sigmoid(x):
    return 1.0 / (1.0 + jnp.exp(-x))


def _silu(x):
    return x * _sigmoid(x)


def _softplus(x):
    return jnp.maximum(x, 0.0) + jnp.log(1.0 + jnp.exp(-jnp.abs(x)))


def _gelu_tanh(x):
    c = math.sqrt(2.0 / math.pi)
    return x * (0.5 * (1.0 + jnp.tanh(c * (x + 0.044715 * (x * x * x)))))


def _rms(x, g):
    return x * lax.rsqrt(jnp.mean(x * x, axis=-1, keepdims=True) + NORM_EPS) * g


def _split3(x):
    a = x.astype(BF16)
    r = x - a.astype(F32)
    b = r.astype(BF16)
    c = (r - b.astype(F32)).astype(BF16)
    return a, b, c


def _dot(a, b):
    return jnp.dot(a, b, preferred_element_type=F32)


def _dot_nt(a, b):
    return lax.dot_general(a, b, (((1,), (1,)), ((), ())), preferred_element_type=F32)


def _dot_tn(a, b):
    return lax.dot_general(a, b, (((0,), (0,)), ((), ())), preferred_element_type=F32)


def _const_spec(shape):
    nd = len(shape)
    return pl.BlockSpec(shape, lambda *_: (0,) * nd)


def _ssd_kernel(x_ref, g_ref, win_ref, cw_ref, cb_ref, dtb_ref, alog_ref, dsk_ref, ng_ref,
                wout_ref, tri_ref, ones_ref, cin_ref, sin_ref,
                xo_ref, cout_ref, sout_ref,
                cscr, yoff_scr, y_scr, xw_scr, *, ns, q):
    c = pl.program_id(1)
    r = ns * q
    seg = 8 + q

    x = x_ref[...]
    hn = _rms(x, g_ref[...]).astype(BF16)
    z = _dot(hn, win_ref[:, 0:D_INNER])
    xbc = _dot(hn, win_ref[:, D_INNER:D_INNER + CONV_DIM])
    dtr = _dot(hn, win_ref[:, D_INNER + CONV_DIM:D_INNER + CONV_DIM + DT_PAD])

    @pl.when(c == 0)
    def _():
        sout_ref[...] = sin_ref[...]
        for s in range(ns):
            for i in range(CONV_W - 1):
                cscr[pl.ds(s * seg + 5 + i, 1), :] = cin_ref[s, pl.ds(i, 1), :]

    for s in range(ns):
        cscr[pl.ds(s * seg + 8, q), :] = xbc[s * q:(s + 1) * q, :]
    parts = []
    for s in range(ns):
        acc = cb_ref[...]
        for k in range(CONV_W):
            acc = acc + cw_ref[pl.ds(k, 1), :] * cscr[pl.ds(s * seg + 5 + k, q), :]
        parts.append(acc)
    xc = parts[0] if ns == 1 else jnp.concatenate(parts, axis=0)
    xc = _silu(xc)
    for s in range(ns):
        for i in range(CONV_W - 1):
            row = cscr[pl.ds(s * seg + 5 + q + i, 1), :]
            cscr[pl.ds(s * seg + 5 + i, 1), :] = row
            cout_ref[s, pl.ds(i, 1), :] = row

    xs = xc[:, 0:D_INNER]
    bb = xc[:, D_INNER:D_INNER + N_GROUPS * D_STATE].astype(BF16)
    cc = xc[:, D_INNER + N_GROUPS * D_STATE:CONV_DIM].astype(BF16)

    dt = _softplus(dtr + dtb_ref[...])
    da = dt * (-jnp.exp(alog_ref[...]))
    tri = tri_ref[...]
    trib = tri.astype(BF16)
    onesb = ones_ref[...].astype(BF16)
    d1, d2, d3 = _split3(da)
    cs = _dot(trib, d1) + _dot(trib, d2) + _dot(trib, d3)
    cl = _dot(onesb, d1) + _dot(onesb, d2) + _dot(onesb, d3)
    ecs = jnp.exp(cs)
    wgt = dt * jnp.exp(cl - cs)
    ecl = jnp.exp(cl)
    cs_t = cs.T
    dt_t = dt.T

    rowseq = lax.shift_right_logical(lax.broadcasted_iota(I32, (r, 1), 0), int(math.log2(q)))
    lane = lax.broadcasted_iota(I32, (r, 128), 1)
    lo = lane < HEAD_DIM
    mask = tri > 0.5

    for g in range(N_GROUPS):
        cg = cc[:, g * D_STATE:(g + 1) * D_STATE]
        acc = None
        for s in range(ns):
            st = sout_ref[s, g * GROUP_W:(g + 1) * GROUP_W, :].astype(BF16)
            yo = _dot_nt(cg, st)
            if ns > 1:
                yo = jnp.where(rowseq == s, yo, 0.0)
            acc = yo if acc is None else acc + yo
        yoff_scr[:, g * GROUP_W:(g + 1) * GROUP_W] = acc

    for g in range(N_GROUPS):
        cg = cc[:, g * D_STATE:(g + 1) * D_STATE]
        bg = bb[:, g * D_STATE:(g + 1) * D_STATE]
        sc = _dot_nt(cg, bg)
        for jj in range(N_HEADS // N_GROUPS // 2):
            j = g * (N_HEADS // N_GROUPS // 2) + jj
            ms = []
            for h in (2 * j, 2 * j + 1):
                diff = cs[:, h:h + 1] - cs_t[h:h + 1, :]
                dec = jnp.exp(jnp.where(mask, diff, -jnp.inf))
                ms.append((sc * dec * dt_t[h:h + 1, :]).astype(BF16))
            lhs = jnp.concatenate(ms, axis=1)
            xp = xs[:, j * 128:(j + 1) * 128]
            rhs = jnp.concatenate([jnp.where(lo, xp, 0.0).astype(BF16),
                                   jnp.where(lo, 0.0, xp).astype(BF16)], axis=0)
            yd = _dot(lhs, rhs)
            ecs_p = jnp.where(lo, ecs[:, 2 * j:2 * j + 1], ecs[:, 2 * j + 1:2 * j + 2])
            wgt_p = jnp.where(lo, wgt[:, 2 * j:2 * j + 1], wgt[:, 2 * j + 1:2 * j + 2])
            cols = slice(j * 128, (j + 1) * 128)
            y_scr[:, cols] = yd + yoff_scr[:, cols] * ecs_p + xp * dsk_ref[:, cols]
            xw_scr[:, cols] = (xp * wgt_p).astype(BF16)

    bf = xc[:, D_INNER:D_INNER + N_GROUPS * D_STATE]
    for g in range(N_GROUPS):
        xwg = xw_scr[:, g * GROUP_W:(g + 1) * GROUP_W]
        for s in range(ns):
            bg = bf[:, g * D_STATE:(g + 1) * D_STATE]
            if ns > 1:
                bg = jnp.where(rowseq == s, bg, 0.0)
            upd = _dot_tn(xwg, bg.astype(BF16))
            dec = jnp.concatenate(
                [jnp.broadcast_to(ecl[s * q:s * q + 1, h:h + 1], (HEAD_DIM, D_STATE))
                 for h in range(g * 8, g * 8 + 8)], axis=0)
            rows = slice(g * GROUP_W, (g + 1) * GROUP_W)
            sout_ref[s, rows, :] = sout_ref[s, rows, :] * dec + upd

    y = y_scr[...] * _silu(z)
    parts = []
    for g in range(N_GROUPS):
        yg = y[:, g * GROUP_W:(g + 1) * GROUP_W]
        parts.append(yg * lax.rsqrt(jnp.mean(yg * yg, axis=-1, keepdims=True) + NORM_EPS))
    yn = jnp.concatenate(parts, axis=1) * ng_ref[...]
    xo_ref[...] = x + _dot(yn.astype(BF16), wout_ref[...])


def _ssd_layer(x2d, conv_prev, ssm_prev, norm_g, w_in, conv_w, conv_b, dt_bias, a_log, d_skip,
               norm_y, w_out, *, n_seq, seq_len, ns, q):
    r = ns * q
    n_chunks = seq_len // q
    n_sb = n_seq // ns
    pad = DT_PAD - N_HEADS
    win = jnp.concatenate([w_in, jnp.zeros((D_MODEL, pad), F32)], axis=1).astype(BF16)
    dtb = jnp.pad(dt_bias, (0, pad)).reshape(1, DT_PAD)
    alog = jnp.pad(a_log, (0, pad)).reshape(1, DT_PAD)
    dsk = jnp.repeat(d_skip, HEAD_DIM).reshape(1, D_INNER)
    blk = np.kron(np.eye(ns), np.ones((q, q)))
    tri = jnp.asarray(blk * np.tril(np.ones((r, r))), F32)
    ones = jnp.asarray(blk, F32)
    state = ssm_prev.reshape(n_seq, D_INNER, D_STATE)

    kern = functools.partial(_ssd_kernel, ns=ns, q=q)
    out_shape = (jax.ShapeDtypeStruct(x2d.shape, F32),
                 jax.ShapeDtypeStruct((n_seq, CONV_W - 1, CONV_DIM), F32),
                 jax.ShapeDtypeStruct((n_seq, D_INNER, D_STATE), F32))
    row_spec = pl.BlockSpec((r, D_MODEL), lambda sb, c: (sb * n_chunks + c, 0))
    conv_spec = pl.BlockSpec((ns, CONV_W - 1, CONV_DIM), lambda sb, c: (sb, 0, 0))
    state_spec = pl.BlockSpec((ns, D_INNER, D_STATE), lambda sb, c: (sb, 0, 0))
    in_specs = [row_spec,
                _const_spec((1, D_MODEL)),
                _const_spec(win.shape),
                _const_spec((CONV_W, CONV_DIM)),
                _const_spec((1, CONV_DIM)),
                _const_spec((1, DT_PAD)),
                _const_spec((1, DT_PAD)),
                _const_spec((1, D_INNER)),
                _const_spec((1, D_INNER)),
                _const_spec((D_INNER, D_MODEL)),
                _const_spec((r, r)),
                _const_spec((r, r)),
                conv_spec, state_spec]
    x_new, conv_new, state_new = pl.pallas_call(
        kern,
        grid=(n_sb, n_chunks),
        in_specs=in_specs,
        out_specs=(row_spec, conv_spec, state_spec),
        out_shape=out_shape,
        scratch_shapes=[pltpu.VMEM((ns * (8 + q), CONV_DIM), F32),
                        pltpu.VMEM((r, D_INNER), F32),
                        pltpu.VMEM((r, D_INNER), F32),
                        pltpu.VMEM((r, D_INNER), BF16)],
        compiler_params=pltpu.CompilerParams(
            dimension_semantics=("arbitrary", "arbitrary"), vmem_limit_bytes=VMEM_LIMIT),
        name="ssd_layer",
    )(x2d, norm_g.reshape(1, D_MODEL), win, conv_w, conv_b.reshape(1, CONV_DIM), dtb, alog, dsk,
      norm_y.reshape(1, D_INNER), w_out.astype(BF16), tri, ones, conv_prev, state)
    return x_new, conv_new, state_new.reshape(n_seq, N_HEADS, HEAD_DIM, D_STATE)


def _sg_kernel(x_ref, g_ref, win_ref, bin_ref, lng_ref, lnb_ref, wmix_ref, bmix_ref, wout_ref,
               xo_ref, *v_out, r):
    x = x_ref[...]
    hn = _rms(x, g_ref[...]).astype(BF16)
    u = _gelu_tanh(_dot(hn, win_ref[:, 0:SG_WIDTH]) + bin_ref[:, 0:SG_WIDTH])
    v = _gelu_tanh(_dot(hn, win_ref[:, SG_WIDTH:2 * SG_WIDTH]) + bin_ref[:, SG_WIDTH:2 * SG_WIDTH])
    mu = jnp.mean(v, axis=-1, keepdims=True)
    vc = v - mu
    var = jnp.mean(vc * vc, axis=-1, keepdims=True)
    vn = vc * lax.rsqrt(var + LN_EPS) * lng_ref[...] + lnb_ref[...]
    if v_out:
        v_out[0][...] = vn
    vb = vn.astype(BF16)
    parts = []
    for g in range(SG_GROUPS):
        cols = slice(g * SG_GROUP_DIM, (g + 1) * SG_GROUP_DIM)
        parts.append(_dot(wmix_ref[g], vb[:, cols]) + bmix_ref[:, cols])
    mixed = jnp.concatenate(parts, axis=1)
    xo_ref[...] = x + _dot((u * mixed).astype(BF16), wout_ref[...])


def _sg_layer(x2d, norm_g, w_in, b_in, ln_g, ln_b, w_s, b_s, w_out, *, seq_len, want_v):
    r = SG_CHUNK
    t = x2d.shape[0]
    q = min(seq_len, SG_CHUNK)
    reps = r // q
    ws = jnp.tril(w_s)[:, :q, :q]
    wmix = jnp.einsum("ab,gts->gatbs", jnp.eye(reps, dtype=F32), ws).reshape(SG_GROUPS, r, r)
    bmix = jnp.tile(jnp.repeat(b_s.T[:q], SG_GROUP_DIM, axis=1), (reps, 1))
    row_spec = pl.BlockSpec((r, D_MODEL), lambda i: (i, 0))
    v_spec = pl.BlockSpec((r, SG_WIDTH), lambda i: (i, 0))
    out_shape = [jax.ShapeDtypeStruct(x2d.shape, F32)]
    out_specs = [row_spec]
    if want_v:
        out_shape.append(jax.ShapeDtypeStruct((t, SG_WIDTH), F32))
        out_specs.append(v_spec)
    outs = pl.pallas_call(
        functools.partial(_sg_kernel, r=r),
        grid=(t // r,),
        in_specs=[row_spec,
                  _const_spec((1, D_MODEL)),
                  _const_spec((D_MODEL, 2 * SG_WIDTH)),
                  _const_spec((1, 2 * SG_WIDTH)),
                  _const_spec((1, SG_WIDTH)),
                  _const_spec((1, SG_WIDTH)),
                  _const_spec((SG_GROUPS, r, r)),
                  _const_spec((r, SG_WIDTH)),
                  _const_spec((SG_WIDTH, D_MODEL))],
        out_specs=out_specs,
        out_shape=out_shape,
        compiler_params=pltpu.CompilerParams(
            dimension_semantics=("arbitrary",), vmem_limit_bytes=VMEM_LIMIT),
        name="sg_layer",
    )(x2d, norm_g.reshape(1, D_MODEL), w_in.astype(BF16), b_in.reshape(1, 2 * SG_WIDTH),
      ln_g.reshape(1, SG_WIDTH), ln_b.reshape(1, SG_WIDTH), wmix.astype(BF16), bmix,
      w_out.astype(BF16))
    return outs if want_v else (outs[0], None)


_LANE_E0 = MOE_GROUPS


def _router_kernel(x_ref, g_ref, wr_ref, br_ref, ltri_ref, utri_ref, meta_ref, cnt_ref):
    tt = x_ref.shape[0]
    hn = _rms(x_ref[...], g_ref[...])
    h1 = hn.astype(BF16)
    h2 = (hn - h1.astype(F32)).astype(BF16)
    wr = wr_ref[...]
    w1 = wr.astype(BF16)
    w2 = (wr - w1.astype(F32)).astype(BF16)
    logits = _dot(h1, w1) + _dot(h1, w2) + _dot(h2, w1) + br_ref[...]

    lane = lax.broadcasted_iota(I32, (tt, 128), 1).astype(F32)
    neg = -jnp.inf
    gl = jnp.where(lane < MOE_GROUPS, logits, neg)
    gmax = jnp.max(gl, axis=-1, keepdims=True)
    g_top = jnp.min(jnp.where(gl == gmax, lane, 128.0), axis=-1, keepdims=True)
    p_g = 1.0 / jnp.sum(jnp.exp(gl - gmax), axis=-1, keepdims=True)

    first = _LANE_E0 + MOE_EPG * g_top
    in_grp = (lane >= first) & (lane < first + MOE_EPG)
    el = jnp.where(in_grp, logits, neg)
    emax = jnp.max(el, axis=-1, keepdims=True)
    ee = jnp.exp(el - emax)
    prob = jnp.where(in_grp, ee / jnp.sum(ee, axis=-1, keepdims=True), -1.0)
    p1 = jnp.max(prob, axis=-1, keepdims=True)
    i1 = jnp.min(jnp.where(prob == p1, lane, 128.0), axis=-1, keepdims=True)
    prob2 = jnp.where(lane == i1, -1.0, prob)
    p2 = jnp.max(prob2, axis=-1, keepdims=True)
    i2 = jnp.min(jnp.where(prob2 == p2, lane, 128.0), axis=-1, keepdims=True)
    psum = p1 + p2
    gate1 = p_g * (p1 / psum)
    gate2 = p_g * (p2 / psum)

    sel1 = lane == i1
    sel2 = lane == i2
    onehot = jnp.where(sel1 | sel2, 1.0, 0.0)
    before = _dot(ltri_ref[...], onehot.astype(BF16))
    cnt = jnp.sum(onehot, axis=0, keepdims=True)
    pad = jnp.floor((cnt + (GRANULE - 1)) * (1.0 / GRANULE)) * GRANULE
    pad8 = jnp.broadcast_to(pad, (8, 128))
    seg_off = _dot(pad8.astype(BF16), utri_ref[...])
    local = before + seg_off[0:1, :]
    rank1 = jnp.sum(jnp.where(sel1, local, 0.0), axis=-1, keepdims=True)
    rank2 = jnp.sum(jnp.where(sel2, local, 0.0), axis=-1, keepdims=True)
    cnt_ref[...] = pad8

    e1 = i1 - _LANE_E0
    e2 = i2 - _LANE_E0
    meta = jnp.where(lane == 0, e1, 0.0)
    meta = jnp.where(lane == 1, e2, meta)
    meta = jnp.where(lane == 2, gate1, meta)
    meta = jnp.where(lane == 3, gate2, meta)
    meta = jnp.where(lane == 4, rank1, meta)
    meta = jnp.where(lane == 5, rank2, meta)
    meta_ref[...] = meta


def _granule(ref, g):
    return ref.at[pl.ds(pl.multiple_of(g * GRANULE, GRANULE), GRANULE), :]


def _dispatch_kernel(ngran_ref, nv_ref, dest_ref, gap_ref, x_ref, g_ref, meta_ref, xs_hbm,
                     buf, zbuf, sem, zsem, nstart):
    i = pl.program_id(0)
    n_tiles = pl.num_programs(0) - 1
    slot = i % 2
    tt = x_ref.shape[0]
    bm = zbuf.shape[0]
    n_blocks = xs_hbm.shape[0] // bm

    def out_copy(sl, g, d):
        return pltpu.make_async_copy(_granule(buf.at[sl], g), _granule(xs_hbm, d), sem.at[sl])

    def drain(sl):
        def body(_, carry):
            out_copy(sl, 0, 0).wait()
            return carry
        lax.fori_loop(0, nstart[sl], body, 0)

    @pl.when(i == 0)
    def _():
        nstart[0] = 0
        nstart[1] = 0

    @pl.when(i < n_tiles)
    def _():
        drain(slot)
        hn = _rms(x_ref[...], g_ref[...]).astype(BF16)
        mt = meta_ref[...].T
        rows = lax.broadcasted_iota(I32, (LOCAL_ROWS, tt), 0).astype(F32)
        onehot = jnp.where((rows == mt[4:5, :]) | (rows == mt[5:6, :]), 1.0, 0.0).astype(BF16)
        buf[slot] = _dot(onehot, hn)
        ng = ngran_ref[i]

        def body(g, carry):
            out_copy(slot, g, dest_ref[0, 0, g]).start()
            return carry
        lax.fori_loop(0, ng, body, 0)
        nstart[slot] = ng

    @pl.when(i == n_tiles)
    def _():
        drain(0)
        drain(1)
        zbuf[...] = jnp.zeros_like(zbuf)

        def gap_copy(d):
            return pltpu.make_async_copy(_granule(zbuf, 0), _granule(xs_hbm, d), zsem.at[0])

        def tail_copy(b):
            return pltpu.make_async_copy(
                zbuf, xs_hbm.at[pl.ds(pl.multiple_of(b * bm, bm), bm), :], zsem.at[1])

        def each_gap(fn):
            def body(j, carry):
                d = gap_ref[0, 0, j]

                @pl.when(d >= 0)
                def _():
                    fn(gap_copy(d))
                return carry
            lax.fori_loop(0, gap_ref.shape[2], body, 0)

        def each_tail(fn):
            def body(b, carry):
                fn(tail_copy(b))
                return carry
            lax.fori_loop(nv_ref[0], n_blocks, body, 0)

        each_gap(lambda cp: cp.start())
        each_tail(lambda cp: cp.start())
        each_gap(lambda cp: cp.wait())
        each_tail(lambda cp: cp.wait())


def _expert_kernel(bexp_ref, nv_ref, xs_ref, wg_ref, wu_ref, wd_ref, ys_ref):
    @pl.when(pl.program_id(0) < nv_ref[0])
    def _():
        xg = xs_ref[...].astype(BF16)
        hg = _dot(xg, wg_ref[...].astype(BF16))
        hu = _dot(xg, wu_ref[...].astype(BF16))
        act = (_silu(hg) * hu).astype(BF16)
        ys_ref[...] = _dot(act, wd_ref[...].astype(BF16))


def _combine_kernel(ngran_ref, src_ref, srcn_ref, x_ref, meta_ref, gf_ref, ys_hbm, o_ref, buf, sem,
                    *, final):
    i = pl.program_id(0)
    n = pl.num_programs(0)
    slot = i % 2
    tt = x_ref.shape[0]

    def in_copy(sl, g, d):
        return pltpu.make_async_copy(_granule(ys_hbm, d), _granule(buf.at[sl], g), sem.at[sl])

    def gather(idx_ref, sl, ng):
        def body(g, carry):
            in_copy(sl, g, idx_ref[0, 0, g]).start()
            return carry
        lax.fori_loop(0, ng, body, 0)

    @pl.when(i == 0)
    def _():
        buf[...] = jnp.zeros_like(buf)
        gather(src_ref, 0, ngran_ref[0])

    @pl.when(i + 1 < n)
    def _():
        gather(srcn_ref, 1 - slot, ngran_ref[i + 1])

    def wait_body(_, carry):
        in_copy(slot, 0, 0).wait()
        return carry
    lax.fori_loop(0, ngran_ref[i], wait_body, 0)

    ys = buf[slot].astype(BF16)
    meta = meta_ref[...]
    cols = lax.broadcasted_iota(I32, (tt, LOCAL_ROWS), 1).astype(F32)
    pick1 = jnp.where(cols == meta[:, 4:5], 1.0, 0.0).astype(BF16)
    pick2 = jnp.where(cols == meta[:, 5:6], 1.0, 0.0).astype(BF16)
    y = x_ref[...] + meta[:, 2:3] * _dot(pick1, ys) + meta[:, 3:4] * _dot(pick2, ys)
    if final:
        y = _rms(y, gf_ref[...])
    o_ref[...] = y


def _moe(x2d, norm_g, w_grp, b_grp, w_rt, b_rt, w_gate, w_up, w_down, norm_final):
    t = x2d.shape[0]
    tt = MOE_TILE
    bm = EXPERT_BLOCK
    n_tiles = t // tt
    n_exp = MOE_EXPERTS
    lane_pad = 128 - MOE_GROUPS - n_exp
    wr = jnp.concatenate([w_grp, w_rt, jnp.zeros((D_MODEL, lane_pad), F32)], axis=1)
    br = jnp.concatenate([b_grp, b_rt, jnp.zeros((lane_pad,), F32)]).reshape(1, 128)
    ltri = jnp.asarray(np.tril(np.ones((tt, tt)), -1), BF16)
    utri = jnp.asarray(np.triu(np.ones((128, 128)), 1), BF16)
    g2d = norm_g.reshape(1, D_MODEL)
    arb = pltpu.CompilerParams(dimension_semantics=("arbitrary",), vmem_limit_bytes=VMEM_LIMIT)

    meta, cnt = pl.pallas_call(
        _router_kernel,
        grid=(n_tiles,),
        in_specs=[pl.BlockSpec((tt, D_MODEL), lambda i: (i, 0)),
                  _const_spec((1, D_MODEL)),
                  _const_spec((D_MODEL, 128)),
                  _const_spec((1, 128)),
                  _const_spec((tt, tt)),
                  _const_spec((128, 128))],
        out_specs=(pl.BlockSpec((tt, 128), lambda i: (i, 0)),
                   pl.BlockSpec((8, 128), lambda i: (i, 0))),
        out_shape=(jax.ShapeDtypeStruct((t, 128), F32),
                   jax.ShapeDtypeStruct((n_tiles * 8, 128), F32)),
        compiler_params=arb,
        name="moe_router",
    )(x2d, g2d, wr, br, ltri, utri)

    runs = cnt.reshape(n_tiles, 8, 128)[:, 0, _LANE_E0:_LANE_E0 + n_exp].astype(I32)
    rows_e = jnp.sum(runs, axis=0)
    nblk = (rows_e + bm - 1) // bm
    blk_end = jnp.cumsum(nblk)
    e_start = (blk_end - nblk) * bm
    n_valid = blk_end[-1]
    run_end = jnp.cumsum(runs, axis=1)
    ngran = (run_end[:, -1] // GRANULE).astype(I32)
    shift = e_start[None, :] + (jnp.cumsum(runs, axis=0) - runs) - (run_end - runs)
    g_row = jnp.arange(N_LOCAL_GRAN, dtype=I32) * GRANULE
    e_of_g = jnp.sum((run_end[:, None, :] <= g_row[None, :, None]).astype(I32), axis=-1)
    shift_g = jnp.sum(jnp.where(e_of_g[..., None] == jnp.arange(n_exp, dtype=I32),
                                shift[:, None, :], 0), axis=-1)
    dest = jnp.where(e_of_g < n_exp, (shift_g + g_row[None, :]) // GRANULE, 0)
    dest = dest.astype(I32).reshape(n_tiles, 1, N_LOCAL_GRAN)
    per_blk = bm // GRANULE
    gap = ((e_start + rows_e) // GRANULE)[:, None] + jnp.arange(per_blk, dtype=I32)[None, :]
    gap = jnp.where(gap < ((e_start + nblk * bm) // GRANULE)[:, None], gap, -1)
    gap = gap.astype(I32).reshape(1, 1, n_exp * per_blk)
    n_blocks = (2 * t + n_tiles * n_exp * (GRANULE - 1)) // bm + 1 + n_exp
    blk = jnp.minimum(jnp.arange(n_blocks, dtype=I32), n_valid - 1)
    block_expert = jnp.sum((blk[:, None] >= blk_end[None, :]).astype(I32), axis=1).astype(I32)
    n_valid = n_valid.reshape(1).astype(I32)

    last = n_tiles - 1
    xs = pl.pallas_call(
        _dispatch_kernel,
        grid_spec=pltpu.PrefetchScalarGridSpec(
            num_scalar_prefetch=2,
            grid=(n_tiles + 1,),
            in_specs=[pl.BlockSpec((1, 1, N_LOCAL_GRAN),
                                   lambda i, ng, nv: (jnp.minimum(i, last), 0, 0),
                                   memory_space=pltpu.SMEM),
                      pl.BlockSpec((1, 1, n_exp * per_blk), lambda i, ng, nv: (0, 0, 0),
                                   memory_space=pltpu.SMEM),
                      pl.BlockSpec((tt, D_MODEL), lambda i, ng, nv: (jnp.minimum(i, last), 0)),
                      _const_spec((1, D_MODEL)),
                      pl.BlockSpec((tt, 128), lambda i, ng, nv: (jnp.minimum(i, last), 0))],
            out_specs=pl.BlockSpec(memory_space=pl.ANY),
            scratch_shapes=[pltpu.VMEM((2, LOCAL_ROWS, D_MODEL), F32),
                            pltpu.VMEM((bm, D_MODEL), F32),
                            pltpu.SemaphoreType.DMA((2,)),
                            pltpu.SemaphoreType.DMA((2,)),
                            pltpu.SMEM((2,), I32)]),
        out_shape=jax.ShapeDtypeStruct((n_blocks * bm, D_MODEL), F32),
        compiler_params=arb,
        name="moe_dispatch",
    )(ngran, n_valid, dest, gap, x2d, g2d, meta)

    w_spec = lambda shape: pl.BlockSpec((None,) + shape, lambda b, be, nv: (be[b], 0, 0))
    row_blk = pl.BlockSpec((bm, D_MODEL), lambda b, be, nv: (jnp.minimum(b, nv[0] - 1), 0))
    ys = pl.pallas_call(
        _expert_kernel,
        grid_spec=pltpu.PrefetchScalarGridSpec(
            num_scalar_prefetch=2,
            grid=(n_blocks,),
            in_specs=[row_blk,
                      w_spec((D_MODEL, MOE_D_FF)),
                      w_spec((D_MODEL, MOE_D_FF)),
                      w_spec((MOE_D_FF, D_MODEL))],
            out_specs=row_blk),
        out_shape=jax.ShapeDtypeStruct((n_blocks * bm, D_MODEL), F32),
        input_output_aliases={2: 0},
        compiler_params=arb,
        name="moe_experts",
    )(block_expert, n_valid, xs, w_gate, w_up, w_down)

    final = norm_final is not None
    gf = (norm_final if final else jnp.ones((D_MODEL,), F32)).reshape(1, D_MODEL)
    src_spec = lambda nxt: pl.BlockSpec(
        (1, 1, N_LOCAL_GRAN), lambda i, ng: (jnp.minimum(i + nxt, last), 0, 0),
        memory_space=pltpu.SMEM)
    out = pl.pallas_call(
        functools.partial(_combine_kernel, final=final),
        grid_spec=pltpu.PrefetchScalarGridSpec(
            num_scalar_prefetch=1,
            grid=(n_tiles,),
            in_specs=[src_spec(0), src_spec(1),
                      pl.BlockSpec((tt, D_MODEL), lambda i, ng: (i, 0)),
                      pl.BlockSpec((tt, 128), lambda i, ng: (i, 0)),
                      _const_spec((1, D_MODEL)),
                      pl.BlockSpec(memory_space=pl.ANY)],
            out_specs=pl.BlockSpec((tt, D_MODEL), lambda i, ng: (i, 0)),
            scratch_shapes=[pltpu.VMEM((2, LOCAL_ROWS, D_MODEL), F32),
                            pltpu.SemaphoreType.DMA((2,))]),
        out_shape=jax.ShapeDtypeStruct((t, D_MODEL), F32),
        compiler_params=arb,
        name="moe_combine",
    )(ngran, dest, dest, x2d, meta, gf, ys)
    return out


def _trunk(x, conv_prev, ssm_prev, p, *, ssd_ns, want_v):
    n_seq, seq_len, _ = x.shape
    q = seq_len if seq_len < SSD_CHUNK else SSD_CHUNK
    x2d = x.reshape(n_seq * seq_len, D_MODEL)
    x2d, conv_new, ssm_new = _ssd_layer(
        x2d, conv_prev[0], ssm_prev[0], p["norm_mix"][0], p["ssd_w_in"][0], p["ssd_conv_w"][0],
        p["ssd_conv_b"][0], p["ssd_dt_bias"][0], p["ssd_a_log"][0], p["ssd_d"][0], p["ssd_norm"][0],
        p["ssd_w_out"][0], n_seq=n_seq, seq_len=seq_len, ns=ssd_ns, q=q)
    x2d = _moe(x2d, p["norm_ffn"][0], p["moe_w_group"][0], p["moe_b_group"][0], p["moe_w_router"][0],
               p["moe_b_router"][0], p["moe_w_gate"][0], p["moe_w_up"][0], p["moe_w_down"][0], None)
    x2d, v = _sg_layer(x2d, p["norm_mix"][1], p["sg_w_in"][0], p["sg_b_in"][0], p["sg_ln_g"][0],
                       p["sg_ln_b"][0], p["sg_w_s"][0], p["sg_b_s"][0], p["sg_w_out"][0],
                       seq_len=seq_len, want_v=want_v)
    y2d = _moe(x2d, p["norm_ffn"][1], p["moe_w_group"][1], p["moe_b_group"][1], p["moe_w_router"][1],
               p["moe_b_router"][1], p["moe_w_gate"][1], p["moe_w_up"][1], p["moe_w_down"][1],
               p["norm_final"])
    y = y2d.reshape(n_seq, seq_len, D_MODEL)
    if want_v:
        v = v.reshape(1, n_seq, seq_len, SG_WIDTH)
    return y, conv_new[None], ssm_new[None], v


def kernel(x_prompt, x_sample, state_ssm, state_conv, norm_mix, norm_ffn, norm_final, ssd_w_in, ssd_conv_w, ssd_conv_b, ssd_dt_bias, ssd_a_log, ssd_d, ssd_norm, ssd_w_out, sg_w_in, sg_b_in, sg_ln_g, sg_ln_b, sg_w_s, sg_b_s, sg_w_out, moe_w_group, moe_b_group, moe_w_router, moe_b_router, moe_w_gate, moe_w_up, moe_w_down):
    p = dict(norm_mix=norm_mix, norm_ffn=norm_ffn, norm_final=norm_final, ssd_w_in=ssd_w_in,
             ssd_conv_w=ssd_conv_w, ssd_conv_b=ssd_conv_b, ssd_dt_bias=ssd_dt_bias,
             ssd_a_log=ssd_a_log, ssd_d=ssd_d, ssd_norm=ssd_norm, ssd_w_out=ssd_w_out,
             sg_w_in=sg_w_in, sg_b_in=sg_b_in, sg_ln_g=sg_ln_g, sg_ln_b=sg_ln_b, sg_w_s=sg_w_s,
             sg_b_s=sg_b_s, sg_w_out=sg_w_out, moe_w_group=moe_w_group, moe_b_group=moe_b_group,
             moe_w_router=moe_w_router, moe_b_router=moe_b_router, moe_w_gate=moe_w_gate,
             moe_w_up=moe_w_up, moe_w_down=moe_w_down)
    nb = x_prompt.shape[0]
    conv0 = jnp.zeros((1, nb, CONV_W - 1, CONV_DIM), F32)
    ssm0 = jnp.zeros((1, nb, N_HEADS, HEAD_DIM, D_STATE), F32)
    y_p, conv_p, ssm_p, _ = _trunk(x_prompt, conv0, ssm0, p, ssd_ns=1, want_v=False)
    y_s, conv_s, ssm_s, v_s = _trunk(x_sample, state_conv, state_ssm, p, ssd_ns=4, want_v=True)
    return (y_p, y_s, ssm_p, conv_p, ssm_s, conv_s, v_s)
```

```python
import functools
import math

import jax
import jax.numpy as jnp
import numpy as np
from jax import lax
from jax.experimental import pallas as pl
from jax.experimental.pallas import tpu as pltpu

F32 = jnp.float32
BF16 = jnp.bfloat16
I32 = jnp.int32

D_MODEL = 1024
N_HEADS = 32
HEAD_DIM = 64
N_GROUPS = 4
D_STATE = 128
D_INNER = N_HEADS * HEAD_DIM
GROUP_W = D_INNER // N_GROUPS
CONV_W = 4
CONV_DIM = D_INNER + 2 * N_GROUPS * D_STATE
DT_PAD = 128
SSD_CHUNK = 128
PROJ_CHUNK = 256
SG_WIDTH = 2 * D_MODEL
SG_GROUPS = 8
SG_GROUP_DIM = SG_WIDTH // SG_GROUPS
SG_CHUNK = 128
SG_ROWS = 256
MOE_GROUPS = 4
MOE_EPG = 8
MOE_EXPERTS = MOE_GROUPS * MOE_EPG
MOE_D_FF = 256
NORM_EPS = 1e-6
LN_EPS = 1e-5

MOE_TILE = 256
EXPERT_BLOCK = 256
GRANULE = 8
LOCAL_ROWS = 2 * MOE_TILE + MOE_EXPERTS * GRANULE
N_LOCAL_GRAN = LOCAL_ROWS // GRANULE
VMEM_LIMIT = 56 * 1024 * 1024


def _sigmoid(x):
    return 0.5 * (jnp.tanh(0.5 * x) + 1.0)


def _silu(x):
    return x * _sigmoid(x)


def _softplus(x):
    return jnp.maximum(x, 0.0) + jnp.log(1.0 + jnp.exp(-jnp.abs(x)))


def _gelu_tanh(x):
    c = math.sqrt(2.0 / math.pi)
    return x * (0.5 * (1.0 + jnp.tanh(c * (x + 0.044715 * (x * x * x)))))


def _rms(x, g):
    return x * lax.rsqrt(jnp.mean(x * x, axis=-1, keepdims=True) + NORM_EPS) * g


def _split3(x):
    a = x.astype(BF16)
    r = x - a.astype(F32)
    b = r.astype(BF16)
    c = (r - b.astype(F32)).astype(BF16)
    return a, b, c


def _dot(a, b):
    return jnp.dot(a, b, preferred_element_type=F32)


def _dot_nt(a, b):
    return lax.dot_general(a, b, (((1,), (1,)), ((), ())), preferred_element_type=F32)


def _dot_tn(a, b):
    return lax.dot_general(a, b, (((0,), (0,)), ((), ())), preferred_element_type=F32)


def _const_spec(shape):
    nd = len(shape)
    return pl.BlockSpec(shape, lambda *_: (0,) * nd)


def _ssd_kernel(x_ref, g_ref, win_ref, cw_ref, cb_ref, dtb_ref, alog_ref, dsk_ref, ng_ref,
                wout_ref, tri_ref, ones_ref, cin_ref, sin_ref,
                xo_ref, cout_ref, sout_ref,
                pend_x, pend_z, pend_xbc, pend_dt, xres_scr, zg_scr,
                prev_scr, xc_scr, yoff_scr, y_scr, xw_scr, hn_scr, yn_scr,
                *, nsub, ns, q, n_chunks):
    i = pl.program_id(0)
    first_chunk = lax.rem(jnp.maximum(i - 1, 0), n_chunks) == 0
    nseq = nsub * ns
    sb = ns * q
    r = nsub * sb

    @pl.when(i == 0)
    def _():
        pend_x[...] = jnp.zeros_like(pend_x)
        pend_z[...] = jnp.zeros_like(pend_z)
        pend_xbc[...] = jnp.zeros_like(pend_xbc)
        pend_dt[...] = jnp.zeros_like(pend_dt)

    tr = min(r, 128)
    row_tiles = [slice(a, a + tr) for a in range(0, r, tr)]

    def col_tiles(total, rows=tr):
        w = min(total, max(128, (32 * 1024) // rows))
        return [slice(c, c + w) for c in range(0, total, w)]

    for rt in row_tiles:
        for ct in col_tiles(D_MODEL):
            xres_scr[rt, ct] = pend_x[rt, ct]
        for ct in col_tiles(D_INNER):
            zg_scr[rt, ct] = _silu(pend_z[rt, ct])

    for s in range(nseq):
        for ct in col_tiles(D_MODEL, q):
            pend_x[s * q:(s + 1) * q, ct] = x_ref[s, :, ct]
    for rt in row_tiles:
        ss = None
        for ct in col_tiles(D_MODEL):
            xv = pend_x[rt, ct]
            part = jnp.sum(xv * xv, axis=-1, keepdims=True)
            ss = part if ss is None else ss + part
        scale = lax.rsqrt(ss * (1.0 / D_MODEL) + NORM_EPS)
        for ct in col_tiles(D_MODEL):
            hn_scr[rt, ct] = (pend_x[rt, ct] * scale * g_ref[:, ct]).astype(BF16)

    def proj_chunk(dst, dst_col, w_col, width):
        def run():
            dst[:, dst_col:dst_col + width] = _dot(hn_scr[...], win_ref[:, w_col:w_col + width])
        return run
    z_chunks = [proj_chunk(pend_z, c, c, PROJ_CHUNK) for c in range(0, D_INNER, PROJ_CHUNK)]
    scan_chunks = [proj_chunk(pend_xbc, c, D_INNER + c, PROJ_CHUNK)
                   for c in range(0, CONV_DIM, PROJ_CHUNK)]
    scan_chunks.append(proj_chunk(pend_dt, 0, D_INNER + CONV_DIM, DT_PAD))

    @pl.when(first_chunk)
    def _():
        sout_ref[...] = sin_ref[...]
        prev_scr[...] = jnp.zeros_like(prev_scr)
        for s in range(nseq):
            for k in range(CONV_W - 1):
                prev_scr[pl.ds(s * 8 + 5 + k, 1), :] = cin_ref[s, pl.ds(k, 1), :]

    sub8 = lax.broadcasted_iota(I32, (8, 1), 0)
    for s in range(nseq):
        srows = slice(s * q, (s + 1) * q)
        for ct in col_tiles(CONV_DIM, 4 * q):
            xq = pend_xbc[srows, ct]
            hist = prev_scr[s * 8:(s + 1) * 8, ct]
            acc = cb_ref[:, ct] + cw_ref[pl.ds(CONV_W - 1, 1), ct] * xq
            for j in range(1, CONV_W):
                sh = pltpu.roll(xq, j, 0)
                head = jnp.where(sub8 < j, pltpu.roll(hist, j, 0), sh[0:8, :])
                sh = head if q == 8 else jnp.concatenate([head, sh[8:, :]], axis=0)
                acc = acc + cw_ref[pl.ds(CONV_W - 1 - j, 1), ct] * sh
            xc_scr[srows, ct] = _silu(acc)
            last = xq[q - 8:q, :]
            prev_scr[s * 8:(s + 1) * 8, ct] = last
            for k in range(CONV_W - 1):
                cout_ref[s, pl.ds(k, 1), ct] = last[5 + k:6 + k, :]
        for _ in range(-(-len(z_chunks) // (nseq - s))):
            z_chunks.pop(0)()
    n_pairs = nsub * (N_HEADS // 2)
    stride = n_pairs // len(scan_chunks)
    emit_at = {k * stride: ch for k, ch in enumerate(scan_chunks)}

    tri = tri_ref[...]
    trib = tri.astype(BF16)
    onesb = ones_ref[...].astype(BF16)
    mask = tri > 0.5
    rowseq = lax.shift_right_logical(lax.broadcasted_iota(I32, (sb, 1), 0), int(math.log2(q)))
    lo = lax.broadcasted_iota(I32, (sb, 128), 1) < HEAD_DIM
    neg_a = -jnp.exp(alog_ref[...])

    for u in range(nsub):
        rows = slice(u * sb, (u + 1) * sb)
        xs_ref = xc_scr.at[rows, 0:D_INNER]

        def b_of(g):
            return xc_scr[rows, D_INNER + g * D_STATE:D_INNER + (g + 1) * D_STATE]

        def c_of(g):
            c0 = D_INNER + (N_GROUPS + g) * D_STATE
            return xc_scr[rows, c0:c0 + D_STATE].astype(BF16)

        dt = _softplus(pend_dt[rows, :] + dtb_ref[...])
        d1, d2, d3 = _split3(dt * neg_a)
        cs = _dot(trib, d1) + _dot(trib, d2) + _dot(trib, d3)
        cl = _dot(onesb, d1) + _dot(onesb, d2) + _dot(onesb, d3)
        ecs = jnp.exp(cs)
        wgt = dt * jnp.exp(cl - cs)
        ecl = jnp.exp(cl)
        cs_t = cs.T
        dt_t = dt.T

        for g in range(N_GROUPS):
            cg = c_of(g)
            acc = None
            for s in range(ns):
                st = sout_ref[u * ns + s, g * GROUP_W:(g + 1) * GROUP_W, :].astype(BF16)
                yo = _dot_nt(cg, st)
                if ns > 1:
                    yo = jnp.where(rowseq == s, yo, 0.0)
                acc = yo if acc is None else acc + yo
            yoff_scr[rows, g * GROUP_W:(g + 1) * GROUP_W] = acc

        for g in range(N_GROUPS):
            sc = _dot_nt(c_of(g), b_of(g).astype(BF16))
            for jj in range(N_HEADS // N_GROUPS // 2):
                j = g * (N_HEADS // N_GROUPS // 2) + jj
                ms = []
                for h in (2 * j, 2 * j + 1):
                    diff = cs[:, h:h + 1] - cs_t[h:h + 1, :]
                    dec = jnp.exp(jnp.where(mask, diff, -jnp.inf))
                    ms.append((sc * dec * dt_t[h:h + 1, :]).astype(BF16))
                lhs = jnp.concatenate(ms, axis=1)
                cols = slice(j * 128, (j + 1) * 128)
                xp = xs_ref[:, cols]
                rhs = jnp.concatenate([jnp.where(lo, xp, 0.0).astype(BF16),
                                       jnp.where(lo, 0.0, xp).astype(BF16)], axis=0)
                yd = _dot(lhs, rhs)
                ecs_p = jnp.where(lo, ecs[:, 2 * j:2 * j + 1], ecs[:, 2 * j + 1:2 * j + 2])
                wgt_p = jnp.where(lo, wgt[:, 2 * j:2 * j + 1], wgt[:, 2 * j + 1:2 * j + 2])
                y_scr[rows, cols] = yd + yoff_scr[rows, cols] * ecs_p + xp * dsk_ref[:, cols]
                xw_scr[rows, cols] = (xp * wgt_p).astype(BF16)
                pair = u * (N_HEADS // 2) + j
                if pair in emit_at:
                    emit_at[pair]()

        for g in range(N_GROUPS):
            xwg = xw_scr[rows, g * GROUP_W:(g + 1) * GROUP_W]
            for s in range(ns):
                bg = b_of(g)
                if ns > 1:
                    bg = jnp.where(rowseq == s, bg, 0.0)
                upd = _dot_tn(xwg, bg.astype(BF16))
                for hh in range(GROUP_W // HEAD_DIM):
                    h = g * (GROUP_W // HEAD_DIM) + hh
                    dec = jnp.broadcast_to(ecl[s * q:s * q + 1, h:h + 1], (HEAD_DIM, D_STATE))
                    hrows = slice(h * HEAD_DIM, (h + 1) * HEAD_DIM)
                    sout_ref[u * ns + s, hrows, :] = (
                        sout_ref[u * ns + s, hrows, :] * dec + upd[hh * HEAD_DIM:(hh + 1) * HEAD_DIM, :])

    for rt in row_tiles:
        for g in range(N_GROUPS):
            cts = [slice(g * GROUP_W + c.start, g * GROUP_W + c.stop) for c in col_tiles(GROUP_W)]
            ss = None
            for ct in cts:
                yz = y_scr[rt, ct] * zg_scr[rt, ct]
                part = jnp.sum(yz * yz, axis=-1, keepdims=True)
                ss = part if ss is None else ss + part
            scale = lax.rsqrt(ss * (1.0 / GROUP_W) + NORM_EPS)
            for ct in cts:
                yn_scr[rt, ct] = (y_scr[rt, ct] * zg_scr[rt, ct] * scale * ng_ref[:, ct]).astype(BF16)

    for c0 in range(0, D_MODEL, PROJ_CHUNK):
        ct = slice(c0, c0 + PROJ_CHUNK)
        o = xres_scr[:, ct] + _dot(yn_scr[...], wout_ref[:, ct])
        for s in range(nseq):
            xo_ref[s, :, ct] = o[s * q:(s + 1) * q, :]


def _ssd_layer(x, conv_prev, ssm_prev, norm_g, w_in, conv_w, conv_b, dt_bias, a_log, d_skip,
               norm_y, w_out, *, nsub, ns, q):
    n_seq, seq_len, _ = x.shape
    nseq = nsub * ns
    sb = ns * q
    r = nsub * sb
    n_chunks = seq_len // q
    pad = DT_PAD - N_HEADS
    win = jnp.concatenate([w_in, jnp.zeros((D_MODEL, pad), F32)], axis=1).astype(BF16)
    dtb = jnp.pad(dt_bias, (0, pad)).reshape(1, DT_PAD)
    alog = jnp.pad(a_log, (0, pad)).reshape(1, DT_PAD)
    dsk = jnp.repeat(d_skip, HEAD_DIM).reshape(1, D_INNER)
    blk = np.kron(np.eye(ns), np.ones((q, q)))
    tri = jnp.asarray(blk * np.tril(np.ones((sb, sb))), F32)
    ones = jnp.asarray(blk, F32)
    state = ssm_prev.reshape(n_seq, D_INNER, D_STATE)
    conv_prev = conv_prev.reshape(n_seq, CONV_W - 1, CONV_DIM)

    kern = functools.partial(_ssd_kernel, nsub=nsub, ns=ns, q=q, n_chunks=n_chunks)
    out_shape = (jax.ShapeDtypeStruct(x.shape, F32),
                 jax.ShapeDtypeStruct((n_seq, CONV_W - 1, CONV_DIM), F32),
                 jax.ShapeDtypeStruct((n_seq, D_INNER, D_STATE), F32))
    n_steps = (n_seq // nseq) * n_chunks
    nxt = lambda i: jnp.minimum(i, n_steps - 1)
    cur = lambda i: jnp.maximum(i - 1, 0)
    in_row_spec = pl.BlockSpec((nseq, q, D_MODEL),
                               lambda i: (nxt(i) // n_chunks, nxt(i) % n_chunks, 0))
    row_spec = pl.BlockSpec((nseq, q, D_MODEL),
                            lambda i: (cur(i) // n_chunks, cur(i) % n_chunks, 0))
    conv_spec = pl.BlockSpec((nseq, CONV_W - 1, CONV_DIM), lambda i: (cur(i) // n_chunks, 0, 0))
    state_spec = pl.BlockSpec((nseq, D_INNER, D_STATE), lambda i: (cur(i) // n_chunks, 0, 0))
    in_specs = [in_row_spec,
                _const_spec((1, D_MODEL)),
                _const_spec(win.shape),
                _const_spec((CONV_W, CONV_DIM)),
                _const_spec((1, CONV_DIM)),
                _const_spec((1, DT_PAD)),
                _const_spec((1, DT_PAD)),
                _const_spec((1, D_INNER)),
                _const_spec((1, D_INNER)),
                _const_spec((D_INNER, D_MODEL)),
                _const_spec((sb, sb)),
                _const_spec((sb, sb)),
                conv_spec, state_spec]
    x_new, conv_new, state_new = pl.pallas_call(
        kern,
        grid=(n_steps + 1,),
        in_specs=in_specs,
        out_specs=(row_spec, conv_spec, state_spec),
        out_shape=out_shape,
        scratch_shapes=[pltpu.VMEM((r, D_MODEL), F32),
                        pltpu.VMEM((r, D_INNER), F32),
                        pltpu.VMEM((r, CONV_DIM), F32),
                        pltpu.VMEM((r, DT_PAD), F32),
                        pltpu.VMEM((r, D_MODEL), F32),
                        pltpu.VMEM((r, D_INNER), F32),
                        pltpu.VMEM((nseq * 8, CONV_DIM), F32),
                        pltpu.VMEM((r, CONV_DIM), F32),
                        pltpu.VMEM((r, D_INNER), F32),
                        pltpu.VMEM((r, D_INNER), F32),
                        pltpu.VMEM((r, D_INNER), BF16),
                        pltpu.VMEM((r, D_MODEL), BF16),
                        pltpu.VMEM((r, D_INNER), BF16)],
        compiler_params=pltpu.CompilerParams(
            dimension_semantics=("arbitrary",), vmem_limit_bytes=VMEM_LIMIT),
        name="ssd_layer",
    )(x, norm_g.reshape(1, D_MODEL), win, conv_w, conv_b.reshape(1, CONV_DIM), dtb, alog, dsk,
      norm_y.reshape(1, D_INNER), w_out.astype(BF16), tri, ones, conv_prev, state)
    return (x_new, conv_new.reshape(1, n_seq, CONV_W - 1, CONV_DIM),
            state_new.reshape(1, n_seq, N_HEADS, HEAD_DIM, D_STATE))


def _sg_kernel(x_ref, g_ref, win_ref, bin_ref, lng_ref, lnb_ref, wmix_ref, bmix_ref, wout_ref,
               xo_ref, *v_out, r):
    x = x_ref[...]
    hn = _rms(x, g_ref[...]).astype(BF16)
    u = _gelu_tanh(_dot(hn, win_ref[:, 0:SG_WIDTH]) + bin_ref[:, 0:SG_WIDTH])
    v = _gelu_tanh(_dot(hn, win_ref[:, SG_WIDTH:2 * SG_WIDTH]) + bin_ref[:, SG_WIDTH:2 * SG_WIDTH])
    mu = jnp.mean(v, axis=-1, keepdims=True)
    vc = v - mu
    var = jnp.mean(vc * vc, axis=-1, keepdims=True)
    vn = vc * lax.rsqrt(var + LN_EPS) * lng_ref[...] + lnb_ref[...]
    if v_out:
        v_out[0][...] = vn
    vb = vn.astype(BF16)
    blocks = []
    for sb in range(r // SG_CHUNK):
        rows = slice(sb * SG_CHUNK, (sb + 1) * SG_CHUNK)
        parts = []
        for g in range(SG_GROUPS):
            cols = slice(g * SG_GROUP_DIM, (g + 1) * SG_GROUP_DIM)
            parts.append(_dot(wmix_ref[g], vb[rows, cols]) + bmix_ref[:, cols])
        blocks.append(jnp.concatenate(parts, axis=1))
    mixed = blocks[0] if len(blocks) == 1 else jnp.concatenate(blocks, axis=0)
    xo_ref[...] = x + _dot((u * mixed).astype(BF16), wout_ref[...])


def _sg_layer(x2d, norm_g, w_in, b_in, ln_g, ln_b, w_s, b_s, w_out, *, seq_len, want_v):
    r = SG_ROWS
    t = x2d.shape[0]
    q = min(seq_len, SG_CHUNK)
    reps = SG_CHUNK // q
    ws = jnp.tril(w_s)[:, :q, :q]
    wmix = jnp.einsum("ab,gts->gatbs", jnp.eye(reps, dtype=F32), ws)
    wmix = wmix.reshape(SG_GROUPS, SG_CHUNK, SG_CHUNK)
    bmix = jnp.tile(jnp.repeat(b_s.T[:q], SG_GROUP_DIM, axis=1), (reps, 1))
    row_spec = pl.BlockSpec((r, D_MODEL), lambda i: (i, 0))
    v_spec = pl.BlockSpec((r, SG_WIDTH), lambda i: (i, 0))
    out_shape = [jax.ShapeDtypeStruct(x2d.shape, F32)]
    out_specs = [row_spec]
    if want_v:
        out_shape.append(jax.ShapeDtypeStruct((t, SG_WIDTH), F32))
        out_specs.append(v_spec)
    outs = pl.pallas_call(
        functools.partial(_sg_kernel, r=r),
        grid=(t // r,),
        in_specs=[row_spec,
                  _const_spec((1, D_MODEL)),
                  _const_spec((D_MODEL, 2 * SG_WIDTH)),
                  _const_spec((1, 2 * SG_WIDTH)),
                  _const_spec((1, SG_WIDTH)),
                  _const_spec((1, SG_WIDTH)),
                  _const_spec((SG_GROUPS, SG_CHUNK, SG_CHUNK)),
                  _const_spec((SG_CHUNK, SG_WIDTH)),
                  _const_spec((SG_WIDTH, D_MODEL))],
        out_specs=out_specs,
        out_shape=out_shape,
        compiler_params=pltpu.CompilerParams(
            dimension_semantics=("arbitrary",), vmem_limit_bytes=VMEM_LIMIT),
        name="sg_layer",
    )(x2d, norm_g.reshape(1, D_MODEL), w_in.astype(BF16), b_in.reshape(1, 2 * SG_WIDTH),
      ln_g.reshape(1, SG_WIDTH), ln_b.reshape(1, SG_WIDTH), wmix.astype(BF16), bmix,
      w_out.astype(BF16))
    return outs if want_v else (outs[0], None)


_LANE_E0 = MOE_GROUPS


def _router_kernel(x_ref, g_ref, wr_ref, br_ref, ltri_ref, utri_ref, meta_ref, cnt_ref):
    tt = x_ref.shape[0]
    hn = _rms(x_ref[...], g_ref[...])
    h1 = hn.astype(BF16)
    h2 = (hn - h1.astype(F32)).astype(BF16)
    wr = wr_ref[...]
    w1 = wr.astype(BF16)
    w2 = (wr - w1.astype(F32)).astype(BF16)
    logits = _dot(h1, w1) + _dot(h1, w2) + _dot(h2, w1) + br_ref[...]

    lane = lax.broadcasted_iota(I32, (tt, 128), 1).astype(F32)
    neg = -jnp.inf
    gl = jnp.where(lane < MOE_GROUPS, logits, neg)
    gmax = jnp.max(gl, axis=-1, keepdims=True)
    g_top = jnp.min(jnp.where(gl == gmax, lane, 128.0), axis=-1, keepdims=True)
    p_g = 1.0 / jnp.sum(jnp.exp(gl - gmax), axis=-1, keepdims=True)

    first = _LANE_E0 + MOE_EPG * g_top
    in_grp = (lane >= first) & (lane < first + MOE_EPG)
    el = jnp.where(in_grp, logits, neg)
    emax = jnp.max(el, axis=-1, keepdims=True)
    ee = jnp.exp(el - emax)
    prob = jnp.where(in_grp, ee / jnp.sum(ee, axis=-1, keepdims=True), -1.0)
    p1 = jnp.max(prob, axis=-1, keepdims=True)
    i1 = jnp.min(jnp.where(prob == p1, lane, 128.0), axis=-1, keepdims=True)
    prob2 = jnp.where(lane == i1, -1.0, prob)
    p2 = jnp.max(prob2, axis=-1, keepdims=True)
    i2 = jnp.min(jnp.where(prob2 == p2, lane, 128.0), axis=-1, keepdims=True)
    psum = p1 + p2
    gate1 = p_g * (p1 / psum)
    gate2 = p_g * (p2 / psum)

    sel1 = lane == i1
    sel2 = lane == i2
    onehot = jnp.where(sel1 | sel2, 1.0, 0.0)
    before = _dot(ltri_ref[...], onehot.astype(BF16))
    cnt = jnp.sum(onehot, axis=0, keepdims=True)
    pad = jnp.floor((cnt + (GRANULE - 1)) * (1.0 / GRANULE)) * GRANULE
    pad8 = jnp.broadcast_to(pad, (8, 128))
    seg_off = _dot(pad8.astype(BF16), utri_ref[...])
    local = before + seg_off[0:1, :]
    rank1 = jnp.sum(jnp.where(sel1, local, 0.0), axis=-1, keepdims=True)
    rank2 = jnp.sum(jnp.where(sel2, local, 0.0), axis=-1, keepdims=True)
    cnt_ref[...] = pad8

    e1 = i1 - _LANE_E0
    e2 = i2 - _LANE_E0
    meta = jnp.where(lane == 0, e1, 0.0)
    meta = jnp.where(lane == 1, e2, meta)
    meta = jnp.where(lane == 2, gate1, meta)
    meta = jnp.where(lane == 3, gate2, meta)
    meta = jnp.where(lane == 4, rank1, meta)
    meta = jnp.where(lane == 5, rank2, meta)
    meta_ref[...] = meta


def _granule(ref, g):
    return ref.at[pl.ds(pl.multiple_of(g * GRANULE, GRANULE), GRANULE), :]


def _dispatch_kernel(ngran_ref, nv_ref, dest_ref, gap_ref, x_ref, g_ref, meta_ref, xs_hbm,
                     buf, zbuf, sem, zsem, nstart):
    i = pl.program_id(0)
    n_tiles = pl.num_programs(0) - 1
    slot = i % 2
    tt = x_ref.shape[0]
    bm = zbuf.shape[0]
    n_blocks = xs_hbm.shape[0] // bm

    def out_copy(sl, g, d):
        return pltpu.make_async_copy(_granule(buf.at[sl], g), _granule(xs_hbm, d), sem.at[sl])

    def drain(sl):
        def body(_, carry):
            out_copy(sl, 0, 0).wait()
            return carry
        lax.fori_loop(0, nstart[sl], body, 0)

    @pl.when(i == 0)
    def _():
        nstart[0] = 0
        nstart[1] = 0

    @pl.when(i < n_tiles)
    def _():
        drain(slot)
        hn = _rms(x_ref[...], g_ref[...]).astype(BF16)
        mt = meta_ref[...].T
        rows = lax.broadcasted_iota(I32, (LOCAL_ROWS, tt), 0).astype(F32)
        onehot = jnp.where((rows == mt[4:5, :]) | (rows == mt[5:6, :]), 1.0, 0.0).astype(BF16)
        buf[slot] = _dot(onehot, hn)
        ng = ngran_ref[i]

        def body(g, carry):
            out_copy(slot, g, dest_ref[0, 0, g]).start()
            return carry
        lax.fori_loop(0, ng, body, 0)
        nstart[slot] = ng

    @pl.when(i == n_tiles)
    def _():
        drain(0)
        drain(1)
        zbuf[...] = jnp.zeros_like(zbuf)

        def gap_copy(d):
            return pltpu.make_async_copy(_granule(zbuf, 0), _granule(xs_hbm, d), zsem.at[0])

        def tail_copy(b):
            return pltpu.make_async_copy(
                zbuf, xs_hbm.at[pl.ds(pl.multiple_of(b * bm, bm), bm), :], zsem.at[1])

        def each_gap(fn):
            def body(j, carry):
                d = gap_ref[0, 0, j]

                @pl.when(d >= 0)
                def _():
                    fn(gap_copy(d))
                return carry
            lax.fori_loop(0, gap_ref.shape[2], body, 0)

        def each_tail(fn):
            def body(b, carry):
                fn(tail_copy(b))
                return carry
            lax.fori_loop(nv_ref[0], n_blocks, body, 0)

        each_gap(lambda cp: cp.start())
        each_tail(lambda cp: cp.start())
        each_gap(lambda cp: cp.wait())
        each_tail(lambda cp: cp.wait())


def _expert_kernel(bexp_ref, nv_ref, xs_ref, wg_ref, wu_ref, wd_ref, ys_ref):
    @pl.when(pl.program_id(0) < nv_ref[0])
    def _():
        xg = xs_ref[...].astype(BF16)
        hg = _dot(xg, wg_ref[...].astype(BF16))
        hu = _dot(xg, wu_ref[...].astype(BF16))
        act = (_silu(hg) * hu).astype(BF16)
        ys_ref[...] = _dot(act, wd_ref[...].astype(BF16))


def _combine_kernel(ngran_ref, src_ref, srcn_ref, x_ref, meta_ref, gf_ref, ys_hbm, o_ref, buf, sem,
                    *, final):
    i = pl.program_id(0)
    n = pl.num_programs(0)
    slot = i % 2
    tt = x_ref.shape[0]

    def in_copy(sl, g, d):
        return pltpu.make_async_copy(_granule(ys_hbm, d), _granule(buf.at[sl], g), sem.at[sl])

    def gather(idx_ref, sl, ng):
        def body(g, carry):
            in_copy(sl, g, idx_ref[0, 0, g]).start()
            return carry
        lax.fori_loop(0, ng, body, 0)

    @pl.when(i == 0)
    def _():
        buf[...] = jnp.zeros_like(buf)
        gather(src_ref, 0, ngran_ref[0])

    @pl.when(i + 1 < n)
    def _():
        gather(srcn_ref, 1 - slot, ngran_ref[i + 1])

    def wait_body(_, carry):
        in_copy(slot, 0, 0).wait()
        return carry
    lax.fori_loop(0, ngran_ref[i], wait_body, 0)

    ys = buf[slot].astype(BF16)
    meta = meta_ref[...]
    cols = lax.broadcasted_iota(I32, (tt, LOCAL_ROWS), 1).astype(F32)
    pick1 = jnp.where(cols == meta[:, 4:5], 1.0, 0.0).astype(BF16)
    pick2 = jnp.where(cols == meta[:, 5:6], 1.0, 0.0).astype(BF16)
    y = x_ref[...] + meta[:, 2:3] * _dot(pick1, ys) + meta[:, 3:4] * _dot(pick2, ys)
    if final:
        y = _rms(y, gf_ref[...])
    o_ref[...] = y


def _moe(x2d, norm_g, w_grp, b_grp, w_rt, b_rt, w_gate, w_up, w_down, norm_final):
    t = x2d.shape[0]
    tt = MOE_TILE
    bm = EXPERT_BLOCK
    n_tiles = t // tt
    n_exp = MOE_EXPERTS
    lane_pad = 128 - MOE_GROUPS - n_exp
    wr = jnp.concatenate([w_grp, w_rt, jnp.zeros((D_MODEL, lane_pad), F32)], axis=1)
    br = jnp.concatenate([b_grp, b_rt, jnp.zeros((lane_pad,), F32)]).reshape(1, 128)
    ltri = jnp.asarray(np.tril(np.ones((tt, tt)), -1), BF16)
    utri = jnp.asarray(np.triu(np.ones((128, 128)), 1), BF16)
    g2d = norm_g.reshape(1, D_MODEL)
    arb = pltpu.CompilerParams(dimension_semantics=("arbitrary",), vmem_limit_bytes=VMEM_LIMIT)

    meta, cnt = pl.pallas_call(
        _router_kernel,
        grid=(n_tiles,),
        in_specs=[pl.BlockSpec((tt, D_MODEL), lambda i: (i, 0)),
                  _const_spec((1, D_MODEL)),
                  _const_spec((D_MODEL, 128)),
                  _const_spec((1, 128)),
                  _const_spec((tt, tt)),
                  _const_spec((128, 128))],
        out_specs=(pl.BlockSpec((tt, 128), lambda i: (i, 0)),
                   pl.BlockSpec((8, 128), lambda i: (i, 0))),
        out_shape=(jax.ShapeDtypeStruct((t, 128), F32),
                   jax.ShapeDtypeStruct((n_tiles * 8, 128), F32)),
        compiler_params=arb,
        name="moe_router",
    )(x2d, g2d, wr, br, ltri, utri)

    runs = cnt.reshape(n_tiles, 8, 128)[:, 0, _LANE_E0:_LANE_E0 + n_exp].astype(I32)
    rows_e = jnp.sum(runs, axis=0)
    nblk = (rows_e + bm - 1) // bm
    blk_end = jnp.cumsum(nblk)
    e_start = (blk_end - nblk) * bm
    n_valid = blk_end[-1]
    run_end = jnp.cumsum(runs, axis=1)
    ngran = (run_end[:, -1] // GRANULE).astype(I32)
    shift = e_start[None, :] + (jnp.cumsum(runs, axis=0) - runs) - (run_end - runs)
    g_row = jnp.arange(N_LOCAL_GRAN, dtype=I32) * GRANULE
    e_of_g = jnp.sum((run_end[:, None, :] <= g_row[None, :, None]).astype(I32), axis=-1)
    shift_g = jnp.sum(jnp.where(e_of_g[..., None] == jnp.arange(n_exp, dtype=I32),
                                shift[:, None, :], 0), axis=-1)
    dest = jnp.where(e_of_g < n_exp, (shift_g + g_row[None, :]) // GRANULE, 0)
    dest = dest.astype(I32).reshape(n_tiles, 1, N_LOCAL_GRAN)
    per_blk = bm // GRANULE
    gap = ((e_start + rows_e) // GRANULE)[:, None] + jnp.arange(per_blk, dtype=I32)[None, :]
    gap = jnp.where(gap < ((e_start + nblk * bm) // GRANULE)[:, None], gap, -1)
    gap = gap.astype(I32).reshape(1, 1, n_exp * per_blk)
    n_blocks = (2 * t + n_tiles * n_exp * (GRANULE - 1)) // bm + 1 + n_exp
    blk = jnp.minimum(jnp.arange(n_blocks, dtype=I32), n_valid - 1)
    block_expert = jnp.sum((blk[:, None] >= blk_end[None, :]).astype(I32), axis=1).astype(I32)
    n_valid = n_valid.reshape(1).astype(I32)

    last = n_tiles - 1
    xs = pl.pallas_call(
        _dispatch_kernel,
        grid_spec=pltpu.PrefetchScalarGridSpec(
            num_scalar_prefetch=2,
            grid=(n_tiles + 1,),
            in_specs=[pl.BlockSpec((1, 1, N_LOCAL_GRAN),
                                   lambda i, ng, nv: (jnp.minimum(i, last), 0, 0),
                                   memory_space=pltpu.SMEM),
                      pl.BlockSpec((1, 1, n_exp * per_blk), lambda i, ng, nv: (0, 0, 0),
                                   memory_space=pltpu.SMEM),
                      pl.BlockSpec((tt, D_MODEL), lambda i, ng, nv: (jnp.minimum(i, last), 0)),
                      _const_spec((1, D_MODEL)),
                      pl.BlockSpec((tt, 128), lambda i, ng, nv: (jnp.minimum(i, last), 0))],
            out_specs=pl.BlockSpec(memory_space=pl.ANY),
            scratch_shapes=[pltpu.VMEM((2, LOCAL_ROWS, D_MODEL), F32),
                            pltpu.VMEM((bm, D_MODEL), F32),
                            pltpu.SemaphoreType.DMA((2,)),
                            pltpu.SemaphoreType.DMA((2,)),
                            pltpu.SMEM((2,), I32)]),
        out_shape=jax.ShapeDtypeStruct((n_blocks * bm, D_MODEL), F32),
        compiler_params=arb,
        name="moe_dispatch",
    )(ngran, n_valid, dest, gap, x2d, g2d, meta)

    w_spec = lambda shape: pl.BlockSpec((None,) + shape, lambda b, be, nv: (be[b], 0, 0))
    row_blk = pl.BlockSpec((bm, D_MODEL), lambda b, be, nv: (jnp.minimum(b, nv[0] - 1), 0))
    ys = pl.pallas_call(
        _expert_kernel,
        grid_spec=pltpu.PrefetchScalarGridSpec(
            num_scalar_prefetch=2,
            grid=(n_blocks,),
            in_specs=[row_blk,
                      w_spec((D_MODEL, MOE_D_FF)),
                      w_spec((D_MODEL, MOE_D_FF)),
                      w_spec((MOE_D_FF, D_MODEL))],
            out_specs=row_blk),
        out_shape=jax.ShapeDtypeStruct((n_blocks * bm, D_MODEL), F32),
        input_output_aliases={2: 0},
        compiler_params=arb,
        name="moe_experts",
    )(block_expert, n_valid, xs, w_gate, w_up, w_down)

    final = norm_final is not None
    gf = (norm_final if final else jnp.ones((D_MODEL,), F32)).reshape(1, D_MODEL)
    src_spec = lambda nxt: pl.BlockSpec(
        (1, 1, N_LOCAL_GRAN), lambda i, ng: (jnp.minimum(i + nxt, last), 0, 0),
        memory_space=pltpu.SMEM)
    out = pl.pallas_call(
        functools.partial(_combine_kernel, final=final),
        grid_spec=pltpu.PrefetchScalarGridSpec(
            num_scalar_prefetch=1,
            grid=(n_tiles,),
            in_specs=[src_spec(0), src_spec(1),
                      pl.BlockSpec((tt, D_MODEL), lambda i, ng: (i, 0)),
                      pl.BlockSpec((tt, 128), lambda i, ng: (i, 0)),
                      _const_spec((1, D_MODEL)),
                      pl.BlockSpec(memory_space=pl.ANY)],
            out_specs=pl.BlockSpec((tt, D_MODEL), lambda i, ng: (i, 0)),
            scratch_shapes=[pltpu.VMEM((2, LOCAL_ROWS, D_MODEL), F32),
                            pltpu.SemaphoreType.DMA((2,))]),
        out_shape=jax.ShapeDtypeStruct((t, D_MODEL), F32),
        compiler_params=arb,
        name="moe_combine",
    )(ngran, dest, dest, x2d, meta, gf, ys)
    return out


def _trunk(x, conv_prev, ssm_prev, p, *, ssd_nsub, ssd_ns, want_v):
    n_seq, seq_len, _ = x.shape
    q = seq_len if seq_len < SSD_CHUNK else SSD_CHUNK
    x, conv_new, ssm_new = _ssd_layer(
        x, conv_prev, ssm_prev, p["norm_mix"][0], p["ssd_w_in"][0], p["ssd_conv_w"][0],
        p["ssd_conv_b"][0], p["ssd_dt_bias"][0], p["ssd_a_log"][0], p["ssd_d"][0], p["ssd_norm"][0],
        p["ssd_w_out"][0], nsub=ssd_nsub, ns=ssd_ns, q=q)
    x2d = x.reshape(n_seq * seq_len, D_MODEL)
    x2d = _moe(x2d, p["norm_ffn"][0], p["moe_w_group"][0], p["moe_b_group"][0], p["moe_w_router"][0],
               p["moe_b_router"][0], p["moe_w_gate"][0], p["moe_w_up"][0], p["moe_w_down"][0], None)
    x2d, v = _sg_layer(x2d, p["norm_mix"][1], p["sg_w_in"][0], p["sg_b_in"][0], p["sg_ln_g"][0],
                       p["sg_ln_b"][0], p["sg_w_s"][0], p["sg_b_s"][0], p["sg_w_out"][0],
                       seq_len=seq_len, want_v=want_v)
    y2d = _moe(x2d, p["norm_ffn"][1], p["moe_w_group"][1], p["moe_b_group"][1], p["moe_w_router"][1],
               p["moe_b_router"][1], p["moe_w_gate"][1], p["moe_w_up"][1], p["moe_w_down"][1],
               p["norm_final"])
    y = y2d.reshape(n_seq, seq_len, D_MODEL)
    if want_v:
        v = v.reshape(1, n_seq, seq_len, SG_WIDTH)
    return y, conv_new, ssm_new, v


def kernel(x_prompt, x_sample, state_ssm, state_conv, norm_mix, norm_ffn, norm_final, ssd_w_in, ssd_conv_w, ssd_conv_b, ssd_dt_bias, ssd_a_log, ssd_d, ssd_norm, ssd_w_out, sg_w_in, sg_b_in, sg_ln_g, sg_ln_b, sg_w_s, sg_b_s, sg_w_out, moe_w_group, moe_b_group, moe_w_router, moe_b_router, moe_w_gate, moe_w_up, moe_w_down):
    p = dict(norm_mix=norm_mix, norm_ffn=norm_ffn, norm_final=norm_final, ssd_w_in=ssd_w_in,
             ssd_conv_w=ssd_conv_w, ssd_conv_b=ssd_conv_b, ssd_dt_bias=ssd_dt_bias,
             ssd_a_log=ssd_a_log, ssd_d=ssd_d, ssd_norm=ssd_norm, ssd_w_out=ssd_w_out,
             sg_w_in=sg_w_in, sg_b_in=sg_b_in, sg_ln_g=sg_ln_g, sg_ln_b=sg_ln_b, sg_w_s=sg_w_s,
             sg_b_s=sg_b_s, sg_w_out=sg_w_out, moe_w_group=moe_w_group, moe_b_group=moe_b_group,
             moe_w_router=moe_w_router, moe_b_router=moe_b_router, moe_w_gate=moe_w_gate,
             moe_w_up=moe_w_up, moe_w_down=moe_w_down)
    nb = x_prompt.shape[0]
    conv0 = jnp.zeros((1, nb, CONV_W - 1, CONV_DIM), F32)
    ssm0 = jnp.zeros((1, nb, N_HEADS, HEAD_DIM, D_STATE), F32)
    y_p, conv_p, ssm_p, _ = _trunk(x_prompt, conv0, ssm0, p, ssd_nsub=2, ssd_ns=1, want_v=False)
    y_s, conv_s, ssm_s, v_s = _trunk(x_sample, state_conv, state_ssm, p, ssd_nsub=1, ssd_ns=4,
                                     want_v=True)
    return (y_p, y_s, ssm_p, conv_p, ssm_s, conv_s, v_s)
```

```python
import functools
import math

import jax
import jax.numpy as jnp
import numpy as np
from jax import lax
from jax.experimental import pallas as pl
from jax.experimental.pallas import tpu as pltpu

F32 = jnp.float32
BF16 = jnp.bfloat16
I32 = jnp.int32

D_MODEL = 1024
N_HEADS = 32
HEAD_DIM = 64
N_GROUPS = 4
D_STATE = 128
D_INNER = N_HEADS * HEAD_DIM
GROUP_W = D_INNER // N_GROUPS
CONV_W = 4
CONV_DIM = D_INNER + 2 * N_GROUPS * D_STATE
DT_PAD = 128
SSD_CHUNK = 128
PROJ_CHUNK = 256
SG_WIDTH = 2 * D_MODEL
SG_GROUPS = 8
SG_GROUP_DIM = SG_WIDTH // SG_GROUPS
SG_CHUNK = 128
SG_ROWS = 256
MOE_GROUPS = 4
MOE_EPG = 8
MOE_EXPERTS = MOE_GROUPS * MOE_EPG
MOE_D_FF = 256
NORM_EPS = 1e-6
LN_EPS = 1e-5

MOE_TILE = 256
ROUTER_TILES = 2
EXPERT_BLOCK = 512
GRANULE = 8
LOCAL_ROWS = 2 * MOE_TILE + MOE_EXPERTS * GRANULE
N_LOCAL_GRAN = LOCAL_ROWS // GRANULE
VMEM_LIMIT = 56 * 1024 * 1024


def _sigmoid(x):
    return 0.5 * (jnp.tanh(0.5 * x) + 1.0)


def _silu(x):
    return x * _sigmoid(x)


def _softplus(x):
    return jnp.maximum(x, 0.0) + jnp.log(1.0 + jnp.exp(-jnp.abs(x)))


def _gelu_tanh(x):
    c = math.sqrt(2.0 / math.pi)
    return x * (0.5 * (1.0 + jnp.tanh(c * (x + 0.044715 * (x * x * x)))))


def _rms(x, g):
    return x * lax.rsqrt(jnp.mean(x * x, axis=-1, keepdims=True) + NORM_EPS) * g


def _split3(x):
    a = x.astype(BF16)
    r = x - a.astype(F32)
    b = r.astype(BF16)
    c = (r - b.astype(F32)).astype(BF16)
    return a, b, c


def _dot(a, b):
    return jnp.dot(a, b, preferred_element_type=F32)


def _dot_nt(a, b):
    return lax.dot_general(a, b, (((1,), (1,)), ((), ())), preferred_element_type=F32)


def _dot_tn(a, b):
    return lax.dot_general(a, b, (((0,), (0,)), ((), ())), preferred_element_type=F32)


def _const_spec(shape):
    nd = len(shape)
    return pl.BlockSpec(shape, lambda *_: (0,) * nd)


def _ssd_kernel(x_ref, g_ref, win_ref, cw_ref, cb_ref, dtb_ref, alog_ref, dsk_ref, ng_ref,
                wout_ref, tri_ref, ones_ref, cin_ref, sin_ref,
                xo_ref, cout_ref, sout_ref,
                pend_x, pend_z, pend_xbc, pend_dt, xres_scr, zg_scr,
                prev_scr, xc_scr, yoff_scr, y_scr, xw_scr, hn_scr, yn_scr,
                *, nsub, ns, q, n_chunks):
    i = pl.program_id(0)
    first_chunk = lax.rem(jnp.maximum(i - 1, 0), n_chunks) == 0
    nseq = nsub * ns
    sb = ns * q
    r = nsub * sb

    @pl.when(i == 0)
    def _():
        pend_x[...] = jnp.zeros_like(pend_x)
        pend_z[...] = jnp.zeros_like(pend_z)
        pend_xbc[...] = jnp.zeros_like(pend_xbc)
        pend_dt[...] = jnp.zeros_like(pend_dt)

    tr = min(r, 128)
    row_tiles = [slice(a, a + tr) for a in range(0, r, tr)]

    def col_tiles(total, rows=tr):
        w = min(total, max(128, (32 * 1024) // rows))
        return [slice(c, c + w) for c in range(0, total, w)]

    for rt in row_tiles:
        for ct in col_tiles(D_MODEL):
            xres_scr[rt, ct] = pend_x[rt, ct]
        for ct in col_tiles(D_INNER):
            zg_scr[rt, ct] = _silu(pend_z[rt, ct])

    for s in range(nseq):
        for ct in col_tiles(D_MODEL, q):
            pend_x[s * q:(s + 1) * q, ct] = x_ref[s, :, ct]
    for rt in row_tiles:
        ss = None
        for ct in col_tiles(D_MODEL):
            xv = pend_x[rt, ct]
            part = jnp.sum(xv * xv, axis=-1, keepdims=True)
            ss = part if ss is None else ss + part
        scale = lax.rsqrt(ss * (1.0 / D_MODEL) + NORM_EPS)
        for ct in col_tiles(D_MODEL):
            hn_scr[rt, ct] = (pend_x[rt, ct] * scale * g_ref[:, ct]).astype(BF16)

    def proj_chunk(dst, dst_col, w_col, width):
        def run():
            dst[:, dst_col:dst_col + width] = _dot(hn_scr[...], win_ref[:, w_col:w_col + width])
        return run
    z_chunks = [proj_chunk(pend_z, c, c, PROJ_CHUNK) for c in range(0, D_INNER, PROJ_CHUNK)]
    scan_chunks = [proj_chunk(pend_xbc, c, D_INNER + c, PROJ_CHUNK)
                   for c in range(0, CONV_DIM, PROJ_CHUNK)]
    scan_chunks.append(proj_chunk(pend_dt, 0, D_INNER + CONV_DIM, DT_PAD))

    @pl.when(first_chunk)
    def _():
        sout_ref[...] = sin_ref[...]
        prev_scr[...] = jnp.zeros_like(prev_scr)
        for s in range(nseq):
            for k in range(CONV_W - 1):
                prev_scr[pl.ds(s * 8 + 5 + k, 1), :] = cin_ref[s, pl.ds(k, 1), :]

    sub8 = lax.broadcasted_iota(I32, (8, 1), 0)
    for s in range(nseq):
        srows = slice(s * q, (s + 1) * q)
        for ct in col_tiles(CONV_DIM, 4 * q):
            xq = pend_xbc[srows, ct]
            hist = prev_scr[s * 8:(s + 1) * 8, ct]
            acc = cb_ref[:, ct] + cw_ref[pl.ds(CONV_W - 1, 1), ct] * xq
            for j in range(1, CONV_W):
                sh = pltpu.roll(xq, j, 0)
                head = jnp.where(sub8 < j, pltpu.roll(hist, j, 0), sh[0:8, :])
                sh = head if q == 8 else jnp.concatenate([head, sh[8:, :]], axis=0)
                acc = acc + cw_ref[pl.ds(CONV_W - 1 - j, 1), ct] * sh
            xc_scr[srows, ct] = _silu(acc)
            last = xq[q - 8:q, :]
            prev_scr[s * 8:(s + 1) * 8, ct] = last
            for k in range(CONV_W - 1):
                cout_ref[s, pl.ds(k, 1), ct] = last[5 + k:6 + k, :]
        for _ in range(-(-len(z_chunks) // (nseq - s))):
            z_chunks.pop(0)()
    n_pairs = nsub * (N_HEADS // 2)
    stride = n_pairs // len(scan_chunks)
    emit_at = {k * stride: ch for k, ch in enumerate(scan_chunks)}

    tri = tri_ref[...]
    trib = tri.astype(BF16)
    onesb = ones_ref[...].astype(BF16)
    mask = tri > 0.5
    rowseq = lax.shift_right_logical(lax.broadcasted_iota(I32, (sb, 1), 0), int(math.log2(q)))
    lo = lax.broadcasted_iota(I32, (sb, 128), 1) < HEAD_DIM
    neg_a = -jnp.exp(alog_ref[...])

    for u in range(nsub):
        rows = slice(u * sb, (u + 1) * sb)
        xs_ref = xc_scr.at[rows, 0:D_INNER]

        def b_of(g):
            return xc_scr[rows, D_INNER + g * D_STATE:D_INNER + (g + 1) * D_STATE]

        def c_of(g):
            c0 = D_INNER + (N_GROUPS + g) * D_STATE
            return xc_scr[rows, c0:c0 + D_STATE].astype(BF16)

        dt = _softplus(pend_dt[rows, :] + dtb_ref[...])
        d1, d2, d3 = _split3(dt * neg_a)
        cs = _dot(trib, d1) + _dot(trib, d2) + _dot(trib, d3)
        cl = _dot(onesb, d1) + _dot(onesb, d2) + _dot(onesb, d3)
        ecs = jnp.exp(cs)
        wgt = dt * jnp.exp(cl - cs)
        ecl = jnp.exp(cl)
        cs_t = cs.T
        dt_t = dt.T

        for g in range(N_GROUPS):
            cg = c_of(g)
            acc = None
            for s in range(ns):
                st = sout_ref[u * ns + s, g * GROUP_W:(g + 1) * GROUP_W, :].astype(BF16)
                yo = _dot_nt(cg, st)
                if ns > 1:
                    yo = jnp.where(rowseq == s, yo, 0.0)
                acc = yo if acc is None else acc + yo
            yoff_scr[rows, g * GROUP_W:(g + 1) * GROUP_W] = acc

        for g in range(N_GROUPS):
            sc = _dot_nt(c_of(g), b_of(g).astype(BF16))
            for jj in range(N_HEADS // N_GROUPS // 2):
                j = g * (N_HEADS // N_GROUPS // 2) + jj
                ms = []
                for h in (2 * j, 2 * j + 1):
                    diff = cs[:, h:h + 1] - cs_t[h:h + 1, :]
                    dec = jnp.exp(jnp.where(mask, diff, -jnp.inf))
                    ms.append((sc * dec * dt_t[h:h + 1, :]).astype(BF16))
                lhs = jnp.concatenate(ms, axis=1)
                cols = slice(j * 128, (j + 1) * 128)
                xp = xs_ref[:, cols]
                rhs = jnp.concatenate([jnp.where(lo, xp, 0.0).astype(BF16),
                                       jnp.where(lo, 0.0, xp).astype(BF16)], axis=0)
                yd = _dot(lhs, rhs)
                ecs_p = jnp.where(lo, ecs[:, 2 * j:2 * j + 1], ecs[:, 2 * j + 1:2 * j + 2])
                wgt_p = jnp.where(lo, wgt[:, 2 * j:2 * j + 1], wgt[:, 2 * j + 1:2 * j + 2])
                y_scr[rows, cols] = yd + yoff_scr[rows, cols] * ecs_p + xp * dsk_ref[:, cols]
                xw_scr[rows, cols] = (xp * wgt_p).astype(BF16)
                pair = u * (N_HEADS // 2) + j
                if pair in emit_at:
                    emit_at[pair]()

        for g in range(N_GROUPS):
            xwg = xw_scr[rows, g * GROUP_W:(g + 1) * GROUP_W]
            for s in range(ns):
                bg = b_of(g)
                if ns > 1:
                    bg = jnp.where(rowseq == s, bg, 0.0)
                upd = _dot_tn(xwg, bg.astype(BF16))
                for hh in range(GROUP_W // HEAD_DIM):
                    h = g * (GROUP_W // HEAD_DIM) + hh
                    dec = jnp.broadcast_to(ecl[s * q:s * q + 1, h:h + 1], (HEAD_DIM, D_STATE))
                    hrows = slice(h * HEAD_DIM, (h + 1) * HEAD_DIM)
                    sout_ref[u * ns + s, hrows, :] = (
                        sout_ref[u * ns + s, hrows, :] * dec + upd[hh * HEAD_DIM:(hh + 1) * HEAD_DIM, :])

    for rt in row_tiles:
        for g in range(N_GROUPS):
            cts = [slice(g * GROUP_W + c.start, g * GROUP_W + c.stop) for c in col_tiles(GROUP_W)]
            ss = None
            for ct in cts:
                yz = y_scr[rt, ct] * zg_scr[rt, ct]
                part = jnp.sum(yz * yz, axis=-1, keepdims=True)
                ss = part if ss is None else ss + part
            scale = lax.rsqrt(ss * (1.0 / GROUP_W) + NORM_EPS)
            for ct in cts:
                yn_scr[rt, ct] = (y_scr[rt, ct] * zg_scr[rt, ct] * scale * ng_ref[:, ct]).astype(BF16)

    for c0 in range(0, D_MODEL, PROJ_CHUNK):
        ct = slice(c0, c0 + PROJ_CHUNK)
        o = xres_scr[:, ct] + _dot(yn_scr[...], wout_ref[:, ct])
        for s in range(nseq):
            xo_ref[s, :, ct] = o[s * q:(s + 1) * q, :]


def _ssd_layer(x, conv_prev, ssm_prev, norm_g, w_in, conv_w, conv_b, dt_bias, a_log, d_skip,
               norm_y, w_out, *, nsub, ns, q):
    n_seq, seq_len, _ = x.shape
    nseq = nsub * ns
    sb = ns * q
    r = nsub * sb
    n_chunks = seq_len // q
    pad = DT_PAD - N_HEADS
    win = jnp.concatenate([w_in, jnp.zeros((D_MODEL, pad), F32)], axis=1).astype(BF16)
    dtb = jnp.pad(dt_bias, (0, pad)).reshape(1, DT_PAD)
    alog = jnp.pad(a_log, (0, pad)).reshape(1, DT_PAD)
    dsk = jnp.repeat(d_skip, HEAD_DIM).reshape(1, D_INNER)
    blk = np.kron(np.eye(ns), np.ones((q, q)))
    tri = jnp.asarray(blk * np.tril(np.ones((sb, sb))), F32)
    ones = jnp.asarray(blk, F32)
    state = ssm_prev.reshape(n_seq, D_INNER, D_STATE)
    conv_prev = conv_prev.reshape(n_seq, CONV_W - 1, CONV_DIM)

    kern = functools.partial(_ssd_kernel, nsub=nsub, ns=ns, q=q, n_chunks=n_chunks)
    out_shape = (jax.ShapeDtypeStruct(x.shape, F32),
                 jax.ShapeDtypeStruct((n_seq, CONV_W - 1, CONV_DIM), F32),
                 jax.ShapeDtypeStruct((n_seq, D_INNER, D_STATE), F32))
    n_steps = (n_seq // nseq) * n_chunks
    nxt = lambda i: jnp.minimum(i, n_steps - 1)
    cur = lambda i: jnp.maximum(i - 1, 0)
    in_row_spec = pl.BlockSpec((nseq, q, D_MODEL),
                               lambda i: (nxt(i) // n_chunks, nxt(i) % n_chunks, 0))
    row_spec = pl.BlockSpec((nseq, q, D_MODEL),
                            lambda i: (cur(i) // n_chunks, cur(i) % n_chunks, 0))
    conv_spec = pl.BlockSpec((nseq, CONV_W - 1, CONV_DIM), lambda i: (cur(i) // n_chunks, 0, 0))
    state_spec = pl.BlockSpec((nseq, D_INNER, D_STATE), lambda i: (cur(i) // n_chunks, 0, 0))
    in_specs = [in_row_spec,
                _const_spec((1, D_MODEL)),
                _const_spec(win.shape),
                _const_spec((CONV_W, CONV_DIM)),
                _const_spec((1, CONV_DIM)),
                _const_spec((1, DT_PAD)),
                _const_spec((1, DT_PAD)),
                _const_spec((1, D_INNER)),
                _const_spec((1, D_INNER)),
                _const_spec((D_INNER, D_MODEL)),
                _const_spec((sb, sb)),
                _const_spec((sb, sb)),
                conv_spec, state_spec]
    x_new, conv_new, state_new = pl.pallas_call(
        kern,
        grid=(n_steps + 1,),
        in_specs=in_specs,
        out_specs=(row_spec, conv_spec, state_spec),
        out_shape=out_shape,
        scratch_shapes=[pltpu.VMEM((r, D_MODEL), F32),
                        pltpu.VMEM((r, D_INNER), F32),
                        pltpu.VMEM((r, CONV_DIM), F32),
                        pltpu.VMEM((r, DT_PAD), F32),
                        pltpu.VMEM((r, D_MODEL), F32),
                        pltpu.VMEM((r, D_INNER), F32),
                        pltpu.VMEM((nseq * 8, CONV_DIM), F32),
                        pltpu.VMEM((r, CONV_DIM), F32),
                        pltpu.VMEM((r, D_INNER), F32),
                        pltpu.VMEM((r, D_INNER), F32),
                        pltpu.VMEM((r, D_INNER), BF16),
                        pltpu.VMEM((r, D_MODEL), BF16),
                        pltpu.VMEM((r, D_INNER), BF16)],
        compiler_params=pltpu.CompilerParams(
            dimension_semantics=("arbitrary",), vmem_limit_bytes=VMEM_LIMIT),
        name="ssd_layer",
    )(x, norm_g.reshape(1, D_MODEL), win, conv_w, conv_b.reshape(1, CONV_DIM), dtb, alog, dsk,
      norm_y.reshape(1, D_INNER), w_out.astype(BF16), tri, ones, conv_prev, state)
    return (x_new, conv_new.reshape(1, n_seq, CONV_W - 1, CONV_DIM),
            state_new.reshape(1, n_seq, N_HEADS, HEAD_DIM, D_STATE))


def _sg_kernel(x_ref, g_ref, win_ref, bin_ref, lng_ref, lnb_ref, wmix_ref, bmix_ref, wout_ref,
               xo_ref, *rest, r, want_v):
    v_ref = rest[0] if want_v else None
    hn_scr, uv_scr, vb_scr, um_scr = rest[-4:]
    row_tiles = [slice(a, a + 128) for a in range(0, r, 128)]
    col128 = lambda total: [slice(c, c + 128) for c in range(0, total, 128)]

    for rt in row_tiles:
        ss = None
        for ct in col128(D_MODEL):
            xv = x_ref[rt, ct]
            part = jnp.sum(xv * xv, axis=-1, keepdims=True)
            ss = part if ss is None else ss + part
        scale = lax.rsqrt(ss * (1.0 / D_MODEL) + NORM_EPS)
        for ct in col128(D_MODEL):
            hn_scr[rt, ct] = (x_ref[rt, ct] * scale * g_ref[:, ct]).astype(BF16)

    for c0 in range(0, 2 * SG_WIDTH, PROJ_CHUNK):
        h = _dot(hn_scr[...], win_ref[:, c0:c0 + PROJ_CHUNK])
        for rt in row_tiles:
            for cc in range(0, PROJ_CHUNK, 128):
                ct = slice(c0 + cc, c0 + cc + 128)
                uv_scr[rt, ct] = _gelu_tanh(h[rt, cc:cc + 128] + bin_ref[:, ct])

    for rt in row_tiles:
        vcols = [slice(SG_WIDTH + c.start, SG_WIDTH + c.stop) for c in col128(SG_WIDTH)]
        tot = None
        for ct in vcols:
            part = jnp.sum(uv_scr[rt, ct], axis=-1, keepdims=True)
            tot = part if tot is None else tot + part
        mu = tot * (1.0 / SG_WIDTH)
        ss = None
        for ct in vcols:
            vc = uv_scr[rt, ct] - mu
            part = jnp.sum(vc * vc, axis=-1, keepdims=True)
            ss = part if ss is None else ss + part
        scale = lax.rsqrt(ss * (1.0 / SG_WIDTH) + LN_EPS)
        for ct, c in zip(vcols, col128(SG_WIDTH)):
            vn = (uv_scr[rt, ct] - mu) * scale * lng_ref[:, c] + lnb_ref[:, c]
            if want_v:
                v_ref[rt, c] = vn
            vb_scr[rt, c] = vn.astype(BF16)

    for rt in row_tiles:
        for g in range(SG_GROUPS):
            cols = slice(g * SG_GROUP_DIM, (g + 1) * SG_GROUP_DIM)
            mixed = _dot(wmix_ref[g], vb_scr[rt, cols]) + bmix_ref[:, cols]
            um_scr[rt, cols] = (uv_scr[rt, cols] * mixed).astype(BF16)

    for c0 in range(0, D_MODEL, PROJ_CHUNK):
        ct = slice(c0, c0 + PROJ_CHUNK)
        xo_ref[:, ct] = x_ref[:, ct] + _dot(um_scr[...], wout_ref[:, ct])


def _sg_layer(x2d, norm_g, w_in, b_in, ln_g, ln_b, w_s, b_s, w_out, *, seq_len, want_v):
    r = SG_ROWS
    t = x2d.shape[0]
    q = min(seq_len, SG_CHUNK)
    reps = SG_CHUNK // q
    ws = jnp.tril(w_s)[:, :q, :q]
    wmix = jnp.einsum("ab,gts->gatbs", jnp.eye(reps, dtype=F32), ws)
    wmix = wmix.reshape(SG_GROUPS, SG_CHUNK, SG_CHUNK)
    bmix = jnp.tile(jnp.repeat(b_s.T[:q], SG_GROUP_DIM, axis=1), (reps, 1))
    row_spec = pl.BlockSpec((r, D_MODEL), lambda i: (i, 0))
    v_spec = pl.BlockSpec((r, SG_WIDTH), lambda i: (i, 0))
    out_shape = [jax.ShapeDtypeStruct(x2d.shape, F32)]
    out_specs = [row_spec]
    if want_v:
        out_shape.append(jax.ShapeDtypeStruct((t, SG_WIDTH), F32))
        out_specs.append(v_spec)
    outs = pl.pallas_call(
        functools.partial(_sg_kernel, r=r, want_v=want_v),
        grid=(t // r,),
        scratch_shapes=[pltpu.VMEM((r, D_MODEL), BF16),
                        pltpu.VMEM((r, 2 * SG_WIDTH), F32),
                        pltpu.VMEM((r, SG_WIDTH), BF16),
                        pltpu.VMEM((r, SG_WIDTH), BF16)],
        in_specs=[row_spec,
                  _const_spec((1, D_MODEL)),
                  _const_spec((D_MODEL, 2 * SG_WIDTH)),
                  _const_spec((1, 2 * SG_WIDTH)),
                  _const_spec((1, SG_WIDTH)),
                  _const_spec((1, SG_WIDTH)),
                  _const_spec((SG_GROUPS, SG_CHUNK, SG_CHUNK)),
                  _const_spec((SG_CHUNK, SG_WIDTH)),
                  _const_spec((SG_WIDTH, D_MODEL))],
        out_specs=out_specs,
        out_shape=out_shape,
        compiler_params=pltpu.CompilerParams(
            dimension_semantics=("arbitrary",), vmem_limit_bytes=VMEM_LIMIT),
        name="sg_layer",
    )(x2d, norm_g.reshape(1, D_MODEL), w_in.astype(BF16), b_in.reshape(1, 2 * SG_WIDTH),
      ln_g.reshape(1, SG_WIDTH), ln_b.reshape(1, SG_WIDTH), wmix.astype(BF16), bmix,
      w_out.astype(BF16))
    return outs if want_v else (outs[0], None)


_LANE_E0 = MOE_GROUPS


def _router_kernel(x_ref, g_ref, wr_ref, br_ref, ltri_ref, utri_ref, meta_ref, cnt_ref):
    wr = wr_ref[...]
    w1 = wr.astype(BF16)
    w2 = (wr - w1.astype(F32)).astype(BF16)
    tt = MOE_TILE
    for k in range(x_ref.shape[0] // tt):
        rows = slice(k * tt, (k + 1) * tt)
        meta, pad8 = _route_tile(x_ref[rows, :], g_ref[...], w1, w2, br_ref[...], ltri_ref[...],
                                 utri_ref[...])
        meta_ref[rows, :] = meta
        cnt_ref[k * 8:(k + 1) * 8, :] = pad8


def _route_tile(x, g, w1, w2, br, ltri, utri):
    tt = x.shape[0]
    hn = _rms(x, g)
    h1 = hn.astype(BF16)
    h2 = (hn - h1.astype(F32)).astype(BF16)
    logits = _dot(h1, w1) + _dot(h1, w2) + _dot(h2, w1) + br

    lane = lax.broadcasted_iota(I32, (tt, 128), 1).astype(F32)
    neg = -jnp.inf
    gl = jnp.where(lane < MOE_GROUPS, logits, neg)
    gmax = jnp.max(gl, axis=-1, keepdims=True)
    g_top = jnp.min(jnp.where(gl == gmax, lane, 128.0), axis=-1, keepdims=True)
    p_g = 1.0 / jnp.sum(jnp.exp(gl - gmax), axis=-1, keepdims=True)

    first = _LANE_E0 + MOE_EPG * g_top
    in_grp = (lane >= first) & (lane < first + MOE_EPG)
    el = jnp.where(in_grp, logits, neg)
    emax = jnp.max(el, axis=-1, keepdims=True)
    ee = jnp.exp(el - emax)
    prob = jnp.where(in_grp, ee / jnp.sum(ee, axis=-1, keepdims=True), -1.0)
    p1 = jnp.max(prob, axis=-1, keepdims=True)
    i1 = jnp.min(jnp.where(prob == p1, lane, 128.0), axis=-1, keepdims=True)
    prob2 = jnp.where(lane == i1, -1.0, prob)
    p2 = jnp.max(prob2, axis=-1, keepdims=True)
    i2 = jnp.min(jnp.where(prob2 == p2, lane, 128.0), axis=-1, keepdims=True)
    psum = p1 + p2
    gate1 = p_g * (p1 / psum)
    gate2 = p_g * (p2 / psum)

    sel1 = lane == i1
    sel2 = lane == i2
    onehot = jnp.where(sel1 | sel2, 1.0, 0.0)
    before = _dot(ltri, onehot.astype(BF16))
    cnt = jnp.sum(onehot, axis=0, keepdims=True)
    pad = jnp.floor((cnt + (GRANULE - 1)) * (1.0 / GRANULE)) * GRANULE
    pad8 = jnp.broadcast_to(pad, (8, 128))
    seg_off = _dot(pad8.astype(BF16), utri)
    local = before + seg_off[0:1, :]
    rank1 = jnp.sum(jnp.where(sel1, local, 0.0), axis=-1, keepdims=True)
    rank2 = jnp.sum(jnp.where(sel2, local, 0.0), axis=-1, keepdims=True)

    e1 = i1 - _LANE_E0
    e2 = i2 - _LANE_E0
    meta = jnp.where(lane == 0, e1, 0.0)
    meta = jnp.where(lane == 1, e2, meta)
    meta = jnp.where(lane == 2, gate1, meta)
    meta = jnp.where(lane == 3, gate2, meta)
    meta = jnp.where(lane == 4, rank1, meta)
    meta = jnp.where(lane == 5, rank2, meta)
    return meta, pad8


def _granule(ref, g):
    return ref.at[pl.ds(pl.multiple_of(g * GRANULE, GRANULE), GRANULE), :]


def _dispatch_kernel(ngran_ref, nv_ref, dest_ref, gap_ref, x_ref, g_ref, meta_ref, xs_hbm,
                     buf, zbuf, sem, zsem, nstart):
    i = pl.program_id(0)
    n_tiles = pl.num_programs(0) - 1
    slot = i % 2
    tt = x_ref.shape[0]
    bm = zbuf.shape[0]
    n_blocks = xs_hbm.shape[0] // bm

    def out_copy(sl, g, d):
        return pltpu.make_async_copy(_granule(buf.at[sl], g), _granule(xs_hbm, d), sem.at[sl])

    def drain(sl):
        def body(_, carry):
            out_copy(sl, 0, 0).wait()
            return carry
        lax.fori_loop(0, nstart[sl], body, 0)

    @pl.when(i == 0)
    def _():
        nstart[0] = 0
        nstart[1] = 0

    @pl.when(i < n_tiles)
    def _():
        drain(slot)
        hn = _rms(x_ref[...], g_ref[...]).astype(BF16)
        mt = meta_ref[...].T
        rows = lax.broadcasted_iota(I32, (LOCAL_ROWS, tt), 0).astype(F32)
        onehot = jnp.where((rows == mt[4:5, :]) | (rows == mt[5:6, :]), 1.0, 0.0).astype(BF16)
        buf[slot] = _dot(onehot, hn)
        ng = ngran_ref[i]

        def body(k, carry):
            g = 2 * k
            out_copy(slot, g, dest_ref[0, 0, g]).start(priority=0)

            @pl.when(g + 1 < ng)
            def _():
                out_copy(slot, g + 1, dest_ref[0, 0, g + 1]).start(priority=1)
            return carry
        lax.fori_loop(0, lax.shift_right_logical(ng + 1, 1), body, 0)
        nstart[slot] = ng

    @pl.when(i == n_tiles)
    def _():
        drain(0)
        drain(1)
        zbuf[...] = jnp.zeros_like(zbuf)

        def gap_copy(d):
            return pltpu.make_async_copy(_granule(zbuf, 0), _granule(xs_hbm, d), zsem.at[0])

        def tail_copy(b):
            return pltpu.make_async_copy(
                zbuf, xs_hbm.at[pl.ds(pl.multiple_of(b * bm, bm), bm), :], zsem.at[1])

        def each_gap(fn):
            def body(j, carry):
                d = gap_ref[0, 0, j]

                @pl.when(d >= 0)
                def _():
                    fn(gap_copy(d))
                return carry
            lax.fori_loop(0, gap_ref.shape[2], body, 0)

        def each_tail(fn):
            def body(b, carry):
                fn(tail_copy(b))
                return carry
            lax.fori_loop(nv_ref[0], n_blocks, body, 0)

        each_gap(lambda cp: cp.start())
        each_tail(lambda cp: cp.start())
        each_gap(lambda cp: cp.wait())
        each_tail(lambda cp: cp.wait())


def _expert_kernel(bexp_ref, nv_ref, xs_ref, wg_ref, wu_ref, wd_ref, ys_ref,
                   wgu_b, wd_b, xb_scr, act_scr):
    b = pl.program_id(0)
    ff = wg_ref.shape[1]

    @pl.when(b < nv_ref[0])
    def _():
        @pl.when((b == 0) | (bexp_ref[b] != bexp_ref[jnp.maximum(b - 1, 0)]))
        def _():
            for k0 in range(0, wg_ref.shape[0], 256):
                wgu_b[k0:k0 + 256, 0:ff] = wg_ref[k0:k0 + 256, :].astype(BF16)
                wgu_b[k0:k0 + 256, ff:2 * ff] = wu_ref[k0:k0 + 256, :].astype(BF16)
            for k0 in range(0, ff, 64):
                wd_b[k0:k0 + 64, :] = wd_ref[k0:k0 + 64, :].astype(BF16)

        bm = xs_ref.shape[0]
        for m0 in range(0, bm, 256):
            for r0 in range(m0, m0 + 256, 128):
                xb_scr[r0:r0 + 128, :] = xs_ref[r0:r0 + 128, :].astype(BF16)
            h = _dot(xb_scr[m0:m0 + 256, :], wgu_b[...])
            for r0 in range(0, 256, 128):
                for c0 in range(0, ff, 128):
                    hg = h[r0:r0 + 128, c0:c0 + 128]
                    hu = h[r0:r0 + 128, ff + c0:ff + c0 + 128]
                    act_scr[m0 + r0:m0 + r0 + 128, c0:c0 + 128] = (_silu(hg) * hu).astype(BF16)
            ys_ref[m0:m0 + 256, :] = _dot(act_scr[m0:m0 + 256, :], wd_b[...])


def _combine_kernel(ngran_ref, src_ref, srcn_ref, x_ref, meta_ref, gf_ref, ys_hbm, o_ref, buf, sem,
                    *, final):
    i = pl.program_id(0)
    n = pl.num_programs(0)
    slot = i % 2
    tt = x_ref.shape[0]

    def in_copy(sl, g, d):
        return pltpu.make_async_copy(_granule(ys_hbm, d), _granule(buf.at[sl], g), sem.at[sl])

    def gather(idx_ref, sl, ng):
        def body(k, carry):
            g = 2 * k
            in_copy(sl, g, idx_ref[0, 0, g]).start(priority=0)

            @pl.when(g + 1 < ng)
            def _():
                in_copy(sl, g + 1, idx_ref[0, 0, g + 1]).start(priority=1)
            return carry
        lax.fori_loop(0, lax.shift_right_logical(ng + 1, 1), body, 0)

    @pl.when(i == 0)
    def _():
        buf[...] = jnp.zeros_like(buf)
        gather(src_ref, 0, ngran_ref[0])

    @pl.when(i + 1 < n)
    def _():
        gather(srcn_ref, 1 - slot, ngran_ref[i + 1])

    def wait_body(_, carry):
        in_copy(slot, 0, 0).wait()
        return carry
    lax.fori_loop(0, ngran_ref[i], wait_body, 0)

    ys = buf[slot].astype(BF16)
    meta = meta_ref[...]
    cols = lax.broadcasted_iota(I32, (tt, LOCAL_ROWS), 1).astype(F32)
    pick1 = jnp.where(cols == meta[:, 4:5], 1.0, 0.0).astype(BF16)
    pick2 = jnp.where(cols == meta[:, 5:6], 1.0, 0.0).astype(BF16)
    y = x_ref[...] + meta[:, 2:3] * _dot(pick1, ys) + meta[:, 3:4] * _dot(pick2, ys)
    if final:
        y = _rms(y, gf_ref[...])
    o_ref[...] = y


def _moe(x2d, norm_g, w_grp, b_grp, w_rt, b_rt, w_gate, w_up, w_down, norm_final, *, layer):
    t = x2d.shape[0]
    tt = MOE_TILE
    bm = EXPERT_BLOCK
    n_tiles = t // tt
    n_exp = MOE_EXPERTS
    lane_pad = 128 - MOE_GROUPS - n_exp
    wr = jnp.concatenate([w_grp, w_rt, jnp.zeros((D_MODEL, lane_pad), F32)], axis=1)
    br = jnp.concatenate([b_grp, b_rt, jnp.zeros((lane_pad,), F32)]).reshape(1, 128)
    ltri = jnp.asarray(np.tril(np.ones((tt, tt)), -1), BF16)
    utri = jnp.asarray(np.triu(np.ones((128, 128)), 1), BF16)
    g2d = norm_g.reshape(1, D_MODEL)
    arb = pltpu.CompilerParams(dimension_semantics=("arbitrary",), vmem_limit_bytes=VMEM_LIMIT)

    meta, cnt = pl.pallas_call(
        _router_kernel,
        grid=(n_tiles // ROUTER_TILES,),
        in_specs=[pl.BlockSpec((ROUTER_TILES * tt, D_MODEL), lambda i: (i, 0)),
                  _const_spec((1, D_MODEL)),
                  _const_spec((D_MODEL, 128)),
                  _const_spec((1, 128)),
                  _const_spec((tt, tt)),
                  _const_spec((128, 128))],
        out_specs=(pl.BlockSpec((ROUTER_TILES * tt, 128), lambda i: (i, 0)),
                   pl.BlockSpec((ROUTER_TILES * 8, 128), lambda i: (i, 0))),
        out_shape=(jax.ShapeDtypeStruct((t, 128), F32),
                   jax.ShapeDtypeStruct((n_tiles * 8, 128), F32)),
        compiler_params=arb,
        name="moe_router",
    )(x2d, g2d, wr, br, ltri, utri)

    runs = cnt.reshape(n_tiles, 8, 128)[:, 0, _LANE_E0:_LANE_E0 + n_exp].astype(I32)
    rows_e = jnp.sum(runs, axis=0)
    nblk = (rows_e + bm - 1) // bm
    blk_end = jnp.cumsum(nblk)
    e_start = (blk_end - nblk) * bm
    n_valid = blk_end[-1]
    run_end = jnp.cumsum(runs, axis=1)
    ngran = (run_end[:, -1] // GRANULE).astype(I32)
    shift = e_start[None, :] + (jnp.cumsum(runs, axis=0) - runs) - (run_end - runs)
    g_row = jnp.arange(N_LOCAL_GRAN, dtype=I32) * GRANULE
    e_of_g = jnp.sum((run_end[:, None, :] <= g_row[None, :, None]).astype(I32), axis=-1)
    shift_g = jnp.sum(jnp.where(e_of_g[..., None] == jnp.arange(n_exp, dtype=I32),
                                shift[:, None, :], 0), axis=-1)
    dest = jnp.where(e_of_g < n_exp, (shift_g + g_row[None, :]) // GRANULE, 0)
    dest = dest.astype(I32).reshape(n_tiles, 1, N_LOCAL_GRAN)
    per_blk = bm // GRANULE
    gap = ((e_start + rows_e) // GRANULE)[:, None] + jnp.arange(per_blk, dtype=I32)[None, :]
    gap = jnp.where(gap < ((e_start + nblk * bm) // GRANULE)[:, None], gap, -1)
    gap = gap.astype(I32).reshape(1, 1, n_exp * per_blk)
    n_blocks = (2 * t + n_tiles * n_exp * (GRANULE - 1)) // bm + 1 + n_exp
    blk = jnp.minimum(jnp.arange(n_blocks, dtype=I32), n_valid - 1)
    block_expert = jnp.sum((blk[:, None] >= blk_end[None, :]).astype(I32), axis=1).astype(I32)
    n_valid = n_valid.reshape(1).astype(I32)

    last = n_tiles - 1
    xs = pl.pallas_call(
        _dispatch_kernel,
        grid_spec=pltpu.PrefetchScalarGridSpec(
            num_scalar_prefetch=2,
            grid=(n_tiles + 1,),
            in_specs=[pl.BlockSpec((1, 1, N_LOCAL_GRAN),
                                   lambda i, ng, nv: (jnp.minimum(i, last), 0, 0),
                                   memory_space=pltpu.SMEM),
                      pl.BlockSpec((1, 1, n_exp * per_blk), lambda i, ng, nv: (0, 0, 0),
                                   memory_space=pltpu.SMEM),
                      pl.BlockSpec((tt, D_MODEL), lambda i, ng, nv: (jnp.minimum(i, last), 0)),
                      _const_spec((1, D_MODEL)),
                      pl.BlockSpec((tt, 128), lambda i, ng, nv: (jnp.minimum(i, last), 0))],
            out_specs=pl.BlockSpec(memory_space=pl.ANY),
            scratch_shapes=[pltpu.VMEM((2, LOCAL_ROWS, D_MODEL), F32),
                            pltpu.VMEM((bm, D_MODEL), F32),
                            pltpu.SemaphoreType.DMA((2,)),
                            pltpu.SemaphoreType.DMA((2,)),
                            pltpu.SMEM((2,), I32)]),
        out_shape=jax.ShapeDtypeStruct((n_blocks * bm, D_MODEL), F32),
        compiler_params=arb,
        name="moe_dispatch",
    )(ngran, n_valid, dest, gap, x2d, g2d, meta)

    w_spec = lambda shape: pl.BlockSpec((None, None) + shape,
                                        lambda b, be, nv: (layer, be[b], 0, 0))
    row_blk = pl.BlockSpec((bm, D_MODEL), lambda b, be, nv: (jnp.minimum(b, nv[0] - 1), 0))
    ys = pl.pallas_call(
        _expert_kernel,
        grid_spec=pltpu.PrefetchScalarGridSpec(
            num_scalar_prefetch=2,
            grid=(n_blocks,),
            in_specs=[row_blk,
                      w_spec((D_MODEL, MOE_D_FF)),
                      w_spec((D_MODEL, MOE_D_FF)),
                      w_spec((MOE_D_FF, D_MODEL))],
            out_specs=row_blk,
            scratch_shapes=[pltpu.VMEM((D_MODEL, 2 * MOE_D_FF), BF16),
                            pltpu.VMEM((MOE_D_FF, D_MODEL), BF16),
                            pltpu.VMEM((bm, D_MODEL), BF16),
                            pltpu.VMEM((bm, MOE_D_FF), BF16)]),
        out_shape=jax.ShapeDtypeStruct((n_blocks * bm, D_MODEL), F32),
        input_output_aliases={2: 0},
        compiler_params=arb,
        name="moe_experts",
    )(block_expert, n_valid, xs, w_gate, w_up, w_down)

    final = norm_final is not None
    gf = (norm_final if final else jnp.ones((D_MODEL,), F32)).reshape(1, D_MODEL)
    src_spec = lambda nxt: pl.BlockSpec(
        (1, 1, N_LOCAL_GRAN), lambda i, ng: (jnp.minimum(i + nxt, last), 0, 0),
        memory_space=pltpu.SMEM)
    out = pl.pallas_call(
        functools.partial(_combine_kernel, final=final),
        grid_spec=pltpu.PrefetchScalarGridSpec(
            num_scalar_prefetch=1,
            grid=(n_tiles,),
            in_specs=[src_spec(0), src_spec(1),
                      pl.BlockSpec((tt, D_MODEL), lambda i, ng: (i, 0)),
                      pl.BlockSpec((tt, 128), lambda i, ng: (i, 0)),
                      _const_spec((1, D_MODEL)),
                      pl.BlockSpec(memory_space=pl.ANY)],
            out_specs=pl.BlockSpec((tt, D_MODEL), lambda i, ng: (i, 0)),
            scratch_shapes=[pltpu.VMEM((2, LOCAL_ROWS, D_MODEL), F32),
                            pltpu.SemaphoreType.DMA((2,))]),
        out_shape=jax.ShapeDtypeStruct((t, D_MODEL), F32),
        compiler_params=arb,
        name="moe_combine",
    )(ngran, dest, dest, x2d, meta, gf, ys)
    return out


def _trunk(x, conv_prev, ssm_prev, p, *, ssd_nsub, ssd_ns, want_v):
    n_seq, seq_len, _ = x.shape
    q = seq_len if seq_len < SSD_CHUNK else SSD_CHUNK
    x, conv_new, ssm_new = _ssd_layer(
        x, conv_prev, ssm_prev, p["norm_mix"][0], p["ssd_w_in"][0], p["ssd_conv_w"][0],
        p["ssd_conv_b"][0], p["ssd_dt_bias"][0], p["ssd_a_log"][0], p["ssd_d"][0], p["ssd_norm"][0],
        p["ssd_w_out"][0], nsub=ssd_nsub, ns=ssd_ns, q=q)
    x2d = x.reshape(n_seq * seq_len, D_MODEL)
    x2d = _moe(x2d, p["norm_ffn"][0], p["moe_w_group"][0], p["moe_b_group"][0], p["moe_w_router"][0],
               p["moe_b_router"][0], p["moe_w_gate"], p["moe_w_up"], p["moe_w_down"], None, layer=0)
    x2d, v = _sg_layer(x2d, p["norm_mix"][1], p["sg_w_in"][0], p["sg_b_in"][0], p["sg_ln_g"][0],
                       p["sg_ln_b"][0], p["sg_w_s"][0], p["sg_b_s"][0], p["sg_w_out"][0],
                       seq_len=seq_len, want_v=want_v)
    y2d = _moe(x2d, p["norm_ffn"][1], p["moe_w_group"][1], p["moe_b_group"][1], p["moe_w_router"][1],
               p["moe_b_router"][1], p["moe_w_gate"], p["moe_w_up"], p["moe_w_down"],
               p["norm_final"], layer=1)
    y = y2d.reshape(n_seq, seq_len, D_MODEL)
    if want_v:
        v = v.reshape(1, n_seq, seq_len, SG_WIDTH)
    return y, conv_new, ssm_new, v


def kernel(x_prompt, x_sample, state_ssm, state_conv, norm_mix, norm_ffn, norm_final, ssd_w_in, ssd_conv_w, ssd_conv_b, ssd_dt_bias, ssd_a_log, ssd_d, ssd_norm, ssd_w_out, sg_w_in, sg_b_in, sg_ln_g, sg_ln_b, sg_w_s, sg_b_s, sg_w_out, moe_w_group, moe_b_group, moe_w_router, moe_b_router, moe_w_gate, moe_w_up, moe_w_down):
    p = dict(norm_mix=norm_mix, norm_ffn=norm_ffn, norm_final=norm_final, ssd_w_in=ssd_w_in,
             ssd_conv_w=ssd_conv_w, ssd_conv_b=ssd_conv_b, ssd_dt_bias=ssd_dt_bias,
             ssd_a_log=ssd_a_log, ssd_d=ssd_d, ssd_norm=ssd_norm, ssd_w_out=ssd_w_out,
             sg_w_in=sg_w_in, sg_b_in=sg_b_in, sg_ln_g=sg_ln_g, sg_ln_b=sg_ln_b, sg_w_s=sg_w_s,
             sg_b_s=sg_b_s, sg_w_out=sg_w_out, moe_w_group=moe_w_group, moe_b_group=moe_b_group,
             moe_w_router=moe_w_router, moe_b_router=moe_b_router, moe_w_gate=moe_w_gate,
             moe_w_up=moe_w_up, moe_w_down=moe_w_down)
    nb = x_prompt.shape[0]
    conv0 = jnp.zeros((1, nb, CONV_W - 1, CONV_DIM), F32)
    ssm0 = jnp.zeros((1, nb, N_HEADS, HEAD_DIM, D_STATE), F32)
    y_p, conv_p, ssm_p, _ = _trunk(x_prompt, conv0, ssm0, p, ssd_nsub=2, ssd_ns=1, want_v=False)
    y_s, conv_s, ssm_s, v_s = _trunk(x_sample, state_conv, state_ssm, p, ssd_nsub=1, ssd_ns=4,
                                     want_v=True)
    return (y_p, y_s, ssm_p, conv_p, ssm_s, conv_s, v_s)
```

```python
import functools
import math

import jax
import jax.numpy as jnp
import numpy as np
from jax import lax
from jax.experimental import pallas as pl
from jax.experimental.pallas import tpu as pltpu

F32 = jnp.float32
BF16 = jnp.bfloat16
I32 = jnp.int32

D_MODEL = 1024
N_HEADS = 32
HEAD_DIM = 64
N_GROUPS = 4
D_STATE = 128
D_INNER = N_HEADS * HEAD_DIM
GROUP_W = D_INNER // N_GROUPS
CONV_W = 4
CONV_DIM = D_INNER + 2 * N_GROUPS * D_STATE
DT_PAD = 128
SSD_CHUNK = 128
PROJ_CHUNK = 256
SG_WIDTH = 2 * D_MODEL
SG_GROUPS = 8
SG_GROUP_DIM = SG_WIDTH // SG_GROUPS
SG_CHUNK = 128
SG_ROWS = 256
MOE_GROUPS = 4
MOE_EPG = 8
MOE_EXPERTS = MOE_GROUPS * MOE_EPG
MOE_D_FF = 256
NORM_EPS = 1e-6
LN_EPS = 1e-5

MOE_TILE = 256
ROUTER_TILES = 2


def _expert_block(n_tokens):
    return 512 if 2 * n_tokens >= 512 * MOE_EXPERTS else 256


GRANULE = 16
LOCAL_ROWS = 2 * MOE_TILE + MOE_EXPERTS * GRANULE
N_LOCAL_GRAN = LOCAL_ROWS // GRANULE
VMEM_LIMIT = 56 * 1024 * 1024


def _sigmoid(x):
    return 0.5 * (jnp.tanh(0.5 * x) + 1.0)


def _silu(x):
    return x * _sigmoid(x)


def _softplus(x):
    return jnp.maximum(x, 0.0) + jnp.log(1.0 + jnp.exp(-jnp.abs(x)))


def _gelu_tanh(x):
    c = math.sqrt(2.0 / math.pi)
    return x * (0.5 * (1.0 + jnp.tanh(c * (x + 0.044715 * (x * x * x)))))


def _rms(x, g):
    return x * lax.rsqrt(jnp.mean(x * x, axis=-1, keepdims=True) + NORM_EPS) * g


def _split3(x):
    a = x.astype(BF16)
    r = x - a.astype(F32)
    b = r.astype(BF16)
    c = (r - b.astype(F32)).astype(BF16)
    return a, b, c


def _dot(a, b):
    return jnp.dot(a, b, preferred_element_type=F32)


def _dot_nt(a, b):
    return lax.dot_general(a, b, (((1,), (1,)), ((), ())), preferred_element_type=F32)


def _dot_tn(a, b):
    return lax.dot_general(a, b, (((0,), (0,)), ((), ())), preferred_element_type=F32)


def _const_spec(shape):
    nd = len(shape)
    return pl.BlockSpec(shape, lambda *_: (0,) * nd)


def _ssd_kernel(x_ref, g_ref, win_ref, cw_ref, cb_ref, dtb_ref, alog_ref, dsk_ref, ng_ref,
                wout_ref, tri_ref, ones_ref, cin_ref, sin_ref,
                xo_ref, cout_ref, sout_ref,
                pend_x, pend_z, pend_xbc, pend_dt, xres_scr, zg_scr,
                prev_scr, xc_scr, yoff_scr, y_scr, xw_scr, hn_scr, yn_scr,
                *, nsub, ns, q, n_chunks):
    i = pl.program_id(0)
    first_chunk = lax.rem(jnp.maximum(i - 1, 0), n_chunks) == 0
    nseq = nsub * ns
    sb = ns * q
    r = nsub * sb

    @pl.when(i == 0)
    def _():
        pend_x[...] = jnp.zeros_like(pend_x)
        pend_z[...] = jnp.zeros_like(pend_z)
        pend_xbc[...] = jnp.zeros_like(pend_xbc)
        pend_dt[...] = jnp.zeros_like(pend_dt)

    tr = min(r, 128)
    row_tiles = [slice(a, a + tr) for a in range(0, r, tr)]

    def col_tiles(total, rows=tr):
        w = min(total, max(128, (32 * 1024) // rows))
        return [slice(c, c + w) for c in range(0, total, w)]

    for rt in row_tiles:
        for ct in col_tiles(D_MODEL):
            xres_scr[rt, ct] = pend_x[rt, ct]
        for ct in col_tiles(D_INNER):
            zg_scr[rt, ct] = _silu(pend_z[rt, ct])

    for s in range(nseq):
        for ct in col_tiles(D_MODEL, q):
            pend_x[s * q:(s + 1) * q, ct] = x_ref[s, :, ct]
    for rt in row_tiles:
        ss = None
        for ct in col_tiles(D_MODEL):
            xv = pend_x[rt, ct]
            part = jnp.sum(xv * xv, axis=-1, keepdims=True)
            ss = part if ss is None else ss + part
        scale = lax.rsqrt(ss * (1.0 / D_MODEL) + NORM_EPS)
        for ct in col_tiles(D_MODEL):
            hn_scr[rt, ct] = (pend_x[rt, ct] * scale * g_ref[:, ct]).astype(BF16)

    def proj_chunk(dst, dst_col, w_col, width):
        def run():
            dst[:, dst_col:dst_col + width] = _dot(hn_scr[...], win_ref[:, w_col:w_col + width])
        return run
    z_chunks = [proj_chunk(pend_z, c, c, PROJ_CHUNK) for c in range(0, D_INNER, PROJ_CHUNK)]
    scan_chunks = [proj_chunk(pend_xbc, c, D_INNER + c, PROJ_CHUNK)
                   for c in range(0, CONV_DIM, PROJ_CHUNK)]
    scan_chunks.append(proj_chunk(pend_dt, 0, D_INNER + CONV_DIM, DT_PAD))

    @pl.when(first_chunk)
    def _():
        sout_ref[...] = sin_ref[...]
        prev_scr[...] = jnp.zeros_like(prev_scr)
        for s in range(nseq):
            for k in range(CONV_W - 1):
                prev_scr[pl.ds(s * 8 + 5 + k, 1), :] = cin_ref[s, pl.ds(k, 1), :]

    sub8 = lax.broadcasted_iota(I32, (8, 1), 0)
    for s in range(nseq):
        srows = slice(s * q, (s + 1) * q)
        for ct in col_tiles(CONV_DIM, 4 * q):
            xq = pend_xbc[srows, ct]
            hist = prev_scr[s * 8:(s + 1) * 8, ct]
            acc = cb_ref[:, ct] + cw_ref[pl.ds(CONV_W - 1, 1), ct] * xq
            for j in range(1, CONV_W):
                sh = pltpu.roll(xq, j, 0)
                head = jnp.where(sub8 < j, pltpu.roll(hist, j, 0), sh[0:8, :])
                sh = head if q == 8 else jnp.concatenate([head, sh[8:, :]], axis=0)
                acc = acc + cw_ref[pl.ds(CONV_W - 1 - j, 1), ct] * sh
            xc_scr[srows, ct] = _silu(acc)
            last = xq[q - 8:q, :]
            prev_scr[s * 8:(s + 1) * 8, ct] = last
            for k in range(CONV_W - 1):
                cout_ref[s, pl.ds(k, 1), ct] = last[5 + k:6 + k, :]
        for _ in range(-(-len(z_chunks) // (nseq - s))):
            z_chunks.pop(0)()
    n_pairs = nsub * (N_HEADS // 2)
    stride = n_pairs // len(scan_chunks)
    emit_at = {k * stride: ch for k, ch in enumerate(scan_chunks)}

    tri = tri_ref[...]
    trib = tri.astype(BF16)
    onesb = ones_ref[...].astype(BF16)
    mask = tri > 0.5
    rowseq = lax.shift_right_logical(lax.broadcasted_iota(I32, (sb, 1), 0), int(math.log2(q)))
    lo = lax.broadcasted_iota(I32, (sb, 128), 1) < HEAD_DIM
    neg_a = -jnp.exp(alog_ref[...])

    for u in range(nsub):
        rows = slice(u * sb, (u + 1) * sb)
        xs_ref = xc_scr.at[rows, 0:D_INNER]

        def b_of(g):
            return xc_scr[rows, D_INNER + g * D_STATE:D_INNER + (g + 1) * D_STATE]

        def c_of(g):
            c0 = D_INNER + (N_GROUPS + g) * D_STATE
            return xc_scr[rows, c0:c0 + D_STATE].astype(BF16)

        dt = _softplus(pend_dt[rows, :] + dtb_ref[...])
        d1, d2, d3 = _split3(dt * neg_a)
        cs = _dot(trib, d1) + _dot(trib, d2) + _dot(trib, d3)
        cl = _dot(onesb, d1) + _dot(onesb, d2) + _dot(onesb, d3)
        ecs = jnp.exp(cs)
        wgt = dt * jnp.exp(cl - cs)
        ecl = jnp.exp(cl)
        cs_t = cs.T
        dt_t = dt.T

        for g in range(N_GROUPS):
            cg = c_of(g)
            acc = None
            for s in range(ns):
                st = sout_ref[u * ns + s, g * GROUP_W:(g + 1) * GROUP_W, :].astype(BF16)
                yo = _dot_nt(cg, st)
                if ns > 1:
                    yo = jnp.where(rowseq == s, yo, 0.0)
                acc = yo if acc is None else acc + yo
            yoff_scr[rows, g * GROUP_W:(g + 1) * GROUP_W] = acc

        for g in range(N_GROUPS):
            sc = _dot_nt(c_of(g), b_of(g).astype(BF16))
            for jj in range(N_HEADS // N_GROUPS // 2):
                j = g * (N_HEADS // N_GROUPS // 2) + jj
                ms = []
                for h in (2 * j, 2 * j + 1):
                    diff = cs[:, h:h + 1] - cs_t[h:h + 1, :]
                    dec = jnp.exp(jnp.where(mask, diff, -jnp.inf))
                    ms.append((sc * dec * dt_t[h:h + 1, :]).astype(BF16))
                lhs = jnp.concatenate(ms, axis=1)
                cols = slice(j * 128, (j + 1) * 128)
                xp = xs_ref[:, cols]
                rhs = jnp.concatenate([jnp.where(lo, xp, 0.0).astype(BF16),
                                       jnp.where(lo, 0.0, xp).astype(BF16)], axis=0)
                yd = _dot(lhs, rhs)
                ecs_p = jnp.where(lo, ecs[:, 2 * j:2 * j + 1], ecs[:, 2 * j + 1:2 * j + 2])
                wgt_p = jnp.where(lo, wgt[:, 2 * j:2 * j + 1], wgt[:, 2 * j + 1:2 * j + 2])
                y_scr[rows, cols] = yd + yoff_scr[rows, cols] * ecs_p + xp * dsk_ref[:, cols]
                xw_scr[rows, cols] = (xp * wgt_p).astype(BF16)
                pair = u * (N_HEADS // 2) + j
                if pair in emit_at:
                    emit_at[pair]()

        for g in range(N_GROUPS):
            xwg = xw_scr[rows, g * GROUP_W:(g + 1) * GROUP_W]
            for s in range(ns):
                bg = b_of(g)
                if ns > 1:
                    bg = jnp.where(rowseq == s, bg, 0.0)
                upd = _dot_tn(xwg, bg.astype(BF16))
                for hh in range(GROUP_W // HEAD_DIM):
                    h = g * (GROUP_W // HEAD_DIM) + hh
                    dec = jnp.broadcast_to(ecl[s * q:s * q + 1, h:h + 1], (HEAD_DIM, D_STATE))
                    hrows = slice(h * HEAD_DIM, (h + 1) * HEAD_DIM)
                    sout_ref[u * ns + s, hrows, :] = (
                        sout_ref[u * ns + s, hrows, :] * dec + upd[hh * HEAD_DIM:(hh + 1) * HEAD_DIM, :])

    for rt in row_tiles:
        for g in range(N_GROUPS):
            cts = [slice(g * GROUP_W + c.start, g * GROUP_W + c.stop) for c in col_tiles(GROUP_W)]
            ss = None
            for ct in cts:
                yz = y_scr[rt, ct] * zg_scr[rt, ct]
                part = jnp.sum(yz * yz, axis=-1, keepdims=True)
                ss = part if ss is None else ss + part
            scale = lax.rsqrt(ss * (1.0 / GROUP_W) + NORM_EPS)
            for ct in cts:
                yn_scr[rt, ct] = (y_scr[rt, ct] * zg_scr[rt, ct] * scale * ng_ref[:, ct]).astype(BF16)

    for c0 in range(0, D_MODEL, PROJ_CHUNK):
        ct = slice(c0, c0 + PROJ_CHUNK)
        o = xres_scr[:, ct] + _dot(yn_scr[...], wout_ref[:, ct])
        for s in range(nseq):
            xo_ref[s, :, ct] = o[s * q:(s + 1) * q, :]


def _ssd_layer(x, conv_prev, ssm_prev, norm_g, w_in, conv_w, conv_b, dt_bias, a_log, d_skip,
               norm_y, w_out, *, nsub, ns, q):
    n_seq, seq_len, _ = x.shape
    nseq = nsub * ns
    sb = ns * q
    r = nsub * sb
    n_chunks = seq_len // q
    pad = DT_PAD - N_HEADS
    win = jnp.concatenate([w_in, jnp.zeros((D_MODEL, pad), F32)], axis=1).astype(BF16)
    dtb = jnp.pad(dt_bias, (0, pad)).reshape(1, DT_PAD)
    alog = jnp.pad(a_log, (0, pad)).reshape(1, DT_PAD)
    dsk = jnp.repeat(d_skip, HEAD_DIM).reshape(1, D_INNER)
    blk = np.kron(np.eye(ns), np.ones((q, q)))
    tri = jnp.asarray(blk * np.tril(np.ones((sb, sb))), F32)
    ones = jnp.asarray(blk, F32)
    state = ssm_prev.reshape(n_seq, D_INNER, D_STATE)
    conv_prev = conv_prev.reshape(n_seq, CONV_W - 1, CONV_DIM)

    kern = functools.partial(_ssd_kernel, nsub=nsub, ns=ns, q=q, n_chunks=n_chunks)
    out_shape = (jax.ShapeDtypeStruct(x.shape, F32),
                 jax.ShapeDtypeStruct((n_seq, CONV_W - 1, CONV_DIM), F32),
                 jax.ShapeDtypeStruct((n_seq, D_INNER, D_STATE), F32))
    n_steps = (n_seq // nseq) * n_chunks
    nxt = lambda i: jnp.minimum(i, n_steps - 1)
    cur = lambda i: jnp.maximum(i - 1, 0)
    in_row_spec = pl.BlockSpec((nseq, q, D_MODEL),
                               lambda i: (nxt(i) // n_chunks, nxt(i) % n_chunks, 0))
    row_spec = pl.BlockSpec((nseq, q, D_MODEL),
                            lambda i: (cur(i) // n_chunks, cur(i) % n_chunks, 0))
    conv_spec = pl.BlockSpec((nseq, CONV_W - 1, CONV_DIM), lambda i: (cur(i) // n_chunks, 0, 0))
    state_spec = pl.BlockSpec((nseq, D_INNER, D_STATE), lambda i: (cur(i) // n_chunks, 0, 0))
    in_specs = [in_row_spec,
                _const_spec((1, D_MODEL)),
                _const_spec(win.shape),
                _const_spec((CONV_W, CONV_DIM)),
                _const_spec((1, CONV_DIM)),
                _const_spec((1, DT_PAD)),
                _const_spec((1, DT_PAD)),
                _const_spec((1, D_INNER)),
                _const_spec((1, D_INNER)),
                _const_spec((D_INNER, D_MODEL)),
                _const_spec((sb, sb)),
                _const_spec((sb, sb)),
                conv_spec, state_spec]
    x_new, conv_new, state_new = pl.pallas_call(
        kern,
        grid=(n_steps + 1,),
        in_specs=in_specs,
        out_specs=(row_spec, conv_spec, state_spec),
        out_shape=out_shape,
        scratch_shapes=[pltpu.VMEM((r, D_MODEL), F32),
                        pltpu.VMEM((r, D_INNER), F32),
                        pltpu.VMEM((r, CONV_DIM), F32),
                        pltpu.VMEM((r, DT_PAD), F32),
                        pltpu.VMEM((r, D_MODEL), F32),
                        pltpu.VMEM((r, D_INNER), F32),
                        pltpu.VMEM((nseq * 8, CONV_DIM), F32),
                        pltpu.VMEM((r, CONV_DIM), F32),
                        pltpu.VMEM((r, D_INNER), F32),
                        pltpu.VMEM((r, D_INNER), F32),
                        pltpu.VMEM((r, D_INNER), BF16),
                        pltpu.VMEM((r, D_MODEL), BF16),
                        pltpu.VMEM((r, D_INNER), BF16)],
        compiler_params=pltpu.CompilerParams(
            dimension_semantics=("arbitrary",), vmem_limit_bytes=VMEM_LIMIT),
        name="ssd_layer",
    )(x, norm_g.reshape(1, D_MODEL), win, conv_w, conv_b.reshape(1, CONV_DIM), dtb, alog, dsk,
      norm_y.reshape(1, D_INNER), w_out.astype(BF16), tri, ones, conv_prev, state)
    return (x_new, conv_new.reshape(1, n_seq, CONV_W - 1, CONV_DIM),
            state_new.reshape(1, n_seq, N_HEADS, HEAD_DIM, D_STATE))


def _sg_kernel(x_ref, g_ref, win_ref, bin_ref, lng_ref, lnb_ref, wmix_ref, bmix_ref, wout_ref,
               xo_ref, *rest, r, want_v):
    v_ref = rest[0] if want_v else None
    hn_scr, uv_scr, vb_scr, um_scr = rest[-4:]
    row_tiles = [slice(a, a + 128) for a in range(0, r, 128)]
    col128 = lambda total: [slice(c, c + 128) for c in range(0, total, 128)]

    for rt in row_tiles:
        ss = None
        for ct in col128(D_MODEL):
            xv = x_ref[rt, ct]
            part = jnp.sum(xv * xv, axis=-1, keepdims=True)
            ss = part if ss is None else ss + part
        scale = lax.rsqrt(ss * (1.0 / D_MODEL) + NORM_EPS)
        for ct in col128(D_MODEL):
            hn_scr[rt, ct] = (x_ref[rt, ct] * scale * g_ref[:, ct]).astype(BF16)

    for c0 in range(0, 2 * SG_WIDTH, PROJ_CHUNK):
        h = _dot(hn_scr[...], win_ref[:, c0:c0 + PROJ_CHUNK])
        for rt in row_tiles:
            for cc in range(0, PROJ_CHUNK, 128):
                ct = slice(c0 + cc, c0 + cc + 128)
                uv_scr[rt, ct] = _gelu_tanh(h[rt, cc:cc + 128] + bin_ref[:, ct])

    for rt in row_tiles:
        vcols = [slice(SG_WIDTH + c.start, SG_WIDTH + c.stop) for c in col128(SG_WIDTH)]
        tot = None
        for ct in vcols:
            part = jnp.sum(uv_scr[rt, ct], axis=-1, keepdims=True)
            tot = part if tot is None else tot + part
        mu = tot * (1.0 / SG_WIDTH)
        ss = None
        for ct in vcols:
            vc = uv_scr[rt, ct] - mu
            part = jnp.sum(vc * vc, axis=-1, keepdims=True)
            ss = part if ss is None else ss + part
        scale = lax.rsqrt(ss * (1.0 / SG_WIDTH) + LN_EPS)
        for ct, c in zip(vcols, col128(SG_WIDTH)):
            vn = (uv_scr[rt, ct] - mu) * scale * lng_ref[:, c] + lnb_ref[:, c]
            if want_v:
                v_ref[rt, c] = vn
            vb_scr[rt, c] = vn.astype(BF16)

    for rt in row_tiles:
        for g in range(SG_GROUPS):
            cols = slice(g * SG_GROUP_DIM, (g + 1) * SG_GROUP_DIM)
            mixed = _dot(wmix_ref[g], vb_scr[rt, cols]) + bmix_ref[:, cols]
            um_scr[rt, cols] = (uv_scr[rt, cols] * mixed).astype(BF16)

    for c0 in range(0, D_MODEL, PROJ_CHUNK):
        ct = slice(c0, c0 + PROJ_CHUNK)
        xo_ref[:, ct] = x_ref[:, ct] + _dot(um_scr[...], wout_ref[:, ct])


def _sg_layer(x2d, norm_g, w_in, b_in, ln_g, ln_b, w_s, b_s, w_out, *, seq_len, want_v):
    r = SG_ROWS
    t = x2d.shape[0]
    q = min(seq_len, SG_CHUNK)
    reps = SG_CHUNK // q
    ws = jnp.tril(w_s)[:, :q, :q]
    wmix = jnp.einsum("ab,gts->gatbs", jnp.eye(reps, dtype=F32), ws)
    wmix = wmix.reshape(SG_GROUPS, SG_CHUNK, SG_CHUNK)
    bmix = jnp.tile(jnp.repeat(b_s.T[:q], SG_GROUP_DIM, axis=1), (reps, 1))
    row_spec = pl.BlockSpec((r, D_MODEL), lambda i: (i, 0))
    v_spec = pl.BlockSpec((r, SG_WIDTH), lambda i: (i, 0))
    out_shape = [jax.ShapeDtypeStruct(x2d.shape, F32)]
    out_specs = [row_spec]
    if want_v:
        out_shape.append(jax.ShapeDtypeStruct((t, SG_WIDTH), F32))
        out_specs.append(v_spec)
    outs = pl.pallas_call(
        functools.partial(_sg_kernel, r=r, want_v=want_v),
        grid=(t // r,),
        scratch_shapes=[pltpu.VMEM((r, D_MODEL), BF16),
                        pltpu.VMEM((r, 2 * SG_WIDTH), F32),
                        pltpu.VMEM((r, SG_WIDTH), BF16),
                        pltpu.VMEM((r, SG_WIDTH), BF16)],
        in_specs=[row_spec,
                  _const_spec((1, D_MODEL)),
                  _const_spec((D_MODEL, 2 * SG_WIDTH)),
                  _const_spec((1, 2 * SG_WIDTH)),
                  _const_spec((1, SG_WIDTH)),
                  _const_spec((1, SG_WIDTH)),
                  _const_spec((SG_GROUPS, SG_CHUNK, SG_CHUNK)),
                  _const_spec((SG_CHUNK, SG_WIDTH)),
                  _const_spec((SG_WIDTH, D_MODEL))],
        out_specs=out_specs,
        out_shape=out_shape,
        compiler_params=pltpu.CompilerParams(
            dimension_semantics=("arbitrary",), vmem_limit_bytes=VMEM_LIMIT),
        name="sg_layer",
    )(x2d, norm_g.reshape(1, D_MODEL), w_in.astype(BF16), b_in.reshape(1, 2 * SG_WIDTH),
      ln_g.reshape(1, SG_WIDTH), ln_b.reshape(1, SG_WIDTH), wmix.astype(BF16), bmix,
      w_out.astype(BF16))
    return outs if want_v else (outs[0], None)


_LANE_E0 = MOE_GROUPS


def _router_kernel(x_ref, g_ref, wr_ref, br_ref, ltri_ref, utri_ref, meta_ref, cnt_ref):
    wr = wr_ref[...]
    w1 = wr.astype(BF16)
    w2 = (wr - w1.astype(F32)).astype(BF16)
    tt = MOE_TILE
    for k in range(x_ref.shape[0] // tt):
        rows = slice(k * tt, (k + 1) * tt)
        meta, pad8 = _route_tile(x_ref[rows, :], g_ref[...], w1, w2, br_ref[...], ltri_ref[...],
                                 utri_ref[...])
        meta_ref[rows, :] = meta
        cnt_ref[k * 8:(k + 1) * 8, :] = pad8


def _route_tile(x, g, w1, w2, br, ltri, utri):
    tt = x.shape[0]
    hn = _rms(x, g)
    h1 = hn.astype(BF16)
    h2 = (hn - h1.astype(F32)).astype(BF16)
    logits = _dot(h1, w1) + _dot(h1, w2) + _dot(h2, w1) + br

    lane = lax.broadcasted_iota(I32, (tt, 128), 1).astype(F32)
    neg = -jnp.inf
    gl = jnp.where(lane < MOE_GROUPS, logits, neg)
    gmax = jnp.max(gl, axis=-1, keepdims=True)
    g_top = jnp.min(jnp.where(gl == gmax, lane, 128.0), axis=-1, keepdims=True)
    p_g = 1.0 / jnp.sum(jnp.exp(gl - gmax), axis=-1, keepdims=True)

    first = _LANE_E0 + MOE_EPG * g_top
    in_grp = (lane >= first) & (lane < first + MOE_EPG)
    el = jnp.where(in_grp, logits, neg)
    emax = jnp.max(el, axis=-1, keepdims=True)
    ee = jnp.exp(el - emax)
    prob = jnp.where(in_grp, ee / jnp.sum(ee, axis=-1, keepdims=True), -1.0)
    p1 = jnp.max(prob, axis=-1, keepdims=True)
    i1 = jnp.min(jnp.where(prob == p1, lane, 128.0), axis=-1, keepdims=True)
    prob2 = jnp.where(lane == i1, -1.0, prob)
    p2 = jnp.max(prob2, axis=-1, keepdims=True)
    i2 = jnp.min(jnp.where(prob2 == p2, lane, 128.0), axis=-1, keepdims=True)
    psum = p1 + p2
    gate1 = p_g * (p1 / psum)
    gate2 = p_g * (p2 / psum)

    sel1 = lane == i1
    sel2 = lane == i2
    onehot = jnp.where(sel1 | sel2, 1.0, 0.0)
    before = _dot(ltri, onehot.astype(BF16))
    cnt = jnp.sum(onehot, axis=0, keepdims=True)
    pad = jnp.floor((cnt + (GRANULE - 1)) * (1.0 / GRANULE)) * GRANULE
    pad8 = jnp.broadcast_to(pad, (8, 128))
    seg_off = _dot(pad8.astype(BF16), utri)
    local = before + seg_off[0:1, :]
    rank1 = jnp.sum(jnp.where(sel1, local, 0.0), axis=-1, keepdims=True)
    rank2 = jnp.sum(jnp.where(sel2, local, 0.0), axis=-1, keepdims=True)

    e1 = i1 - _LANE_E0
    e2 = i2 - _LANE_E0
    meta = jnp.where(lane == 0, e1, 0.0)
    meta = jnp.where(lane == 1, e2, meta)
    meta = jnp.where(lane == 2, gate1, meta)
    meta = jnp.where(lane == 3, gate2, meta)
    meta = jnp.where(lane == 4, rank1, meta)
    meta = jnp.where(lane == 5, rank2, meta)
    return meta, pad8


def _granule(ref, g):
    return ref.at[pl.ds(pl.multiple_of(g * GRANULE, GRANULE), GRANULE), :]


def _dispatch_kernel(ngran_ref, nv_ref, dest_ref, gap_ref, x_ref, g_ref, meta_ref, xs_hbm,
                     buf, zbuf, sem, zsem, nstart):
    i = pl.program_id(0)
    n_tiles = pl.num_programs(0) - 1
    slot = i % 2
    tt = x_ref.shape[0]
    bm = zbuf.shape[0]
    n_blocks = xs_hbm.shape[0] // bm

    def out_copy(sl, g, d):
        return pltpu.make_async_copy(_granule(buf.at[sl], g), _granule(xs_hbm, d), sem.at[sl])

    def drain(sl):
        def body(_, carry):
            out_copy(sl, 0, 0).wait()
            return carry
        lax.fori_loop(0, nstart[sl], body, 0)

    @pl.when(i == 0)
    def _():
        nstart[0] = 0
        nstart[1] = 0

    @pl.when(i < n_tiles)
    def _():
        drain(slot)
        hn = _rms(x_ref[...], g_ref[...]).astype(BF16)
        mt = meta_ref[...].T
        for r0 in range(0, LOCAL_ROWS, 256):
            rows = (lax.broadcasted_iota(I32, (256, tt), 0) + r0).astype(F32)
            onehot = jnp.where((rows == mt[4:5, :]) | (rows == mt[5:6, :]), 1.0, 0.0).astype(BF16)
            buf[slot, r0:r0 + 256, :] = _dot(onehot, hn).astype(BF16)
        ng = ngran_ref[i]

        def body(g, carry):
            out_copy(slot, g, dest_ref[0, 0, g]).start()
            return carry
        lax.fori_loop(0, ng, body, 0)
        nstart[slot] = ng

    @pl.when(i == n_tiles)
    def _():
        drain(0)
        drain(1)
        zbuf[...] = jnp.zeros_like(zbuf)

        def gap_copy(d):
            return pltpu.make_async_copy(_granule(zbuf, 0), _granule(xs_hbm, d), zsem.at[0])

        def tail_copy(b):
            return pltpu.make_async_copy(
                zbuf, xs_hbm.at[pl.ds(pl.multiple_of(b * bm, bm), bm), :], zsem.at[1])

        def each_gap(fn):
            def body(j, carry):
                d = gap_ref[0, 0, j]

                @pl.when(d >= 0)
                def _():
                    fn(gap_copy(d))
                return carry
            lax.fori_loop(0, gap_ref.shape[2], body, 0)

        def each_tail(fn):
            def body(b, carry):
                fn(tail_copy(b))
                return carry
            lax.fori_loop(nv_ref[0], n_blocks, body, 0)

        each_gap(lambda cp: cp.start())
        each_tail(lambda cp: cp.start())
        each_gap(lambda cp: cp.wait())
        each_tail(lambda cp: cp.wait())


def _expert_kernel(bexp_ref, nv_ref, xs_ref, wg_ref, wu_ref, wd_ref, ys_ref,
                   wgu_b, wd_b, act_scr):
    b = pl.program_id(0)
    ff = wg_ref.shape[1]

    @pl.when(b < nv_ref[0])
    def _():
        @pl.when((b == 0) | (bexp_ref[b] != bexp_ref[jnp.maximum(b - 1, 0)]))
        def _():
            for k0 in range(0, wg_ref.shape[0], 256):
                wgu_b[k0:k0 + 256, 0:ff] = wg_ref[k0:k0 + 256, :].astype(BF16)
                wgu_b[k0:k0 + 256, ff:2 * ff] = wu_ref[k0:k0 + 256, :].astype(BF16)
            for k0 in range(0, ff, 64):
                wd_b[k0:k0 + 64, :] = wd_ref[k0:k0 + 64, :].astype(BF16)

        bm = xs_ref.shape[0]
        for m0 in range(0, bm, 256):
            h = _dot(xs_ref[m0:m0 + 256, :], wgu_b[...])
            for r0 in range(0, 256, 128):
                for c0 in range(0, ff, 128):
                    hg = h[r0:r0 + 128, c0:c0 + 128]
                    hu = h[r0:r0 + 128, ff + c0:ff + c0 + 128]
                    act_scr[m0 + r0:m0 + r0 + 128, c0:c0 + 128] = (_silu(hg) * hu).astype(BF16)
            ys_ref[m0:m0 + 256, :] = _dot(act_scr[m0:m0 + 256, :], wd_b[...]).astype(BF16)


def _combine_kernel(ngran_ref, src_ref, srcn_ref, x_ref, meta_ref, gf_ref, ys_hbm, o_ref, buf, sem,
                    *, final):
    i = pl.program_id(0)
    n = pl.num_programs(0)
    slot = i % 2
    tt = x_ref.shape[0]

    def in_copy(sl, g, d):
        return pltpu.make_async_copy(_granule(ys_hbm, d), _granule(buf.at[sl], g), sem.at[sl])

    def gather(idx_ref, sl, ng):
        def body(g, carry):
            in_copy(sl, g, idx_ref[0, 0, g]).start()
            return carry
        lax.fori_loop(0, ng, body, 0)

    @pl.when(i == 0)
    def _():
        buf[...] = jnp.zeros_like(buf)
        gather(src_ref, 0, ngran_ref[0])

    @pl.when(i + 1 < n)
    def _():
        gather(srcn_ref, 1 - slot, ngran_ref[i + 1])

    def wait_body(_, carry):
        in_copy(slot, 0, 0).wait()
        return carry
    lax.fori_loop(0, ngran_ref[i], wait_body, 0)

    ys = buf[slot]
    meta = meta_ref[...]
    cols = lax.broadcasted_iota(I32, (tt, LOCAL_ROWS), 1).astype(F32)
    pick1 = jnp.where(cols == meta[:, 4:5], 1.0, 0.0).astype(BF16)
    pick2 = jnp.where(cols == meta[:, 5:6], 1.0, 0.0).astype(BF16)
    y = x_ref[...] + meta[:, 2:3] * _dot(pick1, ys) + meta[:, 3:4] * _dot(pick2, ys)
    if final:
        y = _rms(y, gf_ref[...])
    o_ref[...] = y


def _moe(x2d, norm_g, w_grp, b_grp, w_rt, b_rt, w_gate, w_up, w_down, norm_final, *, layer):
    t = x2d.shape[0]
    tt = MOE_TILE
    bm = _expert_block(t)
    n_tiles = t // tt
    n_exp = MOE_EXPERTS
    lane_pad = 128 - MOE_GROUPS - n_exp
    wr = jnp.concatenate([w_grp, w_rt, jnp.zeros((D_MODEL, lane_pad), F32)], axis=1)
    br = jnp.concatenate([b_grp, b_rt, jnp.zeros((lane_pad,), F32)]).reshape(1, 128)
    ltri = jnp.asarray(np.tril(np.ones((tt, tt)), -1), BF16)
    utri = jnp.asarray(np.triu(np.ones((128, 128)), 1), BF16)
    g2d = norm_g.reshape(1, D_MODEL)
    arb = pltpu.CompilerParams(dimension_semantics=("arbitrary",), vmem_limit_bytes=VMEM_LIMIT)

    meta, cnt = pl.pallas_call(
        _router_kernel,
        grid=(n_tiles // ROUTER_TILES,),
        in_specs=[pl.BlockSpec((ROUTER_TILES * tt, D_MODEL), lambda i: (i, 0)),
                  _const_spec((1, D_MODEL)),
                  _const_spec((D_MODEL, 128)),
                  _const_spec((1, 128)),
                  _const_spec((tt, tt)),
                  _const_spec((128, 128))],
        out_specs=(pl.BlockSpec((ROUTER_TILES * tt, 128), lambda i: (i, 0)),
                   pl.BlockSpec((ROUTER_TILES * 8, 128), lambda i: (i, 0))),
        out_shape=(jax.ShapeDtypeStruct((t, 128), F32),
                   jax.ShapeDtypeStruct((n_tiles * 8, 128), F32)),
        compiler_params=arb,
        name="moe_router",
    )(x2d, g2d, wr, br, ltri, utri)

    runs = cnt.reshape(n_tiles, 8, 128)[:, 0, _LANE_E0:_LANE_E0 + n_exp].astype(I32)
    rows_e = jnp.sum(runs, axis=0)
    nblk = (rows_e + bm - 1) // bm
    blk_end = jnp.cumsum(nblk)
    e_start = (blk_end - nblk) * bm
    n_valid = blk_end[-1]
    run_end = jnp.cumsum(runs, axis=1)
    ngran = (run_end[:, -1] // GRANULE).astype(I32)
    shift = e_start[None, :] + (jnp.cumsum(runs, axis=0) - runs) - (run_end - runs)
    g_row = jnp.arange(N_LOCAL_GRAN, dtype=I32) * GRANULE
    e_of_g = jnp.sum((run_end[:, None, :] <= g_row[None, :, None]).astype(I32), axis=-1)
    shift_g = jnp.sum(jnp.where(e_of_g[..., None] == jnp.arange(n_exp, dtype=I32),
                                shift[:, None, :], 0), axis=-1)
    dest = jnp.where(e_of_g < n_exp, (shift_g + g_row[None, :]) // GRANULE, 0)
    dest = dest.astype(I32).reshape(n_tiles, 1, N_LOCAL_GRAN)
    per_blk = bm // GRANULE
    gap = ((e_start + rows_e) // GRANULE)[:, None] + jnp.arange(per_blk, dtype=I32)[None, :]
    gap = jnp.where(gap < ((e_start + nblk * bm) // GRANULE)[:, None], gap, -1)
    gap = gap.astype(I32).reshape(1, 1, n_exp * per_blk)
    n_blocks = (2 * t + n_tiles * n_exp * (GRANULE - 1)) // bm + 1 + n_exp
    blk = jnp.minimum(jnp.arange(n_blocks, dtype=I32), n_valid - 1)
    block_expert = jnp.sum((blk[:, None] >= blk_end[None, :]).astype(I32), axis=1).astype(I32)
    n_valid = n_valid.reshape(1).astype(I32)

    last = n_tiles - 1
    xs = pl.pallas_call(
        _dispatch_kernel,
        grid_spec=pltpu.PrefetchScalarGridSpec(
            num_scalar_prefetch=2,
            grid=(n_tiles + 1,),
            in_specs=[pl.BlockSpec((1, 1, N_LOCAL_GRAN),
                                   lambda i, ng, nv: (jnp.minimum(i, last), 0, 0),
                                   memory_space=pltpu.SMEM),
                      pl.BlockSpec((1, 1, n_exp * per_blk), lambda i, ng, nv: (0, 0, 0),
                                   memory_space=pltpu.SMEM),
                      pl.BlockSpec((tt, D_MODEL), lambda i, ng, nv: (jnp.minimum(i, last), 0)),
                      _const_spec((1, D_MODEL)),
                      pl.BlockSpec((tt, 128), lambda i, ng, nv: (jnp.minimum(i, last), 0))],
            out_specs=pl.BlockSpec(memory_space=pl.ANY),
            scratch_shapes=[pltpu.VMEM((2, LOCAL_ROWS, D_MODEL), BF16),
                            pltpu.VMEM((bm, D_MODEL), BF16),
                            pltpu.SemaphoreType.DMA((2,)),
                            pltpu.SemaphoreType.DMA((2,)),
                            pltpu.SMEM((2,), I32)]),
        out_shape=jax.ShapeDtypeStruct((n_blocks * bm, D_MODEL), BF16),
        compiler_params=arb,
        name="moe_dispatch",
    )(ngran, n_valid, dest, gap, x2d, g2d, meta)

    w_spec = lambda shape: pl.BlockSpec((None, None) + shape,
                                        lambda b, be, nv: (layer, be[b], 0, 0))
    row_blk = pl.BlockSpec((bm, D_MODEL), lambda b, be, nv: (jnp.minimum(b, nv[0] - 1), 0))
    ys = pl.pallas_call(
        _expert_kernel,
        grid_spec=pltpu.PrefetchScalarGridSpec(
            num_scalar_prefetch=2,
            grid=(n_blocks,),
            in_specs=[row_blk,
                      w_spec((D_MODEL, MOE_D_FF)),
                      w_spec((D_MODEL, MOE_D_FF)),
                      w_spec((MOE_D_FF, D_MODEL))],
            out_specs=row_blk,
            scratch_shapes=[pltpu.VMEM((D_MODEL, 2 * MOE_D_FF), BF16),
                            pltpu.VMEM((MOE_D_FF, D_MODEL), BF16),
                            pltpu.VMEM((bm, MOE_D_FF), BF16)]),
        out_shape=jax.ShapeDtypeStruct((n_blocks * bm, D_MODEL), BF16),
        input_output_aliases={2: 0},
        compiler_params=arb,
        name="moe_experts",
    )(block_expert, n_valid, xs, w_gate, w_up, w_down)

    final = norm_final is not None
    gf = (norm_final if final else jnp.ones((D_MODEL,), F32)).reshape(1, D_MODEL)
    src_spec = lambda nxt: pl.BlockSpec(
        (1, 1, N_LOCAL_GRAN), lambda i, ng: (jnp.minimum(i + nxt, last), 0, 0),
        memory_space=pltpu.SMEM)
    out = pl.pallas_call(
        functools.partial(_combine_kernel, final=final),
        grid_spec=pltpu.PrefetchScalarGridSpec(
            num_scalar_prefetch=1,
            grid=(n_tiles,),
            in_specs=[src_spec(0), src_spec(1),
                      pl.BlockSpec((tt, D_MODEL), lambda i, ng: (i, 0)),
                      pl.BlockSpec((tt, 128), lambda i, ng: (i, 0)),
                      _const_spec((1, D_MODEL)),
                      pl.BlockSpec(memory_space=pl.ANY)],
            out_specs=pl.BlockSpec((tt, D_MODEL), lambda i, ng: (i, 0)),
            scratch_shapes=[pltpu.VMEM((2, LOCAL_ROWS, D_MODEL), BF16),
                            pltpu.SemaphoreType.DMA((2,))]),
        out_shape=jax.ShapeDtypeStruct((t, D_MODEL), F32),
        compiler_params=arb,
        name="moe_combine",
    )(ngran, dest, dest, x2d, meta, gf, ys)
    return out


def _trunk(x, conv_prev, ssm_prev, p, *, ssd_nsub, ssd_ns, want_v):
    n_seq, seq_len, _ = x.shape
    q = seq_len if seq_len < SSD_CHUNK else SSD_CHUNK
    x, conv_new, ssm_new = _ssd_layer(
        x, conv_prev, ssm_prev, p["norm_mix"][0], p["ssd_w_in"][0], p["ssd_conv_w"][0],
        p["ssd_conv_b"][0], p["ssd_dt_bias"][0], p["ssd_a_log"][0], p["ssd_d"][0], p["ssd_norm"][0],
        p["ssd_w_out"][0], nsub=ssd_nsub, ns=ssd_ns, q=q)
    x2d = x.reshape(n_seq * seq_len, D_MODEL)
    x2d = _moe(x2d, p["norm_ffn"][0], p["moe_w_group"][0], p["moe_b_group"][0], p["moe_w_router"][0],
               p["moe_b_router"][0], p["moe_w_gate"], p["moe_w_up"], p["moe_w_down"], None, layer=0)
    x2d, v = _sg_layer(x2d, p["norm_mix"][1], p["sg_w_in"][0], p["sg_b_in"][0], p["sg_ln_g"][0],
                       p["sg_ln_b"][0], p["sg_w_s"][0], p["sg_b_s"][0], p["sg_w_out"][0],
                       seq_len=seq_len, want_v=want_v)
    y2d = _moe(x2d, p["norm_ffn"][1], p["moe_w_group"][1], p["moe_b_group"][1], p["moe_w_router"][1],
               p["moe_b_router"][1], p["moe_w_gate"], p["moe_w_up"], p["moe_w_down"],
               p["norm_final"], layer=1)
    y = y2d.reshape(n_seq, seq_len, D_MODEL)
    if want_v:
        v = v.reshape(1, n_seq, seq_len, SG_WIDTH)
    return y, conv_new, ssm_new, v


def kernel(x_prompt, x_sample, state_ssm, state_conv, norm_mix, norm_ffn, norm_final, ssd_w_in, ssd_conv_w, ssd_conv_b, ssd_dt_bias, ssd_a_log, ssd_d, ssd_norm, ssd_w_out, sg_w_in, sg_b_in, sg_ln_g, sg_ln_b, sg_w_s, sg_b_s, sg_w_out, moe_w_group, moe_b_group, moe_w_router, moe_b_router, moe_w_gate, moe_w_up, moe_w_down):
    p = dict(norm_mix=norm_mix, norm_ffn=norm_ffn, norm_final=norm_final, ssd_w_in=ssd_w_in,
             ssd_conv_w=ssd_conv_w, ssd_conv_b=ssd_conv_b, ssd_dt_bias=ssd_dt_bias,
             ssd_a_log=ssd_a_log, ssd_d=ssd_d, ssd_norm=ssd_norm, ssd_w_out=ssd_w_out,
             sg_w_in=sg_w_in, sg_b_in=sg_b_in, sg_ln_g=sg_ln_g, sg_ln_b=sg_ln_b, sg_w_s=sg_w_s,
             sg_b_s=sg_b_s, sg_w_out=sg_w_out, moe_w_group=moe_w_group, moe_b_group=moe_b_group,
             moe_w_router=moe_w_router, moe_b_router=moe_b_router, moe_w_gate=moe_w_gate,
             moe_w_up=moe_w_up, moe_w_down=moe_w_down)
    nb = x_prompt.shape[0]
    conv0 = jnp.zeros((1, nb, CONV_W - 1, CONV_DIM), F32)
    ssm0 = jnp.zeros((1, nb, N_HEADS, HEAD_DIM, D_STATE), F32)
    y_p, conv_p, ssm_p, _ = _trunk(x_prompt, conv0, ssm0, p, ssd_nsub=2, ssd_ns=1, want_v=False)
    y_s, conv_s, ssm_s, v_s = _trunk(x_sample, state_conv, state_ssm, p, ssd_nsub=1, ssd_ns=4,
                                     want_v=True)
    return (y_p, y_s, ssm_p, conv_p, ssm_s, conv_s, v_s)
```

```python
import functools
import math

import jax
import jax.numpy as jnp
import numpy as np
from jax import lax
from jax.experimental import pallas as pl
from jax.experimental.pallas import tpu as pltpu

F32 = jnp.float32
BF16 = jnp.bfloat16
I32 = jnp.int32

D_MODEL = 1024
N_HEADS = 32
HEAD_DIM = 64
N_GROUPS = 4
D_STATE = 128
D_INNER = N_HEADS * HEAD_DIM
GROUP_W = D_INNER // N_GROUPS
CONV_W = 4
CONV_DIM = D_INNER + 2 * N_GROUPS * D_STATE
DT_PAD = 128
SSD_CHUNK = 128
PROJ_CHUNK = 256
SG_WIDTH = 2 * D_MODEL
SG_GROUPS = 8
SG_GROUP_DIM = SG_WIDTH // SG_GROUPS
SG_CHUNK = 128
SG_ROWS = 256
MOE_GROUPS = 4
MOE_EPG = 8
MOE_EXPERTS = MOE_GROUPS * MOE_EPG
MOE_D_FF = 256
NORM_EPS = 1e-6
LN_EPS = 1e-5

MOE_TILE = 256
ROUTER_TILES = 4


def _expert_block(n_tokens):
    return 512 if 2 * n_tokens >= 512 * MOE_EXPERTS else 256


GRANULE = 16
LOCAL_ROWS = 2 * MOE_TILE + MOE_EXPERTS * GRANULE
N_LOCAL_GRAN = LOCAL_ROWS // GRANULE
VMEM_LIMIT = 56 * 1024 * 1024


def _sigmoid(x):
    return 0.5 * (jnp.tanh(0.5 * x) + 1.0)


def _silu(x):
    return x * _sigmoid(x)


def _softplus(x):
    return jnp.maximum(x, 0.0) + jnp.log(1.0 + jnp.exp(-jnp.abs(x)))


def _gelu_tanh(x):
    c = math.sqrt(2.0 / math.pi)
    return x * (0.5 * (1.0 + jnp.tanh(c * (x + 0.044715 * (x * x * x)))))


def _rms(x, g):
    return x * lax.rsqrt(jnp.mean(x * x, axis=-1, keepdims=True) + NORM_EPS) * g


def _split3(x):
    a = x.astype(BF16)
    r = x - a.astype(F32)
    b = r.astype(BF16)
    c = (r - b.astype(F32)).astype(BF16)
    return a, b, c


def _dot(a, b):
    return jnp.dot(a, b, preferred_element_type=F32)


def _dot_nt(a, b):
    return lax.dot_general(a, b, (((1,), (1,)), ((), ())), preferred_element_type=F32)


def _dot_tn(a, b):
    return lax.dot_general(a, b, (((0,), (0,)), ((), ())), preferred_element_type=F32)


def _const_spec(shape):
    nd = len(shape)
    return pl.BlockSpec(shape, lambda *_: (0,) * nd)


def _ssd_kernel(x_ref, g_ref, win_ref, cw_ref, cb_ref, dtb_ref, alog_ref, dsk_ref, ng_ref,
                wout_ref, tri_ref, ones_ref, cin_ref, sin_ref,
                xo_ref, cout_ref, sout_ref,
                pend_x, pend_z, pend_xbc, pend_dt, xres_scr, zg_scr,
                prev_scr, xc_scr, yoff_scr, y_scr, xw_scr, hn_scr, yn_scr,
                *, nsub, ns, q, n_chunks):
    i = pl.program_id(0)
    first_chunk = lax.rem(jnp.maximum(i - 1, 0), n_chunks) == 0
    nseq = nsub * ns
    sb = ns * q
    r = nsub * sb

    @pl.when(i == 0)
    def _():
        pend_x[...] = jnp.zeros_like(pend_x)
        pend_z[...] = jnp.zeros_like(pend_z)
        pend_xbc[...] = jnp.zeros_like(pend_xbc)
        pend_dt[...] = jnp.zeros_like(pend_dt)

    tr = min(r, 128)
    row_tiles = [slice(a, a + tr) for a in range(0, r, tr)]

    def col_tiles(total, rows=tr):
        w = min(total, max(128, (32 * 1024) // rows))
        return [slice(c, c + w) for c in range(0, total, w)]

    for rt in row_tiles:
        for ct in col_tiles(D_MODEL):
            xres_scr[rt, ct] = pend_x[rt, ct]
        for ct in col_tiles(D_INNER):
            zg_scr[rt, ct] = _silu(pend_z[rt, ct])

    for s in range(nseq):
        for ct in col_tiles(D_MODEL, q):
            pend_x[s * q:(s + 1) * q, ct] = x_ref[s, :, ct]
    for rt in row_tiles:
        ss = None
        for ct in col_tiles(D_MODEL):
            xv = pend_x[rt, ct]
            part = jnp.sum(xv * xv, axis=-1, keepdims=True)
            ss = part if ss is None else ss + part
        scale = lax.rsqrt(ss * (1.0 / D_MODEL) + NORM_EPS)
        for ct in col_tiles(D_MODEL):
            hn_scr[rt, ct] = (pend_x[rt, ct] * scale * g_ref[:, ct]).astype(BF16)

    def proj_chunk(dst, dst_col, w_col, width):
        def run():
            dst[:, dst_col:dst_col + width] = _dot(hn_scr[...], win_ref[:, w_col:w_col + width])
        return run
    z_chunks = [proj_chunk(pend_z, c, c, PROJ_CHUNK) for c in range(0, D_INNER, PROJ_CHUNK)]
    scan_chunks = [proj_chunk(pend_xbc, c, D_INNER + c, PROJ_CHUNK)
                   for c in range(0, CONV_DIM, PROJ_CHUNK)]
    scan_chunks.append(proj_chunk(pend_dt, 0, D_INNER + CONV_DIM, DT_PAD))

    @pl.when(first_chunk)
    def _():
        sout_ref[...] = sin_ref[...]
        prev_scr[...] = jnp.zeros_like(prev_scr)
        for s in range(nseq):
            for k in range(CONV_W - 1):
                prev_scr[pl.ds(s * 8 + 5 + k, 1), :] = cin_ref[s, pl.ds(k, 1), :]

    sub8 = lax.broadcasted_iota(I32, (8, 1), 0)
    conv_cols = col_tiles(CONV_DIM, 4 * q)
    n_conv, n_z, done = nseq * len(conv_cols), len(z_chunks), 0
    for s in range(nseq):
        srows = slice(s * q, (s + 1) * q)
        for ct in conv_cols:
            xq = pend_xbc[srows, ct]
            hist = prev_scr[s * 8:(s + 1) * 8, ct]
            acc = cb_ref[:, ct] + cw_ref[pl.ds(CONV_W - 1, 1), ct] * xq
            for j in range(1, CONV_W):
                sh = pltpu.roll(xq, j, 0)
                head = jnp.where(sub8 < j, pltpu.roll(hist, j, 0), sh[0:8, :])
                sh = head if q == 8 else jnp.concatenate([head, sh[8:, :]], axis=0)
                acc = acc + cw_ref[pl.ds(CONV_W - 1 - j, 1), ct] * sh
            xc_scr[srows, ct] = _silu(acc)
            last = xq[q - 8:q, :]
            prev_scr[s * 8:(s + 1) * 8, ct] = last
            for k in range(CONV_W - 1):
                cout_ref[s, pl.ds(k, 1), ct] = last[5 + k:6 + k, :]
            done += 1
            while len(z_chunks) > n_z - (done * n_z) // n_conv:
                z_chunks.pop(0)()
    n_pairs = nsub * (N_HEADS // 2)
    stride = n_pairs // len(scan_chunks)
    emit_at = {k * stride: ch for k, ch in enumerate(scan_chunks)}

    tri = tri_ref[...]
    trib = tri.astype(BF16)
    onesb = ones_ref[...].astype(BF16)
    mask = tri > 0.5
    rowseq = lax.shift_right_logical(lax.broadcasted_iota(I32, (sb, 1), 0), int(math.log2(q)))
    lo = lax.broadcasted_iota(I32, (sb, 128), 1) < HEAD_DIM
    neg_a = -jnp.exp(alog_ref[...])

    for u in range(nsub):
        rows = slice(u * sb, (u + 1) * sb)
        xs_ref = xc_scr.at[rows, 0:D_INNER]

        def b_of(g):
            return xc_scr[rows, D_INNER + g * D_STATE:D_INNER + (g + 1) * D_STATE]

        def c_of(g):
            c0 = D_INNER + (N_GROUPS + g) * D_STATE
            return xc_scr[rows, c0:c0 + D_STATE].astype(BF16)

        dt = _softplus(pend_dt[rows, :] + dtb_ref[...])
        d1, d2, d3 = _split3(dt * neg_a)
        cs = _dot(trib, d1) + _dot(trib, d2) + _dot(trib, d3)
        cl = _dot(onesb, d1) + _dot(onesb, d2) + _dot(onesb, d3)
        ecs = jnp.exp(cs)
        wgt = dt * jnp.exp(cl - cs)
        ecl = jnp.exp(cl)
        cs_t = cs.T
        dt_t = dt.T

        for g in range(N_GROUPS):
            cg = c_of(g)
            acc = None
            for s in range(ns):
                st = sout_ref[u * ns + s, g * GROUP_W:(g + 1) * GROUP_W, :].astype(BF16)
                yo = _dot_nt(cg, st)
                if ns > 1:
                    yo = jnp.where(rowseq == s, yo, 0.0)
                acc = yo if acc is None else acc + yo
            yoff_scr[rows, g * GROUP_W:(g + 1) * GROUP_W] = acc

        for g in range(N_GROUPS):
            sc = _dot_nt(c_of(g), b_of(g).astype(BF16))
            for jj in range(N_HEADS // N_GROUPS // 2):
                j = g * (N_HEADS // N_GROUPS // 2) + jj
                ms = []
                for h in (2 * j, 2 * j + 1):
                    diff = cs[:, h:h + 1] - cs_t[h:h + 1, :]
                    dec = jnp.exp(jnp.where(mask, diff, -jnp.inf))
                    ms.append((sc * dec * dt_t[h:h + 1, :]).astype(BF16))
                lhs = jnp.concatenate(ms, axis=1)
                cols = slice(j * 128, (j + 1) * 128)
                xp = xs_ref[:, cols]
                rhs = jnp.concatenate([jnp.where(lo, xp, 0.0).astype(BF16),
                                       jnp.where(lo, 0.0, xp).astype(BF16)], axis=0)
                yd = _dot(lhs, rhs)
                ecs_p = jnp.where(lo, ecs[:, 2 * j:2 * j + 1], ecs[:, 2 * j + 1:2 * j + 2])
                wgt_p = jnp.where(lo, wgt[:, 2 * j:2 * j + 1], wgt[:, 2 * j + 1:2 * j + 2])
                y_scr[rows, cols] = yd + yoff_scr[rows, cols] * ecs_p + xp * dsk_ref[:, cols]
                xw_scr[rows, cols] = (xp * wgt_p).astype(BF16)
                pair = u * (N_HEADS // 2) + j
                if pair in emit_at:
                    emit_at[pair]()

        for g in range(N_GROUPS):
            xwg = xw_scr[rows, g * GROUP_W:(g + 1) * GROUP_W]
            for s in range(ns):
                bg = b_of(g)
                if ns > 1:
                    bg = jnp.where(rowseq == s, bg, 0.0)
                upd = _dot_tn(xwg, bg.astype(BF16))
                for hh in range(GROUP_W // HEAD_DIM):
                    h = g * (GROUP_W // HEAD_DIM) + hh
                    dec = jnp.broadcast_to(ecl[s * q:s * q + 1, h:h + 1], (HEAD_DIM, D_STATE))
                    hrows = slice(h * HEAD_DIM, (h + 1) * HEAD_DIM)
                    sout_ref[u * ns + s, hrows, :] = (
                        sout_ref[u * ns + s, hrows, :] * dec + upd[hh * HEAD_DIM:(hh + 1) * HEAD_DIM, :])

    for rt in row_tiles:
        for g in range(N_GROUPS):
            cts = [slice(g * GROUP_W + c.start, g * GROUP_W + c.stop) for c in col_tiles(GROUP_W)]
            ss = None
            for ct in cts:
                yz = y_scr[rt, ct] * zg_scr[rt, ct]
                part = jnp.sum(yz * yz, axis=-1, keepdims=True)
                ss = part if ss is None else ss + part
            scale = lax.rsqrt(ss * (1.0 / GROUP_W) + NORM_EPS)
            for ct in cts:
                yn_scr[rt, ct] = (y_scr[rt, ct] * zg_scr[rt, ct] * scale * ng_ref[:, ct]).astype(BF16)

    for c0 in range(0, D_MODEL, PROJ_CHUNK):
        ct = slice(c0, c0 + PROJ_CHUNK)
        o = xres_scr[:, ct] + _dot(yn_scr[...], wout_ref[:, ct])
        for s in range(nseq):
            xo_ref[s, :, ct] = o[s * q:(s + 1) * q, :]


def _ssd_layer(x, conv_prev, ssm_prev, norm_g, w_in, conv_w, conv_b, dt_bias, a_log, d_skip,
               norm_y, w_out, *, nsub, ns, q):
    n_seq, seq_len, _ = x.shape
    nseq = nsub * ns
    sb = ns * q
    r = nsub * sb
    n_chunks = seq_len // q
    pad = DT_PAD - N_HEADS
    win = jnp.concatenate([w_in, jnp.zeros((D_MODEL, pad), F32)], axis=1).astype(BF16)
    dtb = jnp.pad(dt_bias, (0, pad)).reshape(1, DT_PAD)
    alog = jnp.pad(a_log, (0, pad)).reshape(1, DT_PAD)
    dsk = jnp.repeat(d_skip, HEAD_DIM).reshape(1, D_INNER)
    blk = np.kron(np.eye(ns), np.ones((q, q)))
    tri = jnp.asarray(blk * np.tril(np.ones((sb, sb))), F32)
    ones = jnp.asarray(blk, F32)
    state = ssm_prev.reshape(n_seq, D_INNER, D_STATE)
    conv_prev = conv_prev.reshape(n_seq, CONV_W - 1, CONV_DIM)

    kern = functools.partial(_ssd_kernel, nsub=nsub, ns=ns, q=q, n_chunks=n_chunks)
    out_shape = (jax.ShapeDtypeStruct(x.shape, F32),
                 jax.ShapeDtypeStruct((n_seq, CONV_W - 1, CONV_DIM), F32),
                 jax.ShapeDtypeStruct((n_seq, D_INNER, D_STATE), F32))
    n_steps = (n_seq // nseq) * n_chunks
    nxt = lambda i: jnp.minimum(i, n_steps - 1)
    cur = lambda i: jnp.maximum(i - 1, 0)
    in_row_spec = pl.BlockSpec((nseq, q, D_MODEL),
                               lambda i: (nxt(i) // n_chunks, nxt(i) % n_chunks, 0))
    row_spec = pl.BlockSpec((nseq, q, D_MODEL),
                            lambda i: (cur(i) // n_chunks, cur(i) % n_chunks, 0))
    conv_spec = pl.BlockSpec((nseq, CONV_W - 1, CONV_DIM), lambda i: (cur(i) // n_chunks, 0, 0))
    state_spec = pl.BlockSpec((nseq, D_INNER, D_STATE), lambda i: (cur(i) // n_chunks, 0, 0))
    in_specs = [in_row_spec,
                _const_spec((1, D_MODEL)),
                _const_spec(win.shape),
                _const_spec((CONV_W, CONV_DIM)),
                _const_spec((1, CONV_DIM)),
                _const_spec((1, DT_PAD)),
                _const_spec((1, DT_PAD)),
                _const_spec((1, D_INNER)),
                _const_spec((1, D_INNER)),
                _const_spec((D_INNER, D_MODEL)),
                _const_spec((sb, sb)),
                _const_spec((sb, sb)),
                conv_spec, state_spec]
    x_new, conv_new, state_new = pl.pallas_call(
        kern,
        grid=(n_steps + 1,),
        in_specs=in_specs,
        out_specs=(row_spec, conv_spec, state_spec),
        out_shape=out_shape,
        scratch_shapes=[pltpu.VMEM((r, D_MODEL), F32),
                        pltpu.VMEM((r, D_INNER), F32),
                        pltpu.VMEM((r, CONV_DIM), F32),
                        pltpu.VMEM((r, DT_PAD), F32),
                        pltpu.VMEM((r, D_MODEL), F32),
                        pltpu.VMEM((r, D_INNER), F32),
                        pltpu.VMEM((nseq * 8, CONV_DIM), F32),
                        pltpu.VMEM((r, CONV_DIM), F32),
                        pltpu.VMEM((r, D_INNER), F32),
                        pltpu.VMEM((r, D_INNER), F32),
                        pltpu.VMEM((r, D_INNER), BF16),
                        pltpu.VMEM((r, D_MODEL), BF16),
                        pltpu.VMEM((r, D_INNER), BF16)],
        compiler_params=pltpu.CompilerParams(
            dimension_semantics=("arbitrary",), vmem_limit_bytes=VMEM_LIMIT),
        name="ssd_layer",
    )(x, norm_g.reshape(1, D_MODEL), win, conv_w, conv_b.reshape(1, CONV_DIM), dtb, alog, dsk,
      norm_y.reshape(1, D_INNER), w_out.astype(BF16), tri, ones, conv_prev, state)
    return (x_new, conv_new.reshape(1, n_seq, CONV_W - 1, CONV_DIM),
            state_new.reshape(1, n_seq, N_HEADS, HEAD_DIM, D_STATE))


def _sg_kernel(x_ref, g_ref, win_ref, bin_ref, lng_ref, lnb_ref, wmix_ref, bmix_ref, wout_ref,
               xo_ref, *rest, r, want_v):
    v_ref = rest[0] if want_v else None
    hn_scr, uv_scr, vb_scr, um_scr = rest[-4:]
    row_tiles = [slice(a, a + 128) for a in range(0, r, 128)]
    col128 = lambda total: [slice(c, c + 128) for c in range(0, total, 128)]

    for rt in row_tiles:
        ss = None
        for ct in col128(D_MODEL):
            xv = x_ref[rt, ct]
            part = jnp.sum(xv * xv, axis=-1, keepdims=True)
            ss = part if ss is None else ss + part
        scale = lax.rsqrt(ss * (1.0 / D_MODEL) + NORM_EPS)
        for ct in col128(D_MODEL):
            hn_scr[rt, ct] = (x_ref[rt, ct] * scale * g_ref[:, ct]).astype(BF16)

    for c0 in range(0, 2 * SG_WIDTH, PROJ_CHUNK):
        h = _dot(hn_scr[...], win_ref[:, c0:c0 + PROJ_CHUNK])
        for rt in row_tiles:
            for cc in range(0, PROJ_CHUNK, 128):
                ct = slice(c0 + cc, c0 + cc + 128)
                uv_scr[rt, ct] = _gelu_tanh(h[rt, cc:cc + 128] + bin_ref[:, ct])

    for rt in row_tiles:
        vcols = [slice(SG_WIDTH + c.start, SG_WIDTH + c.stop) for c in col128(SG_WIDTH)]
        tot = None
        for ct in vcols:
            part = jnp.sum(uv_scr[rt, ct], axis=-1, keepdims=True)
            tot = part if tot is None else tot + part
        mu = tot * (1.0 / SG_WIDTH)
        ss = None
        for ct in vcols:
            vc = uv_scr[rt, ct] - mu
            part = jnp.sum(vc * vc, axis=-1, keepdims=True)
            ss = part if ss is None else ss + part
        scale = lax.rsqrt(ss * (1.0 / SG_WIDTH) + LN_EPS)
        for ct, c in zip(vcols, col128(SG_WIDTH)):
            vn = (uv_scr[rt, ct] - mu) * scale * lng_ref[:, c] + lnb_ref[:, c]
            if want_v:
                v_ref[rt, c] = vn
            vb_scr[rt, c] = vn.astype(BF16)

    for rt in row_tiles:
        for g in range(SG_GROUPS):
            cols = slice(g * SG_GROUP_DIM, (g + 1) * SG_GROUP_DIM)
            mixed = _dot(wmix_ref[g], vb_scr[rt, cols]) + bmix_ref[:, cols]
            um_scr[rt, cols] = (uv_scr[rt, cols] * mixed).astype(BF16)

    for c0 in range(0, D_MODEL, PROJ_CHUNK):
        ct = slice(c0, c0 + PROJ_CHUNK)
        xo_ref[:, ct] = x_ref[:, ct] + _dot(um_scr[...], wout_ref[:, ct])


def _sg_layer(x2d, norm_g, w_in, b_in, ln_g, ln_b, w_s, b_s, w_out, *, seq_len, want_v):
    r = SG_ROWS
    t = x2d.shape[0]
    q = min(seq_len, SG_CHUNK)
    reps = SG_CHUNK // q
    ws = jnp.tril(w_s)[:, :q, :q]
    wmix = jnp.einsum("ab,gts->gatbs", jnp.eye(reps, dtype=F32), ws)
    wmix = wmix.reshape(SG_GROUPS, SG_CHUNK, SG_CHUNK)
    bmix = jnp.tile(jnp.repeat(b_s.T[:q], SG_GROUP_DIM, axis=1), (reps, 1))
    row_spec = pl.BlockSpec((r, D_MODEL), lambda i: (i, 0))
    v_spec = pl.BlockSpec((r, SG_WIDTH), lambda i: (i, 0))
    out_shape = [jax.ShapeDtypeStruct(x2d.shape, F32)]
    out_specs = [row_spec]
    if want_v:
        out_shape.append(jax.ShapeDtypeStruct((t, SG_WIDTH), F32))
        out_specs.append(v_spec)
    outs = pl.pallas_call(
        functools.partial(_sg_kernel, r=r, want_v=want_v),
        grid=(t // r,),
        scratch_shapes=[pltpu.VMEM((r, D_MODEL), BF16),
                        pltpu.VMEM((r, 2 * SG_WIDTH), F32),
                        pltpu.VMEM((r, SG_WIDTH), BF16),
                        pltpu.VMEM((r, SG_WIDTH), BF16)],
        in_specs=[row_spec,
                  _const_spec((1, D_MODEL)),
                  _const_spec((D_MODEL, 2 * SG_WIDTH)),
                  _const_spec((1, 2 * SG_WIDTH)),
                  _const_spec((1, SG_WIDTH)),
                  _const_spec((1, SG_WIDTH)),
                  _const_spec((SG_GROUPS, SG_CHUNK, SG_CHUNK)),
                  _const_spec((SG_CHUNK, SG_WIDTH)),
                  _const_spec((SG_WIDTH, D_MODEL))],
        out_specs=out_specs,
        out_shape=out_shape,
        compiler_params=pltpu.CompilerParams(
            dimension_semantics=("arbitrary",), vmem_limit_bytes=VMEM_LIMIT),
        name="sg_layer",
    )(x2d, norm_g.reshape(1, D_MODEL), w_in.astype(BF16), b_in.reshape(1, 2 * SG_WIDTH),
      ln_g.reshape(1, SG_WIDTH), ln_b.reshape(1, SG_WIDTH), wmix.astype(BF16), bmix,
      w_out.astype(BF16))
    return outs if want_v else (outs[0], None)


ROUTER_ROWS = 64


def _router_t_kernel(x_ref, g_ref, wt_ref, bt_ref, usl_ref, lsl_ref, meta_ref, metat_ref, pad_ref,
                     h1_scr, h2_scr):
    wt = wt_ref[...]
    w1 = wt.astype(BF16)
    w2 = (wt - w1.astype(F32)).astype(BF16)
    tt = MOE_TILE
    n = x_ref.shape[0]

    for r0 in range(0, n, 128):
        rt = slice(r0, r0 + 128)
        ss = None
        for c0 in range(0, D_MODEL, 256):
            xv = x_ref[rt, c0:c0 + 256]
            part = jnp.sum(xv * xv, axis=-1, keepdims=True)
            ss = part if ss is None else ss + part
        scale = lax.rsqrt(ss * (1.0 / D_MODEL) + NORM_EPS)
        for c0 in range(0, D_MODEL, 256):
            ct = slice(c0, c0 + 256)
            hn = x_ref[rt, ct] * scale * g_ref[:, ct]
            hi = hn.astype(BF16)
            h1_scr[rt, ct] = hi
            h2_scr[rt, ct] = (hn - hi.astype(F32)).astype(BF16)

    h1 = h1_scr[...]
    logits = _dot_nt(w1, h1) + _dot_nt(w2, h1) + _dot_nt(w1, h2_scr[...])
    select = _select_experts(logits, bt_ref[...])
    for k in range(n // tt):
        cols = slice(k * tt, (k + 1) * tt)
        metat, pad = _local_order(select[:, cols], usl_ref[...], lsl_ref[...])
        metat_ref[k * 8:(k + 1) * 8, :] = metat
        full = jnp.concatenate([metat, jnp.zeros((128 - 8, tt), F32)], axis=0)
        meta_ref[cols, :] = full.T
        pad_ref[k * MOE_EXPERTS:(k + 1) * MOE_EXPERTS, :] = pad


def _select_experts(logits, bt):
    n = logits.shape[1]
    reps = n // bt.shape[1]
    logits = logits + (bt if reps == 1 else jnp.concatenate([bt] * reps, axis=1))
    tt = n
    row8 = lax.broadcasted_iota(I32, (8, tt), 0).astype(F32)
    gl = jnp.where(row8 < MOE_GROUPS, logits[0:8, :], -jnp.inf)
    gmax = jnp.max(gl, axis=0, keepdims=True)
    g_top = jnp.min(jnp.where(gl == gmax, row8, 8.0), axis=0, keepdims=True)
    p_g = 1.0 / jnp.sum(jnp.exp(gl - gmax), axis=0, keepdims=True)

    el = logits[8:16, :]
    for grp in range(1, MOE_GROUPS):
        el = jnp.where(g_top == grp, logits[8 + 8 * grp:16 + 8 * grp, :], el)
    emax = jnp.max(el, axis=0, keepdims=True)
    ee = jnp.exp(el - emax)
    prob = ee / jnp.sum(ee, axis=0, keepdims=True)
    p1 = jnp.max(prob, axis=0, keepdims=True)
    i1 = jnp.min(jnp.where(prob == p1, row8, 8.0), axis=0, keepdims=True)
    prob2 = jnp.where(row8 == i1, -1.0, prob)
    p2 = jnp.max(prob2, axis=0, keepdims=True)
    i2 = jnp.min(jnp.where(prob2 == p2, row8, 8.0), axis=0, keepdims=True)
    psum = p1 + p2
    gate1 = p_g * (p1 / psum)
    gate2 = p_g * (p2 / psum)
    e1 = g_top * MOE_EPG + i1
    e2 = g_top * MOE_EPG + i2
    out = jnp.where(row8 == 0, e1, 0.0)
    out = jnp.where(row8 == 1, e2, out)
    out = jnp.where(row8 == 2, gate1, out)
    return jnp.where(row8 == 3, gate2, out)


def _local_order(select, usl, lsl):
    tt = select.shape[1]
    row8 = lax.broadcasted_iota(I32, (8, tt), 0).astype(F32)
    e1, e2 = select[0:1, :], select[1:2, :]
    rowe = lax.broadcasted_iota(I32, (MOE_EXPERTS, tt), 0).astype(F32)
    sel1 = rowe == e1
    sel2 = rowe == e2
    onehot = jnp.where(sel1 | sel2, 1.0, 0.0)
    before = _dot(onehot.astype(BF16), usl)
    cnt = jnp.sum(onehot, axis=1, keepdims=True)
    pad = jnp.floor((cnt + (GRANULE - 1)) * (1.0 / GRANULE)) * GRANULE
    padb = jnp.broadcast_to(pad, (MOE_EXPERTS, tt))
    local = before + _dot(lsl, padb.astype(BF16))
    slot1 = jnp.sum(jnp.where(sel1, local, 0.0), axis=0, keepdims=True)
    slot2 = jnp.sum(jnp.where(sel2, local, 0.0), axis=0, keepdims=True)

    metat = jnp.where(row8 == 4, slot1, select)
    metat = jnp.where(row8 == 5, slot2, metat)
    return metat, padb[:, 0:128]


def _granule(ref, g):
    return ref.at[pl.ds(pl.multiple_of(g * GRANULE, GRANULE), GRANULE), :]


def _dispatch_kernel(ngran_ref, nv_ref, dest_ref, gap_ref, x_ref, g_ref, meta_ref, xs_hbm,
                     buf, zbuf, sem, zsem, nstart):
    i = pl.program_id(0)
    n_tiles = pl.num_programs(0) - 1
    slot = i % 2
    tt = x_ref.shape[0]
    bm = zbuf.shape[0]
    n_blocks = xs_hbm.shape[0] // bm

    def out_copy(sl, g, d):
        return pltpu.make_async_copy(_granule(buf.at[sl], g), _granule(xs_hbm, d), sem.at[sl])

    def drain(sl):
        def body(_, carry):
            out_copy(sl, 0, 0).wait()
            return carry
        lax.fori_loop(0, nstart[sl], body, 0)

    @pl.when(i == 0)
    def _():
        nstart[0] = 0
        nstart[1] = 0

    @pl.when(i < n_tiles)
    def _():
        drain(slot)
        hn = _rms(x_ref[...], g_ref[...]).astype(BF16)
        mt = meta_ref[...]
        for r0 in range(0, LOCAL_ROWS, 256):
            rows = (lax.broadcasted_iota(I32, (256, tt), 0) + r0).astype(F32)
            onehot = jnp.where((rows == mt[4:5, :]) | (rows == mt[5:6, :]), 1.0, 0.0).astype(BF16)
            buf[slot, r0:r0 + 256, :] = _dot(onehot, hn).astype(BF16)
        ng = ngran_ref[i]

        def body(g, carry):
            out_copy(slot, g, dest_ref[0, 0, g]).start()
            return carry
        lax.fori_loop(0, ng, body, 0)
        nstart[slot] = ng

    @pl.when(i == n_tiles)
    def _():
        drain(0)
        drain(1)
        zbuf[...] = jnp.zeros_like(zbuf)

        def gap_copy(d):
            return pltpu.make_async_copy(_granule(zbuf, 0), _granule(xs_hbm, d), zsem.at[0])

        def tail_copy(b):
            return pltpu.make_async_copy(
                zbuf, xs_hbm.at[pl.ds(pl.multiple_of(b * bm, bm), bm), :], zsem.at[1])

        def each_gap(fn):
            def body(j, carry):
                d = gap_ref[0, 0, j]

                @pl.when(d >= 0)
                def _():
                    fn(gap_copy(d))
                return carry
            lax.fori_loop(0, gap_ref.shape[2], body, 0)

        def each_tail(fn):
            def body(b, carry):
                fn(tail_copy(b))
                return carry
            lax.fori_loop(nv_ref[0], n_blocks, body, 0)

        each_gap(lambda cp: cp.start())
        each_tail(lambda cp: cp.start())
        each_gap(lambda cp: cp.wait())
        each_tail(lambda cp: cp.wait())


def _expert_kernel(bexp_ref, nv_ref, xs_ref, wg_ref, wu_ref, wd_ref, ys_ref,
                   wgu_b, wd_b, act_scr):
    b = pl.program_id(0)
    ff = wg_ref.shape[1]

    @pl.when(b < nv_ref[0])
    def _():
        @pl.when((b == 0) | (bexp_ref[b] != bexp_ref[jnp.maximum(b - 1, 0)]))
        def _():
            for k0 in range(0, wg_ref.shape[0], 256):
                wgu_b[k0:k0 + 256, 0:ff] = wg_ref[k0:k0 + 256, :].astype(BF16)
                wgu_b[k0:k0 + 256, ff:2 * ff] = wu_ref[k0:k0 + 256, :].astype(BF16)
            for k0 in range(0, ff, 64):
                wd_b[k0:k0 + 64, :] = wd_ref[k0:k0 + 64, :].astype(BF16)

        bm = xs_ref.shape[0]
        for m0 in range(0, bm, 256):
            h = _dot(xs_ref[m0:m0 + 256, :], wgu_b[...])
            for r0 in range(0, 256, 128):
                for c0 in range(0, ff, 128):
                    hg = h[r0:r0 + 128, c0:c0 + 128]
                    hu = h[r0:r0 + 128, ff + c0:ff + c0 + 128]
                    act_scr[m0 + r0:m0 + r0 + 128, c0:c0 + 128] = (_silu(hg) * hu).astype(BF16)
            ys_ref[m0:m0 + 256, :] = _dot(act_scr[m0:m0 + 256, :], wd_b[...]).astype(BF16)


def _combine_kernel(ngran_ref, src_ref, srcn_ref, x_ref, meta_ref, gf_ref, ys_hbm, o_ref, buf, sem,
                    *, final):
    i = pl.program_id(0)
    n = pl.num_programs(0)
    slot = i % 2
    tt = x_ref.shape[0]

    def in_copy(sl, g, d):
        return pltpu.make_async_copy(_granule(ys_hbm, d), _granule(buf.at[sl], g), sem.at[sl])

    def gather(idx_ref, sl, ng):
        def body(g, carry):
            in_copy(sl, g, idx_ref[0, 0, g]).start()
            return carry
        lax.fori_loop(0, ng, body, 0)

    @pl.when(i == 0)
    def _():
        buf[...] = jnp.zeros_like(buf)
        gather(src_ref, 0, ngran_ref[0])

    @pl.when(i + 1 < n)
    def _():
        gather(srcn_ref, 1 - slot, ngran_ref[i + 1])

    def wait_body(_, carry):
        in_copy(slot, 0, 0).wait()
        return carry
    lax.fori_loop(0, ngran_ref[i], wait_body, 0)

    ys = buf[slot]
    meta = meta_ref[...]
    cols = lax.broadcasted_iota(I32, (tt, LOCAL_ROWS), 1).astype(F32)
    pick = (jnp.where(cols == meta[:, 4:5], meta[:, 2:3], 0.0)
            + jnp.where(cols == meta[:, 5:6], meta[:, 3:4], 0.0)).astype(BF16)
    y = x_ref[...] + _dot(pick, ys)
    if final:
        y = _rms(y, gf_ref[...])
    o_ref[...] = y


def _moe(x2d, norm_g, w_grp, b_grp, w_rt, b_rt, w_gate, w_up, w_down, norm_final, *, layer):
    t = x2d.shape[0]
    tt = MOE_TILE
    bm = _expert_block(t)
    n_tiles = t // tt
    n_exp = MOE_EXPERTS
    zrow = lambda n: jnp.zeros((n, D_MODEL), F32)
    wt = jnp.concatenate([w_grp.T, zrow(8 - MOE_GROUPS), w_rt.T, zrow(ROUTER_ROWS - 8 - n_exp)], axis=0)
    bt = jnp.concatenate([b_grp, jnp.zeros((8 - MOE_GROUPS,), F32), b_rt,
                          jnp.zeros((ROUTER_ROWS - 8 - n_exp,), F32)])
    bt = jnp.broadcast_to(bt[:, None], (ROUTER_ROWS, tt))
    usl = jnp.asarray(np.triu(np.ones((tt, tt)), 1), BF16)
    lsl = jnp.asarray(np.tril(np.ones((n_exp, n_exp)), -1), BF16)
    g2d = norm_g.reshape(1, D_MODEL)
    arb = pltpu.CompilerParams(dimension_semantics=("arbitrary",), vmem_limit_bytes=VMEM_LIMIT)

    meta, metat, pad = pl.pallas_call(
        _router_t_kernel,
        grid=(n_tiles // ROUTER_TILES,),
        in_specs=[pl.BlockSpec((ROUTER_TILES * tt, D_MODEL), lambda i: (i, 0)),
                  _const_spec((1, D_MODEL)),
                  _const_spec((ROUTER_ROWS, D_MODEL)),
                  _const_spec((ROUTER_ROWS, tt)),
                  _const_spec((tt, tt)),
                  _const_spec((n_exp, n_exp))],
        out_specs=(pl.BlockSpec((ROUTER_TILES * tt, 128), lambda i: (i, 0)),
                   pl.BlockSpec((ROUTER_TILES * 8, tt), lambda i: (i, 0)),
                   pl.BlockSpec((ROUTER_TILES * n_exp, 128), lambda i: (i, 0))),
        out_shape=(jax.ShapeDtypeStruct((t, 128), F32),
                   jax.ShapeDtypeStruct((n_tiles * 8, tt), F32),
                   jax.ShapeDtypeStruct((n_tiles * n_exp, 128), F32)),
        scratch_shapes=[pltpu.VMEM((ROUTER_TILES * tt, D_MODEL), BF16),
                        pltpu.VMEM((ROUTER_TILES * tt, D_MODEL), BF16)],
        compiler_params=arb,
        name="moe_router",
    )(x2d, g2d, wt, bt, usl, lsl)

    runs = pad.reshape(n_tiles, n_exp, 128)[:, :, 0].astype(I32)
    rows_e = jnp.sum(runs, axis=0)
    nblk = (rows_e + bm - 1) // bm
    blk_end = jnp.cumsum(nblk)
    e_start = (blk_end - nblk) * bm
    n_valid = blk_end[-1]
    run_end = jnp.cumsum(runs, axis=1)
    ngran = (run_end[:, -1] // GRANULE).astype(I32)
    shift = e_start[None, :] + (jnp.cumsum(runs, axis=0) - runs) - (run_end - runs)
    g_row = jnp.arange(N_LOCAL_GRAN, dtype=I32) * GRANULE
    e_of_g = jnp.sum((run_end[:, None, :] <= g_row[None, :, None]).astype(I32), axis=-1)
    shift_g = jnp.sum(jnp.where(e_of_g[..., None] == jnp.arange(n_exp, dtype=I32),
                                shift[:, None, :], 0), axis=-1)
    dest = jnp.where(e_of_g < n_exp, (shift_g + g_row[None, :]) // GRANULE, 0)
    dest = dest.astype(I32).reshape(n_tiles, 1, N_LOCAL_GRAN)
    per_blk = bm // GRANULE
    gap = ((e_start + rows_e) // GRANULE)[:, None] + jnp.arange(per_blk, dtype=I32)[None, :]
    gap = jnp.where(gap < ((e_start + nblk * bm) // GRANULE)[:, None], gap, -1)
    gap = gap.astype(I32).reshape(1, 1, n_exp * per_blk)
    n_blocks = (2 * t + n_tiles * n_exp * (GRANULE - 1)) // bm + 1 + n_exp
    blk = jnp.minimum(jnp.arange(n_blocks, dtype=I32), n_valid - 1)
    block_expert = jnp.sum((blk[:, None] >= blk_end[None, :]).astype(I32), axis=1).astype(I32)
    n_valid = n_valid.reshape(1).astype(I32)

    last = n_tiles - 1
    xs = pl.pallas_call(
        _dispatch_kernel,
        grid_spec=pltpu.PrefetchScalarGridSpec(
            num_scalar_prefetch=2,
            grid=(n_tiles + 1,),
            in_specs=[pl.BlockSpec((1, 1, N_LOCAL_GRAN),
                                   lambda i, ng, nv: (jnp.minimum(i, last), 0, 0),
                                   memory_space=pltpu.SMEM),
                      pl.BlockSpec((1, 1, n_exp * per_blk), lambda i, ng, nv: (0, 0, 0),
                                   memory_space=pltpu.SMEM),
                      pl.BlockSpec((tt, D_MODEL), lambda i, ng, nv: (jnp.minimum(i, last), 0)),
                      _const_spec((1, D_MODEL)),
                      pl.BlockSpec((8, tt), lambda i, ng, nv: (jnp.minimum(i, last), 0))],
            out_specs=pl.BlockSpec(memory_space=pl.ANY),
            scratch_shapes=[pltpu.VMEM((2, LOCAL_ROWS, D_MODEL), BF16),
                            pltpu.VMEM((bm, D_MODEL), BF16),
                            pltpu.SemaphoreType.DMA((2,)),
                            pltpu.SemaphoreType.DMA((2,)),
                            pltpu.SMEM((2,), I32)]),
        out_shape=jax.ShapeDtypeStruct((n_blocks * bm, D_MODEL), BF16),
        compiler_params=arb,
        name="moe_dispatch",
    )(ngran, n_valid, dest, gap, x2d, g2d, metat)

    w_spec = lambda shape: pl.BlockSpec((None, None) + shape,
                                        lambda b, be, nv: (layer, be[b], 0, 0))
    row_blk = pl.BlockSpec((bm, D_MODEL), lambda b, be, nv: (jnp.minimum(b, nv[0] - 1), 0))
    ys = pl.pallas_call(
        _expert_kernel,
        grid_spec=pltpu.PrefetchScalarGridSpec(
            num_scalar_prefetch=2,
            grid=(n_blocks,),
            in_specs=[row_blk,
                      w_spec((D_MODEL, MOE_D_FF)),
                      w_spec((D_MODEL, MOE_D_FF)),
                      w_spec((MOE_D_FF, D_MODEL))],
            out_specs=row_blk,
            scratch_shapes=[pltpu.VMEM((D_MODEL, 2 * MOE_D_FF), BF16),
                            pltpu.VMEM((MOE_D_FF, D_MODEL), BF16),
                            pltpu.VMEM((bm, MOE_D_FF), BF16)]),
        out_shape=jax.ShapeDtypeStruct((n_blocks * bm, D_MODEL), BF16),
        input_output_aliases={2: 0},
        compiler_params=arb,
        name="moe_experts",
    )(block_expert, n_valid, xs, w_gate, w_up, w_down)

    final = norm_final is not None
    gf = (norm_final if final else jnp.ones((D_MODEL,), F32)).reshape(1, D_MODEL)
    src_spec = lambda nxt: pl.BlockSpec(
        (1, 1, N_LOCAL_GRAN), lambda i, ng: (jnp.minimum(i + nxt, last), 0, 0),
        memory_space=pltpu.SMEM)
    out = pl.pallas_call(
        functools.partial(_combine_kernel, final=final),
        grid_spec=pltpu.PrefetchScalarGridSpec(
            num_scalar_prefetch=1,
            grid=(n_tiles,),
            in_specs=[src_spec(0), src_spec(1),
                      pl.BlockSpec((tt, D_MODEL), lambda i, ng: (i, 0)),
                      pl.BlockSpec((tt, 128), lambda i, ng: (i, 0)),
                      _const_spec((1, D_MODEL)),
                      pl.BlockSpec(memory_space=pl.ANY)],
            out_specs=pl.BlockSpec((tt, D_MODEL), lambda i, ng: (i, 0)),
            scratch_shapes=[pltpu.VMEM((2, LOCAL_ROWS, D_MODEL), BF16),
                            pltpu.SemaphoreType.DMA((2,))]),
        out_shape=jax.ShapeDtypeStruct((t, D_MODEL), F32),
        compiler_params=arb,
        name="moe_combine",
    )(ngran, dest, dest, x2d, meta, gf, ys)
    return out


def _trunk(x, conv_prev, ssm_prev, p, *, ssd_nsub, ssd_ns, want_v):
    n_seq, seq_len, _ = x.shape
    q = seq_len if seq_len < SSD_CHUNK else SSD_CHUNK
    x, conv_new, ssm_new = _ssd_layer(
        x, conv_prev, ssm_prev, p["norm_mix"][0], p["ssd_w_in"][0], p["ssd_conv_w"][0],
        p["ssd_conv_b"][0], p["ssd_dt_bias"][0], p["ssd_a_log"][0], p["ssd_d"][0], p["ssd_norm"][0],
        p["ssd_w_out"][0], nsub=ssd_nsub, ns=ssd_ns, q=q)
    x2d = x.reshape(n_seq * seq_len, D_MODEL)
    x2d = _moe(x2d, p["norm_ffn"][0], p["moe_w_group"][0], p["moe_b_group"][0], p["moe_w_router"][0],
               p["moe_b_router"][0], p["moe_w_gate"], p["moe_w_up"], p["moe_w_down"], None, layer=0)
    x2d, v = _sg_layer(x2d, p["norm_mix"][1], p["sg_w_in"][0], p["sg_b_in"][0], p["sg_ln_g"][0],
                       p["sg_ln_b"][0], p["sg_w_s"][0], p["sg_b_s"][0], p["sg_w_out"][0],
                       seq_len=seq_len, want_v=want_v)
    y2d = _moe(x2d, p["norm_ffn"][1], p["moe_w_group"][1], p["moe_b_group"][1], p["moe_w_router"][1],
               p["moe_b_router"][1], p["moe_w_gate"], p["moe_w_up"], p["moe_w_down"],
               p["norm_final"], layer=1)
    y = y2d.reshape(n_seq, seq_len, D_MODEL)
    if want_v:
        v = v.reshape(1, n_seq, seq_len, SG_WIDTH)
    return y, conv_new, ssm_new, v


def kernel(x_prompt, x_sample, state_ssm, state_conv, norm_mix, norm_ffn, norm_final, ssd_w_in, ssd_conv_w, ssd_conv_b, ssd_dt_bias, ssd_a_log, ssd_d, ssd_norm, ssd_w_out, sg_w_in, sg_b_in, sg_ln_g, sg_ln_b, sg_w_s, sg_b_s, sg_w_out, moe_w_group, moe_b_group, moe_w_router, moe_b_router, moe_w_gate, moe_w_up, moe_w_down):
    p = dict(norm_mix=norm_mix, norm_ffn=norm_ffn, norm_final=norm_final, ssd_w_in=ssd_w_in,
             ssd_conv_w=ssd_conv_w, ssd_conv_b=ssd_conv_b, ssd_dt_bias=ssd_dt_bias,
             ssd_a_log=ssd_a_log, ssd_d=ssd_d, ssd_norm=ssd_norm, ssd_w_out=ssd_w_out,
             sg_w_in=sg_w_in, sg_b_in=sg_b_in, sg_ln_g=sg_ln_g, sg_ln_b=sg_ln_b, sg_w_s=sg_w_s,
             sg_b_s=sg_b_s, sg_w_out=sg_w_out, moe_w_group=moe_w_group, moe_b_group=moe_b_group,
             moe_w_router=moe_w_router, moe_b_router=moe_b_router, moe_w_gate=moe_w_gate,
             moe_w_up=moe_w_up, moe_w_down=moe_w_down)
    nb = x_prompt.shape[0]
    conv0 = jnp.zeros((1, nb, CONV_W - 1, CONV_DIM), F32)
    ssm0 = jnp.zeros((1, nb, N_HEADS, HEAD_DIM, D_STATE), F32)
    y_p, conv_p, ssm_p, _ = _trunk(x_prompt, conv0, ssm0, p, ssd_nsub=2, ssd_ns=1, want_v=False)
    y_s, conv_s, ssm_s, v_s = _trunk(x_sample, state_conv, state_ssm, p, ssd_nsub=1, ssd_ns=4,
                                     want_v=True)
    return (y_p, y_s, ssm_p, conv_p, ssm_s, conv_s, v_s)
```

```python
import functools
import math

import jax
import jax.numpy as jnp
import numpy as np
from jax import lax
from jax.experimental import pallas as pl
from jax.experimental.pallas import tpu as pltpu

F32 = jnp.float32
BF16 = jnp.bfloat16
I32 = jnp.int32

D_MODEL = 1024
N_HEADS = 32
HEAD_DIM = 64
N_GROUPS = 4
D_STATE = 128
D_INNER = N_HEADS * HEAD_DIM
GROUP_W = D_INNER // N_GROUPS
CONV_W = 4
CONV_DIM = D_INNER + 2 * N_GROUPS * D_STATE
DT_PAD = 128
SSD_CHUNK = 128
PROJ_CHUNK = 256
SG_WIDTH = 2 * D_MODEL
SG_GROUPS = 8
SG_GROUP_DIM = SG_WIDTH // SG_GROUPS
SG_CHUNK = 128
SG_ROWS = 256
MOE_GROUPS = 4
MOE_EPG = 8
MOE_EXPERTS = MOE_GROUPS * MOE_EPG
MOE_D_FF = 256
NORM_EPS = 1e-6
LN_EPS = 1e-5

MOE_TILE = 256
ROUTER_TILES = 4


def _expert_block(n_tokens):
    return 512 if 2 * n_tokens >= 512 * MOE_EXPERTS else 256


GRANULE = 16
LOCAL_ROWS = 2 * MOE_TILE + MOE_EXPERTS * GRANULE
N_LOCAL_GRAN = LOCAL_ROWS // GRANULE
VMEM_LIMIT = 56 * 1024 * 1024


def _sigmoid(x):
    return 0.5 * (jnp.tanh(0.5 * x) + 1.0)


def _silu(x):
    return x * _sigmoid(x)


def _softplus(x):
    return jnp.maximum(x, 0.0) + jnp.log(1.0 + jnp.exp(-jnp.abs(x)))


def _gelu_tanh(x):
    c = math.sqrt(2.0 / math.pi)
    return x * (0.5 * (1.0 + jnp.tanh(c * (x + 0.044715 * (x * x * x)))))


def _rms(x, g):
    return x * lax.rsqrt(jnp.mean(x * x, axis=-1, keepdims=True) + NORM_EPS) * g


def _split3(x):
    a = x.astype(BF16)
    r = x - a.astype(F32)
    b = r.astype(BF16)
    c = (r - b.astype(F32)).astype(BF16)
    return a, b, c


def _dot(a, b):
    return jnp.dot(a, b, preferred_element_type=F32)


def _dot_nt(a, b):
    return lax.dot_general(a, b, (((1,), (1,)), ((), ())), preferred_element_type=F32)


def _dot_tn(a, b):
    return lax.dot_general(a, b, (((0,), (0,)), ((), ())), preferred_element_type=F32)


def _const_spec(shape):
    nd = len(shape)
    return pl.BlockSpec(shape, lambda *_: (0,) * nd)


def _ssd_kernel(x_ref, g_ref, win_ref, cw_ref, cb_ref, dtb_ref, alog_ref, dsk_ref, ng_ref,
                wout_ref, tri_ref, ones_ref, cin_ref, sin_ref,
                xo_ref, cout_ref, sout_ref,
                pend_x, pend_z, pend_xbc, pend_dt, xres_scr, zg_scr,
                prev_scr, xc_scr, yoff_scr, y_scr, xw_scr, hn_scr, yn_scr,
                *, nsub, ns, q, n_chunks):
    i = pl.program_id(0)
    first_chunk = lax.rem(jnp.maximum(i - 1, 0), n_chunks) == 0
    nseq = nsub * ns
    sb = ns * q
    r = nsub * sb

    @pl.when(i == 0)
    def _():
        pend_x[...] = jnp.zeros_like(pend_x)
        pend_z[...] = jnp.zeros_like(pend_z)
        pend_xbc[...] = jnp.zeros_like(pend_xbc)
        pend_dt[...] = jnp.zeros_like(pend_dt)

    tr = min(r, 128)
    row_tiles = [slice(a, a + tr) for a in range(0, r, tr)]

    def col_tiles(total, rows=tr):
        w = min(total, max(128, (32 * 1024) // rows))
        return [slice(c, c + w) for c in range(0, total, w)]

    for rt in row_tiles:
        for ct in col_tiles(D_MODEL):
            xres_scr[rt, ct] = pend_x[rt, ct]
        for ct in col_tiles(D_INNER):
            zg_scr[rt, ct] = _silu(pend_z[rt, ct])

    for s in range(nseq):
        for ct in col_tiles(D_MODEL, q):
            pend_x[s * q:(s + 1) * q, ct] = x_ref[s, :, ct]
    for rt in row_tiles:
        ss = None
        for ct in col_tiles(D_MODEL):
            xv = pend_x[rt, ct]
            part = jnp.sum(xv * xv, axis=-1, keepdims=True)
            ss = part if ss is None else ss + part
        scale = lax.rsqrt(ss * (1.0 / D_MODEL) + NORM_EPS)
        for ct in col_tiles(D_MODEL):
            hn_scr[rt, ct] = (pend_x[rt, ct] * scale * g_ref[:, ct]).astype(BF16)

    def proj_chunk(dst, dst_col, w_col, width):
        def run():
            dst[:, dst_col:dst_col + width] = _dot(hn_scr[...], win_ref[:, w_col:w_col + width])
        return run
    z_chunks = [proj_chunk(pend_z, c, c, PROJ_CHUNK) for c in range(0, D_INNER, PROJ_CHUNK)]
    scan_chunks = [proj_chunk(pend_xbc, c, D_INNER + c, PROJ_CHUNK)
                   for c in range(0, CONV_DIM, PROJ_CHUNK)]
    scan_chunks.append(proj_chunk(pend_dt, 0, D_INNER + CONV_DIM, DT_PAD))

    @pl.when(first_chunk)
    def _():
        sout_ref[...] = sin_ref[...]
        prev_scr[...] = jnp.zeros_like(prev_scr)
        for s in range(nseq):
            for k in range(CONV_W - 1):
                prev_scr[pl.ds(s * 8 + 5 + k, 1), :] = cin_ref[s, pl.ds(k, 1), :]

    sub8 = lax.broadcasted_iota(I32, (8, 1), 0)
    for s in range(nseq):
        srows = slice(s * q, (s + 1) * q)
        for ct in col_tiles(CONV_DIM, 4 * q):
            xq = pend_xbc[srows, ct]
            hist = prev_scr[s * 8:(s + 1) * 8, ct]
            acc = cb_ref[:, ct] + cw_ref[pl.ds(CONV_W - 1, 1), ct] * xq
            for j in range(1, CONV_W):
                sh = pltpu.roll(xq, j, 0)
                head = jnp.where(sub8 < j, pltpu.roll(hist, j, 0), sh[0:8, :])
                sh = head if q == 8 else jnp.concatenate([head, sh[8:, :]], axis=0)
                acc = acc + cw_ref[pl.ds(CONV_W - 1 - j, 1), ct] * sh
            xc_scr[srows, ct] = _silu(acc)
            last = xq[q - 8:q, :]
            prev_scr[s * 8:(s + 1) * 8, ct] = last
            for k in range(CONV_W - 1):
                cout_ref[s, pl.ds(k, 1), ct] = last[5 + k:6 + k, :]
    chunks = z_chunks + scan_chunks
    n_pairs = nsub * (N_HEADS // 2)
    emit_at = {}
    for k, ch in enumerate(chunks):
        emit_at.setdefault((k * n_pairs) // len(chunks), []).append(ch)

    tri = tri_ref[...]
    trib = tri.astype(BF16)
    onesb = ones_ref[...].astype(BF16)
    mask = tri > 0.5
    rowseq = lax.shift_right_logical(lax.broadcasted_iota(I32, (sb, 1), 0), int(math.log2(q)))
    lo = lax.broadcasted_iota(I32, (sb, 128), 1) < HEAD_DIM
    neg_a = -jnp.exp(alog_ref[...])

    for u in range(nsub):
        rows = slice(u * sb, (u + 1) * sb)
        xs_ref = xc_scr.at[rows, 0:D_INNER]

        def b_of(g):
            return xc_scr[rows, D_INNER + g * D_STATE:D_INNER + (g + 1) * D_STATE]

        def c_of(g):
            c0 = D_INNER + (N_GROUPS + g) * D_STATE
            return xc_scr[rows, c0:c0 + D_STATE].astype(BF16)

        dt = _softplus(pend_dt[rows, :] + dtb_ref[...])
        d1, d2, d3 = _split3(dt * neg_a)
        cs = _dot(trib, d1) + _dot(trib, d2) + _dot(trib, d3)
        cl = _dot(onesb, d1) + _dot(onesb, d2) + _dot(onesb, d3)
        ecs = jnp.exp(cs)
        wgt = dt * jnp.exp(cl - cs)
        ecl = jnp.exp(cl)
        cs_t = cs.T
        dt_t = dt.T

        for g in range(N_GROUPS):
            cg = c_of(g)
            acc = None
            for s in range(ns):
                st = sout_ref[u * ns + s, g * GROUP_W:(g + 1) * GROUP_W, :].astype(BF16)
                yo = _dot_nt(cg, st)
                if ns > 1:
                    yo = jnp.where(rowseq == s, yo, 0.0)
                acc = yo if acc is None else acc + yo
            yoff_scr[rows, g * GROUP_W:(g + 1) * GROUP_W] = acc

        for g in range(N_GROUPS):
            sc = _dot_nt(c_of(g), b_of(g).astype(BF16))
            for jj in range(N_HEADS // N_GROUPS // 2):
                j = g * (N_HEADS // N_GROUPS // 2) + jj
                ms = []
                for h in (2 * j, 2 * j + 1):
                    diff = cs[:, h:h + 1] - cs_t[h:h + 1, :]
                    dec = jnp.exp(jnp.where(mask, diff, -jnp.inf))
                    ms.append((sc * dec * dt_t[h:h + 1, :]).astype(BF16))
                lhs = jnp.concatenate(ms, axis=1)
                cols = slice(j * 128, (j + 1) * 128)
                xp = xs_ref[:, cols]
                rhs = jnp.concatenate([jnp.where(lo, xp, 0.0).astype(BF16),
                                       jnp.where(lo, 0.0, xp).astype(BF16)], axis=0)
                yd = _dot(lhs, rhs)
                ecs_p = jnp.where(lo, ecs[:, 2 * j:2 * j + 1], ecs[:, 2 * j + 1:2 * j + 2])
                wgt_p = jnp.where(lo, wgt[:, 2 * j:2 * j + 1], wgt[:, 2 * j + 1:2 * j + 2])
                y_scr[rows, cols] = yd + yoff_scr[rows, cols] * ecs_p + xp * dsk_ref[:, cols]
                xw_scr[rows, cols] = (xp * wgt_p).astype(BF16)
                pair = u * (N_HEADS // 2) + j
                for ch in emit_at.get(pair, []):
                    ch()

        for g in range(N_GROUPS):
            xwg = xw_scr[rows, g * GROUP_W:(g + 1) * GROUP_W]
            for s in range(ns):
                bg = b_of(g)
                if ns > 1:
                    bg = jnp.where(rowseq == s, bg, 0.0)
                upd = _dot_tn(xwg, bg.astype(BF16))
                for hh in range(GROUP_W // HEAD_DIM):
                    h = g * (GROUP_W // HEAD_DIM) + hh
                    dec = jnp.broadcast_to(ecl[s * q:s * q + 1, h:h + 1], (HEAD_DIM, D_STATE))
                    hrows = slice(h * HEAD_DIM, (h + 1) * HEAD_DIM)
                    sout_ref[u * ns + s, hrows, :] = (
                        sout_ref[u * ns + s, hrows, :] * dec + upd[hh * HEAD_DIM:(hh + 1) * HEAD_DIM, :])

    for rt in row_tiles:
        for g in range(N_GROUPS):
            cts = [slice(g * GROUP_W + c.start, g * GROUP_W + c.stop) for c in col_tiles(GROUP_W)]
            ss = None
            for ct in cts:
                yz = y_scr[rt, ct] * zg_scr[rt, ct]
                part = jnp.sum(yz * yz, axis=-1, keepdims=True)
                ss = part if ss is None else ss + part
            scale = lax.rsqrt(ss * (1.0 / GROUP_W) + NORM_EPS)
            for ct in cts:
                yn_scr[rt, ct] = (y_scr[rt, ct] * zg_scr[rt, ct] * scale * ng_ref[:, ct]).astype(BF16)

    for c0 in range(0, D_MODEL, PROJ_CHUNK):
        ct = slice(c0, c0 + PROJ_CHUNK)
        o = xres_scr[:, ct] + _dot(yn_scr[...], wout_ref[:, ct])
        for s in range(nseq):
            xo_ref[s, :, ct] = o[s * q:(s + 1) * q, :]


def _ssd_layer(x, conv_prev, ssm_prev, norm_g, w_in, conv_w, conv_b, dt_bias, a_log, d_skip,
               norm_y, w_out, *, nsub, ns, q):
    n_seq, seq_len, _ = x.shape
    nseq = nsub * ns
    sb = ns * q
    r = nsub * sb
    n_chunks = seq_len // q
    pad = DT_PAD - N_HEADS
    win = jnp.concatenate([w_in, jnp.zeros((D_MODEL, pad), F32)], axis=1).astype(BF16)
    dtb = jnp.pad(dt_bias, (0, pad)).reshape(1, DT_PAD)
    alog = jnp.pad(a_log, (0, pad)).reshape(1, DT_PAD)
    dsk = jnp.repeat(d_skip, HEAD_DIM).reshape(1, D_INNER)
    blk = np.kron(np.eye(ns), np.ones((q, q)))
    tri = jnp.asarray(blk * np.tril(np.ones((sb, sb))), F32)
    ones = jnp.asarray(blk, F32)
    state = ssm_prev.reshape(n_seq, D_INNER, D_STATE)
    conv_prev = conv_prev.reshape(n_seq, CONV_W - 1, CONV_DIM)

    kern = functools.partial(_ssd_kernel, nsub=nsub, ns=ns, q=q, n_chunks=n_chunks)
    out_shape = (jax.ShapeDtypeStruct(x.shape, F32),
                 jax.ShapeDtypeStruct((n_seq, CONV_W - 1, CONV_DIM), F32),
                 jax.ShapeDtypeStruct((n_seq, D_INNER, D_STATE), F32))
    n_steps = (n_seq // nseq) * n_chunks
    nxt = lambda i: jnp.minimum(i, n_steps - 1)
    cur = lambda i: jnp.maximum(i - 1, 0)
    in_row_spec = pl.BlockSpec((nseq, q, D_MODEL),
                               lambda i: (nxt(i) // n_chunks, nxt(i) % n_chunks, 0))
    row_spec = pl.BlockSpec((nseq, q, D_MODEL),
                            lambda i: (cur(i) // n_chunks, cur(i) % n_chunks, 0))
    conv_spec = pl.BlockSpec((nseq, CONV_W - 1, CONV_DIM), lambda i: (cur(i) // n_chunks, 0, 0))
    state_spec = pl.BlockSpec((nseq, D_INNER, D_STATE), lambda i: (cur(i) // n_chunks, 0, 0))
    in_specs = [in_row_spec,
                _const_spec((1, D_MODEL)),
                _const_spec(win.shape),
                _const_spec((CONV_W, CONV_DIM)),
                _const_spec((1, CONV_DIM)),
                _const_spec((1, DT_PAD)),
                _const_spec((1, DT_PAD)),
                _const_spec((1, D_INNER)),
                _const_spec((1, D_INNER)),
                _const_spec((D_INNER, D_MODEL)),
                _const_spec((sb, sb)),
                _const_spec((sb, sb)),
                conv_spec, state_spec]
    x_new, conv_new, state_new = pl.pallas_call(
        kern,
        grid=(n_steps + 1,),
        in_specs=in_specs,
        out_specs=(row_spec, conv_spec, state_spec),
        out_shape=out_shape,
        scratch_shapes=[pltpu.VMEM((r, D_MODEL), F32),
                        pltpu.VMEM((r, D_INNER), F32),
                        pltpu.VMEM((r, CONV_DIM), F32),
                        pltpu.VMEM((r, DT_PAD), F32),
                        pltpu.VMEM((r, D_MODEL), F32),
                        pltpu.VMEM((r, D_INNER), F32),
                        pltpu.VMEM((nseq * 8, CONV_DIM), F32),
                        pltpu.VMEM((r, CONV_DIM), F32),
                        pltpu.VMEM((r, D_INNER), F32),
                        pltpu.VMEM((r, D_INNER), F32),
                        pltpu.VMEM((r, D_INNER), BF16),
                        pltpu.VMEM((r, D_MODEL), BF16),
                        pltpu.VMEM((r, D_INNER), BF16)],
        compiler_params=pltpu.CompilerParams(
            dimension_semantics=("arbitrary",), vmem_limit_bytes=VMEM_LIMIT),
        name="ssd_layer",
    )(x, norm_g.reshape(1, D_MODEL), win, conv_w, conv_b.reshape(1, CONV_DIM), dtb, alog, dsk,
      norm_y.reshape(1, D_INNER), w_out.astype(BF16), tri, ones, conv_prev, state)
    return (x_new, conv_new.reshape(1, n_seq, CONV_W - 1, CONV_DIM),
            state_new.reshape(1, n_seq, N_HEADS, HEAD_DIM, D_STATE))


def _sg_kernel(x_ref, g_ref, win_ref, bin_ref, lng_ref, lnb_ref, wmix_ref, bmix_ref, wout_ref,
               xo_ref, *rest, r, want_v):
    v_ref = rest[0] if want_v else None
    hn_scr, uv_scr, vb_scr, um_scr = rest[-4:]
    row_tiles = [slice(a, a + 128) for a in range(0, r, 128)]
    col128 = lambda total: [slice(c, c + 128) for c in range(0, total, 128)]

    for rt in row_tiles:
        ss = None
        for ct in col128(D_MODEL):
            xv = x_ref[rt, ct]
            part = jnp.sum(xv * xv, axis=-1, keepdims=True)
            ss = part if ss is None else ss + part
        scale = lax.rsqrt(ss * (1.0 / D_MODEL) + NORM_EPS)
        for ct in col128(D_MODEL):
            hn_scr[rt, ct] = (x_ref[rt, ct] * scale * g_ref[:, ct]).astype(BF16)

    for c0 in range(0, 2 * SG_WIDTH, PROJ_CHUNK):
        h = _dot(hn_scr[...], win_ref[:, c0:c0 + PROJ_CHUNK])
        for rt in row_tiles:
            for cc in range(0, PROJ_CHUNK, 128):
                ct = slice(c0 + cc, c0 + cc + 128)
                uv_scr[rt, ct] = _gelu_tanh(h[rt, cc:cc + 128] + bin_ref[:, ct])

    for rt in row_tiles:
        vcols = [slice(SG_WIDTH + c.start, SG_WIDTH + c.stop) for c in col128(SG_WIDTH)]
        tot = None
        for ct in vcols:
            part = jnp.sum(uv_scr[rt, ct], axis=-1, keepdims=True)
            tot = part if tot is None else tot + part
        mu = tot * (1.0 / SG_WIDTH)
        ss = None
        for ct in vcols:
            vc = uv_scr[rt, ct] - mu
            part = jnp.sum(vc * vc, axis=-1, keepdims=True)
            ss = part if ss is None else ss + part
        scale = lax.rsqrt(ss * (1.0 / SG_WIDTH) + LN_EPS)
        for ct, c in zip(vcols, col128(SG_WIDTH)):
            vn = (uv_scr[rt, ct] - mu) * scale * lng_ref[:, c] + lnb_ref[:, c]
            if want_v:
                v_ref[rt, c] = vn
            vb_scr[rt, c] = vn.astype(BF16)

    for rt in row_tiles:
        for g in range(SG_GROUPS):
            cols = slice(g * SG_GROUP_DIM, (g + 1) * SG_GROUP_DIM)
            mixed = _dot(wmix_ref[g], vb_scr[rt, cols]) + bmix_ref[:, cols]
            um_scr[rt, cols] = (uv_scr[rt, cols] * mixed).astype(BF16)

    for c0 in range(0, D_MODEL, PROJ_CHUNK):
        ct = slice(c0, c0 + PROJ_CHUNK)
        xo_ref[:, ct] = x_ref[:, ct] + _dot(um_scr[...], wout_ref[:, ct])


def _sg_layer(x2d, norm_g, w_in, b_in, ln_g, ln_b, w_s, b_s, w_out, *, seq_len, want_v):
    r = SG_ROWS
    t = x2d.shape[0]
    q = min(seq_len, SG_CHUNK)
    reps = SG_CHUNK // q
    ws = jnp.tril(w_s)[:, :q, :q]
    wmix = jnp.einsum("ab,gts->gatbs", jnp.eye(reps, dtype=F32), ws)
    wmix = wmix.reshape(SG_GROUPS, SG_CHUNK, SG_CHUNK)
    bmix = jnp.tile(jnp.repeat(b_s.T[:q], SG_GROUP_DIM, axis=1), (reps, 1))
    row_spec = pl.BlockSpec((r, D_MODEL), lambda i: (i, 0))
    v_spec = pl.BlockSpec((r, SG_WIDTH), lambda i: (i, 0))
    out_shape = [jax.ShapeDtypeStruct(x2d.shape, F32)]
    out_specs = [row_spec]
    if want_v:
        out_shape.append(jax.ShapeDtypeStruct((t, SG_WIDTH), F32))
        out_specs.append(v_spec)
    outs = pl.pallas_call(
        functools.partial(_sg_kernel, r=r, want_v=want_v),
        grid=(t // r,),
        scratch_shapes=[pltpu.VMEM((r, D_MODEL), BF16),
                        pltpu.VMEM((r, 2 * SG_WIDTH), F32),
                        pltpu.VMEM((r, SG_WIDTH), BF16),
                        pltpu.VMEM((r, SG_WIDTH), BF16)],
        in_specs=[row_spec,
                  _const_spec((1, D_MODEL)),
                  _const_spec((D_MODEL, 2 * SG_WIDTH)),
                  _const_spec((1, 2 * SG_WIDTH)),
                  _const_spec((1, SG_WIDTH)),
                  _const_spec((1, SG_WIDTH)),
                  _const_spec((SG_GROUPS, SG_CHUNK, SG_CHUNK)),
                  _const_spec((SG_CHUNK, SG_WIDTH)),
                  _const_spec((SG_WIDTH, D_MODEL))],
        out_specs=out_specs,
        out_shape=out_shape,
        compiler_params=pltpu.CompilerParams(
            dimension_semantics=("arbitrary",), vmem_limit_bytes=VMEM_LIMIT),
        name="sg_layer",
    )(x2d, norm_g.reshape(1, D_MODEL), w_in.astype(BF16), b_in.reshape(1, 2 * SG_WIDTH),
      ln_g.reshape(1, SG_WIDTH), ln_b.reshape(1, SG_WIDTH), wmix.astype(BF16), bmix,
      w_out.astype(BF16))
    return outs if want_v else (outs[0], None)


ROUTER_ROWS = 64


def _router_t_kernel(x_ref, g_ref, wt_ref, bt_ref, usl_ref, lsl_ref, meta_ref, metat_ref, pad_ref,
                     h1_scr, h2_scr):
    wt = wt_ref[...]
    w1 = wt.astype(BF16)
    w2 = (wt - w1.astype(F32)).astype(BF16)
    tt = MOE_TILE
    n = x_ref.shape[0]

    for r0 in range(0, n, 128):
        rt = slice(r0, r0 + 128)
        ss = None
        for c0 in range(0, D_MODEL, 256):
            xv = x_ref[rt, c0:c0 + 256]
            part = jnp.sum(xv * xv, axis=-1, keepdims=True)
            ss = part if ss is None else ss + part
        scale = lax.rsqrt(ss * (1.0 / D_MODEL) + NORM_EPS)
        for c0 in range(0, D_MODEL, 256):
            ct = slice(c0, c0 + 256)
            hn = x_ref[rt, ct] * scale * g_ref[:, ct]
            hi = hn.astype(BF16)
            h1_scr[rt, ct] = hi
            h2_scr[rt, ct] = (hn - hi.astype(F32)).astype(BF16)

    h1 = h1_scr[...]
    logits = _dot_nt(w1, h1) + _dot_nt(w2, h1) + _dot_nt(w1, h2_scr[...])
    select = _select_experts(logits, bt_ref[...])
    for k in range(n // tt):
        cols = slice(k * tt, (k + 1) * tt)
        metat, pad = _local_order(select[:, cols], usl_ref[...], lsl_ref[...])
        metat_ref[k * 8:(k + 1) * 8, :] = metat
        full = jnp.concatenate([metat, jnp.zeros((128 - 8, tt), F32)], axis=0)
        meta_ref[cols, :] = full.T
        pad_ref[k * MOE_EXPERTS:(k + 1) * MOE_EXPERTS, :] = pad


def _select_experts(logits, bt):
    n = logits.shape[1]
    reps = n // bt.shape[1]
    logits = logits + (bt if reps == 1 else jnp.concatenate([bt] * reps, axis=1))
    tt = n
    row8 = lax.broadcasted_iota(I32, (8, tt), 0).astype(F32)
    gl = jnp.where(row8 < MOE_GROUPS, logits[0:8, :], -jnp.inf)
    gmax = jnp.max(gl, axis=0, keepdims=True)
    g_top = jnp.min(jnp.where(gl == gmax, row8, 8.0), axis=0, keepdims=True)
    p_g = 1.0 / jnp.sum(jnp.exp(gl - gmax), axis=0, keepdims=True)

    el = logits[8:16, :]
    for grp in range(1, MOE_GROUPS):
        el = jnp.where(g_top == grp, logits[8 + 8 * grp:16 + 8 * grp, :], el)
    emax = jnp.max(el, axis=0, keepdims=True)
    ee = jnp.exp(el - emax)
    prob = ee / jnp.sum(ee, axis=0, keepdims=True)
    p1 = jnp.max(prob, axis=0, keepdims=True)
    i1 = jnp.min(jnp.where(prob == p1, row8, 8.0), axis=0, keepdims=True)
    prob2 = jnp.where(row8 == i1, -1.0, prob)
    p2 = jnp.max(prob2, axis=0, keepdims=True)
    i2 = jnp.min(jnp.where(prob2 == p2, row8, 8.0), axis=0, keepdims=True)
    psum = p1 + p2
    gate1 = p_g * (p1 / psum)
    gate2 = p_g * (p2 / psum)
    e1 = g_top * MOE_EPG + i1
    e2 = g_top * MOE_EPG + i2
    out = jnp.where(row8 == 0, e1, 0.0)
    out = jnp.where(row8 == 1, e2, out)
    out = jnp.where(row8 == 2, gate1, out)
    return jnp.where(row8 == 3, gate2, out)


def _local_order(select, usl, lsl):
    tt = select.shape[1]
    row8 = lax.broadcasted_iota(I32, (8, tt), 0).astype(F32)
    e1, e2 = select[0:1, :], select[1:2, :]
    rowe = lax.broadcasted_iota(I32, (MOE_EXPERTS, tt), 0).astype(F32)
    sel1 = rowe == e1
    sel2 = rowe == e2
    onehot = jnp.where(sel1 | sel2, 1.0, 0.0)
    before = _dot(onehot.astype(BF16), usl)
    cnt = jnp.sum(onehot, axis=1, keepdims=True)
    pad = jnp.floor((cnt + (GRANULE - 1)) * (1.0 / GRANULE)) * GRANULE
    padb = jnp.broadcast_to(pad, (MOE_EXPERTS, tt))
    local = before + _dot(lsl, padb.astype(BF16))
    slot1 = jnp.sum(jnp.where(sel1, local, 0.0), axis=0, keepdims=True)
    slot2 = jnp.sum(jnp.where(sel2, local, 0.0), axis=0, keepdims=True)

    metat = jnp.where(row8 == 4, slot1, select)
    metat = jnp.where(row8 == 5, slot2, metat)
    return metat, padb[:, 0:128]


def _granule(ref, g):
    return ref.at[pl.ds(pl.multiple_of(g * GRANULE, GRANULE), GRANULE), :]


def _dispatch_kernel(ngran_ref, nv_ref, dest_ref, gap_ref, x_ref, g_ref, meta_ref, xs_hbm,
                     buf, zbuf, sem, zsem, nstart):
    i = pl.program_id(0)
    n_tiles = pl.num_programs(0) - 1
    slot = i % 2
    tt = x_ref.shape[0]
    bm = zbuf.shape[0]
    n_blocks = xs_hbm.shape[0] // bm

    def out_copy(sl, g, d):
        return pltpu.make_async_copy(_granule(buf.at[sl], g), _granule(xs_hbm, d), sem.at[sl])

    def drain(sl):
        def body(_, carry):
            out_copy(sl, 0, 0).wait()
            return carry
        lax.fori_loop(0, nstart[sl], body, 0)

    @pl.when(i == 0)
    def _():
        nstart[0] = 0
        nstart[1] = 0

    @pl.when(i < n_tiles)
    def _():
        drain(slot)
        hn = _rms(x_ref[...], g_ref[...]).astype(BF16)
        mt = meta_ref[...]
        for r0 in range(0, LOCAL_ROWS, 256):
            rows = (lax.broadcasted_iota(I32, (256, tt), 0) + r0).astype(F32)
            onehot = jnp.where((rows == mt[4:5, :]) | (rows == mt[5:6, :]), 1.0, 0.0).astype(BF16)
            buf[slot, r0:r0 + 256, :] = _dot(onehot, hn).astype(BF16)
        ng = ngran_ref[i]

        def body(g, carry):
            out_copy(slot, g, dest_ref[0, 0, g]).start()
            return carry
        lax.fori_loop(0, ng, body, 0)
        nstart[slot] = ng

    @pl.when(i == n_tiles)
    def _():
        drain(0)
        drain(1)
        zbuf[...] = jnp.zeros_like(zbuf)

        def gap_copy(d):
            return pltpu.make_async_copy(_granule(zbuf, 0), _granule(xs_hbm, d), zsem.at[0])

        def tail_copy(b):
            return pltpu.make_async_copy(
                zbuf, xs_hbm.at[pl.ds(pl.multiple_of(b * bm, bm), bm), :], zsem.at[1])

        def each_gap(fn):
            def body(j, carry):
                d = gap_ref[0, 0, j]

                @pl.when(d >= 0)
                def _():
                    fn(gap_copy(d))
                return carry
            lax.fori_loop(0, gap_ref.shape[2], body, 0)

        def each_tail(fn):
            def body(b, carry):
                fn(tail_copy(b))
                return carry
            lax.fori_loop(nv_ref[0], n_blocks, body, 0)

        each_gap(lambda cp: cp.start())
        each_tail(lambda cp: cp.start())
        each_gap(lambda cp: cp.wait())
        each_tail(lambda cp: cp.wait())


def _expert_kernel(bexp_ref, nv_ref, xs_ref, wg_ref, wu_ref, wd_ref, ys_ref,
                   wgu_b, wd_b, act_scr):
    b = pl.program_id(0)
    ff = wg_ref.shape[1]

    @pl.when(b < nv_ref[0])
    def _():
        @pl.when((b == 0) | (bexp_ref[b] != bexp_ref[jnp.maximum(b - 1, 0)]))
        def _():
            for k0 in range(0, wg_ref.shape[0], 256):
                wgu_b[k0:k0 + 256, 0:ff] = wg_ref[k0:k0 + 256, :].astype(BF16)
                wgu_b[k0:k0 + 256, ff:2 * ff] = wu_ref[k0:k0 + 256, :].astype(BF16)
            for k0 in range(0, ff, 64):
                wd_b[k0:k0 + 64, :] = wd_ref[k0:k0 + 64, :].astype(BF16)

        bm = xs_ref.shape[0]
        for m0 in range(0, bm, 256):
            h = _dot(xs_ref[m0:m0 + 256, :], wgu_b[...])
            for r0 in range(0, 256, 128):
                for c0 in range(0, ff, 128):
                    hg = h[r0:r0 + 128, c0:c0 + 128]
                    hu = h[r0:r0 + 128, ff + c0:ff + c0 + 128]
                    act_scr[m0 + r0:m0 + r0 + 128, c0:c0 + 128] = (_silu(hg) * hu).astype(BF16)
            ys_ref[m0:m0 + 256, :] = _dot(act_scr[m0:m0 + 256, :], wd_b[...]).astype(BF16)


def _combine_kernel(ngran_ref, src_ref, srcn_ref, x_ref, meta_ref, gf_ref, ys_hbm, o_ref, buf, sem,
                    *, final):
    i = pl.program_id(0)
    n = pl.num_programs(0)
    slot = i % 2
    tt = x_ref.shape[0]

    def in_copy(sl, g, d):
        return pltpu.make_async_copy(_granule(ys_hbm, d), _granule(buf.at[sl], g), sem.at[sl])

    def gather(idx_ref, sl, ng):
        def body(g, carry):
            in_copy(sl, g, idx_ref[0, 0, g]).start()
            return carry
        lax.fori_loop(0, ng, body, 0)

    @pl.when(i == 0)
    def _():
        buf[...] = jnp.zeros_like(buf)
        gather(src_ref, 0, ngran_ref[0])

    @pl.when(i + 1 < n)
    def _():
        gather(srcn_ref, 1 - slot, ngran_ref[i + 1])

    def wait_body(_, carry):
        in_copy(slot, 0, 0).wait()
        return carry
    lax.fori_loop(0, ngran_ref[i], wait_body, 0)

    ys = buf[slot]
    meta = meta_ref[...]
    cols = lax.broadcasted_iota(I32, (tt, LOCAL_ROWS), 1).astype(F32)
    pick = (jnp.where(cols == meta[:, 4:5], meta[:, 2:3], 0.0)
            + jnp.where(cols == meta[:, 5:6], meta[:, 3:4], 0.0)).astype(BF16)
    y = x_ref[...] + _dot(pick, ys)
    if final:
        y = _rms(y, gf_ref[...])
    o_ref[...] = y


def _moe(x2d, norm_g, w_grp, b_grp, w_rt, b_rt, w_gate, w_up, w_down, norm_final, *, layer):
    t = x2d.shape[0]
    tt = MOE_TILE
    bm = _expert_block(t)
    n_tiles = t // tt
    n_exp = MOE_EXPERTS
    zrow = lambda n: jnp.zeros((n, D_MODEL), F32)
    wt = jnp.concatenate([w_grp.T, zrow(8 - MOE_GROUPS), w_rt.T, zrow(ROUTER_ROWS - 8 - n_exp)], axis=0)
    bt = jnp.concatenate([b_grp, jnp.zeros((8 - MOE_GROUPS,), F32), b_rt,
                          jnp.zeros((ROUTER_ROWS - 8 - n_exp,), F32)])
    bt = jnp.broadcast_to(bt[:, None], (ROUTER_ROWS, tt))
    usl = jnp.asarray(np.triu(np.ones((tt, tt)), 1), BF16)
    lsl = jnp.asarray(np.tril(np.ones((n_exp, n_exp)), -1), BF16)
    g2d = norm_g.reshape(1, D_MODEL)
    arb = pltpu.CompilerParams(dimension_semantics=("arbitrary",), vmem_limit_bytes=VMEM_LIMIT)

    meta, metat, pad = pl.pallas_call(
        _router_t_kernel,
        grid=(n_tiles // ROUTER_TILES,),
        in_specs=[pl.BlockSpec((ROUTER_TILES * tt, D_MODEL), lambda i: (i, 0)),
                  _const_spec((1, D_MODEL)),
                  _const_spec((ROUTER_ROWS, D_MODEL)),
                  _const_spec((ROUTER_ROWS, tt)),
                  _const_spec((tt, tt)),
                  _const_spec((n_exp, n_exp))],
        out_specs=(pl.BlockSpec((ROUTER_TILES * tt, 128), lambda i: (i, 0)),
                   pl.BlockSpec((ROUTER_TILES * 8, tt), lambda i: (i, 0)),
                   pl.BlockSpec((ROUTER_TILES * n_exp, 128), lambda i: (i, 0))),
        out_shape=(jax.ShapeDtypeStruct((t, 128), F32),
                   jax.ShapeDtypeStruct((n_tiles * 8, tt), F32),
                   jax.ShapeDtypeStruct((n_tiles * n_exp, 128), F32)),
        scratch_shapes=[pltpu.VMEM((ROUTER_TILES * tt, D_MODEL), BF16),
                        pltpu.VMEM((ROUTER_TILES * tt, D_MODEL), BF16)],
        compiler_params=arb,
        name="moe_router",
    )(x2d, g2d, wt, bt, usl, lsl)

    runs = pad.reshape(n_tiles, n_exp, 128)[:, :, 0].astype(I32)
    rows_e = jnp.sum(runs, axis=0)
    nblk = (rows_e + bm - 1) // bm
    blk_end = jnp.cumsum(nblk)
    e_start = (blk_end - nblk) * bm
    n_valid = blk_end[-1]
    run_end = jnp.cumsum(runs, axis=1)
    ngran = (run_end[:, -1] // GRANULE).astype(I32)
    shift = e_start[None, :] + (jnp.cumsum(runs, axis=0) - runs) - (run_end - runs)
    g_row = jnp.arange(N_LOCAL_GRAN, dtype=I32) * GRANULE
    e_of_g = jnp.sum((run_end[:, None, :] <= g_row[None, :, None]).astype(I32), axis=-1)
    shift_g = jnp.sum(jnp.where(e_of_g[..., None] == jnp.arange(n_exp, dtype=I32),
                                shift[:, None, :], 0), axis=-1)
    dest = jnp.where(e_of_g < n_exp, (shift_g + g_row[None, :]) // GRANULE, 0)
    dest = dest.astype(I32).reshape(n_tiles, 1, N_LOCAL_GRAN)
    per_blk = bm // GRANULE
    gap = ((e_start + rows_e) // GRANULE)[:, None] + jnp.arange(per_blk, dtype=I32)[None, :]
    gap = jnp.where(gap < ((e_start + nblk * bm) // GRANULE)[:, None], gap, -1)
    gap = gap.astype(I32).reshape(1, 1, n_exp * per_blk)
    n_blocks = (2 * t + n_tiles * n_exp * (GRANULE - 1)) // bm + 1 + n_exp
    blk = jnp.minimum(jnp.arange(n_blocks, dtype=I32), n_valid - 1)
    block_expert = jnp.sum((blk[:, None] >= blk_end[None, :]).astype(I32), axis=1).astype(I32)
    n_valid = n_valid.reshape(1).astype(I32)

    last = n_tiles - 1
    xs = pl.pallas_call(
        _dispatch_kernel,
        grid_spec=pltpu.PrefetchScalarGridSpec(
            num_scalar_prefetch=2,
            grid=(n_tiles + 1,),
            in_specs=[pl.BlockSpec((1, 1, N_LOCAL_GRAN),
                                   lambda i, ng, nv: (jnp.minimum(i, last), 0, 0),
                                   memory_space=pltpu.SMEM),
                      pl.BlockSpec((1, 1, n_exp * per_blk), lambda i, ng, nv: (0, 0, 0),
                                   memory_space=pltpu.SMEM),
                      pl.BlockSpec((tt, D_MODEL), lambda i, ng, nv: (jnp.minimum(i, last), 0)),
                      _const_spec((1, D_MODEL)),
                      pl.BlockSpec((8, tt), lambda i, ng, nv: (jnp.minimum(i, last), 0))],
            out_specs=pl.BlockSpec(memory_space=pl.ANY),
            scratch_shapes=[pltpu.VMEM((2, LOCAL_ROWS, D_MODEL), BF16),
                            pltpu.VMEM((bm, D_MODEL), BF16),
                            pltpu.SemaphoreType.DMA((2,)),
                            pltpu.SemaphoreType.DMA((2,)),
                            pltpu.SMEM((2,), I32)]),
        out_shape=jax.ShapeDtypeStruct((n_blocks * bm, D_MODEL), BF16),
        compiler_params=arb,
        name="moe_dispatch",
    )(ngran, n_valid, dest, gap, x2d, g2d, metat)

    w_spec = lambda shape: pl.BlockSpec((None, None) + shape,
                                        lambda b, be, nv: (layer, be[b], 0, 0))
    row_blk = pl.BlockSpec((bm, D_MODEL), lambda b, be, nv: (jnp.minimum(b, nv[0] - 1), 0))
    ys = pl.pallas_call(
        _expert_kernel,
        grid_spec=pltpu.PrefetchScalarGridSpec(
            num_scalar_prefetch=2,
            grid=(n_blocks,),
            in_specs=[row_blk,
                      w_spec((D_MODEL, MOE_D_FF)),
                      w_spec((D_MODEL, MOE_D_FF)),
                      w_spec((MOE_D_FF, D_MODEL))],
            out_specs=row_blk,
            scratch_shapes=[pltpu.VMEM((D_MODEL, 2 * MOE_D_FF), BF16),
                            pltpu.VMEM((MOE_D_FF, D_MODEL), BF16),
                            pltpu.VMEM((bm, MOE_D_FF), BF16)]),
        out_shape=jax.ShapeDtypeStruct((n_blocks * bm, D_MODEL), BF16),
        input_output_aliases={2: 0},
        compiler_params=arb,
        name="moe_experts",
    )(block_expert, n_valid, xs, w_gate, w_up, w_down)

    final = norm_final is not None
    gf = (norm_final if final else jnp.ones((D_MODEL,), F32)).reshape(1, D_MODEL)
    src_spec = lambda nxt: pl.BlockSpec(
        (1, 1, N_LOCAL_GRAN), lambda i, ng: (jnp.minimum(i + nxt, last), 0, 0),
        memory_space=pltpu.SMEM)
    out = pl.pallas_call(
        functools.partial(_combine_kernel, final=final),
        grid_spec=pltpu.PrefetchScalarGridSpec(
            num_scalar_prefetch=1,
            grid=(n_tiles,),
            in_specs=[src_spec(0), src_spec(1),
                      pl.BlockSpec((tt, D_MODEL), lambda i, ng: (i, 0)),
                      pl.BlockSpec((tt, 128), lambda i, ng: (i, 0)),
                      _const_spec((1, D_MODEL)),
                      pl.BlockSpec(memory_space=pl.ANY)],
            out_specs=pl.BlockSpec((tt, D_MODEL), lambda i, ng: (i, 0)),
            scratch_shapes=[pltpu.VMEM((2, LOCAL_ROWS, D_MODEL), BF16),
                            pltpu.SemaphoreType.DMA((2,))]),
        out_shape=jax.ShapeDtypeStruct((t, D_MODEL), F32),
        compiler_params=arb,
        name="moe_combine",
    )(ngran, dest, dest, x2d, meta, gf, ys)
    return out


def _trunk(x, conv_prev, ssm_prev, p, *, ssd_nsub, ssd_ns, want_v):
    n_seq, seq_len, _ = x.shape
    q = seq_len if seq_len < SSD_CHUNK else SSD_CHUNK
    x, conv_new, ssm_new = _ssd_layer(
        x, conv_prev, ssm_prev, p["norm_mix"][0], p["ssd_w_in"][0], p["ssd_conv_w"][0],
        p["ssd_conv_b"][0], p["ssd_dt_bias"][0], p["ssd_a_log"][0], p["ssd_d"][0], p["ssd_norm"][0],
        p["ssd_w_out"][0], nsub=ssd_nsub, ns=ssd_ns, q=q)
    x2d = x.reshape(n_seq * seq_len, D_MODEL)
    x2d = _moe(x2d, p["norm_ffn"][0], p["moe_w_group"][0], p["moe_b_group"][0], p["moe_w_router"][0],
               p["moe_b_router"][0], p["moe_w_gate"], p["moe_w_up"], p["moe_w_down"], None, layer=0)
    x2d, v = _sg_layer(x2d, p["norm_mix"][1], p["sg_w_in"][0], p["sg_b_in"][0], p["sg_ln_g"][0],
                       p["sg_ln_b"][0], p["sg_w_s"][0], p["sg_b_s"][0], p["sg_w_out"][0],
                       seq_len=seq_len, want_v=want_v)
    y2d = _moe(x2d, p["norm_ffn"][1], p["moe_w_group"][1], p["moe_b_group"][1], p["moe_w_router"][1],
               p["moe_b_router"][1], p["moe_w_gate"], p["moe_w_up"], p["moe_w_down"],
               p["norm_final"], layer=1)
    y = y2d.reshape(n_seq, seq_len, D_MODEL)
    if want_v:
        v = v.reshape(1, n_seq, seq_len, SG_WIDTH)
    return y, conv_new, ssm_new, v


def kernel(x_prompt, x_sample, state_ssm, state_conv, norm_mix, norm_ffn, norm_final, ssd_w_in, ssd_conv_w, ssd_conv_b, ssd_dt_bias, ssd_a_log, ssd_d, ssd_norm, ssd_w_out, sg_w_in, sg_b_in, sg_ln_g, sg_ln_b, sg_w_s, sg_b_s, sg_w_out, moe_w_group, moe_b_group, moe_w_router, moe_b_router, moe_w_gate, moe_w_up, moe_w_down):
    p = dict(norm_mix=norm_mix, norm_ffn=norm_ffn, norm_final=norm_final, ssd_w_in=ssd_w_in,
             ssd_conv_w=ssd_conv_w, ssd_conv_b=ssd_conv_b, ssd_dt_bias=ssd_dt_bias,
             ssd_a_log=ssd_a_log, ssd_d=ssd_d, ssd_norm=ssd_norm, ssd_w_out=ssd_w_out,
             sg_w_in=sg_w_in, sg_b_in=sg_b_in, sg_ln_g=sg_ln_g, sg_ln_b=sg_ln_b, sg_w_s=sg_w_s,
             sg_b_s=sg_b_s, sg_w_out=sg_w_out, moe_w_group=moe_w_group, moe_b_group=moe_b_group,
             moe_w_router=moe_w_router, moe_b_router=moe_b_router, moe_w_gate=moe_w_gate,
             moe_w_up=moe_w_up, moe_w_down=moe_w_down)
    nb = x_prompt.shape[0]
    conv0 = jnp.zeros((1, nb, CONV_W - 1, CONV_DIM), F32)
    ssm0 = jnp.zeros((1, nb, N_HEADS, HEAD_DIM, D_STATE), F32)
    y_p, conv_p, ssm_p, _ = _trunk(x_prompt, conv0, ssm0, p, ssd_nsub=2, ssd_ns=1, want_v=False)
    y_s, conv_s, ssm_s, v_s = _trunk(x_sample, state_conv, state_ssm, p, ssd_nsub=1, ssd_ns=4,
                                     want_v=True)
    return (y_p, y_s, ssm_p, conv_p, ssm_s, conv_s, v_s)
```

```python
import functools
import math

import jax
import jax.numpy as jnp
import numpy as np
from jax import lax
from jax.experimental import pallas as pl
from jax.experimental.pallas import tpu as pltpu

F32 = jnp.float32
BF16 = jnp.bfloat16
I32 = jnp.int32

D_MODEL = 1024
N_HEADS = 32
HEAD_DIM = 64
N_GROUPS = 4
D_STATE = 128
D_INNER = N_HEADS * HEAD_DIM
GROUP_W = D_INNER // N_GROUPS
CONV_W = 4
CONV_DIM = D_INNER + 2 * N_GROUPS * D_STATE
DT_PAD = 128
SSD_CHUNK = 128
PROJ_CHUNK = 256
SG_WIDTH = 2 * D_MODEL
SG_GROUPS = 8
SG_GROUP_DIM = SG_WIDTH // SG_GROUPS
SG_CHUNK = 128
SG_ROWS = 256
MOE_GROUPS = 4
MOE_EPG = 8
MOE_EXPERTS = MOE_GROUPS * MOE_EPG
MOE_D_FF = 256
NORM_EPS = 1e-6
LN_EPS = 1e-5

MOE_TILE = 256
ROUTER_TILES = 4


def _expert_block(n_tokens):
    return 512 if 2 * n_tokens >= 512 * MOE_EXPERTS else 256


GRANULE = 16
LOCAL_ROWS = 2 * MOE_TILE + MOE_EXPERTS * GRANULE
N_LOCAL_GRAN = LOCAL_ROWS // GRANULE
VMEM_LIMIT = 56 * 1024 * 1024


def _sigmoid(x):
    return 0.5 * (jnp.tanh(0.5 * x) + 1.0)


def _silu(x):
    return x * _sigmoid(x)


def _softplus(x):
    return jnp.maximum(x, 0.0) + jnp.log(1.0 + jnp.exp(-jnp.abs(x)))


def _gelu_tanh(x):
    c = math.sqrt(2.0 / math.pi)
    return x * (0.5 * (1.0 + jnp.tanh(c * (x + 0.044715 * (x * x * x)))))


def _rms(x, g):
    return x * lax.rsqrt(jnp.mean(x * x, axis=-1, keepdims=True) + NORM_EPS) * g


def _split3(x):
    a = x.astype(BF16)
    r = x - a.astype(F32)
    b = r.astype(BF16)
    c = (r - b.astype(F32)).astype(BF16)
    return a, b, c


def _dot(a, b):
    return jnp.dot(a, b, preferred_element_type=F32)


def _dot_nt(a, b):
    return lax.dot_general(a, b, (((1,), (1,)), ((), ())), preferred_element_type=F32)


def _dot_tn(a, b):
    return lax.dot_general(a, b, (((0,), (0,)), ((), ())), preferred_element_type=F32)


def _const_spec(shape):
    nd = len(shape)
    return pl.BlockSpec(shape, lambda *_: (0,) * nd)


def _ssd_kernel(x_ref, g_ref, win_ref, cw_ref, cb_ref, dtb_ref, alog_ref, dsk_ref, ng_ref,
                wout_ref, tri_ref, ones_ref, cin_ref, sin_ref,
                xo_ref, cout_ref, sout_ref,
                pend_x, pend_z, pend_xbc, pend_dt, xres_scr, zg_scr,
                prev_scr, xc_scr, yoff_scr, y_scr, xw_scr, hn_scr, yn_scr,
                *, nsub, ns, q, n_chunks):
    i = pl.program_id(0)
    first_chunk = lax.rem(jnp.maximum(i - 1, 0), n_chunks) == 0
    nseq = nsub * ns
    sb = ns * q
    r = nsub * sb

    @pl.when(i == 0)
    def _():
        pend_x[...] = jnp.zeros_like(pend_x)
        pend_z[...] = jnp.zeros_like(pend_z)
        pend_xbc[...] = jnp.zeros_like(pend_xbc)
        pend_dt[...] = jnp.zeros_like(pend_dt)

    tr = min(r, 128)
    row_tiles = [slice(a, a + tr) for a in range(0, r, tr)]

    def col_tiles(total, rows=tr):
        w = min(total, max(128, (32 * 1024) // rows))
        return [slice(c, c + w) for c in range(0, total, w)]

    for rt in row_tiles:
        for ct in col_tiles(D_MODEL):
            xres_scr[rt, ct] = pend_x[rt, ct]
        for ct in col_tiles(D_INNER):
            zg_scr[rt, ct] = _silu(pend_z[rt, ct])

    for s in range(nseq):
        for ct in col_tiles(D_MODEL, q):
            pend_x[s * q:(s + 1) * q, ct] = x_ref[s, :, ct]
    for rt in row_tiles:
        ss = None
        for ct in col_tiles(D_MODEL):
            xv = pend_x[rt, ct]
            part = jnp.sum(xv * xv, axis=-1, keepdims=True)
            ss = part if ss is None else ss + part
        scale = lax.rsqrt(ss * (1.0 / D_MODEL) + NORM_EPS)
        for ct in col_tiles(D_MODEL):
            hn_scr[rt, ct] = (pend_x[rt, ct] * scale * g_ref[:, ct]).astype(BF16)

    def proj_chunk(dst, dst_col, w_col, width):
        def run():
            dst[:, dst_col:dst_col + width] = _dot(hn_scr[...], win_ref[:, w_col:w_col + width])
        return run
    z_chunks = [proj_chunk(pend_z, c, c, PROJ_CHUNK) for c in range(0, D_INNER, PROJ_CHUNK)]
    scan_chunks = [proj_chunk(pend_xbc, c, D_INNER + c, PROJ_CHUNK)
                   for c in range(0, CONV_DIM, PROJ_CHUNK)]
    scan_chunks.append(proj_chunk(pend_dt, 0, D_INNER + CONV_DIM, DT_PAD))

    @pl.when(first_chunk)
    def _():
        sout_ref[...] = sin_ref[...]
        prev_scr[...] = jnp.zeros_like(prev_scr)
        for s in range(nseq):
            for k in range(CONV_W - 1):
                prev_scr[pl.ds(s * 8 + 5 + k, 1), :] = cin_ref[s, pl.ds(k, 1), :]

    sub8 = lax.broadcasted_iota(I32, (8, 1), 0)
    for s in range(nseq):
        srows = slice(s * q, (s + 1) * q)
        for ct in col_tiles(CONV_DIM, 4 * q):
            xq = pend_xbc[srows, ct]
            hist = prev_scr[s * 8:(s + 1) * 8, ct]
            acc = cb_ref[:, ct] + cw_ref[pl.ds(CONV_W - 1, 1), ct] * xq
            for j in range(1, CONV_W):
                sh = pltpu.roll(xq, j, 0)
                head = jnp.where(sub8 < j, pltpu.roll(hist, j, 0), sh[0:8, :])
                sh = head if q == 8 else jnp.concatenate([head, sh[8:, :]], axis=0)
                acc = acc + cw_ref[pl.ds(CONV_W - 1 - j, 1), ct] * sh
            xc_scr[srows, ct] = _silu(acc)
            last = xq[q - 8:q, :]
            prev_scr[s * 8:(s + 1) * 8, ct] = last
            for k in range(CONV_W - 1):
                cout_ref[s, pl.ds(k, 1), ct] = last[5 + k:6 + k, :]
    chunks = z_chunks + scan_chunks
    n_pairs = nsub * (N_HEADS // 2)
    emit_at = {}
    for k, ch in enumerate(chunks):
        emit_at.setdefault((k * n_pairs) // len(chunks), []).append(ch)

    tri = tri_ref[...]
    trib = tri.astype(BF16)
    onesb = ones_ref[...].astype(BF16)
    mask = tri > 0.5
    rowseq = lax.shift_right_logical(lax.broadcasted_iota(I32, (sb, 1), 0), int(math.log2(q)))
    lo = lax.broadcasted_iota(I32, (sb, 128), 1) < HEAD_DIM
    neg_a = -jnp.exp(alog_ref[...])

    for u in range(nsub):
        rows = slice(u * sb, (u + 1) * sb)
        xs_ref = xc_scr.at[rows, 0:D_INNER]

        def b_of(g):
            return xc_scr[rows, D_INNER + g * D_STATE:D_INNER + (g + 1) * D_STATE]

        def c_of(g):
            c0 = D_INNER + (N_GROUPS + g) * D_STATE
            return xc_scr[rows, c0:c0 + D_STATE].astype(BF16)

        dt = _softplus(pend_dt[rows, :] + dtb_ref[...])
        d1, d2, d3 = _split3(dt * neg_a)
        cs = _dot(trib, d1) + _dot(trib, d2) + _dot(trib, d3)
        cl = _dot(onesb, d1) + _dot(onesb, d2) + _dot(onesb, d3)
        ecs = jnp.exp(cs)
        wgt = dt * jnp.exp(cl - cs)
        ecl = jnp.exp(cl)
        cs_t = cs.T
        dt_t = dt.T

        for g in range(N_GROUPS):
            cg = c_of(g)
            acc = None
            for s in range(ns):
                st = sout_ref[u * ns + s, g * GROUP_W:(g + 1) * GROUP_W, :].astype(BF16)
                yo = _dot_nt(cg, st)
                if ns > 1:
                    yo = jnp.where(rowseq == s, yo, 0.0)
                acc = yo if acc is None else acc + yo
            yoff_scr[rows, g * GROUP_W:(g + 1) * GROUP_W] = acc

        for g in range(N_GROUPS):
            sc = _dot_nt(c_of(g), b_of(g).astype(BF16))
            for jj in range(N_HEADS // N_GROUPS // 2):
                j = g * (N_HEADS // N_GROUPS // 2) + jj
                ms = []
                for h in (2 * j, 2 * j + 1):
                    diff = cs[:, h:h + 1] - cs_t[h:h + 1, :]
                    dec = jnp.exp(jnp.where(mask, diff, -jnp.inf))
                    ms.append((sc * dec * dt_t[h:h + 1, :]).astype(BF16))
                lhs = jnp.concatenate(ms, axis=1)
                cols = slice(j * 128, (j + 1) * 128)
                xp = xs_ref[:, cols]
                rhs = jnp.concatenate([jnp.where(lo, xp, 0.0).astype(BF16),
                                       jnp.where(lo, 0.0, xp).astype(BF16)], axis=0)
                yd = _dot(lhs, rhs)
                ecs_p = jnp.where(lo, ecs[:, 2 * j:2 * j + 1], ecs[:, 2 * j + 1:2 * j + 2])
                wgt_p = jnp.where(lo, wgt[:, 2 * j:2 * j + 1], wgt[:, 2 * j + 1:2 * j + 2])
                y_scr[rows, cols] = yd + yoff_scr[rows, cols] * ecs_p + xp * dsk_ref[:, cols]
                xw_scr[rows, cols] = (xp * wgt_p).astype(BF16)
                pair = u * (N_HEADS // 2) + j
                for ch in emit_at.get(pair, []):
                    ch()

        for g in range(N_GROUPS):
            xwg = xw_scr[rows, g * GROUP_W:(g + 1) * GROUP_W]
            for s in range(ns):
                bg = b_of(g)
                if ns > 1:
                    bg = jnp.where(rowseq == s, bg, 0.0)
                upd = _dot_tn(xwg, bg.astype(BF16))
                for hh in range(GROUP_W // HEAD_DIM):
                    h = g * (GROUP_W // HEAD_DIM) + hh
                    dec = jnp.broadcast_to(ecl[s * q:s * q + 1, h:h + 1], (HEAD_DIM, D_STATE))
                    hrows = slice(h * HEAD_DIM, (h + 1) * HEAD_DIM)
                    sout_ref[u * ns + s, hrows, :] = (
                        sout_ref[u * ns + s, hrows, :] * dec + upd[hh * HEAD_DIM:(hh + 1) * HEAD_DIM, :])

    for rt in row_tiles:
        for g in range(N_GROUPS):
            cts = [slice(g * GROUP_W + c.start, g * GROUP_W + c.stop) for c in col_tiles(GROUP_W)]
            ss = None
            for ct in cts:
                yz = y_scr[rt, ct] * zg_scr[rt, ct]
                part = jnp.sum(yz * yz, axis=-1, keepdims=True)
                ss = part if ss is None else ss + part
            scale = lax.rsqrt(ss * (1.0 / GROUP_W) + NORM_EPS)
            for ct in cts:
                yn_scr[rt, ct] = (y_scr[rt, ct] * zg_scr[rt, ct] * scale * ng_ref[:, ct]).astype(BF16)

    for c0 in range(0, D_MODEL, PROJ_CHUNK):
        ct = slice(c0, c0 + PROJ_CHUNK)
        o = xres_scr[:, ct] + _dot(yn_scr[...], wout_ref[:, ct])
        for s in range(nseq):
            xo_ref[s, :, ct] = o[s * q:(s + 1) * q, :]


def _ssd_layer(x, conv_prev, ssm_prev, norm_g, w_in, conv_w, conv_b, dt_bias, a_log, d_skip,
               norm_y, w_out, *, nsub, ns, q):
    n_seq, seq_len, _ = x.shape
    nseq = nsub * ns
    sb = ns * q
    r = nsub * sb
    n_chunks = seq_len // q
    pad = DT_PAD - N_HEADS
    win = jnp.concatenate([w_in, jnp.zeros((D_MODEL, pad), F32)], axis=1).astype(BF16)
    dtb = jnp.pad(dt_bias, (0, pad)).reshape(1, DT_PAD)
    alog = jnp.pad(a_log, (0, pad)).reshape(1, DT_PAD)
    dsk = jnp.repeat(d_skip, HEAD_DIM).reshape(1, D_INNER)
    blk = np.kron(np.eye(ns), np.ones((q, q)))
    tri = jnp.asarray(blk * np.tril(np.ones((sb, sb))), F32)
    ones = jnp.asarray(blk, F32)
    state = ssm_prev.reshape(n_seq, D_INNER, D_STATE)
    conv_prev = conv_prev.reshape(n_seq, CONV_W - 1, CONV_DIM)

    kern = functools.partial(_ssd_kernel, nsub=nsub, ns=ns, q=q, n_chunks=n_chunks)
    out_shape = (jax.ShapeDtypeStruct(x.shape, F32),
                 jax.ShapeDtypeStruct((n_seq, CONV_W - 1, CONV_DIM), F32),
                 jax.ShapeDtypeStruct((n_seq, D_INNER, D_STATE), F32))
    n_steps = (n_seq // nseq) * n_chunks
    nxt = lambda i: jnp.minimum(i, n_steps - 1)
    cur = lambda i: jnp.maximum(i - 1, 0)
    in_row_spec = pl.BlockSpec((nseq, q, D_MODEL),
                               lambda i: (nxt(i) // n_chunks, nxt(i) % n_chunks, 0))
    row_spec = pl.BlockSpec((nseq, q, D_MODEL),
                            lambda i: (cur(i) // n_chunks, cur(i) % n_chunks, 0))
    conv_spec = pl.BlockSpec((nseq, CONV_W - 1, CONV_DIM), lambda i: (cur(i) // n_chunks, 0, 0))
    state_spec = pl.BlockSpec((nseq, D_INNER, D_STATE), lambda i: (cur(i) // n_chunks, 0, 0))
    in_specs = [in_row_spec,
                _const_spec((1, D_MODEL)),
                _const_spec(win.shape),
                _const_spec((CONV_W, CONV_DIM)),
                _const_spec((1, CONV_DIM)),
                _const_spec((1, DT_PAD)),
                _const_spec((1, DT_PAD)),
                _const_spec((1, D_INNER)),
                _const_spec((1, D_INNER)),
                _const_spec((D_INNER, D_MODEL)),
                _const_spec((sb, sb)),
                _const_spec((sb, sb)),
                conv_spec, state_spec]
    x_new, conv_new, state_new = pl.pallas_call(
        kern,
        grid=(n_steps + 1,),
        in_specs=in_specs,
        out_specs=(row_spec, conv_spec, state_spec),
        out_shape=out_shape,
        scratch_shapes=[pltpu.VMEM((r, D_MODEL), F32),
                        pltpu.VMEM((r, D_INNER), F32),
                        pltpu.VMEM((r, CONV_DIM), F32),
                        pltpu.VMEM((r, DT_PAD), F32),
                        pltpu.VMEM((r, D_MODEL), F32),
                        pltpu.VMEM((r, D_INNER), F32),
                        pltpu.VMEM((nseq * 8, CONV_DIM), F32),
                        pltpu.VMEM((r, CONV_DIM), F32),
                        pltpu.VMEM((r, D_INNER), F32),
                        pltpu.VMEM((r, D_INNER), F32),
                        pltpu.VMEM((r, D_INNER), BF16),
                        pltpu.VMEM((r, D_MODEL), BF16),
                        pltpu.VMEM((r, D_INNER), BF16)],
        compiler_params=pltpu.CompilerParams(
            dimension_semantics=("arbitrary",), vmem_limit_bytes=VMEM_LIMIT),
        name="ssd_layer",
    )(x, norm_g.reshape(1, D_MODEL), win, conv_w, conv_b.reshape(1, CONV_DIM), dtb, alog, dsk,
      norm_y.reshape(1, D_INNER), w_out.astype(BF16), tri, ones, conv_prev, state)
    return (x_new, conv_new.reshape(1, n_seq, CONV_W - 1, CONV_DIM),
            state_new.reshape(1, n_seq, N_HEADS, HEAD_DIM, D_STATE))


def _sg_kernel(x_ref, g_ref, win_ref, bin_ref, lng_ref, lnb_ref, wmix_ref, bmix_ref, wout_ref,
               xo_ref, *rest, r, want_v):
    v_ref = rest[0] if want_v else None
    hn_scr, uv_scr, vb_scr, um_scr = rest[-4:]
    row_tiles = [slice(a, a + 128) for a in range(0, r, 128)]
    col128 = lambda total: [slice(c, c + 128) for c in range(0, total, 128)]

    for rt in row_tiles:
        ss = None
        for ct in col128(D_MODEL):
            xv = x_ref[rt, ct]
            part = jnp.sum(xv * xv, axis=-1, keepdims=True)
            ss = part if ss is None else ss + part
        scale = lax.rsqrt(ss * (1.0 / D_MODEL) + NORM_EPS)
        for ct in col128(D_MODEL):
            hn_scr[rt, ct] = (x_ref[rt, ct] * scale * g_ref[:, ct]).astype(BF16)

    for c0 in range(0, 2 * SG_WIDTH, PROJ_CHUNK):
        h = _dot(hn_scr[...], win_ref[:, c0:c0 + PROJ_CHUNK])
        for rt in row_tiles:
            for cc in range(0, PROJ_CHUNK, 128):
                ct = slice(c0 + cc, c0 + cc + 128)
                uv_scr[rt, ct] = _gelu_tanh(h[rt, cc:cc + 128] + bin_ref[:, ct])

    for rt in row_tiles:
        vcols = [slice(SG_WIDTH + c.start, SG_WIDTH + c.stop) for c in col128(SG_WIDTH)]
        tot = None
        for ct in vcols:
            part = jnp.sum(uv_scr[rt, ct], axis=-1, keepdims=True)
            tot = part if tot is None else tot + part
        mu = tot * (1.0 / SG_WIDTH)
        ss = None
        for ct in vcols:
            vc = uv_scr[rt, ct] - mu
            part = jnp.sum(vc * vc, axis=-1, keepdims=True)
            ss = part if ss is None else ss + part
        scale = lax.rsqrt(ss * (1.0 / SG_WIDTH) + LN_EPS)
        for ct, c in zip(vcols, col128(SG_WIDTH)):
            vn = (uv_scr[rt, ct] - mu) * scale * lng_ref[:, c] + lnb_ref[:, c]
            if want_v:
                v_ref[rt, c] = vn
            vb_scr[rt, c] = vn.astype(BF16)

    for rt in row_tiles:
        for g in range(SG_GROUPS):
            cols = slice(g * SG_GROUP_DIM, (g + 1) * SG_GROUP_DIM)
            mixed = _dot(wmix_ref[g], vb_scr[rt, cols]) + bmix_ref[:, cols]
            um_scr[rt, cols] = (uv_scr[rt, cols] * mixed).astype(BF16)

    for c0 in range(0, D_MODEL, PROJ_CHUNK):
        ct = slice(c0, c0 + PROJ_CHUNK)
        xo_ref[:, ct] = x_ref[:, ct] + _dot(um_scr[...], wout_ref[:, ct])


def _sg_layer(x2d, norm_g, w_in, b_in, ln_g, ln_b, w_s, b_s, w_out, *, seq_len, want_v):
    r = SG_ROWS
    t = x2d.shape[0]
    q = min(seq_len, SG_CHUNK)
    reps = SG_CHUNK // q
    ws = jnp.tril(w_s)[:, :q, :q]
    wmix = jnp.einsum("ab,gts->gatbs", jnp.eye(reps, dtype=F32), ws)
    wmix = wmix.reshape(SG_GROUPS, SG_CHUNK, SG_CHUNK)
    bmix = jnp.tile(jnp.repeat(b_s.T[:q], SG_GROUP_DIM, axis=1), (reps, 1))
    row_spec = pl.BlockSpec((r, D_MODEL), lambda i: (i, 0))
    v_spec = pl.BlockSpec((r, SG_WIDTH), lambda i: (i, 0))
    out_shape = [jax.ShapeDtypeStruct(x2d.shape, F32)]
    out_specs = [row_spec]
    if want_v:
        out_shape.append(jax.ShapeDtypeStruct((t, SG_WIDTH), F32))
        out_specs.append(v_spec)
    outs = pl.pallas_call(
        functools.partial(_sg_kernel, r=r, want_v=want_v),
        grid=(t // r,),
        scratch_shapes=[pltpu.VMEM((r, D_MODEL), BF16),
                        pltpu.VMEM((r, 2 * SG_WIDTH), F32),
                        pltpu.VMEM((r, SG_WIDTH), BF16),
                        pltpu.VMEM((r, SG_WIDTH), BF16)],
        in_specs=[row_spec,
                  _const_spec((1, D_MODEL)),
                  _const_spec((D_MODEL, 2 * SG_WIDTH)),
                  _const_spec((1, 2 * SG_WIDTH)),
                  _const_spec((1, SG_WIDTH)),
                  _const_spec((1, SG_WIDTH)),
                  _const_spec((SG_GROUPS, SG_CHUNK, SG_CHUNK)),
                  _const_spec((SG_CHUNK, SG_WIDTH)),
                  _const_spec((SG_WIDTH, D_MODEL))],
        out_specs=out_specs,
        out_shape=out_shape,
        compiler_params=pltpu.CompilerParams(
            dimension_semantics=("arbitrary",), vmem_limit_bytes=VMEM_LIMIT),
        name="sg_layer",
    )(x2d, norm_g.reshape(1, D_MODEL), w_in.astype(BF16), b_in.reshape(1, 2 * SG_WIDTH),
      ln_g.reshape(1, SG_WIDTH), ln_b.reshape(1, SG_WIDTH), wmix.astype(BF16), bmix,
      w_out.astype(BF16))
    return outs if want_v else (outs[0], None)


ROUTER_ROWS = 64


def _norm_tiles(x_ref, g_ref, store):
    for r0 in range(0, x_ref.shape[0], 128):
        rt = slice(r0, r0 + 128)
        ss = None
        for c0 in range(0, D_MODEL, 256):
            xv = x_ref[rt, c0:c0 + 256]
            part = jnp.sum(xv * xv, axis=-1, keepdims=True)
            ss = part if ss is None else ss + part
        scale = lax.rsqrt(ss * (1.0 / D_MODEL) + NORM_EPS)
        for c0 in range(0, D_MODEL, 256):
            ct = slice(c0, c0 + 256)
            store(rt, ct, x_ref[rt, ct] * scale * g_ref[:, ct])


def _router_t_kernel(xa_ref, xb_ref, g_ref, wt_ref, bt_ref, usl_ref, lsl_ref,
                     meta_ref, metat_ref, pad_ref, h1_scr, h2_scr, *, steps_a):
    wt = wt_ref[...]
    w1 = wt.astype(BF16)
    w2 = (wt - w1.astype(F32)).astype(BF16)
    tt = MOE_TILE
    n = xa_ref.shape[0]

    def split_store(rt, ct, hn):
        hi = hn.astype(BF16)
        h1_scr[rt, ct] = hi
        h2_scr[rt, ct] = (hn - hi.astype(F32)).astype(BF16)

    @pl.when(pl.program_id(0) < steps_a)
    def _():
        _norm_tiles(xa_ref, g_ref, split_store)

    @pl.when(pl.program_id(0) >= steps_a)
    def _():
        _norm_tiles(xb_ref, g_ref, split_store)

    h1 = h1_scr[...]
    logits = _dot_nt(w1, h1) + _dot_nt(w2, h1) + _dot_nt(w1, h2_scr[...])
    select = _select_experts(logits, bt_ref[...])
    for k in range(n // tt):
        cols = slice(k * tt, (k + 1) * tt)
        metat, pad = _local_order(select[:, cols], usl_ref[...], lsl_ref[...])
        metat_ref[k * 8:(k + 1) * 8, :] = metat
        full = jnp.concatenate([metat, jnp.zeros((128 - 8, tt), F32)], axis=0)
        meta_ref[cols, :] = full.T
        pad_ref[k * MOE_EXPERTS:(k + 1) * MOE_EXPERTS, :] = pad


def _select_experts(logits, bt):
    n = logits.shape[1]
    reps = n // bt.shape[1]
    logits = logits + (bt if reps == 1 else jnp.concatenate([bt] * reps, axis=1))
    tt = n
    row8 = lax.broadcasted_iota(I32, (8, tt), 0).astype(F32)
    gl = jnp.where(row8 < MOE_GROUPS, logits[0:8, :], -jnp.inf)
    gmax = jnp.max(gl, axis=0, keepdims=True)
    g_top = jnp.min(jnp.where(gl == gmax, row8, 8.0), axis=0, keepdims=True)
    p_g = 1.0 / jnp.sum(jnp.exp(gl - gmax), axis=0, keepdims=True)

    el = logits[8:16, :]
    for grp in range(1, MOE_GROUPS):
        el = jnp.where(g_top == grp, logits[8 + 8 * grp:16 + 8 * grp, :], el)
    emax = jnp.max(el, axis=0, keepdims=True)
    ee = jnp.exp(el - emax)
    prob = ee / jnp.sum(ee, axis=0, keepdims=True)
    p1 = jnp.max(prob, axis=0, keepdims=True)
    i1 = jnp.min(jnp.where(prob == p1, row8, 8.0), axis=0, keepdims=True)
    prob2 = jnp.where(row8 == i1, -1.0, prob)
    p2 = jnp.max(prob2, axis=0, keepdims=True)
    i2 = jnp.min(jnp.where(prob2 == p2, row8, 8.0), axis=0, keepdims=True)
    psum = p1 + p2
    gate1 = p_g * (p1 / psum)
    gate2 = p_g * (p2 / psum)
    e1 = g_top * MOE_EPG + i1
    e2 = g_top * MOE_EPG + i2
    out = jnp.where(row8 == 0, e1, 0.0)
    out = jnp.where(row8 == 1, e2, out)
    out = jnp.where(row8 == 2, gate1, out)
    return jnp.where(row8 == 3, gate2, out)


def _local_order(select, usl, lsl):
    tt = select.shape[1]
    row8 = lax.broadcasted_iota(I32, (8, tt), 0).astype(F32)
    e1, e2 = select[0:1, :], select[1:2, :]
    rowe = lax.broadcasted_iota(I32, (MOE_EXPERTS, tt), 0).astype(F32)
    sel1 = rowe == e1
    sel2 = rowe == e2
    onehot = jnp.where(sel1 | sel2, 1.0, 0.0)
    before = _dot(onehot.astype(BF16), usl)
    cnt = jnp.sum(onehot, axis=1, keepdims=True)
    pad = jnp.floor((cnt + (GRANULE - 1)) * (1.0 / GRANULE)) * GRANULE
    padb = jnp.broadcast_to(pad, (MOE_EXPERTS, tt))
    local = before + _dot(lsl, padb.astype(BF16))
    slot1 = jnp.sum(jnp.where(sel1, local, 0.0), axis=0, keepdims=True)
    slot2 = jnp.sum(jnp.where(sel2, local, 0.0), axis=0, keepdims=True)

    metat = jnp.where(row8 == 4, slot1, select)
    metat = jnp.where(row8 == 5, slot2, metat)
    return metat, padb[:, 0:128]


def _granule(ref, g):
    return ref.at[pl.ds(pl.multiple_of(g * GRANULE, GRANULE), GRANULE), :]


def _dispatch_kernel(ngran_ref, nv_ref, dest_ref, gap_ref, xa_ref, xb_ref, g_ref, meta_ref, xs_hbm,
                     buf, zbuf, hn_scr, sem, zsem, nstart, *, tiles_a):
    i = pl.program_id(0)
    n_tiles = pl.num_programs(0) - 1
    slot = i % 2
    tt = xa_ref.shape[0]
    bm = zbuf.shape[0]
    n_blocks = xs_hbm.shape[0] // bm

    def out_copy(sl, g, d):
        return pltpu.make_async_copy(_granule(buf.at[sl], g), _granule(xs_hbm, d), sem.at[sl])

    def drain(sl):
        def body(_, carry):
            out_copy(sl, 0, 0).wait()
            return carry
        lax.fori_loop(0, nstart[sl], body, 0)

    @pl.when(i == 0)
    def _():
        nstart[0] = 0
        nstart[1] = 0

    def norm_store(rt, ct, hn):
        hn_scr[rt, ct] = hn.astype(BF16)

    @pl.when(i < tiles_a)
    def _():
        _norm_tiles(xa_ref, g_ref, norm_store)

    @pl.when((i >= tiles_a) & (i < n_tiles))
    def _():
        _norm_tiles(xb_ref, g_ref, norm_store)

    @pl.when(i < n_tiles)
    def _():
        drain(slot)
        hn = hn_scr[...]
        mt = meta_ref[...]
        for r0 in range(0, LOCAL_ROWS, 256):
            rows = (lax.broadcasted_iota(I32, (256, tt), 0) + r0).astype(F32)
            onehot = jnp.where((rows == mt[4:5, :]) | (rows == mt[5:6, :]), 1.0, 0.0).astype(BF16)
            buf[slot, r0:r0 + 256, :] = _dot(onehot, hn).astype(BF16)
        ng = ngran_ref[i]

        def body(g, carry):
            out_copy(slot, g, dest_ref[0, 0, g]).start()
            return carry
        lax.fori_loop(0, ng, body, 0)
        nstart[slot] = ng

    @pl.when(i == n_tiles)
    def _():
        drain(0)
        drain(1)
        zbuf[...] = jnp.zeros_like(zbuf)

        def gap_copy(d):
            return pltpu.make_async_copy(_granule(zbuf, 0), _granule(xs_hbm, d), zsem.at[0])

        def tail_copy(b):
            return pltpu.make_async_copy(
                zbuf, xs_hbm.at[pl.ds(pl.multiple_of(b * bm, bm), bm), :], zsem.at[1])

        def each_gap(fn):
            def body(j, carry):
                d = gap_ref[0, 0, j]

                @pl.when(d >= 0)
                def _():
                    fn(gap_copy(d))
                return carry
            lax.fori_loop(0, gap_ref.shape[2], body, 0)

        def each_tail(fn):
            def body(b, carry):
                fn(tail_copy(b))
                return carry
            lax.fori_loop(nv_ref[0], n_blocks, body, 0)

        each_gap(lambda cp: cp.start())
        each_tail(lambda cp: cp.start())
        each_gap(lambda cp: cp.wait())
        each_tail(lambda cp: cp.wait())


def _expert_kernel(bexp_ref, nv_ref, xs_ref, wg_ref, wu_ref, wd_ref, ys_ref,
                   wgu_b, wd_b, act_scr):
    b = pl.program_id(0)
    ff = wg_ref.shape[1]

    @pl.when(b < nv_ref[0])
    def _():
        @pl.when((b == 0) | (bexp_ref[b] != bexp_ref[jnp.maximum(b - 1, 0)]))
        def _():
            for k0 in range(0, wg_ref.shape[0], 256):
                wgu_b[k0:k0 + 256, 0:ff] = wg_ref[k0:k0 + 256, :].astype(BF16)
                wgu_b[k0:k0 + 256, ff:2 * ff] = wu_ref[k0:k0 + 256, :].astype(BF16)
            for k0 in range(0, ff, 64):
                wd_b[k0:k0 + 64, :] = wd_ref[k0:k0 + 64, :].astype(BF16)

        bm = xs_ref.shape[0]
        for m0 in range(0, bm, 256):
            h = _dot(xs_ref[m0:m0 + 256, :], wgu_b[...])
            for r0 in range(0, 256, 128):
                for c0 in range(0, ff, 128):
                    hg = h[r0:r0 + 128, c0:c0 + 128]
                    hu = h[r0:r0 + 128, ff + c0:ff + c0 + 128]
                    act_scr[m0 + r0:m0 + r0 + 128, c0:c0 + 128] = (_silu(hg) * hu).astype(BF16)
            ys_ref[m0:m0 + 256, :] = _dot(act_scr[m0:m0 + 256, :], wd_b[...]).astype(BF16)


def _combine_kernel(ngran_ref, src_ref, srcn_ref, xa_ref, xb_ref, meta_ref, gf_ref, ys_hbm,
                    oa_ref, ob_ref, buf, moe_scr, sem, *, final, tiles_a):
    i = pl.program_id(0)
    n = pl.num_programs(0)
    slot = i % 2
    tt = xa_ref.shape[0]

    def in_copy(sl, g, d):
        return pltpu.make_async_copy(_granule(ys_hbm, d), _granule(buf.at[sl], g), sem.at[sl])

    def gather(idx_ref, sl, ng):
        def body(g, carry):
            in_copy(sl, g, idx_ref[0, 0, g]).start()
            return carry
        lax.fori_loop(0, ng, body, 0)

    @pl.when(i == 0)
    def _():
        buf[...] = jnp.zeros_like(buf)
        gather(src_ref, 0, ngran_ref[0])

    @pl.when(i + 1 < n)
    def _():
        gather(srcn_ref, 1 - slot, ngran_ref[i + 1])

    def wait_body(_, carry):
        in_copy(slot, 0, 0).wait()
        return carry
    lax.fori_loop(0, ngran_ref[i], wait_body, 0)

    ys = buf[slot]
    meta = meta_ref[...]
    cols = lax.broadcasted_iota(I32, (tt, LOCAL_ROWS), 1).astype(F32)
    pick = (jnp.where(cols == meta[:, 4:5], meta[:, 2:3], 0.0)
            + jnp.where(cols == meta[:, 5:6], meta[:, 3:4], 0.0)).astype(BF16)
    moe_scr[...] = _dot(pick, ys)

    def finish(x_ref, o_ref):
        for r0 in range(0, tt, 128):
            rt = slice(r0, r0 + 128)
            if final:
                ss = None
                for c0 in range(0, D_MODEL, 256):
                    yv = x_ref[rt, c0:c0 + 256] + moe_scr[rt, c0:c0 + 256]
                    part = jnp.sum(yv * yv, axis=-1, keepdims=True)
                    ss = part if ss is None else ss + part
                scale = lax.rsqrt(ss * (1.0 / D_MODEL) + NORM_EPS)
            for c0 in range(0, D_MODEL, 256):
                ct = slice(c0, c0 + 256)
                yv = x_ref[rt, ct] + moe_scr[rt, ct]
                o_ref[rt, ct] = yv * scale * gf_ref[:, ct] if final else yv

    @pl.when(i < tiles_a)
    def _():
        finish(xa_ref, oa_ref)

    @pl.when(i >= tiles_a)
    def _():
        finish(xb_ref, ob_ref)


def _moe(xa, xb, norm_g, w_grp, b_grp, w_rt, b_rt, w_gate, w_up, w_down, norm_final, *, layer):
    tt = MOE_TILE
    tiles_a, tiles_b = xa.shape[0] // tt, xb.shape[0] // tt
    steps_a, steps_b = tiles_a // ROUTER_TILES, tiles_b // ROUTER_TILES
    n_tiles = tiles_a + tiles_b
    t = n_tiles * tt
    bm = _expert_block(t)
    n_exp = MOE_EXPERTS
    a_blk = lambda i, n_a: jnp.minimum(i, n_a - 1)
    b_blk = lambda i, n_a, n_b: jnp.clip(i - n_a, 0, n_b - 1)
    zrow = lambda n: jnp.zeros((n, D_MODEL), F32)
    wt = jnp.concatenate([w_grp.T, zrow(8 - MOE_GROUPS), w_rt.T, zrow(ROUTER_ROWS - 8 - n_exp)], axis=0)
    bt = jnp.concatenate([b_grp, jnp.zeros((8 - MOE_GROUPS,), F32), b_rt,
                          jnp.zeros((ROUTER_ROWS - 8 - n_exp,), F32)])
    bt = jnp.broadcast_to(bt[:, None], (ROUTER_ROWS, tt))
    usl = jnp.asarray(np.triu(np.ones((tt, tt)), 1), BF16)
    lsl = jnp.asarray(np.tril(np.ones((n_exp, n_exp)), -1), BF16)
    g2d = norm_g.reshape(1, D_MODEL)
    arb = pltpu.CompilerParams(dimension_semantics=("arbitrary",), vmem_limit_bytes=VMEM_LIMIT)

    meta, metat, pad = pl.pallas_call(
        functools.partial(_router_t_kernel, steps_a=steps_a),
        grid=(steps_a + steps_b,),
        in_specs=[pl.BlockSpec((ROUTER_TILES * tt, D_MODEL), lambda i: (a_blk(i, steps_a), 0)),
                  pl.BlockSpec((ROUTER_TILES * tt, D_MODEL),
                               lambda i: (b_blk(i, steps_a, steps_b), 0)),
                  _const_spec((1, D_MODEL)),
                  _const_spec((ROUTER_ROWS, D_MODEL)),
                  _const_spec((ROUTER_ROWS, tt)),
                  _const_spec((tt, tt)),
                  _const_spec((n_exp, n_exp))],
        out_specs=(pl.BlockSpec((ROUTER_TILES * tt, 128), lambda i: (i, 0)),
                   pl.BlockSpec((ROUTER_TILES * 8, tt), lambda i: (i, 0)),
                   pl.BlockSpec((ROUTER_TILES * n_exp, 128), lambda i: (i, 0))),
        out_shape=(jax.ShapeDtypeStruct((t, 128), F32),
                   jax.ShapeDtypeStruct((n_tiles * 8, tt), F32),
                   jax.ShapeDtypeStruct((n_tiles * n_exp, 128), F32)),
        scratch_shapes=[pltpu.VMEM((ROUTER_TILES * tt, D_MODEL), BF16),
                        pltpu.VMEM((ROUTER_TILES * tt, D_MODEL), BF16)],
        compiler_params=arb,
        name="moe_router",
    )(xa, xb, g2d, wt, bt, usl, lsl)

    runs = pad.reshape(n_tiles, n_exp, 128)[:, :, 0].astype(I32)
    rows_e = jnp.sum(runs, axis=0)
    nblk = (rows_e + bm - 1) // bm
    blk_end = jnp.cumsum(nblk)
    e_start = (blk_end - nblk) * bm
    n_valid = blk_end[-1]
    run_end = jnp.cumsum(runs, axis=1)
    ngran = (run_end[:, -1] // GRANULE).astype(I32)
    shift = e_start[None, :] + (jnp.cumsum(runs, axis=0) - runs) - (run_end - runs)
    g_row = jnp.arange(N_LOCAL_GRAN, dtype=I32) * GRANULE
    e_of_g = jnp.sum((run_end[:, None, :] <= g_row[None, :, None]).astype(I32), axis=-1)
    shift_g = jnp.sum(jnp.where(e_of_g[..., None] == jnp.arange(n_exp, dtype=I32),
                                shift[:, None, :], 0), axis=-1)
    dest = jnp.where(e_of_g < n_exp, (shift_g + g_row[None, :]) // GRANULE, 0)
    dest = dest.astype(I32).reshape(n_tiles, 1, N_LOCAL_GRAN)
    per_blk = bm // GRANULE
    gap = ((e_start + rows_e) // GRANULE)[:, None] + jnp.arange(per_blk, dtype=I32)[None, :]
    gap = jnp.where(gap < ((e_start + nblk * bm) // GRANULE)[:, None], gap, -1)
    gap = gap.astype(I32).reshape(1, 1, n_exp * per_blk)
    n_blocks = (2 * t + n_tiles * n_exp * (GRANULE - 1)) // bm + 1 + n_exp
    blk = jnp.minimum(jnp.arange(n_blocks, dtype=I32), n_valid - 1)
    block_expert = jnp.sum((blk[:, None] >= blk_end[None, :]).astype(I32), axis=1).astype(I32)
    n_valid = n_valid.reshape(1).astype(I32)

    last = n_tiles - 1
    xs = pl.pallas_call(
        functools.partial(_dispatch_kernel, tiles_a=tiles_a),
        grid_spec=pltpu.PrefetchScalarGridSpec(
            num_scalar_prefetch=2,
            grid=(n_tiles + 1,),
            in_specs=[pl.BlockSpec((1, 1, N_LOCAL_GRAN),
                                   lambda i, ng, nv: (jnp.minimum(i, last), 0, 0),
                                   memory_space=pltpu.SMEM),
                      pl.BlockSpec((1, 1, n_exp * per_blk), lambda i, ng, nv: (0, 0, 0),
                                   memory_space=pltpu.SMEM),
                      pl.BlockSpec((tt, D_MODEL), lambda i, ng, nv: (a_blk(i, tiles_a), 0)),
                      pl.BlockSpec((tt, D_MODEL),
                                   lambda i, ng, nv: (b_blk(i, tiles_a, tiles_b), 0)),
                      _const_spec((1, D_MODEL)),
                      pl.BlockSpec((8, tt), lambda i, ng, nv: (jnp.minimum(i, last), 0))],
            out_specs=pl.BlockSpec(memory_space=pl.ANY),
            scratch_shapes=[pltpu.VMEM((2, LOCAL_ROWS, D_MODEL), BF16),
                            pltpu.VMEM((bm, D_MODEL), BF16),
                            pltpu.VMEM((tt, D_MODEL), BF16),
                            pltpu.SemaphoreType.DMA((2,)),
                            pltpu.SemaphoreType.DMA((2,)),
                            pltpu.SMEM((2,), I32)]),
        out_shape=jax.ShapeDtypeStruct((n_blocks * bm, D_MODEL), BF16),
        compiler_params=arb,
        name="moe_dispatch",
    )(ngran, n_valid, dest, gap, xa, xb, g2d, metat)

    w_spec = lambda shape: pl.BlockSpec((None, None) + shape,
                                        lambda b, be, nv: (layer, be[b], 0, 0))
    row_blk = pl.BlockSpec((bm, D_MODEL), lambda b, be, nv: (jnp.minimum(b, nv[0] - 1), 0))
    ys = pl.pallas_call(
        _expert_kernel,
        grid_spec=pltpu.PrefetchScalarGridSpec(
            num_scalar_prefetch=2,
            grid=(n_blocks,),
            in_specs=[row_blk,
                      w_spec((D_MODEL, MOE_D_FF)),
                      w_spec((D_MODEL, MOE_D_FF)),
                      w_spec((MOE_D_FF, D_MODEL))],
            out_specs=row_blk,
            scratch_shapes=[pltpu.VMEM((D_MODEL, 2 * MOE_D_FF), BF16),
                            pltpu.VMEM((MOE_D_FF, D_MODEL), BF16),
                            pltpu.VMEM((bm, MOE_D_FF), BF16)]),
        out_shape=jax.ShapeDtypeStruct((n_blocks * bm, D_MODEL), BF16),
        input_output_aliases={2: 0},
        compiler_params=arb,
        name="moe_experts",
    )(block_expert, n_valid, xs, w_gate, w_up, w_down)

    final = norm_final is not None
    gf = (norm_final if final else jnp.ones((D_MODEL,), F32)).reshape(1, D_MODEL)
    src_spec = lambda nxt: pl.BlockSpec(
        (1, 1, N_LOCAL_GRAN), lambda i, ng: (jnp.minimum(i + nxt, last), 0, 0),
        memory_space=pltpu.SMEM)
    a_spec = pl.BlockSpec((tt, D_MODEL), lambda i, ng: (a_blk(i, tiles_a), 0))
    b_spec = pl.BlockSpec((tt, D_MODEL), lambda i, ng: (b_blk(i, tiles_a, tiles_b), 0))
    out_a, out_b = pl.pallas_call(
        functools.partial(_combine_kernel, final=final, tiles_a=tiles_a),
        grid_spec=pltpu.PrefetchScalarGridSpec(
            num_scalar_prefetch=1,
            grid=(n_tiles,),
            in_specs=[src_spec(0), src_spec(1), a_spec, b_spec,
                      pl.BlockSpec((tt, 128), lambda i, ng: (i, 0)),
                      _const_spec((1, D_MODEL)),
                      pl.BlockSpec(memory_space=pl.ANY)],
            out_specs=(a_spec, b_spec),
            scratch_shapes=[pltpu.VMEM((2, LOCAL_ROWS, D_MODEL), BF16),
                            pltpu.VMEM((tt, D_MODEL), F32),
                            pltpu.SemaphoreType.DMA((2,))]),
        out_shape=(jax.ShapeDtypeStruct(xa.shape, F32), jax.ShapeDtypeStruct(xb.shape, F32)),
        compiler_params=arb,
        name="moe_combine",
    )(ngran, dest, dest, xa, xb, meta, gf, ys)
    return out_a, out_b


def _ssd(x, conv_prev, ssm_prev, p, *, nsub, ns):
    seq_len = x.shape[1]
    q = seq_len if seq_len < SSD_CHUNK else SSD_CHUNK
    x, conv_new, ssm_new = _ssd_layer(
        x, conv_prev, ssm_prev, p["norm_mix"][0], p["ssd_w_in"][0], p["ssd_conv_w"][0],
        p["ssd_conv_b"][0], p["ssd_dt_bias"][0], p["ssd_a_log"][0], p["ssd_d"][0], p["ssd_norm"][0],
        p["ssd_w_out"][0], nsub=nsub, ns=ns, q=q)
    return x.reshape(-1, D_MODEL), conv_new, ssm_new


def _sg(x2d, p, *, seq_len, want_v):
    return _sg_layer(x2d, p["norm_mix"][1], p["sg_w_in"][0], p["sg_b_in"][0], p["sg_ln_g"][0],
                     p["sg_ln_b"][0], p["sg_w_s"][0], p["sg_b_s"][0], p["sg_w_out"][0],
                     seq_len=seq_len, want_v=want_v)


def _moe_layer(xa, xb, p, layer, norm_final):
    return _moe(xa, xb, p["norm_ffn"][layer], p["moe_w_group"][layer], p["moe_b_group"][layer],
                p["moe_w_router"][layer], p["moe_b_router"][layer], p["moe_w_gate"], p["moe_w_up"],
                p["moe_w_down"], norm_final, layer=layer)


def kernel(x_prompt, x_sample, state_ssm, state_conv, norm_mix, norm_ffn, norm_final, ssd_w_in, ssd_conv_w, ssd_conv_b, ssd_dt_bias, ssd_a_log, ssd_d, ssd_norm, ssd_w_out, sg_w_in, sg_b_in, sg_ln_g, sg_ln_b, sg_w_s, sg_b_s, sg_w_out, moe_w_group, moe_b_group, moe_w_router, moe_b_router, moe_w_gate, moe_w_up, moe_w_down):
    p = dict(norm_mix=norm_mix, norm_ffn=norm_ffn, norm_final=norm_final, ssd_w_in=ssd_w_in,
             ssd_conv_w=ssd_conv_w, ssd_conv_b=ssd_conv_b, ssd_dt_bias=ssd_dt_bias,
             ssd_a_log=ssd_a_log, ssd_d=ssd_d, ssd_norm=ssd_norm, ssd_w_out=ssd_w_out,
             sg_w_in=sg_w_in, sg_b_in=sg_b_in, sg_ln_g=sg_ln_g, sg_ln_b=sg_ln_b, sg_w_s=sg_w_s,
             sg_b_s=sg_b_s, sg_w_out=sg_w_out, moe_w_group=moe_w_group, moe_b_group=moe_b_group,
             moe_w_router=moe_w_router, moe_b_router=moe_b_router, moe_w_gate=moe_w_gate,
             moe_w_up=moe_w_up, moe_w_down=moe_w_down)
    nb = x_prompt.shape[0]
    conv0 = jnp.zeros((1, nb, CONV_W - 1, CONV_DIM), F32)
    ssm0 = jnp.zeros((1, nb, N_HEADS, HEAD_DIM, D_STATE), F32)
    xp, conv_p, ssm_p = _ssd(x_prompt, conv0, ssm0, p, nsub=2, ns=1)
    xs, conv_s, ssm_s = _ssd(x_sample, state_conv, state_ssm, p, nsub=1, ns=4)
    xp, xs = _moe_layer(xp, xs, p, 0, None)
    xp, _ = _sg(xp, p, seq_len=x_prompt.shape[1], want_v=False)
    xs, v_s = _sg(xs, p, seq_len=x_sample.shape[1], want_v=True)
    y_p, y_s = _moe_layer(xp, xs, p, 1, norm_final)
    return (y_p.reshape(x_prompt.shape), y_s.reshape(x_sample.shape), ssm_p, conv_p, ssm_s, conv_s,
            v_s.reshape((1,) + x_sample.shape[:2] + (SG_WIDTH,)))
```

```python
import functools
import math

import jax
import jax.numpy as jnp
import numpy as np
from jax import lax
from jax.experimental import pallas as pl
from jax.experimental.pallas import tpu as pltpu

F32 = jnp.float32
BF16 = jnp.bfloat16
I32 = jnp.int32

D_MODEL = 1024
N_HEADS = 32
HEAD_DIM = 64
N_GROUPS = 4
D_STATE = 128
D_INNER = N_HEADS * HEAD_DIM
GROUP_W = D_INNER // N_GROUPS
CONV_W = 4
CONV_DIM = D_INNER + 2 * N_GROUPS * D_STATE
DT_PAD = 128
SSD_CHUNK = 128
PROJ_CHUNK = 256
SG_WIDTH = 2 * D_MODEL
SG_GROUPS = 8
SG_GROUP_DIM = SG_WIDTH // SG_GROUPS
SG_CHUNK = 128
SG_ROWS = 512
MOE_GROUPS = 4
MOE_EPG = 8
MOE_EXPERTS = MOE_GROUPS * MOE_EPG
MOE_D_FF = 256
NORM_EPS = 1e-6
LN_EPS = 1e-5

MOE_TILE = 256
ROUTER_TILES = 4


def _expert_block(n_tokens):
    return 512 if 2 * n_tokens >= 512 * MOE_EXPERTS else 256


GRANULE = 16
LOCAL_ROWS = 2 * MOE_TILE + MOE_EXPERTS * GRANULE
N_LOCAL_GRAN = LOCAL_ROWS // GRANULE
VMEM_LIMIT = 56 * 1024 * 1024


def _sigmoid(x):
    return 0.5 * (jnp.tanh(0.5 * x) + 1.0)


def _silu(x):
    return x * _sigmoid(x)


def _softplus(x):
    return jnp.maximum(x, 0.0) + jnp.log(1.0 + jnp.exp(-jnp.abs(x)))


def _gelu_tanh(x):
    c = math.sqrt(2.0 / math.pi)
    t = jnp.tanh(x * (c + (c * 0.044715) * (x * x)))
    hx = 0.5 * x
    return hx + hx * t


def _rms(x, g):
    return x * lax.rsqrt(jnp.mean(x * x, axis=-1, keepdims=True) + NORM_EPS) * g


def _split3(x):
    a = x.astype(BF16)
    r = x - a.astype(F32)
    b = r.astype(BF16)
    c = (r - b.astype(F32)).astype(BF16)
    return a, b, c


def _dot(a, b):
    return jnp.dot(a, b, preferred_element_type=F32)


def _dot_nt(a, b):
    return lax.dot_general(a, b, (((1,), (1,)), ((), ())), preferred_element_type=F32)


def _dot_tn(a, b):
    return lax.dot_general(a, b, (((0,), (0,)), ((), ())), preferred_element_type=F32)


def _const_spec(shape):
    nd = len(shape)
    return pl.BlockSpec(shape, lambda *_: (0,) * nd)


def _ssd_kernel(x_ref, g_ref, win_ref, cw_ref, cb_ref, dtb_ref, alog_ref, dsk_ref, ng_ref,
                wout_ref, tri_ref, ones_ref, cin_ref, sin_ref,
                xo_ref, cout_ref, sout_ref,
                pend_x, pend_z, pend_xbc, pend_dt, xres_scr, zg_scr,
                prev_scr, xc_scr, yoff_scr, y_scr, xw_scr, hn_scr, yn_scr,
                *, nsub, ns, q, n_chunks):
    i = pl.program_id(0)
    first_chunk = lax.rem(jnp.maximum(i - 1, 0), n_chunks) == 0
    nseq = nsub * ns
    sb = ns * q
    r = nsub * sb

    @pl.when(i == 0)
    def _():
        pend_x[...] = jnp.zeros_like(pend_x)
        pend_z[...] = jnp.zeros_like(pend_z)
        pend_xbc[...] = jnp.zeros_like(pend_xbc)
        pend_dt[...] = jnp.zeros_like(pend_dt)

    tr = min(r, 128)
    row_tiles = [slice(a, a + tr) for a in range(0, r, tr)]

    def col_tiles(total, rows=tr):
        w = min(total, max(128, (32 * 1024) // rows))
        return [slice(c, c + w) for c in range(0, total, w)]

    for rt in row_tiles:
        for ct in col_tiles(D_MODEL):
            xres_scr[rt, ct] = pend_x[rt, ct]
        for ct in col_tiles(D_INNER):
            zg_scr[rt, ct] = _silu(pend_z[rt, ct])

    for s in range(nseq):
        for ct in col_tiles(D_MODEL, q):
            pend_x[s * q:(s + 1) * q, ct] = x_ref[s, :, ct]
    for rt in row_tiles:
        ss = None
        for ct in col_tiles(D_MODEL):
            xv = pend_x[rt, ct]
            part = jnp.sum(xv * xv, axis=-1, keepdims=True)
            ss = part if ss is None else ss + part
        scale = lax.rsqrt(ss * (1.0 / D_MODEL) + NORM_EPS)
        for ct in col_tiles(D_MODEL):
            hn_scr[rt, ct] = (pend_x[rt, ct] * scale * g_ref[:, ct]).astype(BF16)

    def proj_chunk(dst, dst_col, w_col, width):
        def run():
            dst[:, dst_col:dst_col + width] = _dot(hn_scr[...], win_ref[:, w_col:w_col + width])
        return run
    z_chunks = [proj_chunk(pend_z, c, c, PROJ_CHUNK) for c in range(0, D_INNER, PROJ_CHUNK)]
    scan_chunks = [proj_chunk(pend_xbc, c, D_INNER + c, PROJ_CHUNK)
                   for c in range(0, CONV_DIM, PROJ_CHUNK)]
    scan_chunks.append(proj_chunk(pend_dt, 0, D_INNER + CONV_DIM, DT_PAD))

    @pl.when(first_chunk)
    def _():
        sout_ref[...] = sin_ref[...]
        prev_scr[...] = jnp.zeros_like(prev_scr)
        for s in range(nseq):
            for k in range(CONV_W - 1):
                prev_scr[pl.ds(s * 8 + 5 + k, 1), :] = cin_ref[s, pl.ds(k, 1), :]

    sub8 = lax.broadcasted_iota(I32, (8, 1), 0)
    for s in range(nseq):
        srows = slice(s * q, (s + 1) * q)
        for ct in col_tiles(CONV_DIM, 4 * q):
            xq = pend_xbc[srows, ct]
            hist = prev_scr[s * 8:(s + 1) * 8, ct]
            acc = cb_ref[:, ct] + cw_ref[pl.ds(CONV_W - 1, 1), ct] * xq
            for j in range(1, CONV_W):
                sh = pltpu.roll(xq, j, 0)
                head = jnp.where(sub8 < j, pltpu.roll(hist, j, 0), sh[0:8, :])
                sh = head if q == 8 else jnp.concatenate([head, sh[8:, :]], axis=0)
                acc = acc + cw_ref[pl.ds(CONV_W - 1 - j, 1), ct] * sh
            xc_scr[srows, ct] = _silu(acc)
            last = xq[q - 8:q, :]
            prev_scr[s * 8:(s + 1) * 8, ct] = last
            for k in range(CONV_W - 1):
                cout_ref[s, pl.ds(k, 1), ct] = last[5 + k:6 + k, :]
    chunks = z_chunks + scan_chunks
    n_pairs = nsub * (N_HEADS // 2)
    emit_at = {}
    for k, ch in enumerate(chunks):
        emit_at.setdefault((k * n_pairs) // len(chunks), []).append(ch)

    tri = tri_ref[...]
    trib = tri.astype(BF16)
    onesb = ones_ref[...].astype(BF16)
    mask = tri > 0.5
    rowseq = lax.shift_right_logical(lax.broadcasted_iota(I32, (sb, 1), 0), int(math.log2(q)))
    lo = lax.broadcasted_iota(I32, (sb, 128), 1) < HEAD_DIM
    neg_a = -jnp.exp(alog_ref[...])

    for u in range(nsub):
        rows = slice(u * sb, (u + 1) * sb)
        xs_ref = xc_scr.at[rows, 0:D_INNER]

        def b_of(g):
            return xc_scr[rows, D_INNER + g * D_STATE:D_INNER + (g + 1) * D_STATE]

        def c_of(g):
            c0 = D_INNER + (N_GROUPS + g) * D_STATE
            return xc_scr[rows, c0:c0 + D_STATE].astype(BF16)

        dt = _softplus(pend_dt[rows, :] + dtb_ref[...])
        d1, d2, d3 = _split3(dt * neg_a)
        cs = _dot(trib, d1) + _dot(trib, d2) + _dot(trib, d3)
        cl = _dot(onesb, d1) + _dot(onesb, d2) + _dot(onesb, d3)
        ecs = jnp.exp(cs)
        wgt = dt * jnp.exp(cl - cs)
        ecl = jnp.exp(cl)
        cs_t = cs.T
        dt_t = dt.T

        for g in range(N_GROUPS):
            cg = c_of(g)
            acc = None
            for s in range(ns):
                st = sout_ref[u * ns + s, g * GROUP_W:(g + 1) * GROUP_W, :].astype(BF16)
                yo = _dot_nt(cg, st)
                if ns > 1:
                    yo = jnp.where(rowseq == s, yo, 0.0)
                acc = yo if acc is None else acc + yo
            yoff_scr[rows, g * GROUP_W:(g + 1) * GROUP_W] = acc

        for g in range(N_GROUPS):
            sc = _dot_nt(c_of(g), b_of(g).astype(BF16))
            for jj in range(N_HEADS // N_GROUPS // 2):
                j = g * (N_HEADS // N_GROUPS // 2) + jj
                ms = []
                for h in (2 * j, 2 * j + 1):
                    diff = cs[:, h:h + 1] - cs_t[h:h + 1, :]
                    dec = jnp.exp(jnp.where(mask, diff, -jnp.inf))
                    ms.append((sc * dec * dt_t[h:h + 1, :]).astype(BF16))
                lhs = jnp.concatenate(ms, axis=1)
                cols = slice(j * 128, (j + 1) * 128)
                xp = xs_ref[:, cols]
                rhs = jnp.concatenate([jnp.where(lo, xp, 0.0).astype(BF16),
                                       jnp.where(lo, 0.0, xp).astype(BF16)], axis=0)
                yd = _dot(lhs, rhs)
                ecs_p = jnp.where(lo, ecs[:, 2 * j:2 * j + 1], ecs[:, 2 * j + 1:2 * j + 2])
                wgt_p = jnp.where(lo, wgt[:, 2 * j:2 * j + 1], wgt[:, 2 * j + 1:2 * j + 2])
                y_scr[rows, cols] = yd + yoff_scr[rows, cols] * ecs_p + xp * dsk_ref[:, cols]
                xw_scr[rows, cols] = (xp * wgt_p).astype(BF16)
                pair = u * (N_HEADS // 2) + j
                for ch in emit_at.get(pair, []):
                    ch()

        for g in range(N_GROUPS):
            xwg = xw_scr[rows, g * GROUP_W:(g + 1) * GROUP_W]
            for s in range(ns):
                bg = b_of(g)
                if ns > 1:
                    bg = jnp.where(rowseq == s, bg, 0.0)
                upd = _dot_tn(xwg, bg.astype(BF16))
                for hh in range(GROUP_W // HEAD_DIM):
                    h = g * (GROUP_W // HEAD_DIM) + hh
                    dec = jnp.broadcast_to(ecl[s * q:s * q + 1, h:h + 1], (HEAD_DIM, D_STATE))
                    hrows = slice(h * HEAD_DIM, (h + 1) * HEAD_DIM)
                    sout_ref[u * ns + s, hrows, :] = (
                        sout_ref[u * ns + s, hrows, :] * dec + upd[hh * HEAD_DIM:(hh + 1) * HEAD_DIM, :])

    for rt in row_tiles:
        for g in range(N_GROUPS):
            cts = [slice(g * GROUP_W + c.start, g * GROUP_W + c.stop) for c in col_tiles(GROUP_W)]
            ss = None
            for ct in cts:
                yz = y_scr[rt, ct] * zg_scr[rt, ct]
                part = jnp.sum(yz * yz, axis=-1, keepdims=True)
                ss = part if ss is None else ss + part
            scale = lax.rsqrt(ss * (1.0 / GROUP_W) + NORM_EPS)
            for ct in cts:
                yn_scr[rt, ct] = (y_scr[rt, ct] * zg_scr[rt, ct] * scale * ng_ref[:, ct]).astype(BF16)

    for c0 in range(0, D_MODEL, PROJ_CHUNK):
        ct = slice(c0, c0 + PROJ_CHUNK)
        o = xres_scr[:, ct] + _dot(yn_scr[...], wout_ref[:, ct])
        for s in range(nseq):
            xo_ref[s, :, ct] = o[s * q:(s + 1) * q, :]


def _ssd_layer(x, conv_prev, ssm_prev, norm_g, w_in, conv_w, conv_b, dt_bias, a_log, d_skip,
               norm_y, w_out, *, nsub, ns, q):
    n_seq, seq_len, _ = x.shape
    nseq = nsub * ns
    sb = ns * q
    r = nsub * sb
    n_chunks = seq_len // q
    pad = DT_PAD - N_HEADS
    win = jnp.concatenate([w_in, jnp.zeros((D_MODEL, pad), F32)], axis=1).astype(BF16)
    dtb = jnp.pad(dt_bias, (0, pad)).reshape(1, DT_PAD)
    alog = jnp.pad(a_log, (0, pad)).reshape(1, DT_PAD)
    dsk = jnp.repeat(d_skip, HEAD_DIM).reshape(1, D_INNER)
    blk = np.kron(np.eye(ns), np.ones((q, q)))
    tri = jnp.asarray(blk * np.tril(np.ones((sb, sb))), F32)
    ones = jnp.asarray(blk, F32)
    state = ssm_prev.reshape(n_seq, D_INNER, D_STATE)
    conv_prev = conv_prev.reshape(n_seq, CONV_W - 1, CONV_DIM)

    kern = functools.partial(_ssd_kernel, nsub=nsub, ns=ns, q=q, n_chunks=n_chunks)
    out_shape = (jax.ShapeDtypeStruct(x.shape, F32),
                 jax.ShapeDtypeStruct((n_seq, CONV_W - 1, CONV_DIM), F32),
                 jax.ShapeDtypeStruct((n_seq, D_INNER, D_STATE), F32))
    n_steps = (n_seq // nseq) * n_chunks
    nxt = lambda i: jnp.minimum(i, n_steps - 1)
    cur = lambda i: jnp.maximum(i - 1, 0)
    in_row_spec = pl.BlockSpec((nseq, q, D_MODEL),
                               lambda i: (nxt(i) // n_chunks, nxt(i) % n_chunks, 0))
    row_spec = pl.BlockSpec((nseq, q, D_MODEL),
                            lambda i: (cur(i) // n_chunks, cur(i) % n_chunks, 0))
    conv_spec = pl.BlockSpec((nseq, CONV_W - 1, CONV_DIM), lambda i: (cur(i) // n_chunks, 0, 0))
    state_spec = pl.BlockSpec((nseq, D_INNER, D_STATE), lambda i: (cur(i) // n_chunks, 0, 0))
    in_specs = [in_row_spec,
                _const_spec((1, D_MODEL)),
                _const_spec(win.shape),
                _const_spec((CONV_W, CONV_DIM)),
                _const_spec((1, CONV_DIM)),
                _const_spec((1, DT_PAD)),
                _const_spec((1, DT_PAD)),
                _const_spec((1, D_INNER)),
                _const_spec((1, D_INNER)),
                _const_spec((D_INNER, D_MODEL)),
                _const_spec((sb, sb)),
                _const_spec((sb, sb)),
                conv_spec, state_spec]
    x_new, conv_new, state_new = pl.pallas_call(
        kern,
        grid=(n_steps + 1,),
        in_specs=in_specs,
        out_specs=(row_spec, conv_spec, state_spec),
        out_shape=out_shape,
        scratch_shapes=[pltpu.VMEM((r, D_MODEL), F32),
                        pltpu.VMEM((r, D_INNER), F32),
                        pltpu.VMEM((r, CONV_DIM), F32),
                        pltpu.VMEM((r, DT_PAD), F32),
                        pltpu.VMEM((r, D_MODEL), F32),
                        pltpu.VMEM((r, D_INNER), F32),
                        pltpu.VMEM((nseq * 8, CONV_DIM), F32),
                        pltpu.VMEM((r, CONV_DIM), F32),
                        pltpu.VMEM((r, D_INNER), F32),
                        pltpu.VMEM((r, D_INNER), F32),
                        pltpu.VMEM((r, D_INNER), BF16),
                        pltpu.VMEM((r, D_MODEL), BF16),
                        pltpu.VMEM((r, D_INNER), BF16)],
        compiler_params=pltpu.CompilerParams(
            dimension_semantics=("arbitrary",), vmem_limit_bytes=VMEM_LIMIT),
        name="ssd_layer",
    )(x, norm_g.reshape(1, D_MODEL), win, conv_w, conv_b.reshape(1, CONV_DIM), dtb, alog, dsk,
      norm_y.reshape(1, D_INNER), w_out.astype(BF16), tri, ones, conv_prev, state)
    return (x_new, conv_new.reshape(1, n_seq, CONV_W - 1, CONV_DIM),
            state_new.reshape(1, n_seq, N_HEADS, HEAD_DIM, D_STATE))


def _sg_kernel(x_ref, g_ref, win_ref, bin_ref, lng_ref, lnb_ref, wmix_ref, bmix_ref, wout_ref,
               xo_ref, *rest, r, want_v):
    v_ref = rest[0] if want_v else None
    hn_scr, uv_scr, vb_scr, um_scr = rest[-4:]
    row_tiles = [slice(a, a + 128) for a in range(0, r, 128)]
    col128 = lambda total: [slice(c, c + 128) for c in range(0, total, 128)]

    for rt in row_tiles:
        ss = None
        for ct in col128(D_MODEL):
            xv = x_ref[rt, ct]
            part = jnp.sum(xv * xv, axis=-1, keepdims=True)
            ss = part if ss is None else ss + part
        scale = lax.rsqrt(ss * (1.0 / D_MODEL) + NORM_EPS)
        for ct in col128(D_MODEL):
            hn_scr[rt, ct] = (x_ref[rt, ct] * scale * g_ref[:, ct]).astype(BF16)

    vcols = [slice(SG_WIDTH + c.start, SG_WIDTH + c.stop) for c in col128(SG_WIDTH)]

    def front(rt):
        def piece(c0):
            def run():
                h = _dot(hn_scr[rt, :], win_ref[:, c0:c0 + PROJ_CHUNK])
                for cc in range(0, PROJ_CHUNK, 128):
                    ct = slice(c0 + cc, c0 + cc + 128)
                    uv_scr[rt, ct] = _gelu_tanh(h[:, cc:cc + 128] + bin_ref[:, ct])
            return run
        return [piece(c0) for c0 in range(0, 2 * SG_WIDTH, PROJ_CHUNK)]

    def back(rt):
        stats = {}

        def ln_stats():
            tot = None
            for ct in vcols:
                part = jnp.sum(uv_scr[rt, ct], axis=-1, keepdims=True)
                tot = part if tot is None else tot + part
            mu = tot * (1.0 / SG_WIDTH)
            ss = None
            for ct in vcols:
                vc = uv_scr[rt, ct] - mu
                part = jnp.sum(vc * vc, axis=-1, keepdims=True)
                ss = part if ss is None else ss + part
            stats["mu"] = mu
            stats["scale"] = lax.rsqrt(ss * (1.0 / SG_WIDTH) + LN_EPS)

        def ln_apply(k0):
            def run():
                for ct, c in list(zip(vcols, col128(SG_WIDTH)))[k0:k0 + 4]:
                    vn = (uv_scr[rt, ct] - stats["mu"]) * stats["scale"] * lng_ref[:, c] + lnb_ref[:, c]
                    if want_v:
                        v_ref[rt, c] = vn
                    vb_scr[rt, c] = vn.astype(BF16)
            return run

        def mix(g):
            def run():
                cols = slice(g * SG_GROUP_DIM, (g + 1) * SG_GROUP_DIM)
                mixed = _dot(wmix_ref[g], vb_scr[rt, cols]) + bmix_ref[:, cols]
                um_scr[rt, cols] = (uv_scr[rt, cols] * mixed).astype(BF16)
            return run

        def out(c0):
            def run():
                ct = slice(c0, c0 + PROJ_CHUNK)
                xo_ref[rt, ct] = x_ref[rt, ct] + _dot(um_scr[rt, :], wout_ref[:, ct])
            return run

        return ([ln_stats] + [ln_apply(k) for k in range(0, len(vcols), 4)]
                + [mix(g) for g in range(SG_GROUPS)]
                + [out(c0) for c0 in range(0, D_MODEL, PROJ_CHUNK)])

    pending = []
    for rt in row_tiles:
        for piece in front(rt):
            piece()
            if pending:
                pending.pop(0)()
        for piece in pending:
            piece()
        pending = back(rt)
    for piece in pending:
        piece()


def _sg_layer(x2d, norm_g, w_in, b_in, ln_g, ln_b, w_s, b_s, w_out, *, seq_len, want_v):
    r = SG_ROWS
    t = x2d.shape[0]
    q = min(seq_len, SG_CHUNK)
    reps = SG_CHUNK // q
    ws = jnp.tril(w_s)[:, :q, :q]
    wmix = jnp.einsum("ab,gts->gatbs", jnp.eye(reps, dtype=F32), ws)
    wmix = wmix.reshape(SG_GROUPS, SG_CHUNK, SG_CHUNK)
    bmix = jnp.tile(jnp.repeat(b_s.T[:q], SG_GROUP_DIM, axis=1), (reps, 1))
    row_spec = pl.BlockSpec((r, D_MODEL), lambda i: (i, 0))
    v_spec = pl.BlockSpec((r, SG_WIDTH), lambda i: (i, 0))
    out_shape = [jax.ShapeDtypeStruct(x2d.shape, F32)]
    out_specs = [row_spec]
    if want_v:
        out_shape.append(jax.ShapeDtypeStruct((t, SG_WIDTH), F32))
        out_specs.append(v_spec)
    outs = pl.pallas_call(
        functools.partial(_sg_kernel, r=r, want_v=want_v),
        grid=(t // r,),
        scratch_shapes=[pltpu.VMEM((r, D_MODEL), BF16),
                        pltpu.VMEM((r, 2 * SG_WIDTH), F32),
                        pltpu.VMEM((r, SG_WIDTH), BF16),
                        pltpu.VMEM((r, SG_WIDTH), BF16)],
        in_specs=[row_spec,
                  _const_spec((1, D_MODEL)),
                  _const_spec((D_MODEL, 2 * SG_WIDTH)),
                  _const_spec((1, 2 * SG_WIDTH)),
                  _const_spec((1, SG_WIDTH)),
                  _const_spec((1, SG_WIDTH)),
                  _const_spec((SG_GROUPS, SG_CHUNK, SG_CHUNK)),
                  _const_spec((SG_CHUNK, SG_WIDTH)),
                  _const_spec((SG_WIDTH, D_MODEL))],
        out_specs=out_specs,
        out_shape=out_shape,
        compiler_params=pltpu.CompilerParams(
            dimension_semantics=("arbitrary",), vmem_limit_bytes=VMEM_LIMIT),
        name="sg_layer",
    )(x2d, norm_g.reshape(1, D_MODEL), w_in.astype(BF16), b_in.reshape(1, 2 * SG_WIDTH),
      ln_g.reshape(1, SG_WIDTH), ln_b.reshape(1, SG_WIDTH), wmix.astype(BF16), bmix,
      w_out.astype(BF16))
    return outs if want_v else (outs[0], None)


ROUTER_ROWS = 64


def _norm_tiles(x_ref, g_ref, store):
    for r0 in range(0, x_ref.shape[0], 128):
        rt = slice(r0, r0 + 128)
        ss = None
        for c0 in range(0, D_MODEL, 256):
            xv = x_ref[rt, c0:c0 + 256]
            part = jnp.sum(xv * xv, axis=-1, keepdims=True)
            ss = part if ss is None else ss + part
        scale = lax.rsqrt(ss * (1.0 / D_MODEL) + NORM_EPS)
        for c0 in range(0, D_MODEL, 256):
            ct = slice(c0, c0 + 256)
            store(rt, ct, x_ref[rt, ct] * scale * g_ref[:, ct])


def _router_t_kernel(xa_ref, xb_ref, g_ref, wt_ref, bt_ref, usl_ref, lsl_ref,
                     meta_ref, metat_ref, pad_ref, h1_scr, h2_scr, *, steps_a):
    wt = wt_ref[...]
    w1 = wt.astype(BF16)
    w2 = (wt - w1.astype(F32)).astype(BF16)
    tt = MOE_TILE
    n = xa_ref.shape[0]

    def split_store(rt, ct, hn):
        hi = hn.astype(BF16)
        h1_scr[rt, ct] = hi
        h2_scr[rt, ct] = (hn - hi.astype(F32)).astype(BF16)

    @pl.when(pl.program_id(0) < steps_a)
    def _():
        _norm_tiles(xa_ref, g_ref, split_store)

    @pl.when(pl.program_id(0) >= steps_a)
    def _():
        _norm_tiles(xb_ref, g_ref, split_store)

    h1 = h1_scr[...]
    logits = _dot_nt(w1, h1) + _dot_nt(w2, h1) + _dot_nt(w1, h2_scr[...])
    select = _select_experts(logits, bt_ref[...])
    for k in range(n // tt):
        cols = slice(k * tt, (k + 1) * tt)
        metat, pad = _local_order(select[:, cols], usl_ref[...], lsl_ref[...])
        metat_ref[k * 8:(k + 1) * 8, :] = metat
        full = jnp.concatenate([metat, jnp.zeros((128 - 8, tt), F32)], axis=0)
        meta_ref[cols, :] = full.T
        pad_ref[k * MOE_EXPERTS:(k + 1) * MOE_EXPERTS, :] = pad


def _select_experts(logits, bt):
    n = logits.shape[1]
    reps = n // bt.shape[1]
    logits = logits + (bt if reps == 1 else jnp.concatenate([bt] * reps, axis=1))
    tt = n
    row8 = lax.broadcasted_iota(I32, (8, tt), 0).astype(F32)
    gl = jnp.where(row8 < MOE_GROUPS, logits[0:8, :], -jnp.inf)
    gmax = jnp.max(gl, axis=0, keepdims=True)
    g_top = jnp.min(jnp.where(gl == gmax, row8, 8.0), axis=0, keepdims=True)
    p_g = 1.0 / jnp.sum(jnp.exp(gl - gmax), axis=0, keepdims=True)

    el = logits[8:16, :]
    for grp in range(1, MOE_GROUPS):
        el = jnp.where(g_top == grp, logits[8 + 8 * grp:16 + 8 * grp, :], el)
    emax = jnp.max(el, axis=0, keepdims=True)
    ee = jnp.exp(el - emax)
    prob = ee / jnp.sum(ee, axis=0, keepdims=True)
    p1 = jnp.max(prob, axis=0, keepdims=True)
    i1 = jnp.min(jnp.where(prob == p1, row8, 8.0), axis=0, keepdims=True)
    prob2 = jnp.where(row8 == i1, -1.0, prob)
    p2 = jnp.max(prob2, axis=0, keepdims=True)
    i2 = jnp.min(jnp.where(prob2 == p2, row8, 8.0), axis=0, keepdims=True)
    psum = p1 + p2
    gate1 = p_g * (p1 / psum)
    gate2 = p_g * (p2 / psum)
    e1 = g_top * MOE_EPG + i1
    e2 = g_top * MOE_EPG + i2
    out = jnp.where(row8 == 0, e1, 0.0)
    out = jnp.where(row8 == 1, e2, out)
    out = jnp.where(row8 == 2, gate1, out)
    return jnp.where(row8 == 3, gate2, out)


def _local_order(select, usl, lsl):
    tt = select.shape[1]
    row8 = lax.broadcasted_iota(I32, (8, tt), 0).astype(F32)
    e1, e2 = select[0:1, :], select[1:2, :]
    rowe = lax.broadcasted_iota(I32, (MOE_EXPERTS, tt), 0).astype(F32)
    sel1 = rowe == e1
    sel2 = rowe == e2
    onehot = jnp.where(sel1 | sel2, 1.0, 0.0)
    before = _dot(onehot.astype(BF16), usl)
    cnt = jnp.sum(onehot, axis=1, keepdims=True)
    pad = jnp.floor((cnt + (GRANULE - 1)) * (1.0 / GRANULE)) * GRANULE
    padb = jnp.broadcast_to(pad, (MOE_EXPERTS, tt))
    local = before + _dot(lsl, padb.astype(BF16))
    slot1 = jnp.sum(jnp.where(sel1, local, 0.0), axis=0, keepdims=True)
    slot2 = jnp.sum(jnp.where(sel2, local, 0.0), axis=0, keepdims=True)

    metat = jnp.where(row8 == 4, slot1, select)
    metat = jnp.where(row8 == 5, slot2, metat)
    return metat, padb[:, 0:128]


def _granule(ref, g):
    return ref.at[pl.ds(pl.multiple_of(g * GRANULE, GRANULE), GRANULE), :]


def _dispatch_kernel(ngran_ref, nv_ref, dest_ref, gap_ref, xa_ref, xb_ref, g_ref, meta_ref, xs_hbm,
                     buf, zbuf, hn_scr, sem, zsem, nstart, *, tiles_a):
    i = pl.program_id(0)
    n_tiles = pl.num_programs(0) - 1
    slot = i % 2
    tt = xa_ref.shape[0]
    bm = zbuf.shape[0]
    n_blocks = xs_hbm.shape[0] // bm

    def out_copy(sl, g, d):
        return pltpu.make_async_copy(_granule(buf.at[sl], g), _granule(xs_hbm, d), sem.at[sl])

    def drain(sl):
        def body(_, carry):
            out_copy(sl, 0, 0).wait()
            return carry
        lax.fori_loop(0, nstart[sl], body, 0)

    @pl.when(i == 0)
    def _():
        nstart[0] = 0
        nstart[1] = 0

    def norm_store(rt, ct, hn):
        hn_scr[rt, ct] = hn.astype(BF16)

    @pl.when(i < tiles_a)
    def _():
        _norm_tiles(xa_ref, g_ref, norm_store)

    @pl.when((i >= tiles_a) & (i < n_tiles))
    def _():
        _norm_tiles(xb_ref, g_ref, norm_store)

    @pl.when(i < n_tiles)
    def _():
        drain(slot)
        hn = hn_scr[...]
        mt = meta_ref[...]
        for r0 in range(0, LOCAL_ROWS, 256):
            rows = (lax.broadcasted_iota(I32, (256, tt), 0) + r0).astype(F32)
            onehot = jnp.where((rows == mt[4:5, :]) | (rows == mt[5:6, :]), 1.0, 0.0).astype(BF16)
            buf[slot, r0:r0 + 256, :] = _dot(onehot, hn).astype(BF16)
        ng = ngran_ref[i]

        def body(g, carry):
            out_copy(slot, g, dest_ref[0, 0, g]).start()
            return carry
        lax.fori_loop(0, ng, body, 0)
        nstart[slot] = ng

    @pl.when(i == n_tiles)
    def _():
        drain(0)
        drain(1)
        zbuf[...] = jnp.zeros_like(zbuf)

        def gap_copy(d):
            return pltpu.make_async_copy(_granule(zbuf, 0), _granule(xs_hbm, d), zsem.at[0])

        def tail_copy(b):
            return pltpu.make_async_copy(
                zbuf, xs_hbm.at[pl.ds(pl.multiple_of(b * bm, bm), bm), :], zsem.at[1])

        def each_gap(fn):
            def body(j, carry):
                d = gap_ref[0, 0, j]

                @pl.when(d >= 0)
                def _():
                    fn(gap_copy(d))
                return carry
            lax.fori_loop(0, gap_ref.shape[2], body, 0)

        def each_tail(fn):
            def body(b, carry):
                fn(tail_copy(b))
                return carry
            lax.fori_loop(nv_ref[0], n_blocks, body, 0)

        each_gap(lambda cp: cp.start())
        each_tail(lambda cp: cp.start())
        each_gap(lambda cp: cp.wait())
        each_tail(lambda cp: cp.wait())


def _expert_kernel(bexp_ref, nv_ref, xs_ref, wg_ref, wu_ref, wd_ref, ys_ref,
                   wgu_b, wd_b, act_scr):
    b = pl.program_id(0)
    ff = wg_ref.shape[1]

    @pl.when(b < nv_ref[0])
    def _():
        @pl.when((b == 0) | (bexp_ref[b] != bexp_ref[jnp.maximum(b - 1, 0)]))
        def _():
            for k0 in range(0, wg_ref.shape[0], 256):
                wgu_b[k0:k0 + 256, 0:ff] = wg_ref[k0:k0 + 256, :].astype(BF16)
                wgu_b[k0:k0 + 256, ff:2 * ff] = wu_ref[k0:k0 + 256, :].astype(BF16)
            for k0 in range(0, ff, 64):
                wd_b[k0:k0 + 64, :] = wd_ref[k0:k0 + 64, :].astype(BF16)

        bm = xs_ref.shape[0]
        for m0 in range(0, bm, 256):
            h = _dot(xs_ref[m0:m0 + 256, :], wgu_b[...])
            for r0 in range(0, 256, 128):
                for c0 in range(0, ff, 128):
                    hg = h[r0:r0 + 128, c0:c0 + 128]
                    hu = h[r0:r0 + 128, ff + c0:ff + c0 + 128]
                    act_scr[m0 + r0:m0 + r0 + 128, c0:c0 + 128] = (_silu(hg) * hu).astype(BF16)
            ys_ref[m0:m0 + 256, :] = _dot(act_scr[m0:m0 + 256, :], wd_b[...]).astype(BF16)


def _combine_kernel(ngran_ref, src_ref, srcn_ref, xa_ref, xb_ref, meta_ref, gf_ref, ys_hbm,
                    oa_ref, ob_ref, buf, moe_scr, sem, *, final, tiles_a):
    i = pl.program_id(0)
    n = pl.num_programs(0)
    slot = i % 2
    tt = xa_ref.shape[0]

    def in_copy(sl, g, d):
        return pltpu.make_async_copy(_granule(ys_hbm, d), _granule(buf.at[sl], g), sem.at[sl])

    def gather(idx_ref, sl, ng):
        def body(g, carry):
            in_copy(sl, g, idx_ref[0, 0, g]).start()
            return carry
        lax.fori_loop(0, ng, body, 0)

    @pl.when(i == 0)
    def _():
        buf[...] = jnp.zeros_like(buf)
        gather(src_ref, 0, ngran_ref[0])

    @pl.when(i + 1 < n)
    def _():
        gather(srcn_ref, 1 - slot, ngran_ref[i + 1])

    def wait_body(_, carry):
        in_copy(slot, 0, 0).wait()
        return carry
    lax.fori_loop(0, ngran_ref[i], wait_body, 0)

    ys = buf[slot]
    meta = meta_ref[...]
    cols = lax.broadcasted_iota(I32, (tt, LOCAL_ROWS), 1).astype(F32)
    pick = (jnp.where(cols == meta[:, 4:5], meta[:, 2:3], 0.0)
            + jnp.where(cols == meta[:, 5:6], meta[:, 3:4], 0.0)).astype(BF16)
    moe_scr[...] = _dot(pick, ys)

    def finish(x_ref, o_ref):
        for r0 in range(0, tt, 128):
            rt = slice(r0, r0 + 128)
            if final:
                ss = None
                for c0 in range(0, D_MODEL, 256):
                    yv = x_ref[rt, c0:c0 + 256] + moe_scr[rt, c0:c0 + 256]
                    part = jnp.sum(yv * yv, axis=-1, keepdims=True)
                    ss = part if ss is None else ss + part
                scale = lax.rsqrt(ss * (1.0 / D_MODEL) + NORM_EPS)
            for c0 in range(0, D_MODEL, 256):
                ct = slice(c0, c0 + 256)
                yv = x_ref[rt, ct] + moe_scr[rt, ct]
                o_ref[rt, ct] = yv * scale * gf_ref[:, ct] if final else yv

    @pl.when(i < tiles_a)
    def _():
        finish(xa_ref, oa_ref)

    @pl.when(i >= tiles_a)
    def _():
        finish(xb_ref, ob_ref)


def _moe(xa, xb, norm_g, w_grp, b_grp, w_rt, b_rt, w_gate, w_up, w_down, norm_final, *, layer):
    tt = MOE_TILE
    tiles_a, tiles_b = xa.shape[0] // tt, xb.shape[0] // tt
    steps_a, steps_b = tiles_a // ROUTER_TILES, tiles_b // ROUTER_TILES
    n_tiles = tiles_a + tiles_b
    t = n_tiles * tt
    bm = _expert_block(t)
    n_exp = MOE_EXPERTS
    a_blk = lambda i, n_a: jnp.minimum(i, n_a - 1)
    b_blk = lambda i, n_a, n_b: jnp.clip(i - n_a, 0, n_b - 1)
    zrow = lambda n: jnp.zeros((n, D_MODEL), F32)
    wt = jnp.concatenate([w_grp.T, zrow(8 - MOE_GROUPS), w_rt.T, zrow(ROUTER_ROWS - 8 - n_exp)], axis=0)
    bt = jnp.concatenate([b_grp, jnp.zeros((8 - MOE_GROUPS,), F32), b_rt,
                          jnp.zeros((ROUTER_ROWS - 8 - n_exp,), F32)])
    bt = jnp.broadcast_to(bt[:, None], (ROUTER_ROWS, tt))
    usl = jnp.asarray(np.triu(np.ones((tt, tt)), 1), BF16)
    lsl = jnp.asarray(np.tril(np.ones((n_exp, n_exp)), -1), BF16)
    g2d = norm_g.reshape(1, D_MODEL)
    arb = pltpu.CompilerParams(dimension_semantics=("arbitrary",), vmem_limit_bytes=VMEM_LIMIT)

    meta, metat, pad = pl.pallas_call(
        functools.partial(_router_t_kernel, steps_a=steps_a),
        grid=(steps_a + steps_b,),
        in_specs=[pl.BlockSpec((ROUTER_TILES * tt, D_MODEL), lambda i: (a_blk(i, steps_a), 0)),
                  pl.BlockSpec((ROUTER_TILES * tt, D_MODEL),
                               lambda i: (b_blk(i, steps_a, steps_b), 0)),
                  _const_spec((1, D_MODEL)),
                  _const_spec((ROUTER_ROWS, D_MODEL)),
                  _const_spec((ROUTER_ROWS, tt)),
                  _const_spec((tt, tt)),
                  _const_spec((n_exp, n_exp))],
        out_specs=(pl.BlockSpec((ROUTER_TILES * tt, 128), lambda i: (i, 0)),
                   pl.BlockSpec((ROUTER_TILES * 8, tt), lambda i: (i, 0)),
                   pl.BlockSpec((ROUTER_TILES * n_exp, 128), lambda i: (i, 0))),
        out_shape=(jax.ShapeDtypeStruct((t, 128), F32),
                   jax.ShapeDtypeStruct((n_tiles * 8, tt), F32),
                   jax.ShapeDtypeStruct((n_tiles * n_exp, 128), F32)),
        scratch_shapes=[pltpu.VMEM((ROUTER_TILES * tt, D_MODEL), BF16),
                        pltpu.VMEM((ROUTER_TILES * tt, D_MODEL), BF16)],
        compiler_params=arb,
        name="moe_router",
    )(xa, xb, g2d, wt, bt, usl, lsl)

    runs = pad.reshape(n_tiles, n_exp, 128)[:, :, 0].astype(I32)
    rows_e = jnp.sum(runs, axis=0)
    nblk = (rows_e + bm - 1) // bm
    blk_end = jnp.cumsum(nblk)
    e_start = (blk_end - nblk) * bm
    n_valid = blk_end[-1]
    run_end = jnp.cumsum(runs, axis=1)
    ngran = (run_end[:, -1] // GRANULE).astype(I32)
    shift = e_start[None, :] + (jnp.cumsum(runs, axis=0) - runs) - (run_end - runs)
    g_row = jnp.arange(N_LOCAL_GRAN, dtype=I32) * GRANULE
    e_of_g = jnp.sum((run_end[:, None, :] <= g_row[None, :, None]).astype(I32), axis=-1)
    shift_g = jnp.sum(jnp.where(e_of_g[..., None] == jnp.arange(n_exp, dtype=I32),
                                shift[:, None, :], 0), axis=-1)
    dest = jnp.where(e_of_g < n_exp, (shift_g + g_row[None, :]) // GRANULE, 0)
    dest = dest.astype(I32).reshape(n_tiles, 1, N_LOCAL_GRAN)
    per_blk = bm // GRANULE
    gap = ((e_start + rows_e) // GRANULE)[:, None] + jnp.arange(per_blk, dtype=I32)[None, :]
    gap = jnp.where(gap < ((e_start + nblk * bm) // GRANULE)[:, None], gap, -1)
    gap = gap.astype(I32).reshape(1, 1, n_exp * per_blk)
    n_blocks = (2 * t + n_tiles * n_exp * (GRANULE - 1)) // bm + 1 + n_exp
    blk = jnp.minimum(jnp.arange(n_blocks, dtype=I32), n_valid - 1)
    block_expert = jnp.sum((blk[:, None] >= blk_end[None, :]).astype(I32), axis=1).astype(I32)
    n_valid = n_valid.reshape(1).astype(I32)

    last = n_tiles - 1
    xs = pl.pallas_call(
        functools.partial(_dispatch_kernel, tiles_a=tiles_a),
        grid_spec=pltpu.PrefetchScalarGridSpec(
            num_scalar_prefetch=2,
            grid=(n_tiles + 1,),
            in_specs=[pl.BlockSpec((1, 1, N_LOCAL_GRAN),
                                   lambda i, ng, nv: (jnp.minimum(i, last), 0, 0),
                                   memory_space=pltpu.SMEM),
                      pl.BlockSpec((1, 1, n_exp * per_blk), lambda i, ng, nv: (0, 0, 0),
                                   memory_space=pltpu.SMEM),
                      pl.BlockSpec((tt, D_MODEL), lambda i, ng, nv: (a_blk(i, tiles_a), 0)),
                      pl.BlockSpec((tt, D_MODEL),
                                   lambda i, ng, nv: (b_blk(i, tiles_a, tiles_b), 0)),
                      _const_spec((1, D_MODEL)),
                      pl.BlockSpec((8, tt), lambda i, ng, nv: (jnp.minimum(i, last), 0))],
            out_specs=pl.BlockSpec(memory_space=pl.ANY),
            scratch_shapes=[pltpu.VMEM((2, LOCAL_ROWS, D_MODEL), BF16),
                            pltpu.VMEM((bm, D_MODEL), BF16),
                            pltpu.VMEM((tt, D_MODEL), BF16),
                            pltpu.SemaphoreType.DMA((2,)),
                            pltpu.SemaphoreType.DMA((2,)),
                            pltpu.SMEM((2,), I32)]),
        out_shape=jax.ShapeDtypeStruct((n_blocks * bm, D_MODEL), BF16),
        compiler_params=arb,
        name="moe_dispatch",
    )(ngran, n_valid, dest, gap, xa, xb, g2d, metat)

    w_spec = lambda shape: pl.BlockSpec((None, None) + shape,
                                        lambda b, be, nv: (layer, be[b], 0, 0))
    row_blk = pl.BlockSpec((bm, D_MODEL), lambda b, be, nv: (jnp.minimum(b, nv[0] - 1), 0))
    ys = pl.pallas_call(
        _expert_kernel,
        grid_spec=pltpu.PrefetchScalarGridSpec(
            num_scalar_prefetch=2,
            grid=(n_blocks,),
            in_specs=[row_blk,
                      w_spec((D_MODEL, MOE_D_FF)),
                      w_spec((D_MODEL, MOE_D_FF)),
                      w_spec((MOE_D_FF, D_MODEL))],
            out_specs=row_blk,
            scratch_shapes=[pltpu.VMEM((D_MODEL, 2 * MOE_D_FF), BF16),
                            pltpu.VMEM((MOE_D_FF, D_MODEL), BF16),
                            pltpu.VMEM((bm, MOE_D_FF), BF16)]),
        out_shape=jax.ShapeDtypeStruct((n_blocks * bm, D_MODEL), BF16),
        input_output_aliases={2: 0},
        compiler_params=arb,
        name="moe_experts",
    )(block_expert, n_valid, xs, w_gate, w_up, w_down)

    final = norm_final is not None
    gf = (norm_final if final else jnp.ones((D_MODEL,), F32)).reshape(1, D_MODEL)
    src_spec = lambda nxt: pl.BlockSpec(
        (1, 1, N_LOCAL_GRAN), lambda i, ng: (jnp.minimum(i + nxt, last), 0, 0),
        memory_space=pltpu.SMEM)
    a_spec = pl.BlockSpec((tt, D_MODEL), lambda i, ng: (a_blk(i, tiles_a), 0))
    b_spec = pl.BlockSpec((tt, D_MODEL), lambda i, ng: (b_blk(i, tiles_a, tiles_b), 0))
    out_a, out_b = pl.pallas_call(
        functools.partial(_combine_kernel, final=final, tiles_a=tiles_a),
        grid_spec=pltpu.PrefetchScalarGridSpec(
            num_scalar_prefetch=1,
            grid=(n_tiles,),
            in_specs=[src_spec(0), src_spec(1), a_spec, b_spec,
                      pl.BlockSpec((tt, 128), lambda i, ng: (i, 0)),
                      _const_spec((1, D_MODEL)),
                      pl.BlockSpec(memory_space=pl.ANY)],
            out_specs=(a_spec, b_spec),
            scratch_shapes=[pltpu.VMEM((2, LOCAL_ROWS, D_MODEL), BF16),
                            pltpu.VMEM((tt, D_MODEL), F32),
                            pltpu.SemaphoreType.DMA((2,))]),
        out_shape=(jax.ShapeDtypeStruct(xa.shape, F32), jax.ShapeDtypeStruct(xb.shape, F32)),
        compiler_params=arb,
        name="moe_combine",
    )(ngran, dest, dest, xa, xb, meta, gf, ys)
    return out_a, out_b


def _ssd(x, conv_prev, ssm_prev, p, *, nsub, ns):
    seq_len = x.shape[1]
    q = seq_len if seq_len < SSD_CHUNK else SSD_CHUNK
    x, conv_new, ssm_new = _ssd_layer(
        x, conv_prev, ssm_prev, p["norm_mix"][0], p["ssd_w_in"][0], p["ssd_conv_w"][0],
        p["ssd_conv_b"][0], p["ssd_dt_bias"][0], p["ssd_a_log"][0], p["ssd_d"][0], p["ssd_norm"][0],
        p["ssd_w_out"][0], nsub=nsub, ns=ns, q=q)
    return x.reshape(-1, D_MODEL), conv_new, ssm_new


def _sg(x2d, p, *, seq_len, want_v):
    return _sg_layer(x2d, p["norm_mix"][1], p["sg_w_in"][0], p["sg_b_in"][0], p["sg_ln_g"][0],
                     p["sg_ln_b"][0], p["sg_w_s"][0], p["sg_b_s"][0], p["sg_w_out"][0],
                     seq_len=seq_len, want_v=want_v)


def _moe_layer(xa, xb, p, layer, norm_final):
    return _moe(xa, xb, p["norm_ffn"][layer], p["moe_w_group"][layer], p["moe_b_group"][layer],
                p["moe_w_router"][layer], p["moe_b_router"][layer], p["moe_w_gate"], p["moe_w_up"],
                p["moe_w_down"], norm_final, layer=layer)


def kernel(x_prompt, x_sample, state_ssm, state_conv, norm_mix, norm_ffn, norm_final, ssd_w_in, ssd_conv_w, ssd_conv_b, ssd_dt_bias, ssd_a_log, ssd_d, ssd_norm, ssd_w_out, sg_w_in, sg_b_in, sg_ln_g, sg_ln_b, sg_w_s, sg_b_s, sg_w_out, moe_w_group, moe_b_group, moe_w_router, moe_b_router, moe_w_gate, moe_w_up, moe_w_down):
    p = dict(norm_mix=norm_mix, norm_ffn=norm_ffn, norm_final=norm_final, ssd_w_in=ssd_w_in,
             ssd_conv_w=ssd_conv_w, ssd_conv_b=ssd_conv_b, ssd_dt_bias=ssd_dt_bias,
             ssd_a_log=ssd_a_log, ssd_d=ssd_d, ssd_norm=ssd_norm, ssd_w_out=ssd_w_out,
             sg_w_in=sg_w_in, sg_b_in=sg_b_in, sg_ln_g=sg_ln_g, sg_ln_b=sg_ln_b, sg_w_s=sg_w_s,
             sg_b_s=sg_b_s, sg_w_out=sg_w_out, moe_w_group=moe_w_group, moe_b_group=moe_b_group,
             moe_w_router=moe_w_router, moe_b_router=moe_b_router, moe_w_gate=moe_w_gate,
             moe_w_up=moe_w_up, moe_w_down=moe_w_down)
    nb = x_prompt.shape[0]
    conv0 = jnp.zeros((1, nb, CONV_W - 1, CONV_DIM), F32)
    ssm0 = jnp.zeros((1, nb, N_HEADS, HEAD_DIM, D_STATE), F32)
    xp, conv_p, ssm_p = _ssd(x_prompt, conv0, ssm0, p, nsub=2, ns=1)
    xs, conv_s, ssm_s = _ssd(x_sample, state_conv, state_ssm, p, nsub=1, ns=4)
    xp, xs = _moe_layer(xp, xs, p, 0, None)
    xp, _ = _sg(xp, p, seq_len=x_prompt.shape[1], want_v=False)
    xs, v_s = _sg(xs, p, seq_len=x_sample.shape[1], want_v=True)
    y_p, y_s = _moe_layer(xp, xs, p, 1, norm_final)
    return (y_p.reshape(x_prompt.shape), y_s.reshape(x_sample.shape), ssm_p, conv_p, ssm_s, conv_s,
            v_s.reshape((1,) + x_sample.shape[:2] + (SG_WIDTH,)))
```

```python
import functools
import math

import jax
import jax.numpy as jnp
import numpy as np
from jax import lax
from jax.experimental import pallas as pl
from jax.experimental.pallas import tpu as pltpu

F32 = jnp.float32
BF16 = jnp.bfloat16
I32 = jnp.int32

D_MODEL = 1024
N_HEADS = 32
HEAD_DIM = 64
N_GROUPS = 4
D_STATE = 128
D_INNER = N_HEADS * HEAD_DIM
GROUP_W = D_INNER // N_GROUPS
CONV_W = 4
CONV_DIM = D_INNER + 2 * N_GROUPS * D_STATE
DT_PAD = 128
SSD_CHUNK = 128
PROJ_CHUNK = 256
SG_WIDTH = 2 * D_MODEL
SG_GROUPS = 8
SG_GROUP_DIM = SG_WIDTH // SG_GROUPS
SG_CHUNK = 128
SG_ROWS = 512
MOE_GROUPS = 4
MOE_EPG = 8
MOE_EXPERTS = MOE_GROUPS * MOE_EPG
MOE_D_FF = 256
NORM_EPS = 1e-6
LN_EPS = 1e-5

MOE_TILE = 256
ROUTER_TILES = 4


def _expert_block(n_tokens):
    return 512 if 2 * n_tokens >= 512 * MOE_EXPERTS else 256


GRANULE = 16
LOCAL_ROWS = 2 * MOE_TILE + MOE_EXPERTS * GRANULE
N_LOCAL_GRAN = LOCAL_ROWS // GRANULE
VMEM_LIMIT = 56 * 1024 * 1024


def _sigmoid(x):
    return 0.5 * (jnp.tanh(0.5 * x) + 1.0)


def _silu(x):
    return x * _sigmoid(x)


def _softplus(x):
    return jnp.maximum(x, 0.0) + jnp.log(1.0 + jnp.exp(-jnp.abs(x)))


def _gelu_tanh(x):
    c = math.sqrt(2.0 / math.pi)
    t = jnp.tanh(x * (c + (c * 0.044715) * (x * x)))
    hx = 0.5 * x
    return hx + hx * t


def _rms(x, g):
    return x * lax.rsqrt(jnp.mean(x * x, axis=-1, keepdims=True) + NORM_EPS) * g


def _split3(x):
    a = x.astype(BF16)
    r = x - a.astype(F32)
    b = r.astype(BF16)
    c = (r - b.astype(F32)).astype(BF16)
    return a, b, c


def _dot(a, b):
    return jnp.dot(a, b, preferred_element_type=F32)


def _dot_nt(a, b):
    return lax.dot_general(a, b, (((1,), (1,)), ((), ())), preferred_element_type=F32)


def _dot_tn(a, b):
    return lax.dot_general(a, b, (((0,), (0,)), ((), ())), preferred_element_type=F32)


def _const_spec(shape):
    nd = len(shape)
    return pl.BlockSpec(shape, lambda *_: (0,) * nd)


def _ssd_kernel(x_ref, g_ref, win_ref, cw_ref, cb_ref, dtb_ref, alog_ref, dsk_ref, ng_ref,
                wout_ref, tri_ref, ones_ref, cin_ref, sin_ref,
                xo_ref, cout_ref, sout_ref,
                pend_x, pend_z, pend_xbc, pend_dt, xres_scr, zg_scr,
                prev_scr, xc_scr, yoff_scr, y_scr, xw_scr, hn_scr, yn_scr,
                *, nsub, ns, q, n_chunks):
    i = pl.program_id(0)
    first_chunk = lax.rem(jnp.maximum(i - 1, 0), n_chunks) == 0
    nseq = nsub * ns
    sb = ns * q
    r = nsub * sb

    @pl.when(i == 0)
    def _():
        pend_x[...] = jnp.zeros_like(pend_x)
        pend_z[...] = jnp.zeros_like(pend_z)
        pend_xbc[...] = jnp.zeros_like(pend_xbc)
        pend_dt[...] = jnp.zeros_like(pend_dt)

    tr = min(r, 128)
    row_tiles = [slice(a, a + tr) for a in range(0, r, tr)]

    def col_tiles(total, rows=tr):
        w = min(total, max(128, (32 * 1024) // rows))
        return [slice(c, c + w) for c in range(0, total, w)]

    for rt in row_tiles:
        for ct in col_tiles(D_MODEL):
            xres_scr[rt, ct] = pend_x[rt, ct]
        for ct in col_tiles(D_INNER):
            zg_scr[rt, ct] = _silu(pend_z[rt, ct])

    for s in range(nseq):
        for ct in col_tiles(D_MODEL, q):
            pend_x[s * q:(s + 1) * q, ct] = x_ref[s, :, ct]
    for rt in row_tiles:
        ss = None
        for ct in col_tiles(D_MODEL):
            xv = pend_x[rt, ct]
            part = jnp.sum(xv * xv, axis=-1, keepdims=True)
            ss = part if ss is None else ss + part
        scale = lax.rsqrt(ss * (1.0 / D_MODEL) + NORM_EPS)
        for ct in col_tiles(D_MODEL):
            hn_scr[rt, ct] = (pend_x[rt, ct] * scale * g_ref[:, ct]).astype(BF16)

    def proj_chunk(dst, dst_col, w_col, width):
        def run():
            dst[:, dst_col:dst_col + width] = _dot(hn_scr[...], win_ref[:, w_col:w_col + width])
        return run
    z_chunks = [proj_chunk(pend_z, c, c, PROJ_CHUNK) for c in range(0, D_INNER, PROJ_CHUNK)]
    scan_chunks = [proj_chunk(pend_xbc, c, D_INNER + c, PROJ_CHUNK)
                   for c in range(0, CONV_DIM, PROJ_CHUNK)]
    scan_chunks.append(proj_chunk(pend_dt, 0, D_INNER + CONV_DIM, DT_PAD))

    @pl.when(first_chunk)
    def _():
        sout_ref[...] = sin_ref[...]
        prev_scr[...] = jnp.zeros_like(prev_scr)
        for s in range(nseq):
            for k in range(CONV_W - 1):
                prev_scr[pl.ds(s * 8 + 5 + k, 1), :] = cin_ref[s, pl.ds(k, 1), :]

    sub8 = lax.broadcasted_iota(I32, (8, 1), 0)
    for s in range(nseq):
        srows = slice(s * q, (s + 1) * q)
        for ct in col_tiles(CONV_DIM, 4 * q):
            xq = pend_xbc[srows, ct]
            hist = prev_scr[s * 8:(s + 1) * 8, ct]
            acc = cb_ref[:, ct] + cw_ref[pl.ds(CONV_W - 1, 1), ct] * xq
            for j in range(1, CONV_W):
                sh = pltpu.roll(xq, j, 0)
                head = jnp.where(sub8 < j, pltpu.roll(hist, j, 0), sh[0:8, :])
                sh = head if q == 8 else jnp.concatenate([head, sh[8:, :]], axis=0)
                acc = acc + cw_ref[pl.ds(CONV_W - 1 - j, 1), ct] * sh
            xc_scr[srows, ct] = _silu(acc)
            last = xq[q - 8:q, :]
            prev_scr[s * 8:(s + 1) * 8, ct] = last
            for k in range(CONV_W - 1):
                cout_ref[s, pl.ds(k, 1), ct] = last[5 + k:6 + k, :]
    chunks = z_chunks + scan_chunks
    n_pairs = nsub * (N_HEADS // 2)
    emit_at = {}
    for k, ch in enumerate(chunks):
        emit_at.setdefault((k * n_pairs) // len(chunks), []).append(ch)

    tri = tri_ref[...]
    trib = tri.astype(BF16)
    onesb = ones_ref[...].astype(BF16)
    mask = tri > 0.5
    rowseq = lax.shift_right_logical(lax.broadcasted_iota(I32, (sb, 1), 0), int(math.log2(q)))
    lo = lax.broadcasted_iota(I32, (sb, 128), 1) < HEAD_DIM
    neg_a = -jnp.exp(alog_ref[...])

    for u in range(nsub):
        rows = slice(u * sb, (u + 1) * sb)
        xs_ref = xc_scr.at[rows, 0:D_INNER]

        def b_of(g):
            return xc_scr[rows, D_INNER + g * D_STATE:D_INNER + (g + 1) * D_STATE]

        def c_of(g):
            c0 = D_INNER + (N_GROUPS + g) * D_STATE
            return xc_scr[rows, c0:c0 + D_STATE].astype(BF16)

        dt = _softplus(pend_dt[rows, :] + dtb_ref[...])
        d1, d2, d3 = _split3(dt * neg_a)
        cs = _dot(trib, d1) + _dot(trib, d2) + _dot(trib, d3)
        cl = _dot(onesb, d1) + _dot(onesb, d2) + _dot(onesb, d3)
        ecs = jnp.exp(cs)
        wgt = dt * jnp.exp(cl - cs)
        ecl = jnp.exp(cl)
        cs_t = cs.T
        dt_t = dt.T

        for g in range(N_GROUPS):
            cg = c_of(g)
            acc = None
            for s in range(ns):
                st = sout_ref[u * ns + s, g * GROUP_W:(g + 1) * GROUP_W, :].astype(BF16)
                yo = _dot_nt(cg, st)
                if ns > 1:
                    yo = jnp.where(rowseq == s, yo, 0.0)
                acc = yo if acc is None else acc + yo
            yoff_scr[rows, g * GROUP_W:(g + 1) * GROUP_W] = acc

        for g in range(N_GROUPS):
            sc = _dot_nt(c_of(g), b_of(g).astype(BF16))
            for jj in range(N_HEADS // N_GROUPS // 2):
                j = g * (N_HEADS // N_GROUPS // 2) + jj
                ms = []
                for h in (2 * j, 2 * j + 1):
                    diff = cs[:, h:h + 1] - cs_t[h:h + 1, :]
                    dec = jnp.exp(jnp.where(mask, diff, -jnp.inf))
                    ms.append((sc * dec * dt_t[h:h + 1, :]).astype(BF16))
                lhs = jnp.concatenate(ms, axis=1)
                cols = slice(j * 128, (j + 1) * 128)
                xp = xs_ref[:, cols]
                rhs = jnp.concatenate([jnp.where(lo, xp, 0.0).astype(BF16),
                                       jnp.where(lo, 0.0, xp).astype(BF16)], axis=0)
                yd = _dot(lhs, rhs)
                ecs_p = jnp.where(lo, ecs[:, 2 * j:2 * j + 1], ecs[:, 2 * j + 1:2 * j + 2])
                wgt_p = jnp.where(lo, wgt[:, 2 * j:2 * j + 1], wgt[:, 2 * j + 1:2 * j + 2])
                y_scr[rows, cols] = yd + yoff_scr[rows, cols] * ecs_p + xp * dsk_ref[:, cols]
                xw_scr[rows, cols] = (xp * wgt_p).astype(BF16)
                pair = u * (N_HEADS // 2) + j
                for ch in emit_at.get(pair, []):
                    ch()

        for g in range(N_GROUPS):
            xwg = xw_scr[rows, g * GROUP_W:(g + 1) * GROUP_W]
            for s in range(ns):
                bg = b_of(g)
                if ns > 1:
                    bg = jnp.where(rowseq == s, bg, 0.0)
                upd = _dot_tn(xwg, bg.astype(BF16))
                for hh in range(GROUP_W // HEAD_DIM):
                    h = g * (GROUP_W // HEAD_DIM) + hh
                    dec = jnp.broadcast_to(ecl[s * q:s * q + 1, h:h + 1], (HEAD_DIM, D_STATE))
                    hrows = slice(h * HEAD_DIM, (h + 1) * HEAD_DIM)
                    sout_ref[u * ns + s, hrows, :] = (
                        sout_ref[u * ns + s, hrows, :] * dec + upd[hh * HEAD_DIM:(hh + 1) * HEAD_DIM, :])

    for rt in row_tiles:
        for g in range(N_GROUPS):
            cts = [slice(g * GROUP_W + c.start, g * GROUP_W + c.stop) for c in col_tiles(GROUP_W)]
            ss = None
            for ct in cts:
                yz = y_scr[rt, ct] * zg_scr[rt, ct]
                part = jnp.sum(yz * yz, axis=-1, keepdims=True)
                ss = part if ss is None else ss + part
            scale = lax.rsqrt(ss * (1.0 / GROUP_W) + NORM_EPS)
            for ct in cts:
                yn_scr[rt, ct] = (y_scr[rt, ct] * zg_scr[rt, ct] * scale * ng_ref[:, ct]).astype(BF16)

    for c0 in range(0, D_MODEL, PROJ_CHUNK):
        ct = slice(c0, c0 + PROJ_CHUNK)
        o = xres_scr[:, ct] + _dot(yn_scr[...], wout_ref[:, ct])
        for s in range(nseq):
            xo_ref[s, :, ct] = o[s * q:(s + 1) * q, :]


def _ssd_layer(x, conv_prev, ssm_prev, norm_g, w_in, conv_w, conv_b, dt_bias, a_log, d_skip,
               norm_y, w_out, *, nsub, ns, q):
    n_seq, seq_len, _ = x.shape
    nseq = nsub * ns
    sb = ns * q
    r = nsub * sb
    n_chunks = seq_len // q
    pad = DT_PAD - N_HEADS
    win = jnp.concatenate([w_in, jnp.zeros((D_MODEL, pad), F32)], axis=1).astype(BF16)
    dtb = jnp.pad(dt_bias, (0, pad)).reshape(1, DT_PAD)
    alog = jnp.pad(a_log, (0, pad)).reshape(1, DT_PAD)
    dsk = jnp.repeat(d_skip, HEAD_DIM).reshape(1, D_INNER)
    blk = np.kron(np.eye(ns), np.ones((q, q)))
    tri = jnp.asarray(blk * np.tril(np.ones((sb, sb))), F32)
    ones = jnp.asarray(blk, F32)
    state = ssm_prev.reshape(n_seq, D_INNER, D_STATE)
    conv_prev = conv_prev.reshape(n_seq, CONV_W - 1, CONV_DIM)

    kern = functools.partial(_ssd_kernel, nsub=nsub, ns=ns, q=q, n_chunks=n_chunks)
    out_shape = (jax.ShapeDtypeStruct(x.shape, F32),
                 jax.ShapeDtypeStruct((n_seq, CONV_W - 1, CONV_DIM), F32),
                 jax.ShapeDtypeStruct((n_seq, D_INNER, D_STATE), F32))
    n_steps = (n_seq // nseq) * n_chunks
    nxt = lambda i: jnp.minimum(i, n_steps - 1)
    cur = lambda i: jnp.maximum(i - 1, 0)
    in_row_spec = pl.BlockSpec((nseq, q, D_MODEL),
                               lambda i: (nxt(i) // n_chunks, nxt(i) % n_chunks, 0))
    row_spec = pl.BlockSpec((nseq, q, D_MODEL),
                            lambda i: (cur(i) // n_chunks, cur(i) % n_chunks, 0))
    conv_spec = pl.BlockSpec((nseq, CONV_W - 1, CONV_DIM), lambda i: (cur(i) // n_chunks, 0, 0))
    state_spec = pl.BlockSpec((nseq, D_INNER, D_STATE), lambda i: (cur(i) // n_chunks, 0, 0))
    in_specs = [in_row_spec,
                _const_spec((1, D_MODEL)),
                _const_spec(win.shape),
                _const_spec((CONV_W, CONV_DIM)),
                _const_spec((1, CONV_DIM)),
                _const_spec((1, DT_PAD)),
                _const_spec((1, DT_PAD)),
                _const_spec((1, D_INNER)),
                _const_spec((1, D_INNER)),
                _const_spec((D_INNER, D_MODEL)),
                _const_spec((sb, sb)),
                _const_spec((sb, sb)),
                conv_spec, state_spec]
    x_new, conv_new, state_new = pl.pallas_call(
        kern,
        grid=(n_steps + 1,),
        in_specs=in_specs,
        out_specs=(row_spec, conv_spec, state_spec),
        out_shape=out_shape,
        scratch_shapes=[pltpu.VMEM((r, D_MODEL), F32),
                        pltpu.VMEM((r, D_INNER), F32),
                        pltpu.VMEM((r, CONV_DIM), F32),
                        pltpu.VMEM((r, DT_PAD), F32),
                        pltpu.VMEM((r, D_MODEL), F32),
                        pltpu.VMEM((r, D_INNER), F32),
                        pltpu.VMEM((nseq * 8, CONV_DIM), F32),
                        pltpu.VMEM((r, CONV_DIM), F32),
                        pltpu.VMEM((r, D_INNER), F32),
                        pltpu.VMEM((r, D_INNER), F32),
                        pltpu.VMEM((r, D_INNER), BF16),
                        pltpu.VMEM((r, D_MODEL), BF16),
                        pltpu.VMEM((r, D_INNER), BF16)],
        compiler_params=pltpu.CompilerParams(
            dimension_semantics=("arbitrary",), vmem_limit_bytes=VMEM_LIMIT),
        name="ssd_layer",
    )(x, norm_g.reshape(1, D_MODEL), win, conv_w, conv_b.reshape(1, CONV_DIM), dtb, alog, dsk,
      norm_y.reshape(1, D_INNER), w_out.astype(BF16), tri, ones, conv_prev, state)
    return (x_new, conv_new.reshape(1, n_seq, CONV_W - 1, CONV_DIM),
            state_new.reshape(1, n_seq, N_HEADS, HEAD_DIM, D_STATE))


def _sg_kernel(x_ref, g_ref, win_ref, bin_ref, lng_ref, lnb_ref, wmix_ref, bmix_ref, wout_ref,
               xo_ref, *rest, r, want_v):
    v_ref = rest[0] if want_v else None
    hn_scr, uv_scr, vb_scr, um_scr = rest[-4:]
    row_tiles = [slice(a, a + 128) for a in range(0, r, 128)]
    col128 = lambda total: [slice(c, c + 128) for c in range(0, total, 128)]

    for rt in row_tiles:
        ss = None
        for ct in col128(D_MODEL):
            xv = x_ref[rt, ct]
            part = jnp.sum(xv * xv, axis=-1, keepdims=True)
            ss = part if ss is None else ss + part
        scale = lax.rsqrt(ss * (1.0 / D_MODEL) + NORM_EPS)
        for ct in col128(D_MODEL):
            hn_scr[rt, ct] = (x_ref[rt, ct] * scale * g_ref[:, ct]).astype(BF16)

    vcols = [slice(SG_WIDTH + c.start, SG_WIDTH + c.stop) for c in col128(SG_WIDTH)]

    def front(rt):
        def piece(c0):
            def run():
                h = _dot(hn_scr[rt, :], win_ref[:, c0:c0 + PROJ_CHUNK])
                for cc in range(0, PROJ_CHUNK, 128):
                    ct = slice(c0 + cc, c0 + cc + 128)
                    uv_scr[rt, ct] = _gelu_tanh(h[:, cc:cc + 128] + bin_ref[:, ct])
            return run
        return [piece(c0) for c0 in range(0, 2 * SG_WIDTH, PROJ_CHUNK)]

    def back(rt):
        stats = {}

        def ln_stats():
            tot = None
            for ct in vcols:
                part = jnp.sum(uv_scr[rt, ct], axis=-1, keepdims=True)
                tot = part if tot is None else tot + part
            mu = tot * (1.0 / SG_WIDTH)
            ss = None
            for ct in vcols:
                vc = uv_scr[rt, ct] - mu
                part = jnp.sum(vc * vc, axis=-1, keepdims=True)
                ss = part if ss is None else ss + part
            stats["mu"] = mu
            stats["scale"] = lax.rsqrt(ss * (1.0 / SG_WIDTH) + LN_EPS)

        def ln_apply(k0):
            def run():
                for ct, c in list(zip(vcols, col128(SG_WIDTH)))[k0:k0 + 4]:
                    vn = (uv_scr[rt, ct] - stats["mu"]) * stats["scale"] * lng_ref[:, c] + lnb_ref[:, c]
                    if want_v:
                        v_ref[rt, c] = vn
                    vb_scr[rt, c] = vn.astype(BF16)
            return run

        def mix(g):
            def run():
                cols = slice(g * SG_GROUP_DIM, (g + 1) * SG_GROUP_DIM)
                mixed = _dot(wmix_ref[g], vb_scr[rt, cols]) + bmix_ref[:, cols]
                um_scr[rt, cols] = (uv_scr[rt, cols] * mixed).astype(BF16)
            return run

        def out(c0):
            def run():
                ct = slice(c0, c0 + PROJ_CHUNK)
                xo_ref[rt, ct] = x_ref[rt, ct] + _dot(um_scr[rt, :], wout_ref[:, ct])
            return run

        return ([ln_stats] + [ln_apply(k) for k in range(0, len(vcols), 4)]
                + [mix(g) for g in range(SG_GROUPS)]
                + [out(c0) for c0 in range(0, D_MODEL, PROJ_CHUNK)])

    pending = []
    for rt in row_tiles:
        for piece in front(rt):
            piece()
            if pending:
                pending.pop(0)()
        for piece in pending:
            piece()
        pending = back(rt)
    for piece in pending:
        piece()


def _sg_layer(x2d, norm_g, w_in, b_in, ln_g, ln_b, w_s, b_s, w_out, *, seq_len, want_v):
    r = SG_ROWS
    t = x2d.shape[0]
    q = min(seq_len, SG_CHUNK)
    reps = SG_CHUNK // q
    ws = jnp.tril(w_s)[:, :q, :q]
    wmix = jnp.einsum("ab,gts->gatbs", jnp.eye(reps, dtype=F32), ws)
    wmix = wmix.reshape(SG_GROUPS, SG_CHUNK, SG_CHUNK)
    bmix = jnp.tile(jnp.repeat(b_s.T[:q], SG_GROUP_DIM, axis=1), (reps, 1))
    row_spec = pl.BlockSpec((r, D_MODEL), lambda i: (i, 0))
    v_spec = pl.BlockSpec((r, SG_WIDTH), lambda i: (i, 0))
    out_shape = [jax.ShapeDtypeStruct(x2d.shape, F32)]
    out_specs = [row_spec]
    if want_v:
        out_shape.append(jax.ShapeDtypeStruct((t, SG_WIDTH), F32))
        out_specs.append(v_spec)
    outs = pl.pallas_call(
        functools.partial(_sg_kernel, r=r, want_v=want_v),
        grid=(t // r,),
        scratch_shapes=[pltpu.VMEM((r, D_MODEL), BF16),
                        pltpu.VMEM((r, 2 * SG_WIDTH), F32),
                        pltpu.VMEM((r, SG_WIDTH), BF16),
                        pltpu.VMEM((r, SG_WIDTH), BF16)],
        in_specs=[row_spec,
                  _const_spec((1, D_MODEL)),
                  _const_spec((D_MODEL, 2 * SG_WIDTH)),
                  _const_spec((1, 2 * SG_WIDTH)),
                  _const_spec((1, SG_WIDTH)),
                  _const_spec((1, SG_WIDTH)),
                  _const_spec((SG_GROUPS, SG_CHUNK, SG_CHUNK)),
                  _const_spec((SG_CHUNK, SG_WIDTH)),
                  _const_spec((SG_WIDTH, D_MODEL))],
        out_specs=out_specs,
        out_shape=out_shape,
        compiler_params=pltpu.CompilerParams(
            dimension_semantics=("arbitrary",), vmem_limit_bytes=VMEM_LIMIT),
        name="sg_layer",
    )(x2d, norm_g.reshape(1, D_MODEL), w_in.astype(BF16), b_in.reshape(1, 2 * SG_WIDTH),
      ln_g.reshape(1, SG_WIDTH), ln_b.reshape(1, SG_WIDTH), wmix.astype(BF16), bmix,
      w_out.astype(BF16))
    return outs if want_v else (outs[0], None)


ROUTER_ROWS = 64


def _norm_tiles(use_a, xa_ref, xb_ref, g_ref, store):
    def load(rt, ct):
        return jnp.where(use_a, xa_ref[rt, ct], xb_ref[rt, ct])

    for r0 in range(0, xa_ref.shape[0], 128):
        rt = slice(r0, r0 + 128)
        ss = None
        for c0 in range(0, D_MODEL, 256):
            xv = load(rt, slice(c0, c0 + 256))
            part = jnp.sum(xv * xv, axis=-1, keepdims=True)
            ss = part if ss is None else ss + part
        scale = lax.rsqrt(ss * (1.0 / D_MODEL) + NORM_EPS)
        for c0 in range(0, D_MODEL, 256):
            ct = slice(c0, c0 + 256)
            store(rt, ct, load(rt, ct) * scale * g_ref[:, ct])


def _router_t_kernel(xa_ref, xb_ref, g_ref, wt_ref, bt_ref, usl_ref, lsl_ref,
                     meta_ref, metat_ref, pad_ref, h1_scr, h2_scr, *, steps_a):
    wt = wt_ref[...]
    w1 = wt.astype(BF16)
    w2 = (wt - w1.astype(F32)).astype(BF16)
    tt = MOE_TILE
    n = xa_ref.shape[0]

    def split_store(rt, ct, hn):
        hi = hn.astype(BF16)
        h1_scr[rt, ct] = hi
        h2_scr[rt, ct] = (hn - hi.astype(F32)).astype(BF16)

    _norm_tiles(pl.program_id(0) < steps_a, xa_ref, xb_ref, g_ref, split_store)

    h1 = h1_scr[...]
    logits = _dot_nt(w1, h1) + _dot_nt(w2, h1) + _dot_nt(w1, h2_scr[...])
    select = _select_experts(logits, bt_ref[...])
    for k in range(n // tt):
        cols = slice(k * tt, (k + 1) * tt)
        metat, pad = _local_order(select[:, cols], usl_ref[...], lsl_ref[...])
        metat_ref[k * 8:(k + 1) * 8, :] = metat
        full = jnp.concatenate([metat, jnp.zeros((128 - 8, tt), F32)], axis=0)
        meta_ref[cols, :] = full.T
        pad_ref[k * MOE_EXPERTS:(k + 1) * MOE_EXPERTS, :] = pad


def _select_experts(logits, bt):
    n = logits.shape[1]
    reps = n // bt.shape[1]
    logits = logits + (bt if reps == 1 else jnp.concatenate([bt] * reps, axis=1))
    tt = n
    row8 = lax.broadcasted_iota(I32, (8, tt), 0).astype(F32)
    gl = jnp.where(row8 < MOE_GROUPS, logits[0:8, :], -jnp.inf)
    gmax = jnp.max(gl, axis=0, keepdims=True)
    g_top = jnp.min(jnp.where(gl == gmax, row8, 8.0), axis=0, keepdims=True)
    p_g = 1.0 / jnp.sum(jnp.exp(gl - gmax), axis=0, keepdims=True)

    el = logits[8:16, :]
    for grp in range(1, MOE_GROUPS):
        el = jnp.where(g_top == grp, logits[8 + 8 * grp:16 + 8 * grp, :], el)
    emax = jnp.max(el, axis=0, keepdims=True)
    ee = jnp.exp(el - emax)
    prob = ee / jnp.sum(ee, axis=0, keepdims=True)
    p1 = jnp.max(prob, axis=0, keepdims=True)
    i1 = jnp.min(jnp.where(prob == p1, row8, 8.0), axis=0, keepdims=True)
    prob2 = jnp.where(row8 == i1, -1.0, prob)
    p2 = jnp.max(prob2, axis=0, keepdims=True)
    i2 = jnp.min(jnp.where(prob2 == p2, row8, 8.0), axis=0, keepdims=True)
    psum = p1 + p2
    gate1 = p_g * (p1 / psum)
    gate2 = p_g * (p2 / psum)
    e1 = g_top * MOE_EPG + i1
    e2 = g_top * MOE_EPG + i2
    out = jnp.where(row8 == 0, e1, 0.0)
    out = jnp.where(row8 == 1, e2, out)
    out = jnp.where(row8 == 2, gate1, out)
    return jnp.where(row8 == 3, gate2, out)


def _local_order(select, usl, lsl):
    tt = select.shape[1]
    row8 = lax.broadcasted_iota(I32, (8, tt), 0).astype(F32)
    e1, e2 = select[0:1, :], select[1:2, :]
    rowe = lax.broadcasted_iota(I32, (MOE_EXPERTS, tt), 0).astype(F32)
    sel1 = rowe == e1
    sel2 = rowe == e2
    onehot = jnp.where(sel1 | sel2, 1.0, 0.0)
    before = _dot(onehot.astype(BF16), usl)
    cnt = jnp.sum(onehot, axis=1, keepdims=True)
    pad = jnp.floor((cnt + (GRANULE - 1)) * (1.0 / GRANULE)) * GRANULE
    padb = jnp.broadcast_to(pad, (MOE_EXPERTS, tt))
    local = before + _dot(lsl, padb.astype(BF16))
    slot1 = jnp.sum(jnp.where(sel1, local, 0.0), axis=0, keepdims=True)
    slot2 = jnp.sum(jnp.where(sel2, local, 0.0), axis=0, keepdims=True)

    metat = jnp.where(row8 == 4, slot1, select)
    metat = jnp.where(row8 == 5, slot2, metat)
    return metat, padb[:, 0:128]


def _granule(ref, g):
    return ref.at[pl.ds(pl.multiple_of(g * GRANULE, GRANULE), GRANULE), :]


def _dispatch_kernel(ngran_ref, nv_ref, dest_ref, gap_ref, xa_ref, xb_ref, g_ref, meta_ref, xs_hbm,
                     buf, zbuf, hn_scr, sem, zsem, nstart, *, tiles_a):
    i = pl.program_id(0)
    n_tiles = pl.num_programs(0) - 1
    slot = i % 2
    tt = xa_ref.shape[0]
    bm = zbuf.shape[0]
    n_blocks = xs_hbm.shape[0] // bm

    def out_copy(sl, g, d):
        return pltpu.make_async_copy(_granule(buf.at[sl], g), _granule(xs_hbm, d), sem.at[sl])

    def drain(sl):
        def body(_, carry):
            out_copy(sl, 0, 0).wait()
            return carry
        lax.fori_loop(0, nstart[sl], body, 0)

    @pl.when(i == 0)
    def _():
        nstart[0] = 0
        nstart[1] = 0

    def norm_store(rt, ct, hn):
        hn_scr[rt, ct] = hn.astype(BF16)

    @pl.when(i < n_tiles)
    def _():
        _norm_tiles(i < tiles_a, xa_ref, xb_ref, g_ref, norm_store)
        drain(slot)
        hn = hn_scr[...]
        mt = meta_ref[...]
        for r0 in range(0, LOCAL_ROWS, 256):
            rows = (lax.broadcasted_iota(I32, (256, tt), 0) + r0).astype(F32)
            onehot = jnp.where((rows == mt[4:5, :]) | (rows == mt[5:6, :]), 1.0, 0.0).astype(BF16)
            buf[slot, r0:r0 + 256, :] = _dot(onehot, hn).astype(BF16)
        ng = ngran_ref[i]

        def body(g, carry):
            out_copy(slot, g, dest_ref[0, 0, g]).start()
            return carry
        lax.fori_loop(0, ng, body, 0)
        nstart[slot] = ng

    @pl.when(i == n_tiles)
    def _():
        drain(0)
        drain(1)
        zbuf[...] = jnp.zeros_like(zbuf)

        def gap_copy(d):
            return pltpu.make_async_copy(_granule(zbuf, 0), _granule(xs_hbm, d), zsem.at[0])

        def tail_copy(b):
            return pltpu.make_async_copy(
                zbuf, xs_hbm.at[pl.ds(pl.multiple_of(b * bm, bm), bm), :], zsem.at[1])

        def each_gap(fn):
            def body(j, carry):
                d = gap_ref[0, 0, j]

                @pl.when(d >= 0)
                def _():
                    fn(gap_copy(d))
                return carry
            lax.fori_loop(0, gap_ref.shape[2], body, 0)

        def each_tail(fn):
            def body(b, carry):
                fn(tail_copy(b))
                return carry
            lax.fori_loop(nv_ref[0], n_blocks, body, 0)

        each_gap(lambda cp: cp.start())
        each_tail(lambda cp: cp.start())
        each_gap(lambda cp: cp.wait())
        each_tail(lambda cp: cp.wait())


def _expert_kernel(bexp_ref, nv_ref, xs_ref, wg_ref, wu_ref, wd_ref, ys_ref,
                   wgu_b, wd_b, act_scr):
    b = pl.program_id(0)
    ff = wg_ref.shape[1]

    @pl.when(b < nv_ref[0])
    def _():
        @pl.when((b == 0) | (bexp_ref[b] != bexp_ref[jnp.maximum(b - 1, 0)]))
        def _():
            for k0 in range(0, wg_ref.shape[0], 256):
                wgu_b[k0:k0 + 256, 0:ff] = wg_ref[k0:k0 + 256, :].astype(BF16)
                wgu_b[k0:k0 + 256, ff:2 * ff] = wu_ref[k0:k0 + 256, :].astype(BF16)
            for k0 in range(0, ff, 64):
                wd_b[k0:k0 + 64, :] = wd_ref[k0:k0 + 64, :].astype(BF16)

        bm = xs_ref.shape[0]
        for m0 in range(0, bm, 256):
            h = _dot(xs_ref[m0:m0 + 256, :], wgu_b[...])
            for r0 in range(0, 256, 128):
                for c0 in range(0, ff, 128):
                    hg = h[r0:r0 + 128, c0:c0 + 128]
                    hu = h[r0:r0 + 128, ff + c0:ff + c0 + 128]
                    act_scr[m0 + r0:m0 + r0 + 128, c0:c0 + 128] = (_silu(hg) * hu).astype(BF16)
            ys_ref[m0:m0 + 256, :] = _dot(act_scr[m0:m0 + 256, :], wd_b[...]).astype(BF16)


def _combine_kernel(ngran_ref, src_ref, srcn_ref, xa_ref, xb_ref, meta_ref, gf_ref, ys_hbm,
                    oa_ref, ob_ref, buf, moe_scr, sem, *, final, tiles_a):
    i = pl.program_id(0)
    n = pl.num_programs(0)
    slot = i % 2
    tt = xa_ref.shape[0]

    def in_copy(sl, g, d):
        return pltpu.make_async_copy(_granule(ys_hbm, d), _granule(buf.at[sl], g), sem.at[sl])

    def gather(idx_ref, sl, ng):
        def body(g, carry):
            in_copy(sl, g, idx_ref[0, 0, g]).start()
            return carry
        lax.fori_loop(0, ng, body, 0)

    @pl.when(i == 0)
    def _():
        buf[...] = jnp.zeros_like(buf)
        gather(src_ref, 0, ngran_ref[0])

    @pl.when(i + 1 < n)
    def _():
        gather(srcn_ref, 1 - slot, ngran_ref[i + 1])

    def wait_body(_, carry):
        in_copy(slot, 0, 0).wait()
        return carry
    lax.fori_loop(0, ngran_ref[i], wait_body, 0)

    ys = buf[slot]
    meta = meta_ref[...]
    cols = lax.broadcasted_iota(I32, (tt, LOCAL_ROWS), 1).astype(F32)
    pick = (jnp.where(cols == meta[:, 4:5], meta[:, 2:3], 0.0)
            + jnp.where(cols == meta[:, 5:6], meta[:, 3:4], 0.0)).astype(BF16)
    moe = _dot(pick, ys)
    is_a = i < tiles_a
    for r0 in range(0, tt, 128):
        rt = slice(r0, r0 + 128)
        for c0 in range(0, D_MODEL, 256):
            ct = slice(c0, c0 + 256)
            moe_scr[rt, ct] = jnp.where(is_a, xa_ref[rt, ct], xb_ref[rt, ct]) + moe[rt, ct]
        if final:
            ss = None
            for c0 in range(0, D_MODEL, 256):
                yv = moe_scr[rt, c0:c0 + 256]
                part = jnp.sum(yv * yv, axis=-1, keepdims=True)
                ss = part if ss is None else ss + part
            scale = lax.rsqrt(ss * (1.0 / D_MODEL) + NORM_EPS)
            for c0 in range(0, D_MODEL, 256):
                ct = slice(c0, c0 + 256)
                moe_scr[rt, ct] = moe_scr[rt, ct] * scale * gf_ref[:, ct]

    def emit(o_ref):
        for r0 in range(0, tt, 128):
            for c0 in range(0, D_MODEL, 256):
                o_ref[r0:r0 + 128, c0:c0 + 256] = moe_scr[r0:r0 + 128, c0:c0 + 256]

    @pl.when(is_a)
    def _():
        emit(oa_ref)

    @pl.when(jnp.logical_not(is_a))
    def _():
        emit(ob_ref)


def _moe(xa, xb, norm_g, w_grp, b_grp, w_rt, b_rt, w_gate, w_up, w_down, norm_final, *, layer):
    tt = MOE_TILE
    tiles_a, tiles_b = xa.shape[0] // tt, xb.shape[0] // tt
    steps_a, steps_b = tiles_a // ROUTER_TILES, tiles_b // ROUTER_TILES
    n_tiles = tiles_a + tiles_b
    t = n_tiles * tt
    bm = _expert_block(t)
    n_exp = MOE_EXPERTS
    a_blk = lambda i, n_a: jnp.minimum(i, n_a - 1)
    b_blk = lambda i, n_a, n_b: jnp.clip(i - n_a, 0, n_b - 1)
    zrow = lambda n: jnp.zeros((n, D_MODEL), F32)
    wt = jnp.concatenate([w_grp.T, zrow(8 - MOE_GROUPS), w_rt.T, zrow(ROUTER_ROWS - 8 - n_exp)], axis=0)
    bt = jnp.concatenate([b_grp, jnp.zeros((8 - MOE_GROUPS,), F32), b_rt,
                          jnp.zeros((ROUTER_ROWS - 8 - n_exp,), F32)])
    bt = jnp.broadcast_to(bt[:, None], (ROUTER_ROWS, tt))
    usl = jnp.asarray(np.triu(np.ones((tt, tt)), 1), BF16)
    lsl = jnp.asarray(np.tril(np.ones((n_exp, n_exp)), -1), BF16)
    g2d = norm_g.reshape(1, D_MODEL)
    arb = pltpu.CompilerParams(dimension_semantics=("arbitrary",), vmem_limit_bytes=VMEM_LIMIT)

    meta, metat, pad = pl.pallas_call(
        functools.partial(_router_t_kernel, steps_a=steps_a),
        grid=(steps_a + steps_b,),
        in_specs=[pl.BlockSpec((ROUTER_TILES * tt, D_MODEL), lambda i: (a_blk(i, steps_a), 0)),
                  pl.BlockSpec((ROUTER_TILES * tt, D_MODEL),
                               lambda i: (b_blk(i, steps_a, steps_b), 0)),
                  _const_spec((1, D_MODEL)),
                  _const_spec((ROUTER_ROWS, D_MODEL)),
                  _const_spec((ROUTER_ROWS, tt)),
                  _const_spec((tt, tt)),
                  _const_spec((n_exp, n_exp))],
        out_specs=(pl.BlockSpec((ROUTER_TILES * tt, 128), lambda i: (i, 0)),
                   pl.BlockSpec((ROUTER_TILES * 8, tt), lambda i: (i, 0)),
                   pl.BlockSpec((ROUTER_TILES * n_exp, 128), lambda i: (i, 0))),
        out_shape=(jax.ShapeDtypeStruct((t, 128), F32),
                   jax.ShapeDtypeStruct((n_tiles * 8, tt), F32),
                   jax.ShapeDtypeStruct((n_tiles * n_exp, 128), F32)),
        scratch_shapes=[pltpu.VMEM((ROUTER_TILES * tt, D_MODEL), BF16),
                        pltpu.VMEM((ROUTER_TILES * tt, D_MODEL), BF16)],
        compiler_params=arb,
        name="moe_router",
    )(xa, xb, g2d, wt, bt, usl, lsl)

    runs = pad.reshape(n_tiles, n_exp, 128)[:, :, 0].astype(I32)
    rows_e = jnp.sum(runs, axis=0)
    nblk = (rows_e + bm - 1) // bm
    blk_end = jnp.cumsum(nblk)
    e_start = (blk_end - nblk) * bm
    n_valid = blk_end[-1]
    run_end = jnp.cumsum(runs, axis=1)
    ngran = (run_end[:, -1] // GRANULE).astype(I32)
    shift = e_start[None, :] + (jnp.cumsum(runs, axis=0) - runs) - (run_end - runs)
    g_row = jnp.arange(N_LOCAL_GRAN, dtype=I32) * GRANULE
    e_of_g = jnp.sum((run_end[:, None, :] <= g_row[None, :, None]).astype(I32), axis=-1)
    shift_g = jnp.sum(jnp.where(e_of_g[..., None] == jnp.arange(n_exp, dtype=I32),
                                shift[:, None, :], 0), axis=-1)
    dest = jnp.where(e_of_g < n_exp, (shift_g + g_row[None, :]) // GRANULE, 0)
    dest = dest.astype(I32).reshape(n_tiles, 1, N_LOCAL_GRAN)
    per_blk = bm // GRANULE
    gap = ((e_start + rows_e) // GRANULE)[:, None] + jnp.arange(per_blk, dtype=I32)[None, :]
    gap = jnp.where(gap < ((e_start + nblk * bm) // GRANULE)[:, None], gap, -1)
    gap = gap.astype(I32).reshape(1, 1, n_exp * per_blk)
    n_blocks = (2 * t + n_tiles * n_exp * (GRANULE - 1)) // bm + 1 + n_exp
    blk = jnp.minimum(jnp.arange(n_blocks, dtype=I32), n_valid - 1)
    block_expert = jnp.sum((blk[:, None] >= blk_end[None, :]).astype(I32), axis=1).astype(I32)
    n_valid = n_valid.reshape(1).astype(I32)

    last = n_tiles - 1
    xs = pl.pallas_call(
        functools.partial(_dispatch_kernel, tiles_a=tiles_a),
        grid_spec=pltpu.PrefetchScalarGridSpec(
            num_scalar_prefetch=2,
            grid=(n_tiles + 1,),
            in_specs=[pl.BlockSpec((1, 1, N_LOCAL_GRAN),
                                   lambda i, ng, nv: (jnp.minimum(i, last), 0, 0),
                                   memory_space=pltpu.SMEM),
                      pl.BlockSpec((1, 1, n_exp * per_blk), lambda i, ng, nv: (0, 0, 0),
                                   memory_space=pltpu.SMEM),
                      pl.BlockSpec((tt, D_MODEL), lambda i, ng, nv: (a_blk(i, tiles_a), 0)),
                      pl.BlockSpec((tt, D_MODEL),
                                   lambda i, ng, nv: (b_blk(i, tiles_a, tiles_b), 0)),
                      _const_spec((1, D_MODEL)),
                      pl.BlockSpec((8, tt), lambda i, ng, nv: (jnp.minimum(i, last), 0))],
            out_specs=pl.BlockSpec(memory_space=pl.ANY),
            scratch_shapes=[pltpu.VMEM((2, LOCAL_ROWS, D_MODEL), BF16),
                            pltpu.VMEM((bm, D_MODEL), BF16),
                            pltpu.VMEM((tt, D_MODEL), BF16),
                            pltpu.SemaphoreType.DMA((2,)),
                            pltpu.SemaphoreType.DMA((2,)),
                            pltpu.SMEM((2,), I32)]),
        out_shape=jax.ShapeDtypeStruct((n_blocks * bm, D_MODEL), BF16),
        compiler_params=arb,
        name="moe_dispatch",
    )(ngran, n_valid, dest, gap, xa, xb, g2d, metat)

    w_spec = lambda shape: pl.BlockSpec((None, None) + shape,
                                        lambda b, be, nv: (layer, be[b], 0, 0))
    row_blk = pl.BlockSpec((bm, D_MODEL), lambda b, be, nv: (jnp.minimum(b, nv[0] - 1), 0))
    ys = pl.pallas_call(
        _expert_kernel,
        grid_spec=pltpu.PrefetchScalarGridSpec(
            num_scalar_prefetch=2,
            grid=(n_blocks,),
            in_specs=[row_blk,
                      w_spec((D_MODEL, MOE_D_FF)),
                      w_spec((D_MODEL, MOE_D_FF)),
                      w_spec((MOE_D_FF, D_MODEL))],
            out_specs=row_blk,
            scratch_shapes=[pltpu.VMEM((D_MODEL, 2 * MOE_D_FF), BF16),
                            pltpu.VMEM((MOE_D_FF, D_MODEL), BF16),
                            pltpu.VMEM((bm, MOE_D_FF), BF16)]),
        out_shape=jax.ShapeDtypeStruct((n_blocks * bm, D_MODEL), BF16),
        input_output_aliases={2: 0},
        compiler_params=arb,
        name="moe_experts",
    )(block_expert, n_valid, xs, w_gate, w_up, w_down)

    final = norm_final is not None
    gf = (norm_final if final else jnp.ones((D_MODEL,), F32)).reshape(1, D_MODEL)
    src_spec = lambda nxt: pl.BlockSpec(
        (1, 1, N_LOCAL_GRAN), lambda i, ng: (jnp.minimum(i + nxt, last), 0, 0),
        memory_space=pltpu.SMEM)
    a_spec = pl.BlockSpec((tt, D_MODEL), lambda i, ng: (a_blk(i, tiles_a), 0))
    b_spec = pl.BlockSpec((tt, D_MODEL), lambda i, ng: (b_blk(i, tiles_a, tiles_b), 0))
    out_a, out_b = pl.pallas_call(
        functools.partial(_combine_kernel, final=final, tiles_a=tiles_a),
        grid_spec=pltpu.PrefetchScalarGridSpec(
            num_scalar_prefetch=1,
            grid=(n_tiles,),
            in_specs=[src_spec(0), src_spec(1), a_spec, b_spec,
                      pl.BlockSpec((tt, 128), lambda i, ng: (i, 0)),
                      _const_spec((1, D_MODEL)),
                      pl.BlockSpec(memory_space=pl.ANY)],
            out_specs=(a_spec, b_spec),
            scratch_shapes=[pltpu.VMEM((2, LOCAL_ROWS, D_MODEL), BF16),
                            pltpu.VMEM((tt, D_MODEL), F32),
                            pltpu.SemaphoreType.DMA((2,))]),
        out_shape=(jax.ShapeDtypeStruct(xa.shape, F32), jax.ShapeDtypeStruct(xb.shape, F32)),
        compiler_params=arb,
        name="moe_combine",
    )(ngran, dest, dest, xa, xb, meta, gf, ys)
    return out_a, out_b


def _ssd(x, conv_prev, ssm_prev, p, *, nsub, ns):
    seq_len = x.shape[1]
    q = seq_len if seq_len < SSD_CHUNK else SSD_CHUNK
    x, conv_new, ssm_new = _ssd_layer(
        x, conv_prev, ssm_prev, p["norm_mix"][0], p["ssd_w_in"][0], p["ssd_conv_w"][0],
        p["ssd_conv_b"][0], p["ssd_dt_bias"][0], p["ssd_a_log"][0], p["ssd_d"][0], p["ssd_norm"][0],
        p["ssd_w_out"][0], nsub=nsub, ns=ns, q=q)
    return x.reshape(-1, D_MODEL), conv_new, ssm_new


def _sg(x2d, p, *, seq_len, want_v):
    return _sg_layer(x2d, p["norm_mix"][1], p["sg_w_in"][0], p["sg_b_in"][0], p["sg_ln_g"][0],
                     p["sg_ln_b"][0], p["sg_w_s"][0], p["sg_b_s"][0], p["sg_w_out"][0],
                     seq_len=seq_len, want_v=want_v)


def _moe_layer(xa, xb, p, layer, norm_final):
    return _moe(xa, xb, p["norm_ffn"][layer], p["moe_w_group"][layer], p["moe_b_group"][layer],
                p["moe_w_router"][layer], p["moe_b_router"][layer], p["moe_w_gate"], p["moe_w_up"],
                p["moe_w_down"], norm_final, layer=layer)


def kernel(x_prompt, x_sample, state_ssm, state_conv, norm_mix, norm_ffn, norm_final, ssd_w_in, ssd_conv_w, ssd_conv_b, ssd_dt_bias, ssd_a_log, ssd_d, ssd_norm, ssd_w_out, sg_w_in, sg_b_in, sg_ln_g, sg_ln_b, sg_w_s, sg_b_s, sg_w_out, moe_w_group, moe_b_group, moe_w_router, moe_b_router, moe_w_gate, moe_w_up, moe_w_down):
    p = dict(norm_mix=norm_mix, norm_ffn=norm_ffn, norm_final=norm_final, ssd_w_in=ssd_w_in,
             ssd_conv_w=ssd_conv_w, ssd_conv_b=ssd_conv_b, ssd_dt_bias=ssd_dt_bias,
             ssd_a_log=ssd_a_log, ssd_d=ssd_d, ssd_norm=ssd_norm, ssd_w_out=ssd_w_out,
             sg_w_in=sg_w_in, sg_b_in=sg_b_in, sg_ln_g=sg_ln_g, sg_ln_b=sg_ln_b, sg_w_s=sg_w_s,
             sg_b_s=sg_b_s, sg_w_out=sg_w_out, moe_w_group=moe_w_group, moe_b_group=moe_b_group,
             moe_w_router=moe_w_router, moe_b_router=moe_b_router, moe_w_gate=moe_w_gate,
             moe_w_up=moe_w_up, moe_w_down=moe_w_down)
    nb = x_prompt.shape[0]
    conv0 = jnp.zeros((1, nb, CONV_W - 1, CONV_DIM), F32)
    ssm0 = jnp.zeros((1, nb, N_HEADS, HEAD_DIM, D_STATE), F32)
    xp, conv_p, ssm_p = _ssd(x_prompt, conv0, ssm0, p, nsub=2, ns=1)
    xs, conv_s, ssm_s = _ssd(x_sample, state_conv, state_ssm, p, nsub=1, ns=4)
    xp, xs = _moe_layer(xp, xs, p, 0, None)
    xp, _ = _sg(xp, p, seq_len=x_prompt.shape[1], want_v=False)
    xs, v_s = _sg(xs, p, seq_len=x_sample.shape[1], want_v=True)
    y_p, y_s = _moe_layer(xp, xs, p, 1, norm_final)
    return (y_p.reshape(x_prompt.shape), y_s.reshape(x_sample.shape), ssm_p, conv_p, ssm_s, conv_s,
            v_s.reshape((1,) + x_sample.shape[:2] + (SG_WIDTH,)))
```

```python
import functools
import math

import jax
import jax.numpy as jnp
import numpy as np
from jax import lax
from jax.experimental import pallas as pl
from jax.experimental.pallas import tpu as pltpu

F32 = jnp.float32
BF16 = jnp.bfloat16
I32 = jnp.int32

D_MODEL = 1024
N_HEADS = 32
HEAD_DIM = 64
N_GROUPS = 4
D_STATE = 128
D_INNER = N_HEADS * HEAD_DIM
GROUP_W = D_INNER // N_GROUPS
CONV_W = 4
CONV_DIM = D_INNER + 2 * N_GROUPS * D_STATE
DT_PAD = 128
SSD_CHUNK = 128
PROJ_CHUNK = 256
SG_WIDTH = 2 * D_MODEL
SG_GROUPS = 8
SG_GROUP_DIM = SG_WIDTH // SG_GROUPS
SG_CHUNK = 128
SG_ROWS = 512
MOE_GROUPS = 4
MOE_EPG = 8
MOE_EXPERTS = MOE_GROUPS * MOE_EPG
MOE_D_FF = 256
NORM_EPS = 1e-6
LN_EPS = 1e-5

MOE_TILE = 256
ROUTER_TILES = 4


def _expert_block(n_tokens):
    return 512 if 2 * n_tokens >= 512 * MOE_EXPERTS else 256


GRANULE = 16
LOCAL_ROWS = 2 * MOE_TILE + MOE_EXPERTS * GRANULE
N_LOCAL_GRAN = LOCAL_ROWS // GRANULE
VMEM_LIMIT = 56 * 1024 * 1024


def _sigmoid(x):
    return 0.5 * (jnp.tanh(0.5 * x) + 1.0)


def _silu(x):
    return x * _sigmoid(x)


def _softplus(x):
    return jnp.maximum(x, 0.0) + jnp.log(1.0 + jnp.exp(-jnp.abs(x)))


def _gelu_tanh(x):
    c = math.sqrt(2.0 / math.pi)
    t = jnp.tanh(x * (c + (c * 0.044715) * (x * x)))
    hx = 0.5 * x
    return hx + hx * t


def _rms(x, g):
    return x * lax.rsqrt(jnp.mean(x * x, axis=-1, keepdims=True) + NORM_EPS) * g


def _split3(x):
    a = x.astype(BF16)
    r = x - a.astype(F32)
    b = r.astype(BF16)
    c = (r - b.astype(F32)).astype(BF16)
    return a, b, c


def _dot(a, b):
    return jnp.dot(a, b, preferred_element_type=F32)


def _dot_nt(a, b):
    return lax.dot_general(a, b, (((1,), (1,)), ((), ())), preferred_element_type=F32)


def _dot_tn(a, b):
    return lax.dot_general(a, b, (((0,), (0,)), ((), ())), preferred_element_type=F32)


def _const_spec(shape):
    nd = len(shape)
    return pl.BlockSpec(shape, lambda *_: (0,) * nd)


def _ssd_kernel(x_ref, g_ref, win_ref, cw_ref, cb_ref, dtb_ref, alog_ref, dsk_ref, ng_ref,
                wout_ref, tri_ref, ones_ref, cin_ref, sin_ref,
                xo_ref, cout_ref, sout_ref,
                pend_x, pend_z, pend_xbc, pend_dt, xres_scr, zg_scr,
                prev_scr, xc_scr, yoff_scr, y_scr, xw_scr, hn_scr, yn_scr,
                *, nsub, ns, q, n_chunks):
    i = pl.program_id(0)
    first_chunk = lax.rem(jnp.maximum(i - 1, 0), n_chunks) == 0
    nseq = nsub * ns
    sb = ns * q
    r = nsub * sb

    @pl.when(i == 0)
    def _():
        pend_x[...] = jnp.zeros_like(pend_x)
        pend_z[...] = jnp.zeros_like(pend_z)
        pend_xbc[...] = jnp.zeros_like(pend_xbc)
        pend_dt[...] = jnp.zeros_like(pend_dt)

    tr = min(r, 128)
    row_tiles = [slice(a, a + tr) for a in range(0, r, tr)]

    def col_tiles(total, rows=tr):
        w = min(total, max(128, (32 * 1024) // rows))
        return [slice(c, c + w) for c in range(0, total, w)]

    for rt in row_tiles:
        for ct in col_tiles(D_MODEL):
            xres_scr[rt, ct] = pend_x[rt, ct]
        for ct in col_tiles(D_INNER):
            zg_scr[rt, ct] = _silu(pend_z[rt, ct])

    for s in range(nseq):
        for ct in col_tiles(D_MODEL, q):
            pend_x[s * q:(s + 1) * q, ct] = x_ref[s, :, ct]
    for rt in row_tiles:
        ss = None
        for ct in col_tiles(D_MODEL):
            xv = pend_x[rt, ct]
            part = jnp.sum(xv * xv, axis=-1, keepdims=True)
            ss = part if ss is None else ss + part
        scale = lax.rsqrt(ss * (1.0 / D_MODEL) + NORM_EPS)
        for ct in col_tiles(D_MODEL):
            hn_scr[rt, ct] = (pend_x[rt, ct] * scale * g_ref[:, ct]).astype(BF16)

    def proj_chunk(dst, dst_col, w_col, width):
        def run():
            dst[:, dst_col:dst_col + width] = _dot(hn_scr[...], win_ref[:, w_col:w_col + width])
        return run
    z_chunks = [proj_chunk(pend_z, c, c, PROJ_CHUNK) for c in range(0, D_INNER, PROJ_CHUNK)]
    scan_chunks = [proj_chunk(pend_xbc, c, D_INNER + c, PROJ_CHUNK)
                   for c in range(0, CONV_DIM, PROJ_CHUNK)]
    scan_chunks.append(proj_chunk(pend_dt, 0, D_INNER + CONV_DIM, DT_PAD))

    @pl.when(first_chunk)
    def _():
        sout_ref[...] = sin_ref[...]
        prev_scr[...] = jnp.zeros_like(prev_scr)
        for s in range(nseq):
            for k in range(CONV_W - 1):
                prev_scr[pl.ds(s * 8 + 5 + k, 1), :] = cin_ref[s, pl.ds(k, 1), :]

    sub8 = lax.broadcasted_iota(I32, (8, 1), 0)
    for s in range(nseq):
        srows = slice(s * q, (s + 1) * q)
        for ct in col_tiles(CONV_DIM, 4 * q):
            xq = pend_xbc[srows, ct]
            hist = prev_scr[s * 8:(s + 1) * 8, ct]
            acc = cb_ref[:, ct] + cw_ref[pl.ds(CONV_W - 1, 1), ct] * xq
            for j in range(1, CONV_W):
                sh = pltpu.roll(xq, j, 0)
                head = jnp.where(sub8 < j, pltpu.roll(hist, j, 0), sh[0:8, :])
                sh = head if q == 8 else jnp.concatenate([head, sh[8:, :]], axis=0)
                acc = acc + cw_ref[pl.ds(CONV_W - 1 - j, 1), ct] * sh
            xc_scr[srows, ct] = _silu(acc)
            last = xq[q - 8:q, :]
            prev_scr[s * 8:(s + 1) * 8, ct] = last
            for k in range(CONV_W - 1):
                cout_ref[s, pl.ds(k, 1), ct] = last[5 + k:6 + k, :]
    chunks = z_chunks + scan_chunks
    n_pairs = nsub * (N_HEADS // 2)
    emit_at = {}
    for k, ch in enumerate(chunks):
        emit_at.setdefault((k * n_pairs) // len(chunks), []).append(ch)

    tri = tri_ref[...]
    trib = tri.astype(BF16)
    onesb = ones_ref[...].astype(BF16)
    mask = tri > 0.5
    rowseq = lax.shift_right_logical(lax.broadcasted_iota(I32, (sb, 1), 0), int(math.log2(q)))
    lo = lax.broadcasted_iota(I32, (sb, 128), 1) < HEAD_DIM
    neg_a = -jnp.exp(alog_ref[...])

    for u in range(nsub):
        rows = slice(u * sb, (u + 1) * sb)
        xs_ref = xc_scr.at[rows, 0:D_INNER]

        def b_of(g):
            return xc_scr[rows, D_INNER + g * D_STATE:D_INNER + (g + 1) * D_STATE]

        def c_of(g):
            c0 = D_INNER + (N_GROUPS + g) * D_STATE
            return xc_scr[rows, c0:c0 + D_STATE].astype(BF16)

        dt = _softplus(pend_dt[rows, :] + dtb_ref[...])
        d1, d2, d3 = _split3(dt * neg_a)
        cs = _dot(trib, d1) + _dot(trib, d2) + _dot(trib, d3)
        cl = _dot(onesb, d1) + _dot(onesb, d2) + _dot(onesb, d3)
        ecs = jnp.exp(cs)
        wgt = dt * jnp.exp(cl - cs)
        ecl = jnp.exp(cl)
        cs_t = cs.T
        dt_t = dt.T

        for g in range(N_GROUPS):
            cg = c_of(g)
            acc = None
            for s in range(ns):
                st = sout_ref[u * ns + s, g * GROUP_W:(g + 1) * GROUP_W, :].astype(BF16)
                yo = _dot_nt(cg, st)
                if ns > 1:
                    yo = jnp.where(rowseq == s, yo, 0.0)
                acc = yo if acc is None else acc + yo
            yoff_scr[rows, g * GROUP_W:(g + 1) * GROUP_W] = acc

        for g in range(N_GROUPS):
            sc = _dot_nt(c_of(g), b_of(g).astype(BF16))
            for jj in range(N_HEADS // N_GROUPS // 2):
                j = g * (N_HEADS // N_GROUPS // 2) + jj
                ms = []
                for h in (2 * j, 2 * j + 1):
                    diff = cs[:, h:h + 1] - cs_t[h:h + 1, :]
                    dec = jnp.exp(jnp.where(mask, diff, -jnp.inf))
                    ms.append((sc * dec * dt_t[h:h + 1, :]).astype(BF16))
                lhs = jnp.concatenate(ms, axis=1)
                cols = slice(j * 128, (j + 1) * 128)
                xp = xs_ref[:, cols]
                rhs = jnp.concatenate([jnp.where(lo, xp, 0.0).astype(BF16),
                                       jnp.where(lo, 0.0, xp).astype(BF16)], axis=0)
                yd = _dot(lhs, rhs)
                ecs_p = jnp.where(lo, ecs[:, 2 * j:2 * j + 1], ecs[:, 2 * j + 1:2 * j + 2])
                wgt_p = jnp.where(lo, wgt[:, 2 * j:2 * j + 1], wgt[:, 2 * j + 1:2 * j + 2])
                y_scr[rows, cols] = yd + yoff_scr[rows, cols] * ecs_p + xp * dsk_ref[:, cols]
                xw_scr[rows, cols] = (xp * wgt_p).astype(BF16)
                pair = u * (N_HEADS // 2) + j
                for ch in emit_at.get(pair, []):
                    ch()

        for g in range(N_GROUPS):
            xwg = xw_scr[rows, g * GROUP_W:(g + 1) * GROUP_W]
            for s in range(ns):
                bg = b_of(g)
                if ns > 1:
                    bg = jnp.where(rowseq == s, bg, 0.0)
                upd = _dot_tn(xwg, bg.astype(BF16))
                for hh in range(GROUP_W // HEAD_DIM):
                    h = g * (GROUP_W // HEAD_DIM) + hh
                    dec = jnp.broadcast_to(ecl[s * q:s * q + 1, h:h + 1], (HEAD_DIM, D_STATE))
                    hrows = slice(h * HEAD_DIM, (h + 1) * HEAD_DIM)
                    sout_ref[u * ns + s, hrows, :] = (
                        sout_ref[u * ns + s, hrows, :] * dec + upd[hh * HEAD_DIM:(hh + 1) * HEAD_DIM, :])

    for rt in row_tiles:
        for g in range(N_GROUPS):
            cts = [slice(g * GROUP_W + c.start, g * GROUP_W + c.stop) for c in col_tiles(GROUP_W)]
            ss = None
            for ct in cts:
                yz = y_scr[rt, ct] * zg_scr[rt, ct]
                part = jnp.sum(yz * yz, axis=-1, keepdims=True)
                ss = part if ss is None else ss + part
            scale = lax.rsqrt(ss * (1.0 / GROUP_W) + NORM_EPS)
            for ct in cts:
                yn_scr[rt, ct] = (y_scr[rt, ct] * zg_scr[rt, ct] * scale * ng_ref[:, ct]).astype(BF16)

    for c0 in range(0, D_MODEL, PROJ_CHUNK):
        ct = slice(c0, c0 + PROJ_CHUNK)
        o = xres_scr[:, ct] + _dot(yn_scr[...], wout_ref[:, ct])
        for s in range(nseq):
            xo_ref[s, :, ct] = o[s * q:(s + 1) * q, :]


def _ssd_layer(x, conv_prev, ssm_prev, norm_g, w_in, conv_w, conv_b, dt_bias, a_log, d_skip,
               norm_y, w_out, *, nsub, ns, q):
    n_seq, seq_len, _ = x.shape
    nseq = nsub * ns
    sb = ns * q
    r = nsub * sb
    n_chunks = seq_len // q
    pad = DT_PAD - N_HEADS
    win = jnp.concatenate([w_in, jnp.zeros((D_MODEL, pad), F32)], axis=1).astype(BF16)
    dtb = jnp.pad(dt_bias, (0, pad)).reshape(1, DT_PAD)
    alog = jnp.pad(a_log, (0, pad)).reshape(1, DT_PAD)
    dsk = jnp.repeat(d_skip, HEAD_DIM).reshape(1, D_INNER)
    blk = np.kron(np.eye(ns), np.ones((q, q)))
    tri = jnp.asarray(blk * np.tril(np.ones((sb, sb))), F32)
    ones = jnp.asarray(blk, F32)
    state = ssm_prev.reshape(n_seq, D_INNER, D_STATE)
    conv_prev = conv_prev.reshape(n_seq, CONV_W - 1, CONV_DIM)

    kern = functools.partial(_ssd_kernel, nsub=nsub, ns=ns, q=q, n_chunks=n_chunks)
    out_shape = (jax.ShapeDtypeStruct(x.shape, F32),
                 jax.ShapeDtypeStruct((n_seq, CONV_W - 1, CONV_DIM), F32),
                 jax.ShapeDtypeStruct((n_seq, D_INNER, D_STATE), F32))
    n_steps = (n_seq // nseq) * n_chunks
    nxt = lambda i: jnp.minimum(i, n_steps - 1)
    cur = lambda i: jnp.maximum(i - 1, 0)
    in_row_spec = pl.BlockSpec((nseq, q, D_MODEL),
                               lambda i: (nxt(i) // n_chunks, nxt(i) % n_chunks, 0))
    row_spec = pl.BlockSpec((nseq, q, D_MODEL),
                            lambda i: (cur(i) // n_chunks, cur(i) % n_chunks, 0))
    conv_spec = pl.BlockSpec((nseq, CONV_W - 1, CONV_DIM), lambda i: (cur(i) // n_chunks, 0, 0))
    state_spec = pl.BlockSpec((nseq, D_INNER, D_STATE), lambda i: (cur(i) // n_chunks, 0, 0))
    in_specs = [in_row_spec,
                _const_spec((1, D_MODEL)),
                _const_spec(win.shape),
                _const_spec((CONV_W, CONV_DIM)),
                _const_spec((1, CONV_DIM)),
                _const_spec((1, DT_PAD)),
                _const_spec((1, DT_PAD)),
                _const_spec((1, D_INNER)),
                _const_spec((1, D_INNER)),
                _const_spec((D_INNER, D_MODEL)),
                _const_spec((sb, sb)),
                _const_spec((sb, sb)),
                conv_spec, state_spec]
    x_new, conv_new, state_new = pl.pallas_call(
        kern,
        grid=(n_steps + 1,),
        in_specs=in_specs,
        out_specs=(row_spec, conv_spec, state_spec),
        out_shape=out_shape,
        scratch_shapes=[pltpu.VMEM((r, D_MODEL), F32),
                        pltpu.VMEM((r, D_INNER), F32),
                        pltpu.VMEM((r, CONV_DIM), F32),
                        pltpu.VMEM((r, DT_PAD), F32),
                        pltpu.VMEM((r, D_MODEL), F32),
                        pltpu.VMEM((r, D_INNER), F32),
                        pltpu.VMEM((nseq * 8, CONV_DIM), F32),
                        pltpu.VMEM((r, CONV_DIM), F32),
                        pltpu.VMEM((r, D_INNER), F32),
                        pltpu.VMEM((r, D_INNER), F32),
                        pltpu.VMEM((r, D_INNER), BF16),
                        pltpu.VMEM((r, D_MODEL), BF16),
                        pltpu.VMEM((r, D_INNER), BF16)],
        compiler_params=pltpu.CompilerParams(
            dimension_semantics=("arbitrary",), vmem_limit_bytes=VMEM_LIMIT),
        name="ssd_layer",
    )(x, norm_g.reshape(1, D_MODEL), win, conv_w, conv_b.reshape(1, CONV_DIM), dtb, alog, dsk,
      norm_y.reshape(1, D_INNER), w_out.astype(BF16), tri, ones, conv_prev, state)
    return (x_new, conv_new.reshape(1, n_seq, CONV_W - 1, CONV_DIM),
            state_new.reshape(1, n_seq, N_HEADS, HEAD_DIM, D_STATE))


def _sg_kernel(x_ref, g_ref, win_ref, bin_ref, lng_ref, lnb_ref, wmix_ref, bmix_ref, wout_ref,
               xo_ref, *rest, r, want_v):
    v_ref = rest[0] if want_v else None
    hn_scr, uv_scr, vb_scr, um_scr = rest[-4:]
    row_tiles = [slice(a, a + 128) for a in range(0, r, 128)]
    col128 = lambda total: [slice(c, c + 128) for c in range(0, total, 128)]

    for rt in row_tiles:
        ss = None
        for ct in col128(D_MODEL):
            xv = x_ref[rt, ct]
            part = jnp.sum(xv * xv, axis=-1, keepdims=True)
            ss = part if ss is None else ss + part
        scale = lax.rsqrt(ss * (1.0 / D_MODEL) + NORM_EPS)
        for ct in col128(D_MODEL):
            hn_scr[rt, ct] = (x_ref[rt, ct] * scale * g_ref[:, ct]).astype(BF16)

    vcols = [slice(SG_WIDTH + c.start, SG_WIDTH + c.stop) for c in col128(SG_WIDTH)]

    def front(rt):
        def piece(c0):
            def run():
                h = _dot(hn_scr[rt, :], win_ref[:, c0:c0 + PROJ_CHUNK])
                for cc in range(0, PROJ_CHUNK, 128):
                    ct = slice(c0 + cc, c0 + cc + 128)
                    uv_scr[rt, ct] = _gelu_tanh(h[:, cc:cc + 128] + bin_ref[:, ct])
            return run
        return [piece(c0) for c0 in range(0, 2 * SG_WIDTH, PROJ_CHUNK)]

    def back(rt):
        stats = {}

        def ln_stats():
            tot = None
            for ct in vcols:
                part = jnp.sum(uv_scr[rt, ct], axis=-1, keepdims=True)
                tot = part if tot is None else tot + part
            mu = tot * (1.0 / SG_WIDTH)
            ss = None
            for ct in vcols:
                vc = uv_scr[rt, ct] - mu
                part = jnp.sum(vc * vc, axis=-1, keepdims=True)
                ss = part if ss is None else ss + part
            stats["mu"] = mu
            stats["scale"] = lax.rsqrt(ss * (1.0 / SG_WIDTH) + LN_EPS)

        def ln_apply(k0):
            def run():
                for ct, c in list(zip(vcols, col128(SG_WIDTH)))[k0:k0 + 4]:
                    vn = (uv_scr[rt, ct] - stats["mu"]) * stats["scale"] * lng_ref[:, c] + lnb_ref[:, c]
                    if want_v:
                        v_ref[rt, c] = vn
                    vb_scr[rt, c] = vn.astype(BF16)
            return run

        def mix(g):
            def run():
                cols = slice(g * SG_GROUP_DIM, (g + 1) * SG_GROUP_DIM)
                mixed = _dot(wmix_ref[g], vb_scr[rt, cols]) + bmix_ref[:, cols]
                um_scr[rt, cols] = (uv_scr[rt, cols] * mixed).astype(BF16)
            return run

        def out(c0):
            def run():
                ct = slice(c0, c0 + PROJ_CHUNK)
                xo_ref[rt, ct] = x_ref[rt, ct] + _dot(um_scr[rt, :], wout_ref[:, ct])
            return run

        return ([ln_stats] + [ln_apply(k) for k in range(0, len(vcols), 4)]
                + [mix(g) for g in range(SG_GROUPS)]
                + [out(c0) for c0 in range(0, D_MODEL, PROJ_CHUNK)])

    pending = []
    for rt in row_tiles:
        for piece in front(rt):
            piece()
            if pending:
                pending.pop(0)()
        for piece in pending:
            piece()
        pending = back(rt)
    for piece in pending:
        piece()


def _sg_layer(x2d, norm_g, w_in, b_in, ln_g, ln_b, w_s, b_s, w_out, *, seq_len, want_v):
    r = SG_ROWS
    t = x2d.shape[0]
    q = min(seq_len, SG_CHUNK)
    reps = SG_CHUNK // q
    ws = jnp.tril(w_s)[:, :q, :q]
    wmix = jnp.einsum("ab,gts->gatbs", jnp.eye(reps, dtype=F32), ws)
    wmix = wmix.reshape(SG_GROUPS, SG_CHUNK, SG_CHUNK)
    bmix = jnp.tile(jnp.repeat(b_s.T[:q], SG_GROUP_DIM, axis=1), (reps, 1))
    row_spec = pl.BlockSpec((r, D_MODEL), lambda i: (i, 0))
    v_spec = pl.BlockSpec((r, SG_WIDTH), lambda i: (i, 0))
    out_shape = [jax.ShapeDtypeStruct(x2d.shape, F32)]
    out_specs = [row_spec]
    if want_v:
        out_shape.append(jax.ShapeDtypeStruct((t, SG_WIDTH), F32))
        out_specs.append(v_spec)
    outs = pl.pallas_call(
        functools.partial(_sg_kernel, r=r, want_v=want_v),
        grid=(t // r,),
        scratch_shapes=[pltpu.VMEM((r, D_MODEL), BF16),
                        pltpu.VMEM((r, 2 * SG_WIDTH), F32),
                        pltpu.VMEM((r, SG_WIDTH), BF16),
                        pltpu.VMEM((r, SG_WIDTH), BF16)],
        in_specs=[row_spec,
                  _const_spec((1, D_MODEL)),
                  _const_spec((D_MODEL, 2 * SG_WIDTH)),
                  _const_spec((1, 2 * SG_WIDTH)),
                  _const_spec((1, SG_WIDTH)),
                  _const_spec((1, SG_WIDTH)),
                  _const_spec((SG_GROUPS, SG_CHUNK, SG_CHUNK)),
                  _const_spec((SG_CHUNK, SG_WIDTH)),
                  _const_spec((SG_WIDTH, D_MODEL))],
        out_specs=out_specs,
        out_shape=out_shape,
        compiler_params=pltpu.CompilerParams(
            dimension_semantics=("arbitrary",), vmem_limit_bytes=VMEM_LIMIT),
        name="sg_layer",
    )(x2d, norm_g.reshape(1, D_MODEL), w_in.astype(BF16), b_in.reshape(1, 2 * SG_WIDTH),
      ln_g.reshape(1, SG_WIDTH), ln_b.reshape(1, SG_WIDTH), wmix.astype(BF16), bmix,
      w_out.astype(BF16))
    return outs if want_v else (outs[0], None)


ROUTER_ROWS = 64


def _norm_tiles(use_a, xa_ref, xb_ref, g_ref, store):
    def load(rt, ct):
        return jnp.where(use_a, xa_ref[rt, ct], xb_ref[rt, ct])

    for r0 in range(0, xa_ref.shape[0], 128):
        rt = slice(r0, r0 + 128)
        ss = None
        for c0 in range(0, D_MODEL, 256):
            xv = load(rt, slice(c0, c0 + 256))
            part = jnp.sum(xv * xv, axis=-1, keepdims=True)
            ss = part if ss is None else ss + part
        scale = lax.rsqrt(ss * (1.0 / D_MODEL) + NORM_EPS)
        for c0 in range(0, D_MODEL, 256):
            ct = slice(c0, c0 + 256)
            store(rt, ct, load(rt, ct) * scale * g_ref[:, ct])


def _router_t_kernel(xa_ref, xb_ref, g_ref, wt_ref, bt_ref, usl_ref, lsl_ref,
                     meta_ref, metat_ref, pad_ref, h1_scr, h2_scr, *, steps_a):
    wt = wt_ref[...]
    w1 = wt.astype(BF16)
    w2 = (wt - w1.astype(F32)).astype(BF16)
    tt = MOE_TILE
    n = xa_ref.shape[0]

    def split_store(rt, ct, hn):
        hi = hn.astype(BF16)
        h1_scr[rt, ct] = hi
        h2_scr[rt, ct] = (hn - hi.astype(F32)).astype(BF16)

    _norm_tiles(pl.program_id(0) < steps_a, xa_ref, xb_ref, g_ref, split_store)

    h1 = h1_scr[...]
    logits = _dot_nt(w1, h1) + _dot_nt(w2, h1) + _dot_nt(w1, h2_scr[...])
    select = _select_experts(logits, bt_ref[...])
    for k in range(n // tt):
        cols = slice(k * tt, (k + 1) * tt)
        metat, pad = _local_order(select[:, cols], usl_ref[...], lsl_ref[...])
        metat_ref[k * 8:(k + 1) * 8, :] = metat
        full = jnp.concatenate([metat, jnp.zeros((128 - 8, tt), F32)], axis=0)
        meta_ref[cols, :] = full.T
        pad_ref[k * MOE_EXPERTS:(k + 1) * MOE_EXPERTS, :] = pad


def _select_experts(logits, bt):
    n = logits.shape[1]
    reps = n // bt.shape[1]
    logits = logits + (bt if reps == 1 else jnp.concatenate([bt] * reps, axis=1))
    tt = n
    row8 = lax.broadcasted_iota(I32, (8, tt), 0).astype(F32)
    gl = jnp.where(row8 < MOE_GROUPS, logits[0:8, :], -jnp.inf)
    gmax = jnp.max(gl, axis=0, keepdims=True)
    g_top = jnp.min(jnp.where(gl == gmax, row8, 8.0), axis=0, keepdims=True)
    p_g = 1.0 / jnp.sum(jnp.exp(gl - gmax), axis=0, keepdims=True)

    el = logits[8:16, :]
    for grp in range(1, MOE_GROUPS):
        el = jnp.where(g_top == grp, logits[8 + 8 * grp:16 + 8 * grp, :], el)
    emax = jnp.max(el, axis=0, keepdims=True)
    ee = jnp.exp(el - emax)
    prob = ee / jnp.sum(ee, axis=0, keepdims=True)
    p1 = jnp.max(prob, axis=0, keepdims=True)
    i1 = jnp.min(jnp.where(prob == p1, row8, 8.0), axis=0, keepdims=True)
    prob2 = jnp.where(row8 == i1, -1.0, prob)
    p2 = jnp.max(prob2, axis=0, keepdims=True)
    i2 = jnp.min(jnp.where(prob2 == p2, row8, 8.0), axis=0, keepdims=True)
    psum = p1 + p2
    gate1 = p_g * (p1 / psum)
    gate2 = p_g * (p2 / psum)
    e1 = g_top * MOE_EPG + i1
    e2 = g_top * MOE_EPG + i2
    out = jnp.where(row8 == 0, e1, 0.0)
    out = jnp.where(row8 == 1, e2, out)
    out = jnp.where(row8 == 2, gate1, out)
    return jnp.where(row8 == 3, gate2, out)


def _local_order(select, usl, lsl):
    tt = select.shape[1]
    row8 = lax.broadcasted_iota(I32, (8, tt), 0).astype(F32)
    e1, e2 = select[0:1, :], select[1:2, :]
    rowe = lax.broadcasted_iota(I32, (MOE_EXPERTS, tt), 0).astype(F32)
    sel1 = rowe == e1
    sel2 = rowe == e2
    onehot = jnp.where(sel1 | sel2, 1.0, 0.0)
    before = _dot(onehot.astype(BF16), usl)
    cnt = jnp.sum(onehot, axis=1, keepdims=True)
    pad = jnp.floor((cnt + (GRANULE - 1)) * (1.0 / GRANULE)) * GRANULE
    padb = jnp.broadcast_to(pad, (MOE_EXPERTS, tt))
    local = before + _dot(lsl, padb.astype(BF16))
    slot1 = jnp.sum(jnp.where(sel1, local, 0.0), axis=0, keepdims=True)
    slot2 = jnp.sum(jnp.where(sel2, local, 0.0), axis=0, keepdims=True)

    metat = jnp.where(row8 == 4, slot1, select)
    metat = jnp.where(row8 == 5, slot2, metat)
    return metat, padb[:, 0:128]


def _granule(ref, g):
    return ref.at[pl.ds(pl.multiple_of(g * GRANULE, GRANULE), GRANULE), :]


def _dispatch_kernel(ngran_ref, nv_ref, dest_ref, gap_ref, xa_ref, xb_ref, g_ref, meta_ref, xs_hbm,
                     buf, zbuf, hn_scr, sem, zsem, nstart, *, tiles_a):
    i = pl.program_id(0)
    n_tiles = pl.num_programs(0) - 1
    slot = i % 2
    tt = xa_ref.shape[0]
    bm = zbuf.shape[0]
    n_blocks = xs_hbm.shape[0] // bm

    def out_copy(sl, g, d):
        return pltpu.make_async_copy(_granule(buf.at[sl], g), _granule(xs_hbm, d), sem.at[sl])

    def drain(sl):
        def body(_, carry):
            out_copy(sl, 0, 0).wait()
            return carry
        lax.fori_loop(0, nstart[sl], body, 0)

    def tail_copy(b):
        return pltpu.make_async_copy(
            zbuf, xs_hbm.at[pl.ds(pl.multiple_of(b * bm, bm), bm), :], zsem.at[1])

    @pl.when(i == 0)
    def _():
        nstart[0] = 0
        nstart[1] = 0
        zbuf[...] = jnp.zeros_like(zbuf)

    def norm_store(rt, ct, hn):
        hn_scr[rt, ct] = hn.astype(BF16)

    tail = nv_ref[0] + i

    @pl.when((i >= 1) & (tail - 1 < n_blocks))
    def _():
        tail_copy(tail - 1).wait()

    @pl.when(i < n_tiles)
    def _():
        @pl.when(tail < n_blocks)
        def _():
            tail_copy(tail).start()
        _norm_tiles(i < tiles_a, xa_ref, xb_ref, g_ref, norm_store)
        drain(slot)
        hn = hn_scr[...]
        mt = meta_ref[...]
        for r0 in range(0, LOCAL_ROWS, 256):
            rows = (lax.broadcasted_iota(I32, (256, tt), 0) + r0).astype(F32)
            onehot = jnp.where((rows == mt[4:5, :]) | (rows == mt[5:6, :]), 1.0, 0.0).astype(BF16)
            buf[slot, r0:r0 + 256, :] = _dot(onehot, hn).astype(BF16)
        ng = ngran_ref[i]

        def body(g, carry):
            out_copy(slot, g, dest_ref[0, 0, g]).start()
            return carry
        lax.fori_loop(0, ng, body, 0)
        nstart[slot] = ng

    @pl.when(i == n_tiles)
    def _():
        drain(0)
        drain(1)

        def gap_copy(d):
            return pltpu.make_async_copy(_granule(zbuf, 0), _granule(xs_hbm, d), zsem.at[0])

        def each_gap(fn):
            def body(j, carry):
                d = gap_ref[0, 0, j]

                @pl.when(d >= 0)
                def _():
                    fn(gap_copy(d))
                return carry
            lax.fori_loop(0, gap_ref.shape[2], body, 0)

        def each_tail(fn):
            def body(b, carry):
                fn(tail_copy(b))
                return carry
            lax.fori_loop(nv_ref[0] + n_tiles, n_blocks, body, 0)

        each_gap(lambda cp: cp.start())
        each_tail(lambda cp: cp.start())
        each_gap(lambda cp: cp.wait())
        each_tail(lambda cp: cp.wait())


def _expert_kernel(bexp_ref, nv_ref, xs_ref, wg_ref, wu_ref, wd_ref, ys_ref,
                   wgu_b, wd_b, act_scr):
    b = pl.program_id(0)
    ff = wg_ref.shape[1]

    @pl.when(b < nv_ref[0])
    def _():
        @pl.when((b == 0) | (bexp_ref[b] != bexp_ref[jnp.maximum(b - 1, 0)]))
        def _():
            for k0 in range(0, wg_ref.shape[0], 256):
                wgu_b[k0:k0 + 256, 0:ff] = wg_ref[k0:k0 + 256, :].astype(BF16)
                wgu_b[k0:k0 + 256, ff:2 * ff] = wu_ref[k0:k0 + 256, :].astype(BF16)
            for k0 in range(0, ff, 64):
                wd_b[k0:k0 + 64, :] = wd_ref[k0:k0 + 64, :].astype(BF16)

        bm = xs_ref.shape[0]
        for m0 in range(0, bm, 256):
            h = _dot(xs_ref[m0:m0 + 256, :], wgu_b[...])
            for r0 in range(0, 256, 128):
                for c0 in range(0, ff, 128):
                    hg = h[r0:r0 + 128, c0:c0 + 128]
                    hu = h[r0:r0 + 128, ff + c0:ff + c0 + 128]
                    act_scr[m0 + r0:m0 + r0 + 128, c0:c0 + 128] = (_silu(hg) * hu).astype(BF16)
            ys_ref[m0:m0 + 256, :] = _dot(act_scr[m0:m0 + 256, :], wd_b[...]).astype(BF16)


def _combine_kernel(ngran_ref, src_ref, srcn_ref, xa_ref, xb_ref, meta_ref, gf_ref, ys_hbm,
                    oa_ref, ob_ref, buf, moe_scr, sem, *, final, tiles_a):
    i = pl.program_id(0)
    n = pl.num_programs(0)
    slot = i % 2
    tt = xa_ref.shape[0]

    def in_copy(sl, g, d):
        return pltpu.make_async_copy(_granule(ys_hbm, d), _granule(buf.at[sl], g), sem.at[sl])

    def gather(idx_ref, sl, ng):
        def body(g, carry):
            in_copy(sl, g, idx_ref[0, 0, g]).start()
            return carry
        lax.fori_loop(0, ng, body, 0)

    @pl.when(i == 0)
    def _():
        buf[...] = jnp.zeros_like(buf)
        gather(src_ref, 0, ngran_ref[0])

    @pl.when(i + 1 < n)
    def _():
        gather(srcn_ref, 1 - slot, ngran_ref[i + 1])

    def wait_body(_, carry):
        in_copy(slot, 0, 0).wait()
        return carry
    lax.fori_loop(0, ngran_ref[i], wait_body, 0)

    ys = buf[slot]
    meta = meta_ref[...]
    cols = lax.broadcasted_iota(I32, (tt, LOCAL_ROWS), 1).astype(F32)
    pick = (jnp.where(cols == meta[:, 4:5], meta[:, 2:3], 0.0)
            + jnp.where(cols == meta[:, 5:6], meta[:, 3:4], 0.0)).astype(BF16)
    moe = _dot(pick, ys)
    is_a = i < tiles_a
    for r0 in range(0, tt, 128):
        rt = slice(r0, r0 + 128)
        for c0 in range(0, D_MODEL, 256):
            ct = slice(c0, c0 + 256)
            moe_scr[rt, ct] = jnp.where(is_a, xa_ref[rt, ct], xb_ref[rt, ct]) + moe[rt, ct]
        if final:
            ss = None
            for c0 in range(0, D_MODEL, 256):
                yv = moe_scr[rt, c0:c0 + 256]
                part = jnp.sum(yv * yv, axis=-1, keepdims=True)
                ss = part if ss is None else ss + part
            scale = lax.rsqrt(ss * (1.0 / D_MODEL) + NORM_EPS)
            for c0 in range(0, D_MODEL, 256):
                ct = slice(c0, c0 + 256)
                moe_scr[rt, ct] = moe_scr[rt, ct] * scale * gf_ref[:, ct]

    def emit(o_ref):
        for r0 in range(0, tt, 128):
            for c0 in range(0, D_MODEL, 256):
                o_ref[r0:r0 + 128, c0:c0 + 256] = moe_scr[r0:r0 + 128, c0:c0 + 256]

    @pl.when(is_a)
    def _():
        emit(oa_ref)

    @pl.when(jnp.logical_not(is_a))
    def _():
        emit(ob_ref)


def _moe(xa, xb, norm_g, w_grp, b_grp, w_rt, b_rt, w_gate, w_up, w_down, norm_final, *, layer):
    tt = MOE_TILE
    tiles_a, tiles_b = xa.shape[0] // tt, xb.shape[0] // tt
    steps_a, steps_b = tiles_a // ROUTER_TILES, tiles_b // ROUTER_TILES
    n_tiles = tiles_a + tiles_b
    t = n_tiles * tt
    bm = _expert_block(t)
    n_exp = MOE_EXPERTS
    a_blk = lambda i, n_a: jnp.minimum(i, n_a - 1)
    b_blk = lambda i, n_a, n_b: jnp.clip(i - n_a, 0, n_b - 1)
    zrow = lambda n: jnp.zeros((n, D_MODEL), F32)
    wt = jnp.concatenate([w_grp.T, zrow(8 - MOE_GROUPS), w_rt.T, zrow(ROUTER_ROWS - 8 - n_exp)], axis=0)
    bt = jnp.concatenate([b_grp, jnp.zeros((8 - MOE_GROUPS,), F32), b_rt,
                          jnp.zeros((ROUTER_ROWS - 8 - n_exp,), F32)])
    bt = jnp.broadcast_to(bt[:, None], (ROUTER_ROWS, tt))
    usl = jnp.asarray(np.triu(np.ones((tt, tt)), 1), BF16)
    lsl = jnp.asarray(np.tril(np.ones((n_exp, n_exp)), -1), BF16)
    g2d = norm_g.reshape(1, D_MODEL)
    arb = pltpu.CompilerParams(dimension_semantics=("arbitrary",), vmem_limit_bytes=VMEM_LIMIT)

    meta, metat, pad = pl.pallas_call(
        functools.partial(_router_t_kernel, steps_a=steps_a),
        grid=(steps_a + steps_b,),
        in_specs=[pl.BlockSpec((ROUTER_TILES * tt, D_MODEL), lambda i: (a_blk(i, steps_a), 0)),
                  pl.BlockSpec((ROUTER_TILES * tt, D_MODEL),
                               lambda i: (b_blk(i, steps_a, steps_b), 0)),
                  _const_spec((1, D_MODEL)),
                  _const_spec((ROUTER_ROWS, D_MODEL)),
                  _const_spec((ROUTER_ROWS, tt)),
                  _const_spec((tt, tt)),
                  _const_spec((n_exp, n_exp))],
        out_specs=(pl.BlockSpec((ROUTER_TILES * tt, 128), lambda i: (i, 0)),
                   pl.BlockSpec((ROUTER_TILES * 8, tt), lambda i: (i, 0)),
                   pl.BlockSpec((ROUTER_TILES * n_exp, 128), lambda i: (i, 0))),
        out_shape=(jax.ShapeDtypeStruct((t, 128), F32),
                   jax.ShapeDtypeStruct((n_tiles * 8, tt), F32),
                   jax.ShapeDtypeStruct((n_tiles * n_exp, 128), F32)),
        scratch_shapes=[pltpu.VMEM((ROUTER_TILES * tt, D_MODEL), BF16),
                        pltpu.VMEM((ROUTER_TILES * tt, D_MODEL), BF16)],
        compiler_params=arb,
        name="moe_router",
    )(xa, xb, g2d, wt, bt, usl, lsl)

    runs = pad.reshape(n_tiles, n_exp, 128)[:, :, 0].astype(I32)
    rows_e = jnp.sum(runs, axis=0)
    nblk = (rows_e + bm - 1) // bm
    blk_end = jnp.cumsum(nblk)
    e_start = (blk_end - nblk) * bm
    n_valid = blk_end[-1]
    run_end = jnp.cumsum(runs, axis=1)
    ngran = (run_end[:, -1] // GRANULE).astype(I32)
    shift = e_start[None, :] + (jnp.cumsum(runs, axis=0) - runs) - (run_end - runs)
    g_row = jnp.arange(N_LOCAL_GRAN, dtype=I32) * GRANULE
    e_of_g = jnp.sum((run_end[:, None, :] <= g_row[None, :, None]).astype(I32), axis=-1)
    shift_g = jnp.sum(jnp.where(e_of_g[..., None] == jnp.arange(n_exp, dtype=I32),
                                shift[:, None, :], 0), axis=-1)
    dest = jnp.where(e_of_g < n_exp, (shift_g + g_row[None, :]) // GRANULE, 0)
    dest = dest.astype(I32).reshape(n_tiles, 1, N_LOCAL_GRAN)
    per_blk = bm // GRANULE
    gap = ((e_start + rows_e) // GRANULE)[:, None] + jnp.arange(per_blk, dtype=I32)[None, :]
    gap = jnp.where(gap < ((e_start + nblk * bm) // GRANULE)[:, None], gap, -1)
    gap = gap.astype(I32).reshape(1, 1, n_exp * per_blk)
    n_blocks = (2 * t + n_tiles * n_exp * (GRANULE - 1)) // bm + 1 + n_exp
    blk = jnp.minimum(jnp.arange(n_blocks, dtype=I32), n_valid - 1)
    block_expert = jnp.sum((blk[:, None] >= blk_end[None, :]).astype(I32), axis=1).astype(I32)
    n_valid = n_valid.reshape(1).astype(I32)

    last = n_tiles - 1
    xs = pl.pallas_call(
        functools.partial(_dispatch_kernel, tiles_a=tiles_a),
        grid_spec=pltpu.PrefetchScalarGridSpec(
            num_scalar_prefetch=2,
            grid=(n_tiles + 1,),
            in_specs=[pl.BlockSpec((1, 1, N_LOCAL_GRAN),
                                   lambda i, ng, nv: (jnp.minimum(i, last), 0, 0),
                                   memory_space=pltpu.SMEM),
                      pl.BlockSpec((1, 1, n_exp * per_blk), lambda i, ng, nv: (0, 0, 0),
                                   memory_space=pltpu.SMEM),
                      pl.BlockSpec((tt, D_MODEL), lambda i, ng, nv: (a_blk(i, tiles_a), 0)),
                      pl.BlockSpec((tt, D_MODEL),
                                   lambda i, ng, nv: (b_blk(i, tiles_a, tiles_b), 0)),
                      _const_spec((1, D_MODEL)),
                      pl.BlockSpec((8, tt), lambda i, ng, nv: (jnp.minimum(i, last), 0))],
            out_specs=pl.BlockSpec(memory_space=pl.ANY),
            scratch_shapes=[pltpu.VMEM((2, LOCAL_ROWS, D_MODEL), BF16),
                            pltpu.VMEM((bm, D_MODEL), BF16),
                            pltpu.VMEM((tt, D_MODEL), BF16),
                            pltpu.SemaphoreType.DMA((2,)),
                            pltpu.SemaphoreType.DMA((2,)),
                            pltpu.SMEM((2,), I32)]),
        out_shape=jax.ShapeDtypeStruct((n_blocks * bm, D_MODEL), BF16),
        compiler_params=arb,
        name="moe_dispatch",
    )(ngran, n_valid, dest, gap, xa, xb, g2d, metat)

    w_spec = lambda shape: pl.BlockSpec((None, None) + shape,
                                        lambda b, be, nv: (layer, be[b], 0, 0))
    row_blk = pl.BlockSpec((bm, D_MODEL), lambda b, be, nv: (jnp.minimum(b, nv[0] - 1), 0))
    ys = pl.pallas_call(
        _expert_kernel,
        grid_spec=pltpu.PrefetchScalarGridSpec(
            num_scalar_prefetch=2,
            grid=(n_blocks,),
            in_specs=[row_blk,
                      w_spec((D_MODEL, MOE_D_FF)),
                      w_spec((D_MODEL, MOE_D_FF)),
                      w_spec((MOE_D_FF, D_MODEL))],
            out_specs=row_blk,
            scratch_shapes=[pltpu.VMEM((D_MODEL, 2 * MOE_D_FF), BF16),
                            pltpu.VMEM((MOE_D_FF, D_MODEL), BF16),
                            pltpu.VMEM((bm, MOE_D_FF), BF16)]),
        out_shape=jax.ShapeDtypeStruct((n_blocks * bm, D_MODEL), BF16),
        input_output_aliases={2: 0},
        compiler_params=arb,
        name="moe_experts",
    )(block_expert, n_valid, xs, w_gate, w_up, w_down)

    final = norm_final is not None
    gf = (norm_final if final else jnp.ones((D_MODEL,), F32)).reshape(1, D_MODEL)
    src_spec = lambda nxt: pl.BlockSpec(
        (1, 1, N_LOCAL_GRAN), lambda i, ng: (jnp.minimum(i + nxt, last), 0, 0),
        memory_space=pltpu.SMEM)
    a_spec = pl.BlockSpec((tt, D_MODEL), lambda i, ng: (a_blk(i, tiles_a), 0))
    b_spec = pl.BlockSpec((tt, D_MODEL), lambda i, ng: (b_blk(i, tiles_a, tiles_b), 0))
    out_a, out_b = pl.pallas_call(
        functools.partial(_combine_kernel, final=final, tiles_a=tiles_a),
        grid_spec=pltpu.PrefetchScalarGridSpec(
            num_scalar_prefetch=1,
            grid=(n_tiles,),
            in_specs=[src_spec(0), src_spec(1), a_spec, b_spec,
                      pl.BlockSpec((tt, 128), lambda i, ng: (i, 0)),
                      _const_spec((1, D_MODEL)),
                      pl.BlockSpec(memory_space=pl.ANY)],
            out_specs=(a_spec, b_spec),
            scratch_shapes=[pltpu.VMEM((2, LOCAL_ROWS, D_MODEL), BF16),
                            pltpu.VMEM((tt, D_MODEL), F32),
                            pltpu.SemaphoreType.DMA((2,))]),
        out_shape=(jax.ShapeDtypeStruct(xa.shape, F32), jax.ShapeDtypeStruct(xb.shape, F32)),
        compiler_params=arb,
        name="moe_combine",
    )(ngran, dest, dest, xa, xb, meta, gf, ys)
    return out_a, out_b


def _ssd(x, conv_prev, ssm_prev, p, *, nsub, ns):
    seq_len = x.shape[1]
    q = seq_len if seq_len < SSD_CHUNK else SSD_CHUNK
    x, conv_new, ssm_new = _ssd_layer(
        x, conv_prev, ssm_prev, p["norm_mix"][0], p["ssd_w_in"][0], p["ssd_conv_w"][0],
        p["ssd_conv_b"][0], p["ssd_dt_bias"][0], p["ssd_a_log"][0], p["ssd_d"][0], p["ssd_norm"][0],
        p["ssd_w_out"][0], nsub=nsub, ns=ns, q=q)
    return x.reshape(-1, D_MODEL), conv_new, ssm_new


def _sg(x2d, p, *, seq_len, want_v):
    return _sg_layer(x2d, p["norm_mix"][1], p["sg_w_in"][0], p["sg_b_in"][0], p["sg_ln_g"][0],
                     p["sg_ln_b"][0], p["sg_w_s"][0], p["sg_b_s"][0], p["sg_w_out"][0],
                     seq_len=seq_len, want_v=want_v)


def _moe_layer(xa, xb, p, layer, norm_final):
    return _moe(xa, xb, p["norm_ffn"][layer], p["moe_w_group"][layer], p["moe_b_group"][layer],
                p["moe_w_router"][layer], p["moe_b_router"][layer], p["moe_w_gate"], p["moe_w_up"],
                p["moe_w_down"], norm_final, layer=layer)


def kernel(x_prompt, x_sample, state_ssm, state_conv, norm_mix, norm_ffn, norm_final, ssd_w_in, ssd_conv_w, ssd_conv_b, ssd_dt_bias, ssd_a_log, ssd_d, ssd_norm, ssd_w_out, sg_w_in, sg_b_in, sg_ln_g, sg_ln_b, sg_w_s, sg_b_s, sg_w_out, moe_w_group, moe_b_group, moe_w_router, moe_b_router, moe_w_gate, moe_w_up, moe_w_down):
    p = dict(norm_mix=norm_mix, norm_ffn=norm_ffn, norm_final=norm_final, ssd_w_in=ssd_w_in,
             ssd_conv_w=ssd_conv_w, ssd_conv_b=ssd_conv_b, ssd_dt_bias=ssd_dt_bias,
             ssd_a_log=ssd_a_log, ssd_d=ssd_d, ssd_norm=ssd_norm, ssd_w_out=ssd_w_out,
             sg_w_in=sg_w_in, sg_b_in=sg_b_in, sg_ln_g=sg_ln_g, sg_ln_b=sg_ln_b, sg_w_s=sg_w_s,
             sg_b_s=sg_b_s, sg_w_out=sg_w_out, moe_w_group=moe_w_group, moe_b_group=moe_b_group,
             moe_w_router=moe_w_router, moe_b_router=moe_b_router, moe_w_gate=moe_w_gate,
             moe_w_up=moe_w_up, moe_w_down=moe_w_down)
    nb = x_prompt.shape[0]
    conv0 = jnp.zeros((1, nb, CONV_W - 1, CONV_DIM), F32)
    ssm0 = jnp.zeros((1, nb, N_HEADS, HEAD_DIM, D_STATE), F32)
    xp, conv_p, ssm_p = _ssd(x_prompt, conv0, ssm0, p, nsub=2, ns=1)
    xs, conv_s, ssm_s = _ssd(x_sample, state_conv, state_ssm, p, nsub=1, ns=4)
    xp, xs = _moe_layer(xp, xs, p, 0, None)
    xp, _ = _sg(xp, p, seq_len=x_prompt.shape[1], want_v=False)
    xs, v_s = _sg(xs, p, seq_len=x_sample.shape[1], want_v=True)
    y_p, y_s = _moe_layer(xp, xs, p, 1, norm_final)
    return (y_p.reshape(x_prompt.shape), y_s.reshape(x_sample.shape), ssm_p, conv_p, ssm_s, conv_s,
            v_s.reshape((1,) + x_sample.shape[:2] + (SG_WIDTH,)))
```

```python
import functools
import math

import jax
import jax.numpy as jnp
import numpy as np
from jax import lax
from jax.experimental import pallas as pl
from jax.experimental.pallas import tpu as pltpu

F32 = jnp.float32
BF16 = jnp.bfloat16
I32 = jnp.int32

D_MODEL = 1024
N_HEADS = 32
HEAD_DIM = 64
N_GROUPS = 4
D_STATE = 128
D_INNER = N_HEADS * HEAD_DIM
GROUP_W = D_INNER // N_GROUPS
CONV_W = 4
CONV_DIM = D_INNER + 2 * N_GROUPS * D_STATE
DT_PAD = 128
SSD_CHUNK = 128
PROJ_CHUNK = 256
SG_WIDTH = 2 * D_MODEL
SG_GROUPS = 8
SG_GROUP_DIM = SG_WIDTH // SG_GROUPS
SG_CHUNK = 128
SG_ROWS = 512
MOE_GROUPS = 4
MOE_EPG = 8
MOE_EXPERTS = MOE_GROUPS * MOE_EPG
MOE_D_FF = 256
NORM_EPS = 1e-6
LN_EPS = 1e-5

MOE_TILE = 512
ROUTER_TILES = 2


def _expert_block(n_tokens):
    return 512 if 2 * n_tokens >= 512 * MOE_EXPERTS else 256


GRANULE = 16
LOCAL_ROWS = 2 * MOE_TILE + MOE_EXPERTS * GRANULE
N_LOCAL_GRAN = LOCAL_ROWS // GRANULE
VMEM_LIMIT = 56 * 1024 * 1024


def _sigmoid(x):
    return 0.5 * (jnp.tanh(0.5 * x) + 1.0)


def _silu(x):
    return x * _sigmoid(x)


def _softplus(x):
    return jnp.maximum(x, 0.0) + jnp.log(1.0 + jnp.exp(-jnp.abs(x)))


def _gelu_tanh(x):
    c = math.sqrt(2.0 / math.pi)
    t = jnp.tanh(x * (c + (c * 0.044715) * (x * x)))
    hx = 0.5 * x
    return hx + hx * t


def _rms(x, g):
    return x * lax.rsqrt(jnp.mean(x * x, axis=-1, keepdims=True) + NORM_EPS) * g


def _split3(x):
    a = x.astype(BF16)
    r = x - a.astype(F32)
    b = r.astype(BF16)
    c = (r - b.astype(F32)).astype(BF16)
    return a, b, c


def _dot(a, b):
    return jnp.dot(a, b, preferred_element_type=F32)


def _dot_nt(a, b):
    return lax.dot_general(a, b, (((1,), (1,)), ((), ())), preferred_element_type=F32)


def _dot_tn(a, b):
    return lax.dot_general(a, b, (((0,), (0,)), ((), ())), preferred_element_type=F32)


def _const_spec(shape):
    nd = len(shape)
    return pl.BlockSpec(shape, lambda *_: (0,) * nd)


def _ssd_kernel(x_ref, g_ref, win_ref, cw_ref, cb_ref, dtb_ref, alog_ref, dsk_ref, ng_ref,
                wout_ref, tri_ref, ones_ref, cin_ref, sin_ref,
                xo_ref, cout_ref, sout_ref,
                pend_x, pend_z, pend_xbc, pend_dt, xres_scr, zg_scr,
                prev_scr, xc_scr, yoff_scr, y_scr, xw_scr, hn_scr, yn_scr,
                *, nsub, ns, q, n_chunks):
    i = pl.program_id(0)
    first_chunk = lax.rem(jnp.maximum(i - 1, 0), n_chunks) == 0
    nseq = nsub * ns
    sb = ns * q
    r = nsub * sb

    @pl.when(i == 0)
    def _():
        pend_x[...] = jnp.zeros_like(pend_x)
        pend_z[...] = jnp.zeros_like(pend_z)
        pend_xbc[...] = jnp.zeros_like(pend_xbc)
        pend_dt[...] = jnp.zeros_like(pend_dt)

    tr = min(r, 128)
    row_tiles = [slice(a, a + tr) for a in range(0, r, tr)]

    def col_tiles(total, rows=tr):
        w = min(total, max(128, (32 * 1024) // rows))
        return [slice(c, c + w) for c in range(0, total, w)]

    for rt in row_tiles:
        for ct in col_tiles(D_MODEL):
            xres_scr[rt, ct] = pend_x[rt, ct]
        for ct in col_tiles(D_INNER):
            zg_scr[rt, ct] = _silu(pend_z[rt, ct])

    for s in range(nseq):
        for ct in col_tiles(D_MODEL, q):
            pend_x[s * q:(s + 1) * q, ct] = x_ref[s, :, ct]
    for rt in row_tiles:
        ss = None
        for ct in col_tiles(D_MODEL):
            xv = pend_x[rt, ct]
            part = jnp.sum(xv * xv, axis=-1, keepdims=True)
            ss = part if ss is None else ss + part
        scale = lax.rsqrt(ss * (1.0 / D_MODEL) + NORM_EPS)
        for ct in col_tiles(D_MODEL):
            hn_scr[rt, ct] = (pend_x[rt, ct] * scale * g_ref[:, ct]).astype(BF16)

    def proj_chunk(dst, dst_col, w_col, width):
        def run():
            dst[:, dst_col:dst_col + width] = _dot(hn_scr[...], win_ref[:, w_col:w_col + width])
        return run
    z_chunks = [proj_chunk(pend_z, c, c, PROJ_CHUNK) for c in range(0, D_INNER, PROJ_CHUNK)]
    scan_chunks = [proj_chunk(pend_xbc, c, D_INNER + c, PROJ_CHUNK)
                   for c in range(0, CONV_DIM, PROJ_CHUNK)]
    scan_chunks.append(proj_chunk(pend_dt, 0, D_INNER + CONV_DIM, DT_PAD))

    @pl.when(first_chunk)
    def _():
        sout_ref[...] = sin_ref[...]
        prev_scr[...] = jnp.zeros_like(prev_scr)
        for s in range(nseq):
            for k in range(CONV_W - 1):
                prev_scr[pl.ds(s * 8 + 5 + k, 1), :] = cin_ref[s, pl.ds(k, 1), :]

    sub8 = lax.broadcasted_iota(I32, (8, 1), 0)
    for s in range(nseq):
        srows = slice(s * q, (s + 1) * q)
        for ct in col_tiles(CONV_DIM, 4 * q):
            xq = pend_xbc[srows, ct]
            hist = prev_scr[s * 8:(s + 1) * 8, ct]
            acc = cb_ref[:, ct] + cw_ref[pl.ds(CONV_W - 1, 1), ct] * xq
            for j in range(1, CONV_W):
                sh = pltpu.roll(xq, j, 0)
                head = jnp.where(sub8 < j, pltpu.roll(hist, j, 0), sh[0:8, :])
                sh = head if q == 8 else jnp.concatenate([head, sh[8:, :]], axis=0)
                acc = acc + cw_ref[pl.ds(CONV_W - 1 - j, 1), ct] * sh
            xc_scr[srows, ct] = _silu(acc)
            last = xq[q - 8:q, :]
            prev_scr[s * 8:(s + 1) * 8, ct] = last
            for k in range(CONV_W - 1):
                cout_ref[s, pl.ds(k, 1), ct] = last[5 + k:6 + k, :]
    chunks = z_chunks + scan_chunks
    n_pairs = nsub * (N_HEADS // 2)
    emit_at = {}
    for k, ch in enumerate(chunks):
        emit_at.setdefault((k * n_pairs) // len(chunks), []).append(ch)

    tri = tri_ref[...]
    trib = tri.astype(BF16)
    onesb = ones_ref[...].astype(BF16)
    mask = tri > 0.5
    rowseq = lax.shift_right_logical(lax.broadcasted_iota(I32, (sb, 1), 0), int(math.log2(q)))
    lo = lax.broadcasted_iota(I32, (sb, 128), 1) < HEAD_DIM
    neg_a = -jnp.exp(alog_ref[...])

    for u in range(nsub):
        rows = slice(u * sb, (u + 1) * sb)
        xs_ref = xc_scr.at[rows, 0:D_INNER]

        def b_of(g):
            return xc_scr[rows, D_INNER + g * D_STATE:D_INNER + (g + 1) * D_STATE]

        def c_of(g):
            c0 = D_INNER + (N_GROUPS + g) * D_STATE
            return xc_scr[rows, c0:c0 + D_STATE].astype(BF16)

        dt = _softplus(pend_dt[rows, :] + dtb_ref[...])
        d1, d2, d3 = _split3(dt * neg_a)
        cs = _dot(trib, d1) + _dot(trib, d2) + _dot(trib, d3)
        cl = _dot(onesb, d1) + _dot(onesb, d2) + _dot(onesb, d3)
        ecs = jnp.exp(cs)
        wgt = dt * jnp.exp(cl - cs)
        ecl = jnp.exp(cl)
        cs_t = cs.T
        dt_t = dt.T

        for g in range(N_GROUPS):
            cg = c_of(g)
            acc = None
            for s in range(ns):
                st = sout_ref[u * ns + s, g * GROUP_W:(g + 1) * GROUP_W, :].astype(BF16)
                yo = _dot_nt(cg, st)
                if ns > 1:
                    yo = jnp.where(rowseq == s, yo, 0.0)
                acc = yo if acc is None else acc + yo
            yoff_scr[rows, g * GROUP_W:(g + 1) * GROUP_W] = acc

        for g in range(N_GROUPS):
            sc = _dot_nt(c_of(g), b_of(g).astype(BF16))
            for jj in range(N_HEADS // N_GROUPS // 2):
                j = g * (N_HEADS // N_GROUPS // 2) + jj
                ms = []
                for h in (2 * j, 2 * j + 1):
                    diff = cs[:, h:h + 1] - cs_t[h:h + 1, :]
                    dec = jnp.exp(jnp.where(mask, diff, -jnp.inf))
                    ms.append((sc * dec * dt_t[h:h + 1, :]).astype(BF16))
                lhs = jnp.concatenate(ms, axis=1)
                cols = slice(j * 128, (j + 1) * 128)
                xp = xs_ref[:, cols]
                rhs = jnp.concatenate([jnp.where(lo, xp, 0.0).astype(BF16),
                                       jnp.where(lo, 0.0, xp).astype(BF16)], axis=0)
                yd = _dot(lhs, rhs)
                ecs_p = jnp.where(lo, ecs[:, 2 * j:2 * j + 1], ecs[:, 2 * j + 1:2 * j + 2])
                wgt_p = jnp.where(lo, wgt[:, 2 * j:2 * j + 1], wgt[:, 2 * j + 1:2 * j + 2])
                y_scr[rows, cols] = yd + yoff_scr[rows, cols] * ecs_p + xp * dsk_ref[:, cols]
                xw_scr[rows, cols] = (xp * wgt_p).astype(BF16)
                pair = u * (N_HEADS // 2) + j
                for ch in emit_at.get(pair, []):
                    ch()

        for g in range(N_GROUPS):
            xwg = xw_scr[rows, g * GROUP_W:(g + 1) * GROUP_W]
            for s in range(ns):
                bg = b_of(g)
                if ns > 1:
                    bg = jnp.where(rowseq == s, bg, 0.0)
                upd = _dot_tn(xwg, bg.astype(BF16))
                for hh in range(GROUP_W // HEAD_DIM):
                    h = g * (GROUP_W // HEAD_DIM) + hh
                    dec = jnp.broadcast_to(ecl[s * q:s * q + 1, h:h + 1], (HEAD_DIM, D_STATE))
                    hrows = slice(h * HEAD_DIM, (h + 1) * HEAD_DIM)
                    sout_ref[u * ns + s, hrows, :] = (
                        sout_ref[u * ns + s, hrows, :] * dec + upd[hh * HEAD_DIM:(hh + 1) * HEAD_DIM, :])

    for rt in row_tiles:
        for g in range(N_GROUPS):
            cts = [slice(g * GROUP_W + c.start, g * GROUP_W + c.stop) for c in col_tiles(GROUP_W)]
            ss = None
            for ct in cts:
                yz = y_scr[rt, ct] * zg_scr[rt, ct]
                part = jnp.sum(yz * yz, axis=-1, keepdims=True)
                ss = part if ss is None else ss + part
            scale = lax.rsqrt(ss * (1.0 / GROUP_W) + NORM_EPS)
            for ct in cts:
                yn_scr[rt, ct] = (y_scr[rt, ct] * zg_scr[rt, ct] * scale * ng_ref[:, ct]).astype(BF16)

    for c0 in range(0, D_MODEL, PROJ_CHUNK):
        ct = slice(c0, c0 + PROJ_CHUNK)
        o = xres_scr[:, ct] + _dot(yn_scr[...], wout_ref[:, ct])
        for s in range(nseq):
            xo_ref[s, :, ct] = o[s * q:(s + 1) * q, :]


def _ssd_layer(x, conv_prev, ssm_prev, norm_g, w_in, conv_w, conv_b, dt_bias, a_log, d_skip,
               norm_y, w_out, *, nsub, ns, q):
    n_seq, seq_len, _ = x.shape
    nseq = nsub * ns
    sb = ns * q
    r = nsub * sb
    n_chunks = seq_len // q
    pad = DT_PAD - N_HEADS
    win = jnp.concatenate([w_in, jnp.zeros((D_MODEL, pad), F32)], axis=1).astype(BF16)
    dtb = jnp.pad(dt_bias, (0, pad)).reshape(1, DT_PAD)
    alog = jnp.pad(a_log, (0, pad)).reshape(1, DT_PAD)
    dsk = jnp.repeat(d_skip, HEAD_DIM).reshape(1, D_INNER)
    blk = np.kron(np.eye(ns), np.ones((q, q)))
    tri = jnp.asarray(blk * np.tril(np.ones((sb, sb))), F32)
    ones = jnp.asarray(blk, F32)
    state = ssm_prev.reshape(n_seq, D_INNER, D_STATE)
    conv_prev = conv_prev.reshape(n_seq, CONV_W - 1, CONV_DIM)

    kern = functools.partial(_ssd_kernel, nsub=nsub, ns=ns, q=q, n_chunks=n_chunks)
    out_shape = (jax.ShapeDtypeStruct(x.shape, F32),
                 jax.ShapeDtypeStruct((n_seq, CONV_W - 1, CONV_DIM), F32),
                 jax.ShapeDtypeStruct((n_seq, D_INNER, D_STATE), F32))
    n_steps = (n_seq // nseq) * n_chunks
    nxt = lambda i: jnp.minimum(i, n_steps - 1)
    cur = lambda i: jnp.maximum(i - 1, 0)
    in_row_spec = pl.BlockSpec((nseq, q, D_MODEL),
                               lambda i: (nxt(i) // n_chunks, nxt(i) % n_chunks, 0))
    row_spec = pl.BlockSpec((nseq, q, D_MODEL),
                            lambda i: (cur(i) // n_chunks, cur(i) % n_chunks, 0))
    conv_spec = pl.BlockSpec((nseq, CONV_W - 1, CONV_DIM), lambda i: (cur(i) // n_chunks, 0, 0))
    state_spec = pl.BlockSpec((nseq, D_INNER, D_STATE), lambda i: (cur(i) // n_chunks, 0, 0))
    in_specs = [in_row_spec,
                _const_spec((1, D_MODEL)),
                _const_spec(win.shape),
                _const_spec((CONV_W, CONV_DIM)),
                _const_spec((1, CONV_DIM)),
                _const_spec((1, DT_PAD)),
                _const_spec((1, DT_PAD)),
                _const_spec((1, D_INNER)),
                _const_spec((1, D_INNER)),
                _const_spec((D_INNER, D_MODEL)),
                _const_spec((sb, sb)),
                _const_spec((sb, sb)),
                conv_spec, state_spec]
    x_new, conv_new, state_new = pl.pallas_call(
        kern,
        grid=(n_steps + 1,),
        in_specs=in_specs,
        out_specs=(row_spec, conv_spec, state_spec),
        out_shape=out_shape,
        scratch_shapes=[pltpu.VMEM((r, D_MODEL), F32),
                        pltpu.VMEM((r, D_INNER), F32),
                        pltpu.VMEM((r, CONV_DIM), F32),
                        pltpu.VMEM((r, DT_PAD), F32),
                        pltpu.VMEM((r, D_MODEL), F32),
                        pltpu.VMEM((r, D_INNER), F32),
                        pltpu.VMEM((nseq * 8, CONV_DIM), F32),
                        pltpu.VMEM((r, CONV_DIM), F32),
                        pltpu.VMEM((r, D_INNER), F32),
                        pltpu.VMEM((r, D_INNER), F32),
                        pltpu.VMEM((r, D_INNER), BF16),
                        pltpu.VMEM((r, D_MODEL), BF16),
                        pltpu.VMEM((r, D_INNER), BF16)],
        compiler_params=pltpu.CompilerParams(
            dimension_semantics=("arbitrary",), vmem_limit_bytes=VMEM_LIMIT),
        name="ssd_layer",
    )(x, norm_g.reshape(1, D_MODEL), win, conv_w, conv_b.reshape(1, CONV_DIM), dtb, alog, dsk,
      norm_y.reshape(1, D_INNER), w_out.astype(BF16), tri, ones, conv_prev, state)
    return (x_new, conv_new.reshape(1, n_seq, CONV_W - 1, CONV_DIM),
            state_new.reshape(1, n_seq, N_HEADS, HEAD_DIM, D_STATE))


def _sg_kernel(x_ref, g_ref, win_ref, bin_ref, lng_ref, lnb_ref, wmix_ref, bmix_ref, wout_ref,
               xo_ref, *rest, r, want_v):
    v_ref = rest[0] if want_v else None
    hn_scr, uv_scr, vb_scr, um_scr = rest[-4:]
    row_tiles = [slice(a, a + 128) for a in range(0, r, 128)]
    col128 = lambda total: [slice(c, c + 128) for c in range(0, total, 128)]

    for rt in row_tiles:
        ss = None
        for ct in col128(D_MODEL):
            xv = x_ref[rt, ct]
            part = jnp.sum(xv * xv, axis=-1, keepdims=True)
            ss = part if ss is None else ss + part
        scale = lax.rsqrt(ss * (1.0 / D_MODEL) + NORM_EPS)
        for ct in col128(D_MODEL):
            hn_scr[rt, ct] = (x_ref[rt, ct] * scale * g_ref[:, ct]).astype(BF16)

    vcols = [slice(SG_WIDTH + c.start, SG_WIDTH + c.stop) for c in col128(SG_WIDTH)]

    def front(rt):
        def piece(c0):
            def run():
                h = _dot(hn_scr[rt, :], win_ref[:, c0:c0 + PROJ_CHUNK])
                for cc in range(0, PROJ_CHUNK, 128):
                    ct = slice(c0 + cc, c0 + cc + 128)
                    uv_scr[rt, ct] = _gelu_tanh(h[:, cc:cc + 128] + bin_ref[:, ct])
            return run
        return [piece(c0) for c0 in range(0, 2 * SG_WIDTH, PROJ_CHUNK)]

    def back(rt):
        stats = {}

        def ln_stats():
            tot = None
            for ct in vcols:
                part = jnp.sum(uv_scr[rt, ct], axis=-1, keepdims=True)
                tot = part if tot is None else tot + part
            mu = tot * (1.0 / SG_WIDTH)
            ss = None
            for ct in vcols:
                vc = uv_scr[rt, ct] - mu
                part = jnp.sum(vc * vc, axis=-1, keepdims=True)
                ss = part if ss is None else ss + part
            stats["mu"] = mu
            stats["scale"] = lax.rsqrt(ss * (1.0 / SG_WIDTH) + LN_EPS)

        def ln_apply(k0):
            def run():
                for ct, c in list(zip(vcols, col128(SG_WIDTH)))[k0:k0 + 4]:
                    vn = (uv_scr[rt, ct] - stats["mu"]) * stats["scale"] * lng_ref[:, c] + lnb_ref[:, c]
                    if want_v:
                        v_ref[rt, c] = vn
                    vb_scr[rt, c] = vn.astype(BF16)
            return run

        def mix(g):
            def run():
                cols = slice(g * SG_GROUP_DIM, (g + 1) * SG_GROUP_DIM)
                mixed = _dot(wmix_ref[g], vb_scr[rt, cols]) + bmix_ref[:, cols]
                um_scr[rt, cols] = (uv_scr[rt, cols] * mixed).astype(BF16)
            return run

        def out(c0):
            def run():
                ct = slice(c0, c0 + PROJ_CHUNK)
                xo_ref[rt, ct] = x_ref[rt, ct] + _dot(um_scr[rt, :], wout_ref[:, ct])
            return run

        return ([ln_stats] + [ln_apply(k) for k in range(0, len(vcols), 4)]
                + [mix(g) for g in range(SG_GROUPS)]
                + [out(c0) for c0 in range(0, D_MODEL, PROJ_CHUNK)])

    pending = []
    for rt in row_tiles:
        for piece in front(rt):
            piece()
            if pending:
                pending.pop(0)()
        for piece in pending:
            piece()
        pending = back(rt)
    for piece in pending:
        piece()


def _sg_layer(x2d, norm_g, w_in, b_in, ln_g, ln_b, w_s, b_s, w_out, *, seq_len, want_v):
    r = SG_ROWS
    t = x2d.shape[0]
    q = min(seq_len, SG_CHUNK)
    reps = SG_CHUNK // q
    ws = jnp.tril(w_s)[:, :q, :q]
    wmix = jnp.einsum("ab,gts->gatbs", jnp.eye(reps, dtype=F32), ws)
    wmix = wmix.reshape(SG_GROUPS, SG_CHUNK, SG_CHUNK)
    bmix = jnp.tile(jnp.repeat(b_s.T[:q], SG_GROUP_DIM, axis=1), (reps, 1))
    row_spec = pl.BlockSpec((r, D_MODEL), lambda i: (i, 0))
    v_spec = pl.BlockSpec((r, SG_WIDTH), lambda i: (i, 0))
    out_shape = [jax.ShapeDtypeStruct(x2d.shape, F32)]
    out_specs = [row_spec]
    if want_v:
        out_shape.append(jax.ShapeDtypeStruct((t, SG_WIDTH), F32))
        out_specs.append(v_spec)
    outs = pl.pallas_call(
        functools.partial(_sg_kernel, r=r, want_v=want_v),
        grid=(t // r,),
        scratch_shapes=[pltpu.VMEM((r, D_MODEL), BF16),
                        pltpu.VMEM((r, 2 * SG_WIDTH), F32),
                        pltpu.VMEM((r, SG_WIDTH), BF16),
                        pltpu.VMEM((r, SG_WIDTH), BF16)],
        in_specs=[row_spec,
                  _const_spec((1, D_MODEL)),
                  _const_spec((D_MODEL, 2 * SG_WIDTH)),
                  _const_spec((1, 2 * SG_WIDTH)),
                  _const_spec((1, SG_WIDTH)),
                  _const_spec((1, SG_WIDTH)),
                  _const_spec((SG_GROUPS, SG_CHUNK, SG_CHUNK)),
                  _const_spec((SG_CHUNK, SG_WIDTH)),
                  _const_spec((SG_WIDTH, D_MODEL))],
        out_specs=out_specs,
        out_shape=out_shape,
        compiler_params=pltpu.CompilerParams(
            dimension_semantics=("arbitrary",), vmem_limit_bytes=VMEM_LIMIT),
        name="sg_layer",
    )(x2d, norm_g.reshape(1, D_MODEL), w_in.astype(BF16), b_in.reshape(1, 2 * SG_WIDTH),
      ln_g.reshape(1, SG_WIDTH), ln_b.reshape(1, SG_WIDTH), wmix.astype(BF16), bmix,
      w_out.astype(BF16))
    return outs if want_v else (outs[0], None)


ROUTER_ROWS = 64


def _norm_tiles(use_a, xa_ref, xb_ref, g_ref, store):
    def load(rt, ct):
        return jnp.where(use_a, xa_ref[rt, ct], xb_ref[rt, ct])

    for r0 in range(0, xa_ref.shape[0], 128):
        rt = slice(r0, r0 + 128)
        ss = None
        for c0 in range(0, D_MODEL, 256):
            xv = load(rt, slice(c0, c0 + 256))
            part = jnp.sum(xv * xv, axis=-1, keepdims=True)
            ss = part if ss is None else ss + part
        scale = lax.rsqrt(ss * (1.0 / D_MODEL) + NORM_EPS)
        for c0 in range(0, D_MODEL, 256):
            ct = slice(c0, c0 + 256)
            store(rt, ct, load(rt, ct) * scale * g_ref[:, ct])


def _router_t_kernel(xa_ref, xb_ref, g_ref, wt_ref, bt_ref, usl_ref, lsl_ref,
                     meta_ref, metat_ref, pad_ref, h1_scr, h2_scr, *, steps_a):
    wt = wt_ref[...]
    w1 = wt.astype(BF16)
    w2 = (wt - w1.astype(F32)).astype(BF16)
    tt = MOE_TILE
    n = xa_ref.shape[0]

    def split_store(rt, ct, hn):
        hi = hn.astype(BF16)
        h1_scr[rt, ct] = hi
        h2_scr[rt, ct] = (hn - hi.astype(F32)).astype(BF16)

    _norm_tiles(pl.program_id(0) < steps_a, xa_ref, xb_ref, g_ref, split_store)

    h1 = h1_scr[...]
    logits = _dot_nt(w1, h1) + _dot_nt(w2, h1) + _dot_nt(w1, h2_scr[...])
    select = _select_experts(logits, bt_ref[...])
    for k in range(n // tt):
        cols = slice(k * tt, (k + 1) * tt)
        metat, pad = _local_order(select[:, cols], usl_ref[...], lsl_ref[...])
        metat_ref[k * 8:(k + 1) * 8, :] = metat
        full = jnp.concatenate([metat, jnp.zeros((128 - 8, tt), F32)], axis=0)
        meta_ref[cols, :] = full.T
        pad_ref[k * MOE_EXPERTS:(k + 1) * MOE_EXPERTS, :] = pad


def _select_experts(logits, bt):
    n = logits.shape[1]
    reps = n // bt.shape[1]
    logits = logits + (bt if reps == 1 else jnp.concatenate([bt] * reps, axis=1))
    tt = n
    row8 = lax.broadcasted_iota(I32, (8, tt), 0).astype(F32)
    gl = jnp.where(row8 < MOE_GROUPS, logits[0:8, :], -jnp.inf)
    gmax = jnp.max(gl, axis=0, keepdims=True)
    g_top = jnp.min(jnp.where(gl == gmax, row8, 8.0), axis=0, keepdims=True)
    p_g = 1.0 / jnp.sum(jnp.exp(gl - gmax), axis=0, keepdims=True)

    el = logits[8:16, :]
    for grp in range(1, MOE_GROUPS):
        el = jnp.where(g_top == grp, logits[8 + 8 * grp:16 + 8 * grp, :], el)
    emax = jnp.max(el, axis=0, keepdims=True)
    ee = jnp.exp(el - emax)
    prob = ee / jnp.sum(ee, axis=0, keepdims=True)
    p1 = jnp.max(prob, axis=0, keepdims=True)
    i1 = jnp.min(jnp.where(prob == p1, row8, 8.0), axis=0, keepdims=True)
    prob2 = jnp.where(row8 == i1, -1.0, prob)
    p2 = jnp.max(prob2, axis=0, keepdims=True)
    i2 = jnp.min(jnp.where(prob2 == p2, row8, 8.0), axis=0, keepdims=True)
    psum = p1 + p2
    gate1 = p_g * (p1 / psum)
    gate2 = p_g * (p2 / psum)
    e1 = g_top * MOE_EPG + i1
    e2 = g_top * MOE_EPG + i2
    out = jnp.where(row8 == 0, e1, 0.0)
    out = jnp.where(row8 == 1, e2, out)
    out = jnp.where(row8 == 2, gate1, out)
    return jnp.where(row8 == 3, gate2, out)


def _local_order(select, usl, lsl):
    tt = select.shape[1]
    row8 = lax.broadcasted_iota(I32, (8, tt), 0).astype(F32)
    e1, e2 = select[0:1, :], select[1:2, :]
    rowe = lax.broadcasted_iota(I32, (MOE_EXPERTS, tt), 0).astype(F32)
    sel1 = rowe == e1
    sel2 = rowe == e2
    onehot = jnp.where(sel1 | sel2, 1.0, 0.0)
    before = _dot(onehot.astype(BF16), usl)
    cnt = jnp.sum(onehot, axis=1, keepdims=True)
    pad = jnp.floor((cnt + (GRANULE - 1)) * (1.0 / GRANULE)) * GRANULE
    padb = jnp.broadcast_to(pad, (MOE_EXPERTS, tt))
    local = before + _dot(lsl, padb.astype(BF16))
    slot1 = jnp.sum(jnp.where(sel1, local, 0.0), axis=0, keepdims=True)
    slot2 = jnp.sum(jnp.where(sel2, local, 0.0), axis=0, keepdims=True)

    metat = jnp.where(row8 == 4, slot1, select)
    metat = jnp.where(row8 == 5, slot2, metat)
    return metat, padb[:, 0:128]


def _granule(ref, g):
    return ref.at[pl.ds(pl.multiple_of(g * GRANULE, GRANULE), GRANULE), :]


def _dispatch_kernel(ngran_ref, nv_ref, dest_ref, gap_ref, xa_ref, xb_ref, g_ref, meta_ref, xs_hbm,
                     buf, zbuf, hn_scr, sem, zsem, nstart, *, tiles_a):
    i = pl.program_id(0)
    n_tiles = pl.num_programs(0) - 1
    slot = i % 2
    tt = xa_ref.shape[0]
    bm = zbuf.shape[0]
    n_blocks = xs_hbm.shape[0] // bm

    def out_copy(sl, g, d):
        return pltpu.make_async_copy(_granule(buf.at[sl], g), _granule(xs_hbm, d), sem.at[sl])

    def drain(sl):
        def body(_, carry):
            out_copy(sl, 0, 0).wait()
            return carry
        lax.fori_loop(0, nstart[sl], body, 0)

    def tail_copy(b):
        return pltpu.make_async_copy(
            zbuf, xs_hbm.at[pl.ds(pl.multiple_of(b * bm, bm), bm), :], zsem.at[1])

    @pl.when(i == 0)
    def _():
        nstart[0] = 0
        nstart[1] = 0
        zbuf[...] = jnp.zeros_like(zbuf)

    def norm_store(rt, ct, hn):
        hn_scr[rt, ct] = hn.astype(BF16)

    tail = nv_ref[0] + i

    @pl.when((i >= 1) & (tail - 1 < n_blocks))
    def _():
        tail_copy(tail - 1).wait()

    @pl.when(i < n_tiles)
    def _():
        @pl.when(tail < n_blocks)
        def _():
            tail_copy(tail).start()
        _norm_tiles(i < tiles_a, xa_ref, xb_ref, g_ref, norm_store)
        drain(slot)
        hn = hn_scr[...]
        mt = meta_ref[...]
        for r0 in range(0, LOCAL_ROWS, 256):
            rows = (lax.broadcasted_iota(I32, (256, tt), 0) + r0).astype(F32)
            onehot = jnp.where((rows == mt[4:5, :]) | (rows == mt[5:6, :]), 1.0, 0.0).astype(BF16)
            buf[slot, r0:r0 + 256, :] = _dot(onehot, hn).astype(BF16)
        ng = ngran_ref[i]

        def body(g, carry):
            out_copy(slot, g, dest_ref[0, 0, g]).start()
            return carry
        lax.fori_loop(0, ng, body, 0)
        nstart[slot] = ng

    @pl.when(i == n_tiles)
    def _():
        drain(0)
        drain(1)

        def gap_copy(d):
            return pltpu.make_async_copy(_granule(zbuf, 0), _granule(xs_hbm, d), zsem.at[0])

        def each_gap(fn):
            def body(j, carry):
                d = gap_ref[0, 0, j]

                @pl.when(d >= 0)
                def _():
                    fn(gap_copy(d))
                return carry
            lax.fori_loop(0, gap_ref.shape[2], body, 0)

        def each_tail(fn):
            def body(b, carry):
                fn(tail_copy(b))
                return carry
            lax.fori_loop(nv_ref[0] + n_tiles, n_blocks, body, 0)

        each_gap(lambda cp: cp.start())
        each_tail(lambda cp: cp.start())
        each_gap(lambda cp: cp.wait())
        each_tail(lambda cp: cp.wait())


def _expert_kernel(bexp_ref, nv_ref, xs_ref, wg_ref, wu_ref, wd_ref, ys_ref,
                   wgu_b, wd_b, act_scr):
    b = pl.program_id(0)
    ff = wg_ref.shape[1]

    @pl.when(b < nv_ref[0])
    def _():
        @pl.when((b == 0) | (bexp_ref[b] != bexp_ref[jnp.maximum(b - 1, 0)]))
        def _():
            for k0 in range(0, wg_ref.shape[0], 256):
                wgu_b[k0:k0 + 256, 0:ff] = wg_ref[k0:k0 + 256, :].astype(BF16)
                wgu_b[k0:k0 + 256, ff:2 * ff] = wu_ref[k0:k0 + 256, :].astype(BF16)
            for k0 in range(0, ff, 64):
                wd_b[k0:k0 + 64, :] = wd_ref[k0:k0 + 64, :].astype(BF16)

        bm = xs_ref.shape[0]
        for m0 in range(0, bm, 256):
            h = _dot(xs_ref[m0:m0 + 256, :], wgu_b[...])
            for r0 in range(0, 256, 128):
                for c0 in range(0, ff, 128):
                    hg = h[r0:r0 + 128, c0:c0 + 128]
                    hu = h[r0:r0 + 128, ff + c0:ff + c0 + 128]
                    act_scr[m0 + r0:m0 + r0 + 128, c0:c0 + 128] = (_silu(hg) * hu).astype(BF16)
            ys_ref[m0:m0 + 256, :] = _dot(act_scr[m0:m0 + 256, :], wd_b[...]).astype(BF16)


def _combine_kernel(ngran_ref, src_ref, srcn_ref, xa_ref, xb_ref, meta_ref, gf_ref, ys_hbm,
                    oa_ref, ob_ref, buf, moe_scr, pick_scr, sem, *, final, tiles_a):
    i = pl.program_id(0)
    n = pl.num_programs(0)
    slot = i % 2
    tt = xa_ref.shape[0]

    def in_copy(sl, g, d):
        return pltpu.make_async_copy(_granule(ys_hbm, d), _granule(buf.at[sl], g), sem.at[sl])

    def gather(idx_ref, sl, ng):
        def body(g, carry):
            in_copy(sl, g, idx_ref[0, 0, g]).start()
            return carry
        lax.fori_loop(0, ng, body, 0)

    @pl.when(i == 0)
    def _():
        buf[...] = jnp.zeros_like(buf)
        gather(src_ref, 0, ngran_ref[0])

    @pl.when(i + 1 < n)
    def _():
        gather(srcn_ref, 1 - slot, ngran_ref[i + 1])

    def wait_body(_, carry):
        in_copy(slot, 0, 0).wait()
        return carry
    lax.fori_loop(0, ngran_ref[i], wait_body, 0)

    for r0 in range(0, tt, 128):
        rt = slice(r0, r0 + 128)
        meta = meta_ref[rt, :]
        for c0 in range(0, LOCAL_ROWS, 256):
            cols = (lax.broadcasted_iota(I32, (128, 256), 1) + c0).astype(F32)
            pick_scr[rt, c0:c0 + 256] = (
                jnp.where(cols == meta[:, 4:5], meta[:, 2:3], 0.0)
                + jnp.where(cols == meta[:, 5:6], meta[:, 3:4], 0.0)).astype(BF16)
    is_a = i < tiles_a
    for m0 in range(0, tt, 256):
        moe = _dot(pick_scr[m0:m0 + 256, :], buf[slot])
        for r0 in range(0, 256, 128):
            rt = slice(m0 + r0, m0 + r0 + 128)
            for c0 in range(0, D_MODEL, 256):
                ct = slice(c0, c0 + 256)
                moe_scr[rt, ct] = (jnp.where(is_a, xa_ref[rt, ct], xb_ref[rt, ct])
                                   + moe[r0:r0 + 128, ct])
    for r0 in range(0, tt, 128):
        rt = slice(r0, r0 + 128)
        if final:
            ss = None
            for c0 in range(0, D_MODEL, 256):
                yv = moe_scr[rt, c0:c0 + 256]
                part = jnp.sum(yv * yv, axis=-1, keepdims=True)
                ss = part if ss is None else ss + part
            scale = lax.rsqrt(ss * (1.0 / D_MODEL) + NORM_EPS)
            for c0 in range(0, D_MODEL, 256):
                ct = slice(c0, c0 + 256)
                moe_scr[rt, ct] = moe_scr[rt, ct] * scale * gf_ref[:, ct]

    def emit(o_ref):
        for r0 in range(0, tt, 128):
            for c0 in range(0, D_MODEL, 256):
                o_ref[r0:r0 + 128, c0:c0 + 256] = moe_scr[r0:r0 + 128, c0:c0 + 256]

    @pl.when(is_a)
    def _():
        emit(oa_ref)

    @pl.when(jnp.logical_not(is_a))
    def _():
        emit(ob_ref)


def _moe(xa, xb, norm_g, w_grp, b_grp, w_rt, b_rt, w_gate, w_up, w_down, norm_final, *, layer):
    tt = MOE_TILE
    tiles_a, tiles_b = xa.shape[0] // tt, xb.shape[0] // tt
    steps_a, steps_b = tiles_a // ROUTER_TILES, tiles_b // ROUTER_TILES
    n_tiles = tiles_a + tiles_b
    t = n_tiles * tt
    bm = _expert_block(t)
    n_exp = MOE_EXPERTS
    a_blk = lambda i, n_a: jnp.minimum(i, n_a - 1)
    b_blk = lambda i, n_a, n_b: jnp.clip(i - n_a, 0, n_b - 1)
    zrow = lambda n: jnp.zeros((n, D_MODEL), F32)
    wt = jnp.concatenate([w_grp.T, zrow(8 - MOE_GROUPS), w_rt.T, zrow(ROUTER_ROWS - 8 - n_exp)], axis=0)
    bt = jnp.concatenate([b_grp, jnp.zeros((8 - MOE_GROUPS,), F32), b_rt,
                          jnp.zeros((ROUTER_ROWS - 8 - n_exp,), F32)])
    bt = jnp.broadcast_to(bt[:, None], (ROUTER_ROWS, tt))
    usl = jnp.asarray(np.triu(np.ones((tt, tt)), 1), BF16)
    lsl = jnp.asarray(np.tril(np.ones((n_exp, n_exp)), -1), BF16)
    g2d = norm_g.reshape(1, D_MODEL)
    arb = pltpu.CompilerParams(dimension_semantics=("arbitrary",), vmem_limit_bytes=VMEM_LIMIT)

    meta, metat, pad = pl.pallas_call(
        functools.partial(_router_t_kernel, steps_a=steps_a),
        grid=(steps_a + steps_b,),
        in_specs=[pl.BlockSpec((ROUTER_TILES * tt, D_MODEL), lambda i: (a_blk(i, steps_a), 0)),
                  pl.BlockSpec((ROUTER_TILES * tt, D_MODEL),
                               lambda i: (b_blk(i, steps_a, steps_b), 0)),
                  _const_spec((1, D_MODEL)),
                  _const_spec((ROUTER_ROWS, D_MODEL)),
                  _const_spec((ROUTER_ROWS, tt)),
                  _const_spec((tt, tt)),
                  _const_spec((n_exp, n_exp))],
        out_specs=(pl.BlockSpec((ROUTER_TILES * tt, 128), lambda i: (i, 0)),
                   pl.BlockSpec((ROUTER_TILES * 8, tt), lambda i: (i, 0)),
                   pl.BlockSpec((ROUTER_TILES * n_exp, 128), lambda i: (i, 0))),
        out_shape=(jax.ShapeDtypeStruct((t, 128), F32),
                   jax.ShapeDtypeStruct((n_tiles * 8, tt), F32),
                   jax.ShapeDtypeStruct((n_tiles * n_exp, 128), F32)),
        scratch_shapes=[pltpu.VMEM((ROUTER_TILES * tt, D_MODEL), BF16),
                        pltpu.VMEM((ROUTER_TILES * tt, D_MODEL), BF16)],
        compiler_params=arb,
        name="moe_router",
    )(xa, xb, g2d, wt, bt, usl, lsl)

    runs = pad.reshape(n_tiles, n_exp, 128)[:, :, 0].astype(I32)
    rows_e = jnp.sum(runs, axis=0)
    nblk = (rows_e + bm - 1) // bm
    blk_end = jnp.cumsum(nblk)
    e_start = (blk_end - nblk) * bm
    n_valid = blk_end[-1]
    run_end = jnp.cumsum(runs, axis=1)
    ngran = (run_end[:, -1] // GRANULE).astype(I32)
    shift = e_start[None, :] + (jnp.cumsum(runs, axis=0) - runs) - (run_end - runs)
    g_row = jnp.arange(N_LOCAL_GRAN, dtype=I32) * GRANULE
    e_of_g = jnp.sum((run_end[:, None, :] <= g_row[None, :, None]).astype(I32), axis=-1)
    shift_g = jnp.sum(jnp.where(e_of_g[..., None] == jnp.arange(n_exp, dtype=I32),
                                shift[:, None, :], 0), axis=-1)
    dest = jnp.where(e_of_g < n_exp, (shift_g + g_row[None, :]) // GRANULE, 0)
    dest = dest.astype(I32).reshape(n_tiles, 1, N_LOCAL_GRAN)
    per_blk = bm // GRANULE
    gap = ((e_start + rows_e) // GRANULE)[:, None] + jnp.arange(per_blk, dtype=I32)[None, :]
    gap = jnp.where(gap < ((e_start + nblk * bm) // GRANULE)[:, None], gap, -1)
    gap = gap.astype(I32).reshape(1, 1, n_exp * per_blk)
    n_blocks = (2 * t + n_tiles * n_exp * (GRANULE - 1)) // bm + 1 + n_exp
    blk = jnp.minimum(jnp.arange(n_blocks, dtype=I32), n_valid - 1)
    block_expert = jnp.sum((blk[:, None] >= blk_end[None, :]).astype(I32), axis=1).astype(I32)
    n_valid = n_valid.reshape(1).astype(I32)

    last = n_tiles - 1
    xs = pl.pallas_call(
        functools.partial(_dispatch_kernel, tiles_a=tiles_a),
        grid_spec=pltpu.PrefetchScalarGridSpec(
            num_scalar_prefetch=2,
            grid=(n_tiles + 1,),
            in_specs=[pl.BlockSpec((1, 1, N_LOCAL_GRAN),
                                   lambda i, ng, nv: (jnp.minimum(i, last), 0, 0),
                                   memory_space=pltpu.SMEM),
                      pl.BlockSpec((1, 1, n_exp * per_blk), lambda i, ng, nv: (0, 0, 0),
                                   memory_space=pltpu.SMEM),
                      pl.BlockSpec((tt, D_MODEL), lambda i, ng, nv: (a_blk(i, tiles_a), 0)),
                      pl.BlockSpec((tt, D_MODEL),
                                   lambda i, ng, nv: (b_blk(i, tiles_a, tiles_b), 0)),
                      _const_spec((1, D_MODEL)),
                      pl.BlockSpec((8, tt), lambda i, ng, nv: (jnp.minimum(i, last), 0))],
            out_specs=pl.BlockSpec(memory_space=pl.ANY),
            scratch_shapes=[pltpu.VMEM((2, LOCAL_ROWS, D_MODEL), BF16),
                            pltpu.VMEM((bm, D_MODEL), BF16),
                            pltpu.VMEM((tt, D_MODEL), BF16),
                            pltpu.SemaphoreType.DMA((2,)),
                            pltpu.SemaphoreType.DMA((2,)),
                            pltpu.SMEM((2,), I32)]),
        out_shape=jax.ShapeDtypeStruct((n_blocks * bm, D_MODEL), BF16),
        compiler_params=arb,
        name="moe_dispatch",
    )(ngran, n_valid, dest, gap, xa, xb, g2d, metat)

    w_spec = lambda shape: pl.BlockSpec((None, None) + shape,
                                        lambda b, be, nv: (layer, be[b], 0, 0))
    row_blk = pl.BlockSpec((bm, D_MODEL), lambda b, be, nv: (jnp.minimum(b, nv[0] - 1), 0))
    ys = pl.pallas_call(
        _expert_kernel,
        grid_spec=pltpu.PrefetchScalarGridSpec(
            num_scalar_prefetch=2,
            grid=(n_blocks,),
            in_specs=[row_blk,
                      w_spec((D_MODEL, MOE_D_FF)),
                      w_spec((D_MODEL, MOE_D_FF)),
                      w_spec((MOE_D_FF, D_MODEL))],
            out_specs=row_blk,
            scratch_shapes=[pltpu.VMEM((D_MODEL, 2 * MOE_D_FF), BF16),
                            pltpu.VMEM((MOE_D_FF, D_MODEL), BF16),
                            pltpu.VMEM((bm, MOE_D_FF), BF16)]),
        out_shape=jax.ShapeDtypeStruct((n_blocks * bm, D_MODEL), BF16),
        input_output_aliases={2: 0},
        compiler_params=arb,
        name="moe_experts",
    )(block_expert, n_valid, xs, w_gate, w_up, w_down)

    final = norm_final is not None
    gf = (norm_final if final else jnp.ones((D_MODEL,), F32)).reshape(1, D_MODEL)
    src_spec = lambda nxt: pl.BlockSpec(
        (1, 1, N_LOCAL_GRAN), lambda i, ng: (jnp.minimum(i + nxt, last), 0, 0),
        memory_space=pltpu.SMEM)
    a_spec = pl.BlockSpec((tt, D_MODEL), lambda i, ng: (a_blk(i, tiles_a), 0))
    b_spec = pl.BlockSpec((tt, D_MODEL), lambda i, ng: (b_blk(i, tiles_a, tiles_b), 0))
    out_a, out_b = pl.pallas_call(
        functools.partial(_combine_kernel, final=final, tiles_a=tiles_a),
        grid_spec=pltpu.PrefetchScalarGridSpec(
            num_scalar_prefetch=1,
            grid=(n_tiles,),
            in_specs=[src_spec(0), src_spec(1), a_spec, b_spec,
                      pl.BlockSpec((tt, 128), lambda i, ng: (i, 0)),
                      _const_spec((1, D_MODEL)),
                      pl.BlockSpec(memory_space=pl.ANY)],
            out_specs=(a_spec, b_spec),
            scratch_shapes=[pltpu.VMEM((2, LOCAL_ROWS, D_MODEL), BF16),
                            pltpu.VMEM((tt, D_MODEL), F32),
                            pltpu.VMEM((tt, LOCAL_ROWS), BF16),
                            pltpu.SemaphoreType.DMA((2,))]),
        out_shape=(jax.ShapeDtypeStruct(xa.shape, F32), jax.ShapeDtypeStruct(xb.shape, F32)),
        compiler_params=arb,
        name="moe_combine",
    )(ngran, dest, dest, xa, xb, meta, gf, ys)
    return out_a, out_b


def _ssd(x, conv_prev, ssm_prev, p, *, nsub, ns):
    seq_len = x.shape[1]
    q = seq_len if seq_len < SSD_CHUNK else SSD_CHUNK
    x, conv_new, ssm_new = _ssd_layer(
        x, conv_prev, ssm_prev, p["norm_mix"][0], p["ssd_w_in"][0], p["ssd_conv_w"][0],
        p["ssd_conv_b"][0], p["ssd_dt_bias"][0], p["ssd_a_log"][0], p["ssd_d"][0], p["ssd_norm"][0],
        p["ssd_w_out"][0], nsub=nsub, ns=ns, q=q)
    return x.reshape(-1, D_MODEL), conv_new, ssm_new


def _sg(x2d, p, *, seq_len, want_v):
    return _sg_layer(x2d, p["norm_mix"][1], p["sg_w_in"][0], p["sg_b_in"][0], p["sg_ln_g"][0],
                     p["sg_ln_b"][0], p["sg_w_s"][0], p["sg_b_s"][0], p["sg_w_out"][0],
                     seq_len=seq_len, want_v=want_v)


def _moe_layer(xa, xb, p, layer, norm_final):
    return _moe(xa, xb, p["norm_ffn"][layer], p["moe_w_group"][layer], p["moe_b_group"][layer],
                p["moe_w_router"][layer], p["moe_b_router"][layer], p["moe_w_gate"], p["moe_w_up"],
                p["moe_w_down"], norm_final, layer=layer)


def kernel(x_prompt, x_sample, state_ssm, state_conv, norm_mix, norm_ffn, norm_final, ssd_w_in, ssd_conv_w, ssd_conv_b, ssd_dt_bias, ssd_a_log, ssd_d, ssd_norm, ssd_w_out, sg_w_in, sg_b_in, sg_ln_g, sg_ln_b, sg_w_s, sg_b_s, sg_w_out, moe_w_group, moe_b_group, moe_w_router, moe_b_router, moe_w_gate, moe_w_up, moe_w_down):
    p = dict(norm_mix=norm_mix, norm_ffn=norm_ffn, norm_final=norm_final, ssd_w_in=ssd_w_in,
             ssd_conv_w=ssd_conv_w, ssd_conv_b=ssd_conv_b, ssd_dt_bias=ssd_dt_bias,
             ssd_a_log=ssd_a_log, ssd_d=ssd_d, ssd_norm=ssd_norm, ssd_w_out=ssd_w_out,
             sg_w_in=sg_w_in, sg_b_in=sg_b_in, sg_ln_g=sg_ln_g, sg_ln_b=sg_ln_b, sg_w_s=sg_w_s,
             sg_b_s=sg_b_s, sg_w_out=sg_w_out, moe_w_group=moe_w_group, moe_b_group=moe_b_group,
             moe_w_router=moe_w_router, moe_b_router=moe_b_router, moe_w_gate=moe_w_gate,
             moe_w_up=moe_w_up, moe_w_down=moe_w_down)
    nb = x_prompt.shape[0]
    conv0 = jnp.zeros((1, nb, CONV_W - 1, CONV_DIM), F32)
    ssm0 = jnp.zeros((1, nb, N_HEADS, HEAD_DIM, D_STATE), F32)
    xp, conv_p, ssm_p = _ssd(x_prompt, conv0, ssm0, p, nsub=2, ns=1)
    xs, conv_s, ssm_s = _ssd(x_sample, state_conv, state_ssm, p, nsub=1, ns=4)
    xp, xs = _moe_layer(xp, xs, p, 0, None)
    xp, _ = _sg(xp, p, seq_len=x_prompt.shape[1], want_v=False)
    xs, v_s = _sg(xs, p, seq_len=x_sample.shape[1], want_v=True)
    y_p, y_s = _moe_layer(xp, xs, p, 1, norm_final)
    return (y_p.reshape(x_prompt.shape), y_s.reshape(x_sample.shape), ssm_p, conv_p, ssm_s, conv_s,
            v_s.reshape((1,) + x_sample.shape[:2] + (SG_WIDTH,)))
```

```python
import functools
import math

import jax
import jax.numpy as jnp
import numpy as np
from jax import lax
from jax.experimental import pallas as pl
from jax.experimental.pallas import tpu as pltpu

F32 = jnp.float32
BF16 = jnp.bfloat16
I32 = jnp.int32

D_MODEL = 1024
N_HEADS = 32
HEAD_DIM = 64
N_GROUPS = 4
D_STATE = 128
D_INNER = N_HEADS * HEAD_DIM
GROUP_W = D_INNER // N_GROUPS
CONV_W = 4
CONV_DIM = D_INNER + 2 * N_GROUPS * D_STATE
DT_PAD = 128
SSD_CHUNK = 128
PROJ_CHUNK = 256
SG_WIDTH = 2 * D_MODEL
SG_GROUPS = 8
SG_GROUP_DIM = SG_WIDTH // SG_GROUPS
SG_CHUNK = 128
SG_ROWS = 512
MOE_GROUPS = 4
MOE_EPG = 8
MOE_EXPERTS = MOE_GROUPS * MOE_EPG
MOE_D_FF = 256
NORM_EPS = 1e-6
LN_EPS = 1e-5

MOE_TILE = 512
ROUTER_TILES = 2


def _expert_block(n_tokens):
    return 512 if 2 * n_tokens >= 512 * MOE_EXPERTS else 256


GRANULE = 16
LOCAL_ROWS = 2 * MOE_TILE + MOE_EXPERTS * GRANULE
N_LOCAL_GRAN = LOCAL_ROWS // GRANULE
VMEM_LIMIT = 56 * 1024 * 1024


def _sigmoid(x):
    return 0.5 * (jnp.tanh(0.5 * x) + 1.0)


def _silu(x):
    return x * _sigmoid(x)


def _softplus(x):
    return jnp.maximum(x, 0.0) + jnp.log(1.0 + jnp.exp(-jnp.abs(x)))


def _gelu_tanh(x):
    c = math.sqrt(2.0 / math.pi)
    t = jnp.tanh(x * (c + (c * 0.044715) * (x * x)))
    hx = 0.5 * x
    return hx + hx * t


def _rms(x, g):
    return x * lax.rsqrt(jnp.mean(x * x, axis=-1, keepdims=True) + NORM_EPS) * g


def _split3(x):
    a = x.astype(BF16)
    r = x - a.astype(F32)
    b = r.astype(BF16)
    c = (r - b.astype(F32)).astype(BF16)
    return a, b, c


def _dot(a, b):
    return jnp.dot(a, b, preferred_element_type=F32)


def _dot_nt(a, b):
    return lax.dot_general(a, b, (((1,), (1,)), ((), ())), preferred_element_type=F32)


def _dot_tn(a, b):
    return lax.dot_general(a, b, (((0,), (0,)), ((), ())), preferred_element_type=F32)


def _const_spec(shape):
    nd = len(shape)
    return pl.BlockSpec(shape, lambda *_: (0,) * nd)


def _ssd_kernel(x_ref, g_ref, win_ref, cw_ref, cb_ref, dtb_ref, alog_ref, dsk_ref, ng_ref,
                wout_ref, tri_ref, ones_ref, cin_ref, sin_ref,
                xo_ref, cout_ref, sout_ref,
                pend_x, pend_z, pend_xbc, pend_dt, xres_scr, zg_scr,
                prev_scr, xc_scr, yoff_scr, y_scr, xw_scr, hn_scr, yn_scr,
                *, nsub, ns, q, n_chunks):
    i = pl.program_id(0)
    first_chunk = lax.rem(jnp.maximum(i - 1, 0), n_chunks) == 0
    nseq = nsub * ns
    sb = ns * q
    r = nsub * sb

    @pl.when(i == 0)
    def _():
        pend_x[...] = jnp.zeros_like(pend_x)
        pend_z[...] = jnp.zeros_like(pend_z)
        pend_xbc[...] = jnp.zeros_like(pend_xbc)
        pend_dt[...] = jnp.zeros_like(pend_dt)

    tr = min(r, 128)
    row_tiles = [slice(a, a + tr) for a in range(0, r, tr)]

    def col_tiles(total, rows=tr):
        w = min(total, max(128, (32 * 1024) // rows))
        return [slice(c, c + w) for c in range(0, total, w)]

    for rt in row_tiles:
        for ct in col_tiles(D_MODEL):
            xres_scr[rt, ct] = pend_x[rt, ct]

    def gate_tile(rt, ct):
        def run():
            zg_scr[rt, ct] = _silu(pend_z[rt, ct])
        return run
    gate_tiles = [gate_tile(rt, ct) for rt in row_tiles for ct in col_tiles(D_INNER)]

    for s in range(nseq):
        for ct in col_tiles(D_MODEL, q):
            pend_x[s * q:(s + 1) * q, ct] = x_ref[s, :, ct]
    for rt in row_tiles:
        ss = None
        for ct in col_tiles(D_MODEL):
            xv = pend_x[rt, ct]
            part = jnp.sum(xv * xv, axis=-1, keepdims=True)
            ss = part if ss is None else ss + part
        scale = lax.rsqrt(ss * (1.0 / D_MODEL) + NORM_EPS)
        for ct in col_tiles(D_MODEL):
            hn_scr[rt, ct] = (pend_x[rt, ct] * scale * g_ref[:, ct]).astype(BF16)

    def proj_chunk(dst, dst_col, w_col, width):
        def run():
            dst[:, dst_col:dst_col + width] = _dot(hn_scr[...], win_ref[:, w_col:w_col + width])
        return run
    z_chunks = [proj_chunk(pend_z, c, c, PROJ_CHUNK) for c in range(0, D_INNER, PROJ_CHUNK)]
    xbc_chunks = [proj_chunk(pend_xbc, c, D_INNER + c, PROJ_CHUNK)
                  for c in range(0, CONV_DIM, PROJ_CHUNK)]
    dt_chunk = proj_chunk(pend_dt, 0, D_INNER + CONV_DIM, DT_PAD)

    @pl.when(first_chunk)
    def _():
        sout_ref[...] = sin_ref[...]
        prev_scr[...] = jnp.zeros_like(prev_scr)
        for s in range(nseq):
            for k in range(CONV_W - 1):
                prev_scr[pl.ds(s * 8 + 5 + k, 1), :] = cin_ref[s, pl.ds(k, 1), :]

    sub8 = lax.broadcasted_iota(I32, (8, 1), 0)
    for s in range(nseq):
        srows = slice(s * q, (s + 1) * q)
        for ct in col_tiles(CONV_DIM, 4 * q):
            xq = pend_xbc[srows, ct]
            hist = prev_scr[s * 8:(s + 1) * 8, ct]
            acc = cb_ref[:, ct] + cw_ref[pl.ds(CONV_W - 1, 1), ct] * xq
            for j in range(1, CONV_W):
                sh = pltpu.roll(xq, j, 0)
                head = jnp.where(sub8 < j, pltpu.roll(hist, j, 0), sh[0:8, :])
                sh = head if q == 8 else jnp.concatenate([head, sh[8:, :]], axis=0)
                acc = acc + cw_ref[pl.ds(CONV_W - 1 - j, 1), ct] * sh
            xc_scr[srows, ct] = _silu(acc)
            last = xq[q - 8:q, :]
            prev_scr[s * 8:(s + 1) * 8, ct] = last
            for k in range(CONV_W - 1):
                cout_ref[s, pl.ds(k, 1), ct] = last[5 + k:6 + k, :]
    chunks = xbc_chunks + z_chunks + [dt_chunk]
    n_pairs = nsub * (N_HEADS // 2)
    emit_at = {}
    for k, tile in enumerate(gate_tiles):
        emit_at.setdefault(k, []).append(tile)
    for k, ch in enumerate(chunks):
        emit_at.setdefault((k * n_pairs) // len(chunks), []).append(ch)
    first_z = (len(xbc_chunks) * n_pairs) // len(chunks)
    assert len(gate_tiles) <= first_z, "gate tiles must be emitted before pend_z is overwritten"

    tri = tri_ref[...]
    trib = tri.astype(BF16)
    onesb = ones_ref[...].astype(BF16)
    mask = tri > 0.5
    rowseq = lax.shift_right_logical(lax.broadcasted_iota(I32, (sb, 1), 0), int(math.log2(q)))
    lo = lax.broadcasted_iota(I32, (sb, 128), 1) < HEAD_DIM
    neg_a = -jnp.exp(alog_ref[...])

    for u in range(nsub):
        rows = slice(u * sb, (u + 1) * sb)
        xs_ref = xc_scr.at[rows, 0:D_INNER]

        def b_of(g):
            return xc_scr[rows, D_INNER + g * D_STATE:D_INNER + (g + 1) * D_STATE]

        def c_of(g):
            c0 = D_INNER + (N_GROUPS + g) * D_STATE
            return xc_scr[rows, c0:c0 + D_STATE].astype(BF16)

        dt = _softplus(pend_dt[rows, :] + dtb_ref[...])
        d1, d2, d3 = _split3(dt * neg_a)
        cs = _dot(trib, d1) + _dot(trib, d2) + _dot(trib, d3)
        cl = _dot(onesb, d1) + _dot(onesb, d2) + _dot(onesb, d3)
        ecs = jnp.exp(cs)
        wgt = dt * jnp.exp(cl - cs)
        ecl = jnp.exp(cl)
        cs_t = cs.T
        dt_t = dt.T

        for g in range(N_GROUPS):
            cg = c_of(g)
            acc = None
            for s in range(ns):
                st = sout_ref[u * ns + s, g * GROUP_W:(g + 1) * GROUP_W, :].astype(BF16)
                yo = _dot_nt(cg, st)
                if ns > 1:
                    yo = jnp.where(rowseq == s, yo, 0.0)
                acc = yo if acc is None else acc + yo
            yoff_scr[rows, g * GROUP_W:(g + 1) * GROUP_W] = acc

        for g in range(N_GROUPS):
            sc = _dot_nt(c_of(g), b_of(g).astype(BF16))
            for jj in range(N_HEADS // N_GROUPS // 2):
                j = g * (N_HEADS // N_GROUPS // 2) + jj
                ms = []
                for h in (2 * j, 2 * j + 1):
                    diff = cs[:, h:h + 1] - cs_t[h:h + 1, :]
                    dec = jnp.exp(jnp.where(mask, diff, -jnp.inf))
                    ms.append((sc * dec * dt_t[h:h + 1, :]).astype(BF16))
                lhs = jnp.concatenate(ms, axis=1)
                cols = slice(j * 128, (j + 1) * 128)
                xp = xs_ref[:, cols]
                rhs = jnp.concatenate([jnp.where(lo, xp, 0.0).astype(BF16),
                                       jnp.where(lo, 0.0, xp).astype(BF16)], axis=0)
                yd = _dot(lhs, rhs)
                ecs_p = jnp.where(lo, ecs[:, 2 * j:2 * j + 1], ecs[:, 2 * j + 1:2 * j + 2])
                wgt_p = jnp.where(lo, wgt[:, 2 * j:2 * j + 1], wgt[:, 2 * j + 1:2 * j + 2])
                y_scr[rows, cols] = yd + yoff_scr[rows, cols] * ecs_p + xp * dsk_ref[:, cols]
                xw_scr[rows, cols] = (xp * wgt_p).astype(BF16)
                pair = u * (N_HEADS // 2) + j
                for ch in emit_at.get(pair, []):
                    ch()

        for g in range(N_GROUPS):
            xwg = xw_scr[rows, g * GROUP_W:(g + 1) * GROUP_W]
            for s in range(ns):
                bg = b_of(g)
                if ns > 1:
                    bg = jnp.where(rowseq == s, bg, 0.0)
                upd = _dot_tn(xwg, bg.astype(BF16))
                for hh in range(GROUP_W // HEAD_DIM):
                    h = g * (GROUP_W // HEAD_DIM) + hh
                    dec = jnp.broadcast_to(ecl[s * q:s * q + 1, h:h + 1], (HEAD_DIM, D_STATE))
                    hrows = slice(h * HEAD_DIM, (h + 1) * HEAD_DIM)
                    sout_ref[u * ns + s, hrows, :] = (
                        sout_ref[u * ns + s, hrows, :] * dec + upd[hh * HEAD_DIM:(hh + 1) * HEAD_DIM, :])

    for rt in row_tiles:
        for g in range(N_GROUPS):
            cts = [slice(g * GROUP_W + c.start, g * GROUP_W + c.stop) for c in col_tiles(GROUP_W)]
            ss = None
            for ct in cts:
                yz = y_scr[rt, ct] * zg_scr[rt, ct]
                part = jnp.sum(yz * yz, axis=-1, keepdims=True)
                ss = part if ss is None else ss + part
            scale = lax.rsqrt(ss * (1.0 / GROUP_W) + NORM_EPS)
            for ct in cts:
                yn_scr[rt, ct] = (y_scr[rt, ct] * zg_scr[rt, ct] * scale * ng_ref[:, ct]).astype(BF16)
        for c0 in range(0, D_MODEL, PROJ_CHUNK):
            ct = slice(c0, c0 + PROJ_CHUNK)
            o = xres_scr[rt, ct] + _dot(yn_scr[rt, :], wout_ref[:, ct])
            for s in range(rt.start // q, rt.stop // q):
                xo_ref[s, :, ct] = o[s * q - rt.start:(s + 1) * q - rt.start, :]


def _ssd_layer(x, conv_prev, ssm_prev, norm_g, w_in, conv_w, conv_b, dt_bias, a_log, d_skip,
               norm_y, w_out, *, nsub, ns, q):
    n_seq, seq_len, _ = x.shape
    nseq = nsub * ns
    sb = ns * q
    r = nsub * sb
    n_chunks = seq_len // q
    pad = DT_PAD - N_HEADS
    win = jnp.concatenate([w_in, jnp.zeros((D_MODEL, pad), F32)], axis=1).astype(BF16)
    dtb = jnp.pad(dt_bias, (0, pad)).reshape(1, DT_PAD)
    alog = jnp.pad(a_log, (0, pad)).reshape(1, DT_PAD)
    dsk = jnp.repeat(d_skip, HEAD_DIM).reshape(1, D_INNER)
    blk = np.kron(np.eye(ns), np.ones((q, q)))
    tri = jnp.asarray(blk * np.tril(np.ones((sb, sb))), F32)
    ones = jnp.asarray(blk, F32)
    state = ssm_prev.reshape(n_seq, D_INNER, D_STATE)
    conv_prev = conv_prev.reshape(n_seq, CONV_W - 1, CONV_DIM)

    kern = functools.partial(_ssd_kernel, nsub=nsub, ns=ns, q=q, n_chunks=n_chunks)
    out_shape = (jax.ShapeDtypeStruct(x.shape, F32),
                 jax.ShapeDtypeStruct((n_seq, CONV_W - 1, CONV_DIM), F32),
                 jax.ShapeDtypeStruct((n_seq, D_INNER, D_STATE), F32))
    n_steps = (n_seq // nseq) * n_chunks
    nxt = lambda i: jnp.minimum(i, n_steps - 1)
    cur = lambda i: jnp.maximum(i - 1, 0)
    in_row_spec = pl.BlockSpec((nseq, q, D_MODEL),
                               lambda i: (nxt(i) // n_chunks, nxt(i) % n_chunks, 0))
    row_spec = pl.BlockSpec((nseq, q, D_MODEL),
                            lambda i: (cur(i) // n_chunks, cur(i) % n_chunks, 0))
    conv_spec = pl.BlockSpec((nseq, CONV_W - 1, CONV_DIM), lambda i: (cur(i) // n_chunks, 0, 0))
    state_spec = pl.BlockSpec((nseq, D_INNER, D_STATE), lambda i: (cur(i) // n_chunks, 0, 0))
    in_specs = [in_row_spec,
                _const_spec((1, D_MODEL)),
                _const_spec(win.shape),
                _const_spec((CONV_W, CONV_DIM)),
                _const_spec((1, CONV_DIM)),
                _const_spec((1, DT_PAD)),
                _const_spec((1, DT_PAD)),
                _const_spec((1, D_INNER)),
                _const_spec((1, D_INNER)),
                _const_spec((D_INNER, D_MODEL)),
                _const_spec((sb, sb)),
                _const_spec((sb, sb)),
                conv_spec, state_spec]
    x_new, conv_new, state_new = pl.pallas_call(
        kern,
        grid=(n_steps + 1,),
        in_specs=in_specs,
        out_specs=(row_spec, conv_spec, state_spec),
        out_shape=out_shape,
        scratch_shapes=[pltpu.VMEM((r, D_MODEL), F32),
                        pltpu.VMEM((r, D_INNER), F32),
                        pltpu.VMEM((r, CONV_DIM), F32),
                        pltpu.VMEM((r, DT_PAD), F32),
                        pltpu.VMEM((r, D_MODEL), F32),
                        pltpu.VMEM((r, D_INNER), F32),
                        pltpu.VMEM((nseq * 8, CONV_DIM), F32),
                        pltpu.VMEM((r, CONV_DIM), F32),
                        pltpu.VMEM((r, D_INNER), F32),
                        pltpu.VMEM((r, D_INNER), F32),
                        pltpu.VMEM((r, D_INNER), BF16),
                        pltpu.VMEM((r, D_MODEL), BF16),
                        pltpu.VMEM((r, D_INNER), BF16)],
        compiler_params=pltpu.CompilerParams(
            dimension_semantics=("arbitrary",), vmem_limit_bytes=VMEM_LIMIT),
        name="ssd_layer",
    )(x, norm_g.reshape(1, D_MODEL), win, conv_w, conv_b.reshape(1, CONV_DIM), dtb, alog, dsk,
      norm_y.reshape(1, D_INNER), w_out.astype(BF16), tri, ones, conv_prev, state)
    return (x_new, conv_new.reshape(1, n_seq, CONV_W - 1, CONV_DIM),
            state_new.reshape(1, n_seq, N_HEADS, HEAD_DIM, D_STATE))


def _sg_kernel(x_ref, g_ref, win_ref, bin_ref, lng_ref, lnb_ref, wmix_ref, bmix_ref, wout_ref,
               xo_ref, *rest, r, want_v):
    v_ref = rest[0] if want_v else None
    hn_scr, uv_scr, vb_scr, um_scr = rest[-4:]
    row_tiles = [slice(a, a + 128) for a in range(0, r, 128)]
    col128 = lambda total: [slice(c, c + 128) for c in range(0, total, 128)]

    for rt in row_tiles:
        ss = None
        for ct in col128(D_MODEL):
            xv = x_ref[rt, ct]
            part = jnp.sum(xv * xv, axis=-1, keepdims=True)
            ss = part if ss is None else ss + part
        scale = lax.rsqrt(ss * (1.0 / D_MODEL) + NORM_EPS)
        for ct in col128(D_MODEL):
            hn_scr[rt, ct] = (x_ref[rt, ct] * scale * g_ref[:, ct]).astype(BF16)

    vcols = [slice(SG_WIDTH + c.start, SG_WIDTH + c.stop) for c in col128(SG_WIDTH)]

    def front(rt):
        def piece(c0):
            def run():
                h = _dot(hn_scr[rt, :], win_ref[:, c0:c0 + PROJ_CHUNK])
                for cc in range(0, PROJ_CHUNK, 128):
                    ct = slice(c0 + cc, c0 + cc + 128)
                    uv_scr[rt, ct] = _gelu_tanh(h[:, cc:cc + 128] + bin_ref[:, ct])
            return run
        return [piece(c0) for c0 in range(0, 2 * SG_WIDTH, PROJ_CHUNK)]

    def back(rt):
        stats = {}

        def ln_stats():
            tot = None
            for ct in vcols:
                part = jnp.sum(uv_scr[rt, ct], axis=-1, keepdims=True)
                tot = part if tot is None else tot + part
            mu = tot * (1.0 / SG_WIDTH)
            ss = None
            for ct in vcols:
                vc = uv_scr[rt, ct] - mu
                part = jnp.sum(vc * vc, axis=-1, keepdims=True)
                ss = part if ss is None else ss + part
            stats["mu"] = mu
            stats["scale"] = lax.rsqrt(ss * (1.0 / SG_WIDTH) + LN_EPS)

        def ln_apply(k0):
            def run():
                for ct, c in list(zip(vcols, col128(SG_WIDTH)))[k0:k0 + 4]:
                    vn = (uv_scr[rt, ct] - stats["mu"]) * stats["scale"] * lng_ref[:, c] + lnb_ref[:, c]
                    if want_v:
                        v_ref[rt, c] = vn
                    vb_scr[rt, c] = vn.astype(BF16)
            return run

        def mix(g):
            def run():
                cols = slice(g * SG_GROUP_DIM, (g + 1) * SG_GROUP_DIM)
                mixed = _dot(wmix_ref[g], vb_scr[rt, cols]) + bmix_ref[:, cols]
                um_scr[rt, cols] = (uv_scr[rt, cols] * mixed).astype(BF16)
            return run

        def out(c0):
            def run():
                ct = slice(c0, c0 + PROJ_CHUNK)
                xo_ref[rt, ct] = x_ref[rt, ct] + _dot(um_scr[rt, :], wout_ref[:, ct])
            return run

        return ([ln_stats] + [ln_apply(k) for k in range(0, len(vcols), 4)]
                + [mix(g) for g in range(SG_GROUPS)]
                + [out(c0) for c0 in range(0, D_MODEL, PROJ_CHUNK)])

    pending = []
    for rt in row_tiles:
        for piece in front(rt):
            piece()
            if pending:
                pending.pop(0)()
        for piece in pending:
            piece()
        pending = back(rt)
    for piece in pending:
        piece()


def _sg_layer(x2d, norm_g, w_in, b_in, ln_g, ln_b, w_s, b_s, w_out, *, seq_len, want_v):
    r = SG_ROWS
    t = x2d.shape[0]
    q = min(seq_len, SG_CHUNK)
    reps = SG_CHUNK // q
    ws = jnp.tril(w_s)[:, :q, :q]
    wmix = jnp.einsum("ab,gts->gatbs", jnp.eye(reps, dtype=F32), ws)
    wmix = wmix.reshape(SG_GROUPS, SG_CHUNK, SG_CHUNK)
    bmix = jnp.tile(jnp.repeat(b_s.T[:q], SG_GROUP_DIM, axis=1), (reps, 1))
    row_spec = pl.BlockSpec((r, D_MODEL), lambda i: (i, 0))
    v_spec = pl.BlockSpec((r, SG_WIDTH), lambda i: (i, 0))
    out_shape = [jax.ShapeDtypeStruct(x2d.shape, F32)]
    out_specs = [row_spec]
    if want_v:
        out_shape.append(jax.ShapeDtypeStruct((t, SG_WIDTH), F32))
        out_specs.append(v_spec)
    outs = pl.pallas_call(
        functools.partial(_sg_kernel, r=r, want_v=want_v),
        grid=(t // r,),
        scratch_shapes=[pltpu.VMEM((r, D_MODEL), BF16),
                        pltpu.VMEM((r, 2 * SG_WIDTH), F32),
                        pltpu.VMEM((r, SG_WIDTH), BF16),
                        pltpu.VMEM((r, SG_WIDTH), BF16)],
        in_specs=[row_spec,
                  _const_spec((1, D_MODEL)),
                  _const_spec((D_MODEL, 2 * SG_WIDTH)),
                  _const_spec((1, 2 * SG_WIDTH)),
                  _const_spec((1, SG_WIDTH)),
                  _const_spec((1, SG_WIDTH)),
                  _const_spec((SG_GROUPS, SG_CHUNK, SG_CHUNK)),
                  _const_spec((SG_CHUNK, SG_WIDTH)),
                  _const_spec((SG_WIDTH, D_MODEL))],
        out_specs=out_specs,
        out_shape=out_shape,
        compiler_params=pltpu.CompilerParams(
            dimension_semantics=("arbitrary",), vmem_limit_bytes=VMEM_LIMIT),
        name="sg_layer",
    )(x2d, norm_g.reshape(1, D_MODEL), w_in.astype(BF16), b_in.reshape(1, 2 * SG_WIDTH),
      ln_g.reshape(1, SG_WIDTH), ln_b.reshape(1, SG_WIDTH), wmix.astype(BF16), bmix,
      w_out.astype(BF16))
    return outs if want_v else (outs[0], None)


ROUTER_ROWS = 64


def _norm_tiles(use_a, xa_ref, xb_ref, g_ref, store):
    def load(rt, ct):
        return jnp.where(use_a, xa_ref[rt, ct], xb_ref[rt, ct])

    for r0 in range(0, xa_ref.shape[0], 128):
        rt = slice(r0, r0 + 128)
        ss = None
        for c0 in range(0, D_MODEL, 256):
            xv = load(rt, slice(c0, c0 + 256))
            part = jnp.sum(xv * xv, axis=-1, keepdims=True)
            ss = part if ss is None else ss + part
        scale = lax.rsqrt(ss * (1.0 / D_MODEL) + NORM_EPS)
        for c0 in range(0, D_MODEL, 256):
            ct = slice(c0, c0 + 256)
            store(rt, ct, load(rt, ct) * scale * g_ref[:, ct])


def _router_t_kernel(xa_ref, xb_ref, g_ref, wt_ref, bt_ref, usl_ref, lsl_ref,
                     meta_ref, metat_ref, pad_ref, h1_scr, h2_scr, *, steps_a):
    wt = wt_ref[...]
    w1 = wt.astype(BF16)
    w2 = (wt - w1.astype(F32)).astype(BF16)
    tt = MOE_TILE
    n = xa_ref.shape[0]

    def split_store(rt, ct, hn):
        hi = hn.astype(BF16)
        h1_scr[rt, ct] = hi
        h2_scr[rt, ct] = (hn - hi.astype(F32)).astype(BF16)

    _norm_tiles(pl.program_id(0) < steps_a, xa_ref, xb_ref, g_ref, split_store)

    h1 = h1_scr[...]
    logits = _dot_nt(w1, h1) + _dot_nt(w2, h1) + _dot_nt(w1, h2_scr[...])
    select = _select_experts(logits, bt_ref[...])
    for k in range(n // tt):
        cols = slice(k * tt, (k + 1) * tt)
        metat, pad = _local_order(select[:, cols], usl_ref[...], lsl_ref[...])
        metat_ref[k * 8:(k + 1) * 8, :] = metat
        full = jnp.concatenate([metat, jnp.zeros((128 - 8, tt), F32)], axis=0)
        meta_ref[cols, :] = full.T
        pad_ref[k * MOE_EXPERTS:(k + 1) * MOE_EXPERTS, :] = pad


def _select_experts(logits, bt):
    n = logits.shape[1]
    reps = n // bt.shape[1]
    logits = logits + (bt if reps == 1 else jnp.concatenate([bt] * reps, axis=1))
    tt = n
    row8 = lax.broadcasted_iota(I32, (8, tt), 0).astype(F32)
    gl = jnp.where(row8 < MOE_GROUPS, logits[0:8, :], -jnp.inf)
    gmax = jnp.max(gl, axis=0, keepdims=True)
    g_top = jnp.min(jnp.where(gl == gmax, row8, 8.0), axis=0, keepdims=True)
    p_g = 1.0 / jnp.sum(jnp.exp(gl - gmax), axis=0, keepdims=True)

    el = logits[8:16, :]
    for grp in range(1, MOE_GROUPS):
        el = jnp.where(g_top == grp, logits[8 + 8 * grp:16 + 8 * grp, :], el)
    emax = jnp.max(el, axis=0, keepdims=True)
    ee = jnp.exp(el - emax)
    prob = ee / jnp.sum(ee, axis=0, keepdims=True)
    p1 = jnp.max(prob, axis=0, keepdims=True)
    i1 = jnp.min(jnp.where(prob == p1, row8, 8.0), axis=0, keepdims=True)
    prob2 = jnp.where(row8 == i1, -1.0, prob)
    p2 = jnp.max(prob2, axis=0, keepdims=True)
    i2 = jnp.min(jnp.where(prob2 == p2, row8, 8.0), axis=0, keepdims=True)
    psum = p1 + p2
    gate1 = p_g * (p1 / psum)
    gate2 = p_g * (p2 / psum)
    e1 = g_top * MOE_EPG + i1
    e2 = g_top * MOE_EPG + i2
    out = jnp.where(row8 == 0, e1, 0.0)
    out = jnp.where(row8 == 1, e2, out)
    out = jnp.where(row8 == 2, gate1, out)
    return jnp.where(row8 == 3, gate2, out)


def _local_order(select, usl, lsl):
    tt = select.shape[1]
    row8 = lax.broadcasted_iota(I32, (8, tt), 0).astype(F32)
    e1, e2 = select[0:1, :], select[1:2, :]
    rowe = lax.broadcasted_iota(I32, (MOE_EXPERTS, tt), 0).astype(F32)
    sel1 = rowe == e1
    sel2 = rowe == e2
    onehot = jnp.where(sel1 | sel2, 1.0, 0.0)
    before = _dot(onehot.astype(BF16), usl)
    cnt = jnp.sum(onehot, axis=1, keepdims=True)
    pad = jnp.floor((cnt + (GRANULE - 1)) * (1.0 / GRANULE)) * GRANULE
    padb = jnp.broadcast_to(pad, (MOE_EXPERTS, tt))
    local = before + _dot(lsl, padb.astype(BF16))
    slot1 = jnp.sum(jnp.where(sel1, local, 0.0), axis=0, keepdims=True)
    slot2 = jnp.sum(jnp.where(sel2, local, 0.0), axis=0, keepdims=True)

    metat = jnp.where(row8 == 4, slot1, select)
    metat = jnp.where(row8 == 5, slot2, metat)
    return metat, padb[:, 0:128]


def _granule(ref, g):
    return ref.at[pl.ds(pl.multiple_of(g * GRANULE, GRANULE), GRANULE), :]


def _dispatch_kernel(ngran_ref, nv_ref, dest_ref, gap_ref, xa_ref, xb_ref, g_ref, meta_ref, xs_hbm,
                     buf, zbuf, hn_scr, sem, zsem, nstart, *, tiles_a):
    i = pl.program_id(0)
    n_tiles = pl.num_programs(0) - 1
    slot = i % 2
    tt = xa_ref.shape[0]
    bm = zbuf.shape[0]
    n_blocks = xs_hbm.shape[0] // bm

    def out_copy(sl, g, d):
        return pltpu.make_async_copy(_granule(buf.at[sl], g), _granule(xs_hbm, d), sem.at[sl])

    def drain(sl):
        def body(_, carry):
            out_copy(sl, 0, 0).wait()
            return carry
        lax.fori_loop(0, nstart[sl], body, 0)

    def tail_copy(b):
        return pltpu.make_async_copy(
            zbuf, xs_hbm.at[pl.ds(pl.multiple_of(b * bm, bm), bm), :], zsem.at[1])

    @pl.when(i == 0)
    def _():
        nstart[0] = 0
        nstart[1] = 0
        zbuf[...] = jnp.zeros_like(zbuf)

    def norm_store(rt, ct, hn):
        hn_scr[rt, ct] = hn.astype(BF16)

    tail = nv_ref[0] + i

    @pl.when((i >= 1) & (tail - 1 < n_blocks))
    def _():
        tail_copy(tail - 1).wait()

    @pl.when(i < n_tiles)
    def _():
        @pl.when(tail < n_blocks)
        def _():
            tail_copy(tail).start()
        _norm_tiles(i < tiles_a, xa_ref, xb_ref, g_ref, norm_store)
        drain(slot)
        hn = hn_scr[...]
        mt = meta_ref[...]
        for r0 in range(0, LOCAL_ROWS, 256):
            rows = (lax.broadcasted_iota(I32, (256, tt), 0) + r0).astype(F32)
            onehot = jnp.where((rows == mt[4:5, :]) | (rows == mt[5:6, :]), 1.0, 0.0).astype(BF16)
            buf[slot, r0:r0 + 256, :] = _dot(onehot, hn).astype(BF16)
        ng = ngran_ref[i]

        def body(g, carry):
            out_copy(slot, g, dest_ref[0, 0, g]).start()
            return carry
        lax.fori_loop(0, ng, body, 0)
        nstart[slot] = ng

    @pl.when(i == n_tiles)
    def _():
        drain(0)
        drain(1)

        def gap_copy(d):
            return pltpu.make_async_copy(_granule(zbuf, 0), _granule(xs_hbm, d), zsem.at[0])

        def each_gap(fn):
            def body(j, carry):
                d = gap_ref[0, 0, j]

                @pl.when(d >= 0)
                def _():
                    fn(gap_copy(d))
                return carry
            lax.fori_loop(0, gap_ref.shape[2], body, 0)

        def each_tail(fn):
            def body(b, carry):
                fn(tail_copy(b))
                return carry
            lax.fori_loop(nv_ref[0] + n_tiles, n_blocks, body, 0)

        each_gap(lambda cp: cp.start())
        each_tail(lambda cp: cp.start())
        each_gap(lambda cp: cp.wait())
        each_tail(lambda cp: cp.wait())


def _expert_kernel(bexp_ref, nv_ref, xs_ref, wg_ref, wu_ref, wd_ref, ys_ref,
                   wgu_b, wd_b, act_scr):
    b = pl.program_id(0)
    ff = wg_ref.shape[1]

    @pl.when(b < nv_ref[0])
    def _():
        @pl.when((b == 0) | (bexp_ref[b] != bexp_ref[jnp.maximum(b - 1, 0)]))
        def _():
            for k0 in range(0, wg_ref.shape[0], 256):
                wgu_b[k0:k0 + 256, 0:ff] = wg_ref[k0:k0 + 256, :].astype(BF16)
                wgu_b[k0:k0 + 256, ff:2 * ff] = wu_ref[k0:k0 + 256, :].astype(BF16)
            for k0 in range(0, ff, 64):
                wd_b[k0:k0 + 64, :] = wd_ref[k0:k0 + 64, :].astype(BF16)

        bm = xs_ref.shape[0]
        for m0 in range(0, bm, 256):
            h = _dot(xs_ref[m0:m0 + 256, :], wgu_b[...])
            for r0 in range(0, 256, 128):
                for c0 in range(0, ff, 128):
                    hg = h[r0:r0 + 128, c0:c0 + 128]
                    hu = h[r0:r0 + 128, ff + c0:ff + c0 + 128]
                    act_scr[m0 + r0:m0 + r0 + 128, c0:c0 + 128] = (_silu(hg) * hu).astype(BF16)
            ys_ref[m0:m0 + 256, :] = _dot(act_scr[m0:m0 + 256, :], wd_b[...]).astype(BF16)


def _combine_kernel(ngran_ref, src_ref, srcn_ref, xa_ref, xb_ref, meta_ref, gf_ref, ys_hbm,
                    oa_ref, ob_ref, buf, moe_scr, pick_scr, sem, *, final, tiles_a):
    i = pl.program_id(0)
    n = pl.num_programs(0)
    slot = i % 2
    tt = xa_ref.shape[0]

    def in_copy(sl, g, d):
        return pltpu.make_async_copy(_granule(ys_hbm, d), _granule(buf.at[sl], g), sem.at[sl])

    def gather(idx_ref, sl, ng):
        def body(g, carry):
            in_copy(sl, g, idx_ref[0, 0, g]).start()
            return carry
        lax.fori_loop(0, ng, body, 0)

    @pl.when(i == 0)
    def _():
        buf[...] = jnp.zeros_like(buf)
        gather(src_ref, 0, ngran_ref[0])

    @pl.when(i + 1 < n)
    def _():
        gather(srcn_ref, 1 - slot, ngran_ref[i + 1])

    def wait_body(_, carry):
        in_copy(slot, 0, 0).wait()
        return carry
    lax.fori_loop(0, ngran_ref[i], wait_body, 0)

    for r0 in range(0, tt, 128):
        rt = slice(r0, r0 + 128)
        meta = meta_ref[rt, :]
        for c0 in range(0, LOCAL_ROWS, 256):
            cols = (lax.broadcasted_iota(I32, (128, 256), 1) + c0).astype(F32)
            pick_scr[rt, c0:c0 + 256] = (
                jnp.where(cols == meta[:, 4:5], meta[:, 2:3], 0.0)
                + jnp.where(cols == meta[:, 5:6], meta[:, 3:4], 0.0)).astype(BF16)
    is_a = i < tiles_a
    for m0 in range(0, tt, 256):
        moe = _dot(pick_scr[m0:m0 + 256, :], buf[slot])
        for r0 in range(0, 256, 128):
            rt = slice(m0 + r0, m0 + r0 + 128)
            for c0 in range(0, D_MODEL, 256):
                ct = slice(c0, c0 + 256)
                moe_scr[rt, ct] = (jnp.where(is_a, xa_ref[rt, ct], xb_ref[rt, ct])
                                   + moe[r0:r0 + 128, ct])
    for r0 in range(0, tt, 128):
        rt = slice(r0, r0 + 128)
        if final:
            ss = None
            for c0 in range(0, D_MODEL, 256):
                yv = moe_scr[rt, c0:c0 + 256]
                part = jnp.sum(yv * yv, axis=-1, keepdims=True)
                ss = part if ss is None else ss + part
            scale = lax.rsqrt(ss * (1.0 / D_MODEL) + NORM_EPS)
            for c0 in range(0, D_MODEL, 256):
                ct = slice(c0, c0 + 256)
                moe_scr[rt, ct] = moe_scr[rt, ct] * scale * gf_ref[:, ct]

    def emit(o_ref):
        for r0 in range(0, tt, 128):
            for c0 in range(0, D_MODEL, 256):
                o_ref[r0:r0 + 128, c0:c0 + 256] = moe_scr[r0:r0 + 128, c0:c0 + 256]

    @pl.when(is_a)
    def _():
        emit(oa_ref)

    @pl.when(jnp.logical_not(is_a))
    def _():
        emit(ob_ref)


def _moe(xa, xb, norm_g, w_grp, b_grp, w_rt, b_rt, w_gate, w_up, w_down, norm_final, *, layer):
    tt = MOE_TILE
    tiles_a, tiles_b = xa.shape[0] // tt, xb.shape[0] // tt
    steps_a, steps_b = tiles_a // ROUTER_TILES, tiles_b // ROUTER_TILES
    n_tiles = tiles_a + tiles_b
    t = n_tiles * tt
    bm = _expert_block(t)
    n_exp = MOE_EXPERTS
    a_blk = lambda i, n_a: jnp.minimum(i, n_a - 1)
    b_blk = lambda i, n_a, n_b: jnp.clip(i - n_a, 0, n_b - 1)
    zrow = lambda n: jnp.zeros((n, D_MODEL), F32)
    wt = jnp.concatenate([w_grp.T, zrow(8 - MOE_GROUPS), w_rt.T, zrow(ROUTER_ROWS - 8 - n_exp)], axis=0)
    bt = jnp.concatenate([b_grp, jnp.zeros((8 - MOE_GROUPS,), F32), b_rt,
                          jnp.zeros((ROUTER_ROWS - 8 - n_exp,), F32)])
    bt = jnp.broadcast_to(bt[:, None], (ROUTER_ROWS, tt))
    usl = jnp.asarray(np.triu(np.ones((tt, tt)), 1), BF16)
    lsl = jnp.asarray(np.tril(np.ones((n_exp, n_exp)), -1), BF16)
    g2d = norm_g.reshape(1, D_MODEL)
    arb = pltpu.CompilerParams(dimension_semantics=("arbitrary",), vmem_limit_bytes=VMEM_LIMIT)

    meta, metat, pad = pl.pallas_call(
        functools.partial(_router_t_kernel, steps_a=steps_a),
        grid=(steps_a + steps_b,),
        in_specs=[pl.BlockSpec((ROUTER_TILES * tt, D_MODEL), lambda i: (a_blk(i, steps_a), 0)),
                  pl.BlockSpec((ROUTER_TILES * tt, D_MODEL),
                               lambda i: (b_blk(i, steps_a, steps_b), 0)),
                  _const_spec((1, D_MODEL)),
                  _const_spec((ROUTER_ROWS, D_MODEL)),
                  _const_spec((ROUTER_ROWS, tt)),
                  _const_spec((tt, tt)),
                  _const_spec((n_exp, n_exp))],
        out_specs=(pl.BlockSpec((ROUTER_TILES * tt, 128), lambda i: (i, 0)),
                   pl.BlockSpec((ROUTER_TILES * 8, tt), lambda i: (i, 0)),
                   pl.BlockSpec((ROUTER_TILES * n_exp, 128), lambda i: (i, 0))),
        out_shape=(jax.ShapeDtypeStruct((t, 128), F32),
                   jax.ShapeDtypeStruct((n_tiles * 8, tt), F32),
                   jax.ShapeDtypeStruct((n_tiles * n_exp, 128), F32)),
        scratch_shapes=[pltpu.VMEM((ROUTER_TILES * tt, D_MODEL), BF16),
                        pltpu.VMEM((ROUTER_TILES * tt, D_MODEL), BF16)],
        compiler_params=arb,
        name="moe_router",
    )(xa, xb, g2d, wt, bt, usl, lsl)

    runs = pad.reshape(n_tiles, n_exp, 128)[:, :, 0].astype(I32)
    rows_e = jnp.sum(runs, axis=0)
    nblk = (rows_e + bm - 1) // bm
    blk_end = jnp.cumsum(nblk)
    e_start = (blk_end - nblk) * bm
    n_valid = blk_end[-1]
    run_end = jnp.cumsum(runs, axis=1)
    ngran = (run_end[:, -1] // GRANULE).astype(I32)
    shift = e_start[None, :] + (jnp.cumsum(runs, axis=0) - runs) - (run_end - runs)
    g_row = jnp.arange(N_LOCAL_GRAN, dtype=I32) * GRANULE
    e_of_g = jnp.sum((run_end[:, None, :] <= g_row[None, :, None]).astype(I32), axis=-1)
    shift_g = jnp.sum(jnp.where(e_of_g[..., None] == jnp.arange(n_exp, dtype=I32),
                                shift[:, None, :], 0), axis=-1)
    dest = jnp.where(e_of_g < n_exp, (shift_g + g_row[None, :]) // GRANULE, 0)
    dest = dest.astype(I32).reshape(n_tiles, 1, N_LOCAL_GRAN)
    per_blk = bm // GRANULE
    gap = ((e_start + rows_e) // GRANULE)[:, None] + jnp.arange(per_blk, dtype=I32)[None, :]
    gap = jnp.where(gap < ((e_start + nblk * bm) // GRANULE)[:, None], gap, -1)
    gap = gap.astype(I32).reshape(1, 1, n_exp * per_blk)
    n_blocks = (2 * t + n_tiles * n_exp * (GRANULE - 1)) // bm + 1 + n_exp
    blk = jnp.minimum(jnp.arange(n_blocks, dtype=I32), n_valid - 1)
    block_expert = jnp.sum((blk[:, None] >= blk_end[None, :]).astype(I32), axis=1).astype(I32)
    n_valid = n_valid.reshape(1).astype(I32)

    last = n_tiles - 1
    xs = pl.pallas_call(
        functools.partial(_dispatch_kernel, tiles_a=tiles_a),
        grid_spec=pltpu.PrefetchScalarGridSpec(
            num_scalar_prefetch=2,
            grid=(n_tiles + 1,),
            in_specs=[pl.BlockSpec((1, 1, N_LOCAL_GRAN),
                                   lambda i, ng, nv: (jnp.minimum(i, last), 0, 0),
                                   memory_space=pltpu.SMEM),
                      pl.BlockSpec((1, 1, n_exp * per_blk), lambda i, ng, nv: (0, 0, 0),
                                   memory_space=pltpu.SMEM),
                      pl.BlockSpec((tt, D_MODEL), lambda i, ng, nv: (a_blk(i, tiles_a), 0)),
                      pl.BlockSpec((tt, D_MODEL),
                                   lambda i, ng, nv: (b_blk(i, tiles_a, tiles_b), 0)),
                      _const_spec((1, D_MODEL)),
                      pl.BlockSpec((8, tt), lambda i, ng, nv: (jnp.minimum(i, last), 0))],
            out_specs=pl.BlockSpec(memory_space=pl.ANY),
            scratch_shapes=[pltpu.VMEM((2, LOCAL_ROWS, D_MODEL), BF16),
                            pltpu.VMEM((bm, D_MODEL), BF16),
                            pltpu.VMEM((tt, D_MODEL), BF16),
                            pltpu.SemaphoreType.DMA((2,)),
                            pltpu.SemaphoreType.DMA((2,)),
                            pltpu.SMEM((2,), I32)]),
        out_shape=jax.ShapeDtypeStruct((n_blocks * bm, D_MODEL), BF16),
        compiler_params=arb,
        name="moe_dispatch",
    )(ngran, n_valid, dest, gap, xa, xb, g2d, metat)

    w_spec = lambda shape: pl.BlockSpec((None, None) + shape,
                                        lambda b, be, nv: (layer, be[b], 0, 0))
    row_blk = pl.BlockSpec((bm, D_MODEL), lambda b, be, nv: (jnp.minimum(b, nv[0] - 1), 0))
    ys = pl.pallas_call(
        _expert_kernel,
        grid_spec=pltpu.PrefetchScalarGridSpec(
            num_scalar_prefetch=2,
            grid=(n_blocks,),
            in_specs=[row_blk,
                      w_spec((D_MODEL, MOE_D_FF)),
                      w_spec((D_MODEL, MOE_D_FF)),
                      w_spec((MOE_D_FF, D_MODEL))],
            out_specs=row_blk,
            scratch_shapes=[pltpu.VMEM((D_MODEL, 2 * MOE_D_FF), BF16),
                            pltpu.VMEM((MOE_D_FF, D_MODEL), BF16),
                            pltpu.VMEM((bm, MOE_D_FF), BF16)]),
        out_shape=jax.ShapeDtypeStruct((n_blocks * bm, D_MODEL), BF16),
        input_output_aliases={2: 0},
        compiler_params=arb,
        name="moe_experts",
    )(block_expert, n_valid, xs, w_gate, w_up, w_down)

    final = norm_final is not None
    gf = (norm_final if final else jnp.ones((D_MODEL,), F32)).reshape(1, D_MODEL)
    src_spec = lambda nxt: pl.BlockSpec(
        (1, 1, N_LOCAL_GRAN), lambda i, ng: (jnp.minimum(i + nxt, last), 0, 0),
        memory_space=pltpu.SMEM)
    a_spec = pl.BlockSpec((tt, D_MODEL), lambda i, ng: (a_blk(i, tiles_a), 0))
    b_spec = pl.BlockSpec((tt, D_MODEL), lambda i, ng: (b_blk(i, tiles_a, tiles_b), 0))
    out_a, out_b = pl.pallas_call(
        functools.partial(_combine_kernel, final=final, tiles_a=tiles_a),
        grid_spec=pltpu.PrefetchScalarGridSpec(
            num_scalar_prefetch=1,
            grid=(n_tiles,),
            in_specs=[src_spec(0), src_spec(1), a_spec, b_spec,
                      pl.BlockSpec((tt, 128), lambda i, ng: (i, 0)),
                      _const_spec((1, D_MODEL)),
                      pl.BlockSpec(memory_space=pl.ANY)],
            out_specs=(a_spec, b_spec),
            scratch_shapes=[pltpu.VMEM((2, LOCAL_ROWS, D_MODEL), BF16),
                            pltpu.VMEM((tt, D_MODEL), F32),
                            pltpu.VMEM((tt, LOCAL_ROWS), BF16),
                            pltpu.SemaphoreType.DMA((2,))]),
        out_shape=(jax.ShapeDtypeStruct(xa.shape, F32), jax.ShapeDtypeStruct(xb.shape, F32)),
        compiler_params=arb,
        name="moe_combine",
    )(ngran, dest, dest, xa, xb, meta, gf, ys)
    return out_a, out_b


def _ssd(x, conv_prev, ssm_prev, p, *, nsub, ns):
    seq_len = x.shape[1]
    q = seq_len if seq_len < SSD_CHUNK else SSD_CHUNK
    x, conv_new, ssm_new = _ssd_layer(
        x, conv_prev, ssm_prev, p["norm_mix"][0], p["ssd_w_in"][0], p["ssd_conv_w"][0],
        p["ssd_conv_b"][0], p["ssd_dt_bias"][0], p["ssd_a_log"][0], p["ssd_d"][0], p["ssd_norm"][0],
        p["ssd_w_out"][0], nsub=nsub, ns=ns, q=q)
    return x.reshape(-1, D_MODEL), conv_new, ssm_new


def _sg(x2d, p, *, seq_len, want_v):
    return _sg_layer(x2d, p["norm_mix"][1], p["sg_w_in"][0], p["sg_b_in"][0], p["sg_ln_g"][0],
                     p["sg_ln_b"][0], p["sg_w_s"][0], p["sg_b_s"][0], p["sg_w_out"][0],
                     seq_len=seq_len, want_v=want_v)


def _moe_layer(xa, xb, p, layer, norm_final):
    return _moe(xa, xb, p["norm_ffn"][layer], p["moe_w_group"][layer], p["moe_b_group"][layer],
                p["moe_w_router"][layer], p["moe_b_router"][layer], p["moe_w_gate"], p["moe_w_up"],
                p["moe_w_down"], norm_final, layer=layer)


def kernel(x_prompt, x_sample, state_ssm, state_conv, norm_mix, norm_ffn, norm_final, ssd_w_in, ssd_conv_w, ssd_conv_b, ssd_dt_bias, ssd_a_log, ssd_d, ssd_norm, ssd_w_out, sg_w_in, sg_b_in, sg_ln_g, sg_ln_b, sg_w_s, sg_b_s, sg_w_out, moe_w_group, moe_b_group, moe_w_router, moe_b_router, moe_w_gate, moe_w_up, moe_w_down):
    p = dict(norm_mix=norm_mix, norm_ffn=norm_ffn, norm_final=norm_final, ssd_w_in=ssd_w_in,
             ssd_conv_w=ssd_conv_w, ssd_conv_b=ssd_conv_b, ssd_dt_bias=ssd_dt_bias,
             ssd_a_log=ssd_a_log, ssd_d=ssd_d, ssd_norm=ssd_norm, ssd_w_out=ssd_w_out,
             sg_w_in=sg_w_in, sg_b_in=sg_b_in, sg_ln_g=sg_ln_g, sg_ln_b=sg_ln_b, sg_w_s=sg_w_s,
             sg_b_s=sg_b_s, sg_w_out=sg_w_out, moe_w_group=moe_w_group, moe_b_group=moe_b_group,
             moe_w_router=moe_w_router, moe_b_router=moe_b_router, moe_w_gate=moe_w_gate,
             moe_w_up=moe_w_up, moe_w_down=moe_w_down)
    nb = x_prompt.shape[0]
    conv0 = jnp.zeros((1, nb, CONV_W - 1, CONV_DIM), F32)
    ssm0 = jnp.zeros((1, nb, N_HEADS, HEAD_DIM, D_STATE), F32)
    xp, conv_p, ssm_p = _ssd(x_prompt, conv0, ssm0, p, nsub=2, ns=1)
    xs, conv_s, ssm_s = _ssd(x_sample, state_conv, state_ssm, p, nsub=1, ns=4)
    xp, xs = _moe_layer(xp, xs, p, 0, None)
    xp, _ = _sg(xp, p, seq_len=x_prompt.shape[1], want_v=False)
    xs, v_s = _sg(xs, p, seq_len=x_sample.shape[1], want_v=True)
    y_p, y_s = _moe_layer(xp, xs, p, 1, norm_final)
    return (y_p.reshape(x_prompt.shape), y_s.reshape(x_sample.shape), ssm_p, conv_p, ssm_s, conv_s,
            v_s.reshape((1,) + x_sample.shape[:2] + (SG_WIDTH,)))
```

```python
import functools
import math

import jax
import jax.numpy as jnp
import numpy as np
from jax import lax
from jax.experimental import pallas as pl
from jax.experimental.pallas import tpu as pltpu

F32 = jnp.float32
BF16 = jnp.bfloat16
I32 = jnp.int32

D_MODEL = 1024
N_HEADS = 32
HEAD_DIM = 64
N_GROUPS = 4
D_STATE = 128
D_INNER = N_HEADS * HEAD_DIM
GROUP_W = D_INNER // N_GROUPS
CONV_W = 4
CONV_DIM = D_INNER + 2 * N_GROUPS * D_STATE
DT_PAD = 128
SSD_CHUNK = 128
PROJ_CHUNK = 256
SG_WIDTH = 2 * D_MODEL
SG_GROUPS = 8
SG_GROUP_DIM = SG_WIDTH // SG_GROUPS
SG_CHUNK = 128
SG_ROWS = 512
MOE_GROUPS = 4
MOE_EPG = 8
MOE_EXPERTS = MOE_GROUPS * MOE_EPG
MOE_D_FF = 256
NORM_EPS = 1e-6
LN_EPS = 1e-5

MOE_TILE = 512
ROUTER_TILES = 2


def _expert_block(n_tokens):
    return 512 if 2 * n_tokens >= 512 * MOE_EXPERTS else 256


GRANULE = 16
LOCAL_ROWS = 2 * MOE_TILE + MOE_EXPERTS * GRANULE
N_LOCAL_GRAN = LOCAL_ROWS // GRANULE
VMEM_LIMIT = 56 * 1024 * 1024


def _sigmoid(x):
    return 0.5 * (jnp.tanh(0.5 * x) + 1.0)


def _silu(x):
    return x * _sigmoid(x)


def _softplus(x):
    return jnp.maximum(x, 0.0) + jnp.log(1.0 + jnp.exp(-jnp.abs(x)))


def _gelu_tanh(x):
    c = math.sqrt(2.0 / math.pi)
    t = jnp.tanh(x * (c + (c * 0.044715) * (x * x)))
    hx = 0.5 * x
    return hx + hx * t


def _rms(x, g):
    return x * lax.rsqrt(jnp.mean(x * x, axis=-1, keepdims=True) + NORM_EPS) * g


def _split3(x):
    a = x.astype(BF16)
    r = x - a.astype(F32)
    b = r.astype(BF16)
    c = (r - b.astype(F32)).astype(BF16)
    return a, b, c


def _dot(a, b):
    return jnp.dot(a, b, preferred_element_type=F32)


def _dot_nt(a, b):
    return lax.dot_general(a, b, (((1,), (1,)), ((), ())), preferred_element_type=F32)


def _dot_tn(a, b):
    return lax.dot_general(a, b, (((0,), (0,)), ((), ())), preferred_element_type=F32)


def _const_spec(shape):
    nd = len(shape)
    return pl.BlockSpec(shape, lambda *_: (0,) * nd)


def _ssd_kernel(x_ref, g_ref, win_ref, cw_ref, cb_ref, dtb_ref, alog_ref, dsk_ref, ng_ref,
                wout_ref, tri_ref, ones_ref, cin_ref, sin_ref,
                xo_ref, cout_ref, sout_ref,
                pend_x, pend_z, pend_xbc, pend_dt, xres_scr, zg_scr,
                prev_scr, xc_scr, yoff_scr, y_scr, xw_scr, hn_scr, yn_scr,
                *, nsub, ns, q, n_chunks):
    i = pl.program_id(0)
    first_chunk = lax.rem(jnp.maximum(i - 1, 0), n_chunks) == 0
    nseq = nsub * ns
    sb = ns * q
    r = nsub * sb

    @pl.when(i == 0)
    def _():
        pend_x[...] = jnp.zeros_like(pend_x)
        pend_z[...] = jnp.zeros_like(pend_z)
        pend_xbc[...] = jnp.zeros_like(pend_xbc)
        pend_dt[...] = jnp.zeros_like(pend_dt)

    tr = min(r, 128)
    row_tiles = [slice(a, a + tr) for a in range(0, r, tr)]

    def col_tiles(total, rows=tr):
        w = min(total, max(128, (32 * 1024) // rows))
        return [slice(c, c + w) for c in range(0, total, w)]

    for rt in row_tiles:
        for ct in col_tiles(D_MODEL):
            xres_scr[rt, ct] = pend_x[rt, ct]

    def gate_tile(rt, ct):
        def run():
            zg_scr[rt, ct] = _silu(pend_z[rt, ct])
        return run
    gate_tiles = [gate_tile(rt, ct) for rt in row_tiles for ct in col_tiles(D_INNER)]

    for s in range(nseq):
        for ct in col_tiles(D_MODEL, q):
            pend_x[s * q:(s + 1) * q, ct] = x_ref[s, :, ct]
    for rt in row_tiles:
        ss = None
        for ct in col_tiles(D_MODEL):
            xv = pend_x[rt, ct]
            part = jnp.sum(xv * xv, axis=-1, keepdims=True)
            ss = part if ss is None else ss + part
        scale = lax.rsqrt(ss * (1.0 / D_MODEL) + NORM_EPS)
        for ct in col_tiles(D_MODEL):
            hn_scr[rt, ct] = (pend_x[rt, ct] * scale * g_ref[:, ct]).astype(BF16)

    def proj_chunk(dst, dst_col, w_col, width):
        def run():
            dst[:, dst_col:dst_col + width] = _dot(hn_scr[...], win_ref[:, w_col:w_col + width])
        return run
    z_chunks = [proj_chunk(pend_z, c, c, PROJ_CHUNK) for c in range(0, D_INNER, PROJ_CHUNK)]
    xbc_chunks = [proj_chunk(pend_xbc, c, D_INNER + c, PROJ_CHUNK)
                  for c in range(0, CONV_DIM, PROJ_CHUNK)]
    dt_chunk = proj_chunk(pend_dt, 0, D_INNER + CONV_DIM, DT_PAD)

    @pl.when(first_chunk)
    def _():
        sout_ref[...] = sin_ref[...]
        prev_scr[...] = jnp.zeros_like(prev_scr)
        for s in range(nseq):
            for k in range(CONV_W - 1):
                prev_scr[pl.ds(s * 8 + 5 + k, 1), :] = cin_ref[s, pl.ds(k, 1), :]

    sub8 = lax.broadcasted_iota(I32, (8, 1), 0)
    for s in range(nseq):
        srows = slice(s * q, (s + 1) * q)
        for ct in col_tiles(CONV_DIM, 4 * q):
            xq = pend_xbc[srows, ct]
            hist = prev_scr[s * 8:(s + 1) * 8, ct]
            acc = cb_ref[:, ct] + cw_ref[pl.ds(CONV_W - 1, 1), ct] * xq
            for j in range(1, CONV_W):
                sh = pltpu.roll(xq, j, 0)
                head = jnp.where(sub8 < j, pltpu.roll(hist, j, 0), sh[0:8, :])
                sh = head if q == 8 else jnp.concatenate([head, sh[8:, :]], axis=0)
                acc = acc + cw_ref[pl.ds(CONV_W - 1 - j, 1), ct] * sh
            xc_scr[srows, ct] = _silu(acc)
            last = xq[q - 8:q, :]
            prev_scr[s * 8:(s + 1) * 8, ct] = last
            for k in range(CONV_W - 1):
                cout_ref[s, pl.ds(k, 1), ct] = last[5 + k:6 + k, :]
    chunks = xbc_chunks + z_chunks + [dt_chunk]
    n_pairs = nsub * (N_HEADS // 2)
    emit_at = {}
    for k, tile in enumerate(gate_tiles):
        emit_at.setdefault(k, []).append(tile)
    for k, ch in enumerate(chunks):
        emit_at.setdefault((k * n_pairs) // len(chunks), []).append(ch)
    first_z = (len(xbc_chunks) * n_pairs) // len(chunks)
    assert len(gate_tiles) <= first_z, "gate tiles must be emitted before pend_z is overwritten"

    tri = tri_ref[...]
    trib = tri.astype(BF16)
    onesb = ones_ref[...].astype(BF16)
    mask = tri > 0.5
    rowseq = lax.shift_right_logical(lax.broadcasted_iota(I32, (sb, 1), 0), int(math.log2(q)))
    lo = lax.broadcasted_iota(I32, (sb, 128), 1) < HEAD_DIM
    neg_a = -jnp.exp(alog_ref[...])

    for u in range(nsub):
        rows = slice(u * sb, (u + 1) * sb)
        xs_ref = xc_scr.at[rows, 0:D_INNER]

        def b_of(g):
            return xc_scr[rows, D_INNER + g * D_STATE:D_INNER + (g + 1) * D_STATE]

        def c_of(g):
            c0 = D_INNER + (N_GROUPS + g) * D_STATE
            return xc_scr[rows, c0:c0 + D_STATE].astype(BF16)

        dt = _softplus(pend_dt[rows, :] + dtb_ref[...])
        d1, d2, d3 = _split3(dt * neg_a)
        cs = _dot(trib, d1) + _dot(trib, d2) + _dot(trib, d3)
        cl = _dot(onesb, d1) + _dot(onesb, d2) + _dot(onesb, d3)
        ecs = jnp.exp(cs)
        wgt = dt * jnp.exp(cl - cs)
        ecl = jnp.exp(cl)
        cs_t = cs.T
        dt_t = dt.T

        for g in range(N_GROUPS):
            cg = c_of(g)
            acc = None
            for s in range(ns):
                st = sout_ref[u * ns + s, g * GROUP_W:(g + 1) * GROUP_W, :].astype(BF16)
                yo = _dot_nt(cg, st)
                if ns > 1:
                    yo = jnp.where(rowseq == s, yo, 0.0)
                acc = yo if acc is None else acc + yo
            yoff_scr[rows, g * GROUP_W:(g + 1) * GROUP_W] = acc

        for g in range(N_GROUPS):
            sc = _dot_nt(c_of(g), b_of(g).astype(BF16))
            for jj in range(N_HEADS // N_GROUPS // 2):
                j = g * (N_HEADS // N_GROUPS // 2) + jj
                ms = []
                for h in (2 * j, 2 * j + 1):
                    diff = cs[:, h:h + 1] - cs_t[h:h + 1, :]
                    dec = jnp.exp(jnp.where(mask, diff, -jnp.inf))
                    ms.append((sc * dec * dt_t[h:h + 1, :]).astype(BF16))
                lhs = jnp.concatenate(ms, axis=1)
                cols = slice(j * 128, (j + 1) * 128)
                xp = xs_ref[:, cols]
                rhs = jnp.concatenate([jnp.where(lo, xp, 0.0).astype(BF16),
                                       jnp.where(lo, 0.0, xp).astype(BF16)], axis=0)
                yd = _dot(lhs, rhs)
                ecs_p = jnp.where(lo, ecs[:, 2 * j:2 * j + 1], ecs[:, 2 * j + 1:2 * j + 2])
                wgt_p = jnp.where(lo, wgt[:, 2 * j:2 * j + 1], wgt[:, 2 * j + 1:2 * j + 2])
                y_scr[rows, cols] = yd + yoff_scr[rows, cols] * ecs_p + xp * dsk_ref[:, cols]
                xw_scr[rows, cols] = (xp * wgt_p).astype(BF16)
                pair = u * (N_HEADS // 2) + j
                for ch in emit_at.get(pair, []):
                    ch()

        for g in range(N_GROUPS):
            xwg = xw_scr[rows, g * GROUP_W:(g + 1) * GROUP_W]
            for s in range(ns):
                bg = b_of(g)
                if ns > 1:
                    bg = jnp.where(rowseq == s, bg, 0.0)
                upd = _dot_tn(xwg, bg.astype(BF16))
                for hh in range(GROUP_W // HEAD_DIM):
                    h = g * (GROUP_W // HEAD_DIM) + hh
                    dec = jnp.broadcast_to(ecl[s * q:s * q + 1, h:h + 1], (HEAD_DIM, D_STATE))
                    hrows = slice(h * HEAD_DIM, (h + 1) * HEAD_DIM)
                    sout_ref[u * ns + s, hrows, :] = (
                        sout_ref[u * ns + s, hrows, :] * dec + upd[hh * HEAD_DIM:(hh + 1) * HEAD_DIM, :])

    for rt in row_tiles:
        for g in range(N_GROUPS):
            cts = [slice(g * GROUP_W + c.start, g * GROUP_W + c.stop) for c in col_tiles(GROUP_W)]
            ss = None
            for ct in cts:
                yz = y_scr[rt, ct] * zg_scr[rt, ct]
                part = jnp.sum(yz * yz, axis=-1, keepdims=True)
                ss = part if ss is None else ss + part
            scale = lax.rsqrt(ss * (1.0 / GROUP_W) + NORM_EPS)
            for ct in cts:
                yn_scr[rt, ct] = (y_scr[rt, ct] * zg_scr[rt, ct] * scale * ng_ref[:, ct]).astype(BF16)
        for c0 in range(0, D_MODEL, PROJ_CHUNK):
            ct = slice(c0, c0 + PROJ_CHUNK)
            o = xres_scr[rt, ct] + _dot(yn_scr[rt, :], wout_ref[:, ct])
            for s in range(rt.start // q, rt.stop // q):
                xo_ref[s, :, ct] = o[s * q - rt.start:(s + 1) * q - rt.start, :]


def _ssd_layer(x, conv_prev, ssm_prev, norm_g, w_in, conv_w, conv_b, dt_bias, a_log, d_skip,
               norm_y, w_out, *, nsub, ns, q):
    n_seq, seq_len, _ = x.shape
    nseq = nsub * ns
    sb = ns * q
    r = nsub * sb
    n_chunks = seq_len // q
    pad = DT_PAD - N_HEADS
    win = jnp.concatenate([w_in, jnp.zeros((D_MODEL, pad), F32)], axis=1).astype(BF16)
    dtb = jnp.pad(dt_bias, (0, pad)).reshape(1, DT_PAD)
    alog = jnp.pad(a_log, (0, pad)).reshape(1, DT_PAD)
    dsk = jnp.repeat(d_skip, HEAD_DIM).reshape(1, D_INNER)
    blk = np.kron(np.eye(ns), np.ones((q, q)))
    tri = jnp.asarray(blk * np.tril(np.ones((sb, sb))), F32)
    ones = jnp.asarray(blk, F32)
    state = ssm_prev.reshape(n_seq, D_INNER, D_STATE)
    conv_prev = conv_prev.reshape(n_seq, CONV_W - 1, CONV_DIM)

    kern = functools.partial(_ssd_kernel, nsub=nsub, ns=ns, q=q, n_chunks=n_chunks)
    out_shape = (jax.ShapeDtypeStruct(x.shape, F32),
                 jax.ShapeDtypeStruct((n_seq, CONV_W - 1, CONV_DIM), F32),
                 jax.ShapeDtypeStruct((n_seq, D_INNER, D_STATE), F32))
    n_steps = (n_seq // nseq) * n_chunks
    nxt = lambda i: jnp.minimum(i, n_steps - 1)
    cur = lambda i: jnp.maximum(i - 1, 0)
    in_row_spec = pl.BlockSpec((nseq, q, D_MODEL),
                               lambda i: (nxt(i) // n_chunks, nxt(i) % n_chunks, 0))
    row_spec = pl.BlockSpec((nseq, q, D_MODEL),
                            lambda i: (cur(i) // n_chunks, cur(i) % n_chunks, 0))
    conv_spec = pl.BlockSpec((nseq, CONV_W - 1, CONV_DIM), lambda i: (cur(i) // n_chunks, 0, 0))
    state_spec = pl.BlockSpec((nseq, D_INNER, D_STATE), lambda i: (cur(i) // n_chunks, 0, 0))
    in_specs = [in_row_spec,
                _const_spec((1, D_MODEL)),
                _const_spec(win.shape),
                _const_spec((CONV_W, CONV_DIM)),
                _const_spec((1, CONV_DIM)),
                _const_spec((1, DT_PAD)),
                _const_spec((1, DT_PAD)),
                _const_spec((1, D_INNER)),
                _const_spec((1, D_INNER)),
                _const_spec((D_INNER, D_MODEL)),
                _const_spec((sb, sb)),
                _const_spec((sb, sb)),
                conv_spec, state_spec]
    x_new, conv_new, state_new = pl.pallas_call(
        kern,
        grid=(n_steps + 1,),
        in_specs=in_specs,
        out_specs=(row_spec, conv_spec, state_spec),
        out_shape=out_shape,
        scratch_shapes=[pltpu.VMEM((r, D_MODEL), F32),
                        pltpu.VMEM((r, D_INNER), F32),
                        pltpu.VMEM((r, CONV_DIM), F32),
                        pltpu.VMEM((r, DT_PAD), F32),
                        pltpu.VMEM((r, D_MODEL), F32),
                        pltpu.VMEM((r, D_INNER), F32),
                        pltpu.VMEM((nseq * 8, CONV_DIM), F32),
                        pltpu.VMEM((r, CONV_DIM), F32),
                        pltpu.VMEM((r, D_INNER), F32),
                        pltpu.VMEM((r, D_INNER), F32),
                        pltpu.VMEM((r, D_INNER), BF16),
                        pltpu.VMEM((r, D_MODEL), BF16),
                        pltpu.VMEM((r, D_INNER), BF16)],
        compiler_params=pltpu.CompilerParams(
            dimension_semantics=("arbitrary",), vmem_limit_bytes=VMEM_LIMIT),
        name="ssd_layer",
    )(x, norm_g.reshape(1, D_MODEL), win, conv_w, conv_b.reshape(1, CONV_DIM), dtb, alog, dsk,
      norm_y.reshape(1, D_INNER), w_out.astype(BF16), tri, ones, conv_prev, state)
    return (x_new, conv_new.reshape(1, n_seq, CONV_W - 1, CONV_DIM),
            state_new.reshape(1, n_seq, N_HEADS, HEAD_DIM, D_STATE))


def _sg_kernel(x_ref, g_ref, win_ref, bin_ref, lng_ref, lnb_ref, wmix_ref, bmix_ref, wout_ref,
               xo_ref, *rest, r, want_v):
    v_ref = rest[0] if want_v else None
    hn_scr, uv_scr, vb_scr, um_scr = rest[-4:]
    row_tiles = [slice(a, a + 128) for a in range(0, r, 128)]
    col128 = lambda total: [slice(c, c + 128) for c in range(0, total, 128)]

    for rt in row_tiles:
        ss = None
        for ct in col128(D_MODEL):
            xv = x_ref[rt, ct]
            part = jnp.sum(xv * xv, axis=-1, keepdims=True)
            ss = part if ss is None else ss + part
        scale = lax.rsqrt(ss * (1.0 / D_MODEL) + NORM_EPS)
        for ct in col128(D_MODEL):
            hn_scr[rt, ct] = (x_ref[rt, ct] * scale * g_ref[:, ct]).astype(BF16)

    vcols = [slice(SG_WIDTH + c.start, SG_WIDTH + c.stop) for c in col128(SG_WIDTH)]

    def front(rt):
        def piece(c0):
            def run():
                h = _dot(hn_scr[rt, :], win_ref[:, c0:c0 + PROJ_CHUNK])
                for cc in range(0, PROJ_CHUNK, 128):
                    ct = slice(c0 + cc, c0 + cc + 128)
                    uv_scr[rt, ct] = _gelu_tanh(h[:, cc:cc + 128] + bin_ref[:, ct])
            return run
        return [piece(c0) for c0 in range(0, 2 * SG_WIDTH, PROJ_CHUNK)]

    def back(rt):
        stats = {}

        def ln_stats():
            tot = None
            for ct in vcols:
                part = jnp.sum(uv_scr[rt, ct], axis=-1, keepdims=True)
                tot = part if tot is None else tot + part
            mu = tot * (1.0 / SG_WIDTH)
            ss = None
            for ct in vcols:
                vc = uv_scr[rt, ct] - mu
                part = jnp.sum(vc * vc, axis=-1, keepdims=True)
                ss = part if ss is None else ss + part
            stats["mu"] = mu
            stats["scale"] = lax.rsqrt(ss * (1.0 / SG_WIDTH) + LN_EPS)

        def ln_apply(k0):
            def run():
                for ct, c in list(zip(vcols, col128(SG_WIDTH)))[k0:k0 + 4]:
                    vn = (uv_scr[rt, ct] - stats["mu"]) * stats["scale"] * lng_ref[:, c] + lnb_ref[:, c]
                    if want_v:
                        v_ref[rt, c] = vn
                    vb_scr[rt, c] = vn.astype(BF16)
            return run

        def mix(g):
            def run():
                cols = slice(g * SG_GROUP_DIM, (g + 1) * SG_GROUP_DIM)
                mixed = _dot(wmix_ref[g], vb_scr[rt, cols]) + bmix_ref[:, cols]
                um_scr[rt, cols] = (uv_scr[rt, cols] * mixed).astype(BF16)
            return run

        def out(c0):
            def run():
                ct = slice(c0, c0 + PROJ_CHUNK)
                xo_ref[rt, ct] = x_ref[rt, ct] + _dot(um_scr[rt, :], wout_ref[:, ct])
            return run

        return ([ln_stats] + [ln_apply(k) for k in range(0, len(vcols), 4)]
                + [mix(g) for g in range(SG_GROUPS)]
                + [out(c0) for c0 in range(0, D_MODEL, PROJ_CHUNK)])

    pending = []
    for rt in row_tiles:
        for piece in front(rt):
            piece()
            if pending:
                pending.pop(0)()
        for piece in pending:
            piece()
        pending = back(rt)
    for piece in pending:
        piece()


def _sg_layer(x2d, norm_g, w_in, b_in, ln_g, ln_b, w_s, b_s, w_out, *, seq_len, want_v):
    r = SG_ROWS
    t = x2d.shape[0]
    q = min(seq_len, SG_CHUNK)
    reps = SG_CHUNK // q
    ws = jnp.tril(w_s)[:, :q, :q]
    wmix = jnp.einsum("ab,gts->gatbs", jnp.eye(reps, dtype=F32), ws)
    wmix = wmix.reshape(SG_GROUPS, SG_CHUNK, SG_CHUNK)
    bmix = jnp.tile(jnp.repeat(b_s.T[:q], SG_GROUP_DIM, axis=1), (reps, 1))
    row_spec = pl.BlockSpec((r, D_MODEL), lambda i: (i, 0))
    v_spec = pl.BlockSpec((r, SG_WIDTH), lambda i: (i, 0))
    out_shape = [jax.ShapeDtypeStruct(x2d.shape, F32)]
    out_specs = [row_spec]
    if want_v:
        out_shape.append(jax.ShapeDtypeStruct((t, SG_WIDTH), F32))
        out_specs.append(v_spec)
    outs = pl.pallas_call(
        functools.partial(_sg_kernel, r=r, want_v=want_v),
        grid=(t // r,),
        scratch_shapes=[pltpu.VMEM((r, D_MODEL), BF16),
                        pltpu.VMEM((r, 2 * SG_WIDTH), F32),
                        pltpu.VMEM((r, SG_WIDTH), BF16),
                        pltpu.VMEM((r, SG_WIDTH), BF16)],
        in_specs=[row_spec,
                  _const_spec((1, D_MODEL)),
                  _const_spec((D_MODEL, 2 * SG_WIDTH)),
                  _const_spec((1, 2 * SG_WIDTH)),
                  _const_spec((1, SG_WIDTH)),
                  _const_spec((1, SG_WIDTH)),
                  _const_spec((SG_GROUPS, SG_CHUNK, SG_CHUNK)),
                  _const_spec((SG_CHUNK, SG_WIDTH)),
                  _const_spec((SG_WIDTH, D_MODEL))],
        out_specs=out_specs,
        out_shape=out_shape,
        compiler_params=pltpu.CompilerParams(
            dimension_semantics=("arbitrary",), vmem_limit_bytes=VMEM_LIMIT),
        name="sg_layer",
    )(x2d, norm_g.reshape(1, D_MODEL), w_in.astype(BF16), b_in.reshape(1, 2 * SG_WIDTH),
      ln_g.reshape(1, SG_WIDTH), ln_b.reshape(1, SG_WIDTH), wmix.astype(BF16), bmix,
      w_out.astype(BF16))
    return outs if want_v else (outs[0], None)


ROUTER_ROWS = 64


def _norm_tiles(use_a, xa_ref, xb_ref, g_ref, store):
    def load(rt, ct):
        return jnp.where(use_a, xa_ref[rt, ct], xb_ref[rt, ct])

    for r0 in range(0, xa_ref.shape[0], 128):
        rt = slice(r0, r0 + 128)
        ss = None
        for c0 in range(0, D_MODEL, 256):
            xv = load(rt, slice(c0, c0 + 256))
            part = jnp.sum(xv * xv, axis=-1, keepdims=True)
            ss = part if ss is None else ss + part
        scale = lax.rsqrt(ss * (1.0 / D_MODEL) + NORM_EPS)
        for c0 in range(0, D_MODEL, 256):
            ct = slice(c0, c0 + 256)
            store(rt, ct, load(rt, ct) * scale * g_ref[:, ct])


def _router_t_kernel(xa_ref, xb_ref, g_ref, wt_ref, bt_ref, usl_ref, lsl_ref,
                     meta_ref, metat_ref, pad_ref, h1_scr, h2_scr, *, steps_a):
    wt = wt_ref[...]
    w1 = wt.astype(BF16)
    w2 = (wt - w1.astype(F32)).astype(BF16)
    tt = MOE_TILE
    n = xa_ref.shape[0]

    def split_store(rt, ct, hn):
        hi = hn.astype(BF16)
        h1_scr[rt, ct] = hi
        h2_scr[rt, ct] = (hn - hi.astype(F32)).astype(BF16)

    _norm_tiles(pl.program_id(0) < steps_a, xa_ref, xb_ref, g_ref, split_store)

    h1 = h1_scr[...]
    logits = _dot_nt(w1, h1) + _dot_nt(w2, h1) + _dot_nt(w1, h2_scr[...])
    select = _select_experts(logits, bt_ref[...])
    for k in range(n // tt):
        cols = slice(k * tt, (k + 1) * tt)
        metat, pad = _local_order(select[:, cols], usl_ref[...], lsl_ref[...])
        metat_ref[k * 8:(k + 1) * 8, :] = metat
        full = jnp.concatenate([metat, jnp.zeros((128 - 8, tt), F32)], axis=0)
        meta_ref[cols, :] = full.T
        pad_ref[k * MOE_EXPERTS:(k + 1) * MOE_EXPERTS, :] = pad


def _select_experts(logits, bt):
    n = logits.shape[1]
    reps = n // bt.shape[1]
    logits = logits + (bt if reps == 1 else jnp.concatenate([bt] * reps, axis=1))
    tt = n
    row8 = lax.broadcasted_iota(I32, (8, tt), 0).astype(F32)
    gl = jnp.where(row8 < MOE_GROUPS, logits[0:8, :], -jnp.inf)
    gmax = jnp.max(gl, axis=0, keepdims=True)
    g_top = jnp.min(jnp.where(gl == gmax, row8, 8.0), axis=0, keepdims=True)
    p_g = 1.0 / jnp.sum(jnp.exp(gl - gmax), axis=0, keepdims=True)

    el = logits[8:16, :]
    for grp in range(1, MOE_GROUPS):
        el = jnp.where(g_top == grp, logits[8 + 8 * grp:16 + 8 * grp, :], el)
    emax = jnp.max(el, axis=0, keepdims=True)
    ee = jnp.exp(el - emax)
    prob = ee / jnp.sum(ee, axis=0, keepdims=True)
    p1 = jnp.max(prob, axis=0, keepdims=True)
    i1 = jnp.min(jnp.where(prob == p1, row8, 8.0), axis=0, keepdims=True)
    prob2 = jnp.where(row8 == i1, -1.0, prob)
    p2 = jnp.max(prob2, axis=0, keepdims=True)
    i2 = jnp.min(jnp.where(prob2 == p2, row8, 8.0), axis=0, keepdims=True)
    psum = p1 + p2
    gate1 = p_g * (p1 / psum)
    gate2 = p_g * (p2 / psum)
    e1 = g_top * MOE_EPG + i1
    e2 = g_top * MOE_EPG + i2
    out = jnp.where(row8 == 0, e1, 0.0)
    out = jnp.where(row8 == 1, e2, out)
    out = jnp.where(row8 == 2, gate1, out)
    return jnp.where(row8 == 3, gate2, out)


def _local_order(select, usl, lsl):
    tt = select.shape[1]
    row8 = lax.broadcasted_iota(I32, (8, tt), 0).astype(F32)
    e1, e2 = select[0:1, :], select[1:2, :]
    rowe = lax.broadcasted_iota(I32, (MOE_EXPERTS, tt), 0).astype(F32)
    sel1 = rowe == e1
    sel2 = rowe == e2
    onehot = jnp.where(sel1 | sel2, 1.0, 0.0)
    before = _dot(onehot.astype(BF16), usl)
    cnt = jnp.sum(onehot, axis=1, keepdims=True)
    pad = jnp.floor((cnt + (GRANULE - 1)) * (1.0 / GRANULE)) * GRANULE
    padb = jnp.broadcast_to(pad, (MOE_EXPERTS, tt))
    local = before + _dot(lsl, padb.astype(BF16))
    slot1 = jnp.sum(jnp.where(sel1, local, 0.0), axis=0, keepdims=True)
    slot2 = jnp.sum(jnp.where(sel2, local, 0.0), axis=0, keepdims=True)

    metat = jnp.where(row8 == 4, slot1, select)
    metat = jnp.where(row8 == 5, slot2, metat)
    return metat, padb[:, 0:128]


def _granule(ref, g):
    return ref.at[pl.ds(pl.multiple_of(g * GRANULE, GRANULE), GRANULE), :]


def _dispatch_kernel(ngran_ref, nv_ref, dest_ref, gap_ref, xa_ref, xb_ref, g_ref, meta_ref, *rest,
                     tiles_a, fill):
    xs_hbm, buf, zbuf, hn_scr, sem, zsem, nstart = rest if fill else rest[1:]
    i = pl.program_id(0)
    n_tiles = pl.num_programs(0) - 1
    slot = i % 2
    tt = xa_ref.shape[0]
    bm = zbuf.shape[0]
    n_blocks = xs_hbm.shape[0] // bm

    def out_copy(sl, g, d):
        return pltpu.make_async_copy(_granule(buf.at[sl], g), _granule(xs_hbm, d), sem.at[sl])

    def drain(sl):
        def body(_, carry):
            out_copy(sl, 0, 0).wait()
            return carry
        lax.fori_loop(0, nstart[sl], body, 0)

    def tail_copy(b):
        return pltpu.make_async_copy(
            zbuf, xs_hbm.at[pl.ds(pl.multiple_of(b * bm, bm), bm), :], zsem.at[1])

    @pl.when(i == 0)
    def _():
        nstart[0] = 0
        nstart[1] = 0
        if fill:
            zbuf[...] = jnp.zeros_like(zbuf)

    def norm_store(rt, ct, hn):
        hn_scr[rt, ct] = hn.astype(BF16)

    tail = nv_ref[0] + i

    if fill:
        @pl.when((i >= 1) & (tail - 1 < n_blocks))
        def _():
            tail_copy(tail - 1).wait()

    @pl.when(i < n_tiles)
    def _():
        if fill:
            @pl.when(tail < n_blocks)
            def _():
                tail_copy(tail).start()
        _norm_tiles(i < tiles_a, xa_ref, xb_ref, g_ref, norm_store)
        drain(slot)
        hn = hn_scr[...]
        mt = meta_ref[...]
        for r0 in range(0, LOCAL_ROWS, 256):
            rows = (lax.broadcasted_iota(I32, (256, tt), 0) + r0).astype(F32)
            onehot = jnp.where((rows == mt[4:5, :]) | (rows == mt[5:6, :]), 1.0, 0.0).astype(BF16)
            buf[slot, r0:r0 + 256, :] = _dot(onehot, hn).astype(BF16)
        ng = ngran_ref[i]

        def body(g, carry):
            out_copy(slot, g, dest_ref[0, 0, g]).start()
            return carry
        lax.fori_loop(0, ng, body, 0)
        nstart[slot] = ng

    @pl.when(i == n_tiles)
    def _():
        drain(0)
        drain(1)

        def gap_copy(d):
            return pltpu.make_async_copy(_granule(zbuf, 0), _granule(xs_hbm, d), zsem.at[0])

        def each_gap(fn):
            def body(j, carry):
                d = gap_ref[0, 0, j]

                @pl.when(d >= 0)
                def _():
                    fn(gap_copy(d))
                return carry
            lax.fori_loop(0, gap_ref.shape[2], body, 0)

        def each_tail(fn):
            def body(b, carry):
                fn(tail_copy(b))
                return carry
            lax.fori_loop(nv_ref[0] + n_tiles, n_blocks, body, 0)

        if fill:
            each_gap(lambda cp: cp.start())
            each_tail(lambda cp: cp.start())
            each_gap(lambda cp: cp.wait())
            each_tail(lambda cp: cp.wait())


def _expert_kernel(bexp_ref, nv_ref, xs_ref, wg_ref, wu_ref, wd_ref, ys_ref,
                   wgu_b, wd_b, act_scr):
    b = pl.program_id(0)
    ff = wg_ref.shape[1]

    @pl.when(b < nv_ref[0])
    def _():
        @pl.when((b == 0) | (bexp_ref[b] != bexp_ref[jnp.maximum(b - 1, 0)]))
        def _():
            for k0 in range(0, wg_ref.shape[0], 256):
                wgu_b[k0:k0 + 256, 0:ff] = wg_ref[k0:k0 + 256, :].astype(BF16)
                wgu_b[k0:k0 + 256, ff:2 * ff] = wu_ref[k0:k0 + 256, :].astype(BF16)
            for k0 in range(0, ff, 64):
                wd_b[k0:k0 + 64, :] = wd_ref[k0:k0 + 64, :].astype(BF16)

        bm = xs_ref.shape[0]
        for m0 in range(0, bm, 256):
            h = _dot(xs_ref[m0:m0 + 256, :], wgu_b[...])
            for r0 in range(0, 256, 128):
                for c0 in range(0, ff, 128):
                    hg = h[r0:r0 + 128, c0:c0 + 128]
                    hu = h[r0:r0 + 128, ff + c0:ff + c0 + 128]
                    act_scr[m0 + r0:m0 + r0 + 128, c0:c0 + 128] = (_silu(hg) * hu).astype(BF16)
            ys_ref[m0:m0 + 256, :] = _dot(act_scr[m0:m0 + 256, :], wd_b[...]).astype(BF16)


def _combine_kernel(ngran_ref, src_ref, srcn_ref, xa_ref, xb_ref, meta_ref, gf_ref, ys_hbm,
                    oa_ref, ob_ref, buf, moe_scr, pick_scr, sem, *, final, tiles_a):
    i = pl.program_id(0)
    n = pl.num_programs(0)
    slot = i % 2
    tt = xa_ref.shape[0]

    def in_copy(sl, g, d):
        return pltpu.make_async_copy(_granule(ys_hbm, d), _granule(buf.at[sl], g), sem.at[sl])

    def gather(idx_ref, sl, ng):
        def body(g, carry):
            in_copy(sl, g, idx_ref[0, 0, g]).start()
            return carry
        lax.fori_loop(0, ng, body, 0)

    @pl.when(i == 0)
    def _():
        buf[...] = jnp.zeros_like(buf)
        gather(src_ref, 0, ngran_ref[0])

    @pl.when(i + 1 < n)
    def _():
        gather(srcn_ref, 1 - slot, ngran_ref[i + 1])

    def wait_body(_, carry):
        in_copy(slot, 0, 0).wait()
        return carry
    lax.fori_loop(0, ngran_ref[i], wait_body, 0)

    for r0 in range(0, tt, 128):
        rt = slice(r0, r0 + 128)
        meta = meta_ref[rt, :]
        for c0 in range(0, LOCAL_ROWS, 256):
            cols = (lax.broadcasted_iota(I32, (128, 256), 1) + c0).astype(F32)
            pick_scr[rt, c0:c0 + 256] = (
                jnp.where(cols == meta[:, 4:5], meta[:, 2:3], 0.0)
                + jnp.where(cols == meta[:, 5:6], meta[:, 3:4], 0.0)).astype(BF16)
    is_a = i < tiles_a
    for m0 in range(0, tt, 256):
        moe = _dot(pick_scr[m0:m0 + 256, :], buf[slot])
        for r0 in range(0, 256, 128):
            rt = slice(m0 + r0, m0 + r0 + 128)
            for c0 in range(0, D_MODEL, 256):
                ct = slice(c0, c0 + 256)
                moe_scr[rt, ct] = (jnp.where(is_a, xa_ref[rt, ct], xb_ref[rt, ct])
                                   + moe[r0:r0 + 128, ct])
    for r0 in range(0, tt, 128):
        rt = slice(r0, r0 + 128)
        if final:
            ss = None
            for c0 in range(0, D_MODEL, 256):
                yv = moe_scr[rt, c0:c0 + 256]
                part = jnp.sum(yv * yv, axis=-1, keepdims=True)
                ss = part if ss is None else ss + part
            scale = lax.rsqrt(ss * (1.0 / D_MODEL) + NORM_EPS)
            for c0 in range(0, D_MODEL, 256):
                ct = slice(c0, c0 + 256)
                moe_scr[rt, ct] = moe_scr[rt, ct] * scale * gf_ref[:, ct]

    def emit(o_ref):
        for r0 in range(0, tt, 128):
            for c0 in range(0, D_MODEL, 256):
                o_ref[r0:r0 + 128, c0:c0 + 256] = moe_scr[r0:r0 + 128, c0:c0 + 256]

    @pl.when(is_a)
    def _():
        emit(oa_ref)

    @pl.when(jnp.logical_not(is_a))
    def _():
        emit(ob_ref)


def _moe(xa, xb, norm_g, w_grp, b_grp, w_rt, b_rt, w_gate, w_up, w_down, norm_final, sorted_buf,
         *, layer):
    tt = MOE_TILE
    tiles_a, tiles_b = xa.shape[0] // tt, xb.shape[0] // tt
    steps_a, steps_b = tiles_a // ROUTER_TILES, tiles_b // ROUTER_TILES
    n_tiles = tiles_a + tiles_b
    t = n_tiles * tt
    bm = _expert_block(t)
    n_exp = MOE_EXPERTS
    a_blk = lambda i, n_a: jnp.minimum(i, n_a - 1)
    b_blk = lambda i, n_a, n_b: jnp.clip(i - n_a, 0, n_b - 1)
    zrow = lambda n: jnp.zeros((n, D_MODEL), F32)
    wt = jnp.concatenate([w_grp.T, zrow(8 - MOE_GROUPS), w_rt.T, zrow(ROUTER_ROWS - 8 - n_exp)], axis=0)
    bt = jnp.concatenate([b_grp, jnp.zeros((8 - MOE_GROUPS,), F32), b_rt,
                          jnp.zeros((ROUTER_ROWS - 8 - n_exp,), F32)])
    bt = jnp.broadcast_to(bt[:, None], (ROUTER_ROWS, tt))
    usl = jnp.asarray(np.triu(np.ones((tt, tt)), 1), BF16)
    lsl = jnp.asarray(np.tril(np.ones((n_exp, n_exp)), -1), BF16)
    g2d = norm_g.reshape(1, D_MODEL)
    arb = pltpu.CompilerParams(dimension_semantics=("arbitrary",), vmem_limit_bytes=VMEM_LIMIT)

    meta, metat, pad = pl.pallas_call(
        functools.partial(_router_t_kernel, steps_a=steps_a),
        grid=(steps_a + steps_b,),
        in_specs=[pl.BlockSpec((ROUTER_TILES * tt, D_MODEL), lambda i: (a_blk(i, steps_a), 0)),
                  pl.BlockSpec((ROUTER_TILES * tt, D_MODEL),
                               lambda i: (b_blk(i, steps_a, steps_b), 0)),
                  _const_spec((1, D_MODEL)),
                  _const_spec((ROUTER_ROWS, D_MODEL)),
                  _const_spec((ROUTER_ROWS, tt)),
                  _const_spec((tt, tt)),
                  _const_spec((n_exp, n_exp))],
        out_specs=(pl.BlockSpec((ROUTER_TILES * tt, 128), lambda i: (i, 0)),
                   pl.BlockSpec((ROUTER_TILES * 8, tt), lambda i: (i, 0)),
                   pl.BlockSpec((ROUTER_TILES * n_exp, 128), lambda i: (i, 0))),
        out_shape=(jax.ShapeDtypeStruct((t, 128), F32),
                   jax.ShapeDtypeStruct((n_tiles * 8, tt), F32),
                   jax.ShapeDtypeStruct((n_tiles * n_exp, 128), F32)),
        scratch_shapes=[pltpu.VMEM((ROUTER_TILES * tt, D_MODEL), BF16),
                        pltpu.VMEM((ROUTER_TILES * tt, D_MODEL), BF16)],
        compiler_params=arb,
        name="moe_router",
    )(xa, xb, g2d, wt, bt, usl, lsl)

    runs = pad.reshape(n_tiles, n_exp, 128)[:, :, 0].astype(I32)
    rows_e = jnp.sum(runs, axis=0)
    nblk = (rows_e + bm - 1) // bm
    blk_end = jnp.cumsum(nblk)
    e_start = (blk_end - nblk) * bm
    n_valid = blk_end[-1]
    run_end = jnp.cumsum(runs, axis=1)
    ngran = (run_end[:, -1] // GRANULE).astype(I32)
    shift = e_start[None, :] + (jnp.cumsum(runs, axis=0) - runs) - (run_end - runs)
    g_row = jnp.arange(N_LOCAL_GRAN, dtype=I32) * GRANULE
    e_of_g = jnp.sum((run_end[:, None, :] <= g_row[None, :, None]).astype(I32), axis=-1)
    shift_g = jnp.sum(jnp.where(e_of_g[..., None] == jnp.arange(n_exp, dtype=I32),
                                shift[:, None, :], 0), axis=-1)
    dest = jnp.where(e_of_g < n_exp, (shift_g + g_row[None, :]) // GRANULE, 0)
    dest = dest.astype(I32).reshape(n_tiles, 1, N_LOCAL_GRAN)
    per_blk = bm // GRANULE
    gap = ((e_start + rows_e) // GRANULE)[:, None] + jnp.arange(per_blk, dtype=I32)[None, :]
    gap = jnp.where(gap < ((e_start + nblk * bm) // GRANULE)[:, None], gap, -1)
    gap = gap.astype(I32).reshape(1, 1, n_exp * per_blk)
    n_blocks = (2 * t + n_tiles * n_exp * (GRANULE - 1)) // bm + 1 + n_exp
    blk = jnp.minimum(jnp.arange(n_blocks, dtype=I32), n_valid - 1)
    block_expert = jnp.sum((blk[:, None] >= blk_end[None, :]).astype(I32), axis=1).astype(I32)
    n_valid = n_valid.reshape(1).astype(I32)

    last = n_tiles - 1
    fill = sorted_buf is None
    reuse_spec = [] if fill else [pl.BlockSpec(memory_space=pl.ANY)]
    reuse_arg = [] if fill else [sorted_buf]
    xs = pl.pallas_call(
        functools.partial(_dispatch_kernel, tiles_a=tiles_a, fill=fill),
        grid_spec=pltpu.PrefetchScalarGridSpec(
            num_scalar_prefetch=2,
            grid=(n_tiles + 1,),
            in_specs=[pl.BlockSpec((1, 1, N_LOCAL_GRAN),
                                   lambda i, ng, nv: (jnp.minimum(i, last), 0, 0),
                                   memory_space=pltpu.SMEM),
                      pl.BlockSpec((1, 1, n_exp * per_blk), lambda i, ng, nv: (0, 0, 0),
                                   memory_space=pltpu.SMEM),
                      pl.BlockSpec((tt, D_MODEL), lambda i, ng, nv: (a_blk(i, tiles_a), 0)),
                      pl.BlockSpec((tt, D_MODEL),
                                   lambda i, ng, nv: (b_blk(i, tiles_a, tiles_b), 0)),
                      _const_spec((1, D_MODEL)),
                      pl.BlockSpec((8, tt), lambda i, ng, nv: (jnp.minimum(i, last), 0))]
            + reuse_spec,
            out_specs=pl.BlockSpec(memory_space=pl.ANY),
            scratch_shapes=[pltpu.VMEM((2, LOCAL_ROWS, D_MODEL), BF16),
                            pltpu.VMEM((bm, D_MODEL), BF16),
                            pltpu.VMEM((tt, D_MODEL), BF16),
                            pltpu.SemaphoreType.DMA((2,)),
                            pltpu.SemaphoreType.DMA((2,)),
                            pltpu.SMEM((2,), I32)]),
        out_shape=jax.ShapeDtypeStruct((n_blocks * bm, D_MODEL), BF16),
        input_output_aliases={} if fill else {8: 0},
        compiler_params=arb,
        name="moe_dispatch",
    )(ngran, n_valid, dest, gap, xa, xb, g2d, metat, *reuse_arg)

    w_spec = lambda shape: pl.BlockSpec((None, None) + shape,
                                        lambda b, be, nv: (layer, be[b], 0, 0))
    row_blk = pl.BlockSpec((bm, D_MODEL), lambda b, be, nv: (jnp.minimum(b, nv[0] - 1), 0))
    ys = pl.pallas_call(
        _expert_kernel,
        grid_spec=pltpu.PrefetchScalarGridSpec(
            num_scalar_prefetch=2,
            grid=(n_blocks,),
            in_specs=[row_blk,
                      w_spec((D_MODEL, MOE_D_FF)),
                      w_spec((D_MODEL, MOE_D_FF)),
                      w_spec((MOE_D_FF, D_MODEL))],
            out_specs=row_blk,
            scratch_shapes=[pltpu.VMEM((D_MODEL, 2 * MOE_D_FF), BF16),
                            pltpu.VMEM((MOE_D_FF, D_MODEL), BF16),
                            pltpu.VMEM((bm, MOE_D_FF), BF16)]),
        out_shape=jax.ShapeDtypeStruct((n_blocks * bm, D_MODEL), BF16),
        input_output_aliases={2: 0},
        compiler_params=arb,
        name="moe_experts",
    )(block_expert, n_valid, xs, w_gate, w_up, w_down)

    final = norm_final is not None
    gf = (norm_final if final else jnp.ones((D_MODEL,), F32)).reshape(1, D_MODEL)
    src_spec = lambda nxt: pl.BlockSpec(
        (1, 1, N_LOCAL_GRAN), lambda i, ng: (jnp.minimum(i + nxt, last), 0, 0),
        memory_space=pltpu.SMEM)
    a_spec = pl.BlockSpec((tt, D_MODEL), lambda i, ng: (a_blk(i, tiles_a), 0))
    b_spec = pl.BlockSpec((tt, D_MODEL), lambda i, ng: (b_blk(i, tiles_a, tiles_b), 0))
    out_a, out_b = pl.pallas_call(
        functools.partial(_combine_kernel, final=final, tiles_a=tiles_a),
        grid_spec=pltpu.PrefetchScalarGridSpec(
            num_scalar_prefetch=1,
            grid=(n_tiles,),
            in_specs=[src_spec(0), src_spec(1), a_spec, b_spec,
                      pl.BlockSpec((tt, 128), lambda i, ng: (i, 0)),
                      _const_spec((1, D_MODEL)),
                      pl.BlockSpec(memory_space=pl.ANY)],
            out_specs=(a_spec, b_spec),
            scratch_shapes=[pltpu.VMEM((2, LOCAL_ROWS, D_MODEL), BF16),
                            pltpu.VMEM((tt, D_MODEL), F32),
                            pltpu.VMEM((tt, LOCAL_ROWS), BF16),
                            pltpu.SemaphoreType.DMA((2,))]),
        out_shape=(jax.ShapeDtypeStruct(xa.shape, F32), jax.ShapeDtypeStruct(xb.shape, F32)),
        compiler_params=arb,
        name="moe_combine",
    )(ngran, dest, dest, xa, xb, meta, gf, ys)
    return out_a, out_b, ys


def _ssd(x, conv_prev, ssm_prev, p, *, nsub, ns):
    seq_len = x.shape[1]
    q = seq_len if seq_len < SSD_CHUNK else SSD_CHUNK
    x, conv_new, ssm_new = _ssd_layer(
        x, conv_prev, ssm_prev, p["norm_mix"][0], p["ssd_w_in"][0], p["ssd_conv_w"][0],
        p["ssd_conv_b"][0], p["ssd_dt_bias"][0], p["ssd_a_log"][0], p["ssd_d"][0], p["ssd_norm"][0],
        p["ssd_w_out"][0], nsub=nsub, ns=ns, q=q)
    return x.reshape(-1, D_MODEL), conv_new, ssm_new


def _sg(x2d, p, *, seq_len, want_v):
    return _sg_layer(x2d, p["norm_mix"][1], p["sg_w_in"][0], p["sg_b_in"][0], p["sg_ln_g"][0],
                     p["sg_ln_b"][0], p["sg_w_s"][0], p["sg_b_s"][0], p["sg_w_out"][0],
                     seq_len=seq_len, want_v=want_v)


def _moe_layer(xa, xb, p, layer, norm_final, sorted_buf):
    return _moe(xa, xb, p["norm_ffn"][layer], p["moe_w_group"][layer], p["moe_b_group"][layer],
                p["moe_w_router"][layer], p["moe_b_router"][layer], p["moe_w_gate"], p["moe_w_up"],
                p["moe_w_down"], norm_final, sorted_buf, layer=layer)


def kernel(x_prompt, x_sample, state_ssm, state_conv, norm_mix, norm_ffn, norm_final, ssd_w_in, ssd_conv_w, ssd_conv_b, ssd_dt_bias, ssd_a_log, ssd_d, ssd_norm, ssd_w_out, sg_w_in, sg_b_in, sg_ln_g, sg_ln_b, sg_w_s, sg_b_s, sg_w_out, moe_w_group, moe_b_group, moe_w_router, moe_b_router, moe_w_gate, moe_w_up, moe_w_down):
    p = dict(norm_mix=norm_mix, norm_ffn=norm_ffn, norm_final=norm_final, ssd_w_in=ssd_w_in,
             ssd_conv_w=ssd_conv_w, ssd_conv_b=ssd_conv_b, ssd_dt_bias=ssd_dt_bias,
             ssd_a_log=ssd_a_log, ssd_d=ssd_d, ssd_norm=ssd_norm, ssd_w_out=ssd_w_out,
             sg_w_in=sg_w_in, sg_b_in=sg_b_in, sg_ln_g=sg_ln_g, sg_ln_b=sg_ln_b, sg_w_s=sg_w_s,
             sg_b_s=sg_b_s, sg_w_out=sg_w_out, moe_w_group=moe_w_group, moe_b_group=moe_b_group,
             moe_w_router=moe_w_router, moe_b_router=moe_b_router, moe_w_gate=moe_w_gate,
             moe_w_up=moe_w_up, moe_w_down=moe_w_down)
    nb = x_prompt.shape[0]
    conv0 = jnp.zeros((1, nb, CONV_W - 1, CONV_DIM), F32)
    ssm0 = jnp.zeros((1, nb, N_HEADS, HEAD_DIM, D_STATE), F32)
    xp, conv_p, ssm_p = _ssd(x_prompt, conv0, ssm0, p, nsub=2, ns=1)
    xs, conv_s, ssm_s = _ssd(x_sample, state_conv, state_ssm, p, nsub=1, ns=4)
    xp, xs, sorted_buf = _moe_layer(xp, xs, p, 0, None, None)
    xp, _ = _sg(xp, p, seq_len=x_prompt.shape[1], want_v=False)
    xs, v_s = _sg(xs, p, seq_len=x_sample.shape[1], want_v=True)
    y_p, y_s, _ = _moe_layer(xp, xs, p, 1, norm_final, sorted_buf)
    return (y_p.reshape(x_prompt.shape), y_s.reshape(x_sample.shape), ssm_p, conv_p, ssm_s, conv_s,
            v_s.reshape((1,) + x_sample.shape[:2] + (SG_WIDTH,)))
```

```python
import functools
import math

import jax
import jax.numpy as jnp
import numpy as np
from jax import lax
from jax.experimental import pallas as pl
from jax.experimental.pallas import tpu as pltpu

F32 = jnp.float32
BF16 = jnp.bfloat16
I32 = jnp.int32

D_MODEL = 1024
N_HEADS = 32
HEAD_DIM = 64
N_GROUPS = 4
D_STATE = 128
D_INNER = N_HEADS * HEAD_DIM
GROUP_W = D_INNER // N_GROUPS
CONV_W = 4
CONV_DIM = D_INNER + 2 * N_GROUPS * D_STATE
DT_PAD = 128
SSD_CHUNK = 128
PROJ_CHUNK = 256
SG_WIDTH = 2 * D_MODEL
SG_GROUPS = 8
SG_GROUP_DIM = SG_WIDTH // SG_GROUPS
SG_CHUNK = 128
SG_ROWS = 512
MOE_GROUPS = 4
MOE_EPG = 8
MOE_EXPERTS = MOE_GROUPS * MOE_EPG
MOE_D_FF = 256
NORM_EPS = 1e-6
LN_EPS = 1e-5

MOE_TILE = 512
ROUTER_TILES = 2


def _expert_block(n_tokens):
    return 512 if 2 * n_tokens >= 512 * MOE_EXPERTS else 256


GRANULE = 16
LOCAL_ROWS = 2 * MOE_TILE + MOE_EXPERTS * GRANULE
N_LOCAL_GRAN = LOCAL_ROWS // GRANULE
V7X_VMEM_BYTES = 64 * 1024 * 1024
VMEM_LIMIT = V7X_VMEM_BYTES - 8 * 1024 * 1024


def _sigmoid(x):
    return 0.5 * (jnp.tanh(0.5 * x) + 1.0)


def _silu(x):
    return x * _sigmoid(x)


def _softplus(x):
    return jnp.maximum(x, 0.0) + jnp.log(1.0 + jnp.exp(-jnp.abs(x)))


def _gelu_tanh(x):
    c = math.sqrt(2.0 / math.pi)
    t = jnp.tanh(x * (c + (c * 0.044715) * (x * x)))
    hx = 0.5 * x
    return hx + hx * t


def _rms(x, g):
    return x * lax.rsqrt(jnp.mean(x * x, axis=-1, keepdims=True) + NORM_EPS) * g


def _split3(x):
    a = x.astype(BF16)
    r = x - a.astype(F32)
    b = r.astype(BF16)
    c = (r - b.astype(F32)).astype(BF16)
    return a, b, c


def _dot(a, b):
    return jnp.dot(a, b, preferred_element_type=F32)


def _dot_nt(a, b):
    return lax.dot_general(a, b, (((1,), (1,)), ((), ())), preferred_element_type=F32)


def _dot_tn(a, b):
    return lax.dot_general(a, b, (((0,), (0,)), ((), ())), preferred_element_type=F32)


def _const_spec(shape):
    nd = len(shape)
    return pl.BlockSpec(shape, lambda *_: (0,) * nd)


def _ssd_kernel(x_ref, g_ref, win_ref, cw_ref, cb_ref, dtb_ref, alog_ref, dsk_ref, ng_ref,
                wout_ref, tri_ref, ones_ref, cin_ref, sin_ref,
                xo_ref, cout_ref, sout_ref,
                pend_x, pend_z, pend_xbc, pend_dt, xres_scr, zg_scr,
                prev_scr, xc_scr, yoff_scr, y_scr, xw_scr, hn_scr, yn_scr,
                *, nsub, ns, q, n_chunks):
    i = pl.program_id(0)
    first_chunk = lax.rem(jnp.maximum(i - 1, 0), n_chunks) == 0
    nseq = nsub * ns
    sb = ns * q
    r = nsub * sb

    @pl.when(i == 0)
    def _():
        pend_x[...] = jnp.zeros_like(pend_x)
        pend_z[...] = jnp.zeros_like(pend_z)
        pend_xbc[...] = jnp.zeros_like(pend_xbc)
        pend_dt[...] = jnp.zeros_like(pend_dt)

    tr = min(r, 128)
    row_tiles = [slice(a, a + tr) for a in range(0, r, tr)]

    def col_tiles(total, rows=tr):
        w = min(total, max(128, (32 * 1024) // rows))
        return [slice(c, c + w) for c in range(0, total, w)]

    for rt in row_tiles:
        for ct in col_tiles(D_MODEL):
            xres_scr[rt, ct] = pend_x[rt, ct]

    def gate_tile(rt, ct):
        def run():
            zg_scr[rt, ct] = _silu(pend_z[rt, ct])
        return run
    gate_tiles = [gate_tile(rt, ct) for rt in row_tiles for ct in col_tiles(D_INNER)]

    for s in range(nseq):
        for ct in col_tiles(D_MODEL, q):
            pend_x[s * q:(s + 1) * q, ct] = x_ref[s, :, ct]
    for rt in row_tiles:
        ss = None
        for ct in col_tiles(D_MODEL):
            xv = pend_x[rt, ct]
            part = jnp.sum(xv * xv, axis=-1, keepdims=True)
            ss = part if ss is None else ss + part
        scale = lax.rsqrt(ss * (1.0 / D_MODEL) + NORM_EPS)
        for ct in col_tiles(D_MODEL):
            hn_scr[rt, ct] = (pend_x[rt, ct] * scale * g_ref[:, ct]).astype(BF16)

    def proj_chunk(dst, dst_col, w_col, width):
        def run():
            dst[:, dst_col:dst_col + width] = _dot(hn_scr[...], win_ref[:, w_col:w_col + width])
        return run
    z_chunks = [proj_chunk(pend_z, c, c, PROJ_CHUNK) for c in range(0, D_INNER, PROJ_CHUNK)]
    xbc_chunks = [proj_chunk(pend_xbc, c, D_INNER + c, PROJ_CHUNK)
                  for c in range(0, CONV_DIM, PROJ_CHUNK)]
    dt_chunk = proj_chunk(pend_dt, 0, D_INNER + CONV_DIM, DT_PAD)

    @pl.when(first_chunk)
    def _():
        sout_ref[...] = sin_ref[...]
        prev_scr[...] = jnp.zeros_like(prev_scr)
        for s in range(nseq):
            for k in range(CONV_W - 1):
                prev_scr[pl.ds(s * 8 + 5 + k, 1), :] = cin_ref[s, pl.ds(k, 1), :]

    sub8 = lax.broadcasted_iota(I32, (8, 1), 0)
    for s in range(nseq):
        srows = slice(s * q, (s + 1) * q)
        for ct in col_tiles(CONV_DIM, 4 * q):
            xq = pend_xbc[srows, ct]
            hist = prev_scr[s * 8:(s + 1) * 8, ct]
            acc = cb_ref[:, ct] + cw_ref[pl.ds(CONV_W - 1, 1), ct] * xq
            for j in range(1, CONV_W):
                sh = pltpu.roll(xq, j, 0)
                head = jnp.where(sub8 < j, pltpu.roll(hist, j, 0), sh[0:8, :])
                sh = head if q == 8 else jnp.concatenate([head, sh[8:, :]], axis=0)
                acc = acc + cw_ref[pl.ds(CONV_W - 1 - j, 1), ct] * sh
            xc_scr[srows, ct] = _silu(acc)
            last = xq[q - 8:q, :]
            prev_scr[s * 8:(s + 1) * 8, ct] = last
            for k in range(CONV_W - 1):
                cout_ref[s, pl.ds(k, 1), ct] = last[5 + k:6 + k, :]
    chunks = xbc_chunks + z_chunks + [dt_chunk]
    n_pairs = nsub * (N_HEADS // 2)
    emit_at = {}
    for k, tile in enumerate(gate_tiles):
        emit_at.setdefault(k, []).append(tile)
    for k, ch in enumerate(chunks):
        emit_at.setdefault((k * n_pairs) // len(chunks), []).append(ch)
    first_z = (len(xbc_chunks) * n_pairs) // len(chunks)
    assert len(gate_tiles) <= first_z, "gate tiles must be emitted before pend_z is overwritten"

    tri = tri_ref[...]
    trib = tri.astype(BF16)
    onesb = ones_ref[...].astype(BF16)
    mask = tri > 0.5
    rowseq = lax.shift_right_logical(lax.broadcasted_iota(I32, (sb, 1), 0), int(math.log2(q)))
    lo = lax.broadcasted_iota(I32, (sb, 128), 1) < HEAD_DIM
    neg_a = -jnp.exp(alog_ref[...])

    for u in range(nsub):
        rows = slice(u * sb, (u + 1) * sb)
        xs_ref = xc_scr.at[rows, 0:D_INNER]

        def b_of(g):
            return xc_scr[rows, D_INNER + g * D_STATE:D_INNER + (g + 1) * D_STATE]

        def c_of(g):
            c0 = D_INNER + (N_GROUPS + g) * D_STATE
            return xc_scr[rows, c0:c0 + D_STATE].astype(BF16)

        dt = _softplus(pend_dt[rows, :] + dtb_ref[...])
        d1, d2, d3 = _split3(dt * neg_a)
        cs = _dot(trib, d1) + _dot(trib, d2) + _dot(trib, d3)
        cl = _dot(onesb, d1) + _dot(onesb, d2) + _dot(onesb, d3)
        ecs = jnp.exp(cs)
        wgt = dt * jnp.exp(cl - cs)
        ecl = jnp.exp(cl)
        cs_t = cs.T
        dt_t = dt.T

        for g in range(N_GROUPS):
            cg = c_of(g)
            acc = None
            for s in range(ns):
                st = sout_ref[u * ns + s, g * GROUP_W:(g + 1) * GROUP_W, :].astype(BF16)
                yo = _dot_nt(cg, st)
                if ns > 1:
                    yo = jnp.where(rowseq == s, yo, 0.0)
                acc = yo if acc is None else acc + yo
            yoff_scr[rows, g * GROUP_W:(g + 1) * GROUP_W] = acc

        for g in range(N_GROUPS):
            sc = _dot_nt(c_of(g), b_of(g).astype(BF16))
            for jj in range(N_HEADS // N_GROUPS // 2):
                j = g * (N_HEADS // N_GROUPS // 2) + jj
                ms = []
                for h in (2 * j, 2 * j + 1):
                    diff = cs[:, h:h + 1] - cs_t[h:h + 1, :]
                    dec = jnp.exp(jnp.where(mask, diff, -jnp.inf))
                    ms.append((sc * dec * dt_t[h:h + 1, :]).astype(BF16))
                lhs = jnp.concatenate(ms, axis=1)
                cols = slice(j * 128, (j + 1) * 128)
                xp = xs_ref[:, cols]
                rhs = jnp.concatenate([jnp.where(lo, xp, 0.0).astype(BF16),
                                       jnp.where(lo, 0.0, xp).astype(BF16)], axis=0)
                yd = _dot(lhs, rhs)
                ecs_p = jnp.where(lo, ecs[:, 2 * j:2 * j + 1], ecs[:, 2 * j + 1:2 * j + 2])
                wgt_p = jnp.where(lo, wgt[:, 2 * j:2 * j + 1], wgt[:, 2 * j + 1:2 * j + 2])
                y_scr[rows, cols] = yd + yoff_scr[rows, cols] * ecs_p + xp * dsk_ref[:, cols]
                xw_scr[rows, cols] = (xp * wgt_p).astype(BF16)
                pair = u * (N_HEADS // 2) + j
                for ch in emit_at.get(pair, []):
                    ch()

        for g in range(N_GROUPS):
            xwg = xw_scr[rows, g * GROUP_W:(g + 1) * GROUP_W]
            for s in range(ns):
                bg = b_of(g)
                if ns > 1:
                    bg = jnp.where(rowseq == s, bg, 0.0)
                upd = _dot_tn(xwg, bg.astype(BF16))
                for hh in range(GROUP_W // HEAD_DIM):
                    h = g * (GROUP_W // HEAD_DIM) + hh
                    dec = jnp.broadcast_to(ecl[s * q:s * q + 1, h:h + 1], (HEAD_DIM, D_STATE))
                    hrows = slice(h * HEAD_DIM, (h + 1) * HEAD_DIM)
                    sout_ref[u * ns + s, hrows, :] = (
                        sout_ref[u * ns + s, hrows, :] * dec + upd[hh * HEAD_DIM:(hh + 1) * HEAD_DIM, :])

    for rt in row_tiles:
        for g in range(N_GROUPS):
            cts = [slice(g * GROUP_W + c.start, g * GROUP_W + c.stop) for c in col_tiles(GROUP_W)]
            ss = None
            for ct in cts:
                yz = y_scr[rt, ct] * zg_scr[rt, ct]
                part = jnp.sum(yz * yz, axis=-1, keepdims=True)
                ss = part if ss is None else ss + part
            scale = lax.rsqrt(ss * (1.0 / GROUP_W) + NORM_EPS)
            for ct in cts:
                yn_scr[rt, ct] = (y_scr[rt, ct] * zg_scr[rt, ct] * scale * ng_ref[:, ct]).astype(BF16)
        for c0 in range(0, D_MODEL, PROJ_CHUNK):
            ct = slice(c0, c0 + PROJ_CHUNK)
            o = xres_scr[rt, ct] + _dot(yn_scr[rt, :], wout_ref[:, ct])
            for s in range(rt.start // q, rt.stop // q):
                xo_ref[s, :, ct] = o[s * q - rt.start:(s + 1) * q - rt.start, :]


def _ssd_layer(x, conv_prev, ssm_prev, norm_g, w_in, conv_w, conv_b, dt_bias, a_log, d_skip,
               norm_y, w_out, *, nsub, ns, q):
    n_seq, seq_len, _ = x.shape
    nseq = nsub * ns
    sb = ns * q
    r = nsub * sb
    n_chunks = seq_len // q
    pad = DT_PAD - N_HEADS
    win = jnp.concatenate([w_in, jnp.zeros((D_MODEL, pad), F32)], axis=1).astype(BF16)
    dtb = jnp.pad(dt_bias, (0, pad)).reshape(1, DT_PAD)
    alog = jnp.pad(a_log, (0, pad)).reshape(1, DT_PAD)
    dsk = jnp.repeat(d_skip, HEAD_DIM).reshape(1, D_INNER)
    blk = np.kron(np.eye(ns), np.ones((q, q)))
    tri = jnp.asarray(blk * np.tril(np.ones((sb, sb))), F32)
    ones = jnp.asarray(blk, F32)
    state = ssm_prev.reshape(n_seq, D_INNER, D_STATE)
    conv_prev = conv_prev.reshape(n_seq, CONV_W - 1, CONV_DIM)

    kern = functools.partial(_ssd_kernel, nsub=nsub, ns=ns, q=q, n_chunks=n_chunks)
    out_shape = (jax.ShapeDtypeStruct(x.shape, F32),
                 jax.ShapeDtypeStruct((n_seq, CONV_W - 1, CONV_DIM), F32),
                 jax.ShapeDtypeStruct((n_seq, D_INNER, D_STATE), F32))
    n_steps = (n_seq // nseq) * n_chunks
    nxt = lambda i: jnp.minimum(i, n_steps - 1)
    cur = lambda i: jnp.maximum(i - 1, 0)
    in_row_spec = pl.BlockSpec((nseq, q, D_MODEL),
                               lambda i: (nxt(i) // n_chunks, nxt(i) % n_chunks, 0))
    row_spec = pl.BlockSpec((nseq, q, D_MODEL),
                            lambda i: (cur(i) // n_chunks, cur(i) % n_chunks, 0))
    conv_spec = pl.BlockSpec((nseq, CONV_W - 1, CONV_DIM), lambda i: (cur(i) // n_chunks, 0, 0))
    state_spec = pl.BlockSpec((nseq, D_INNER, D_STATE), lambda i: (cur(i) // n_chunks, 0, 0))
    in_specs = [in_row_spec,
                _const_spec((1, D_MODEL)),
                _const_spec(win.shape),
                _const_spec((CONV_W, CONV_DIM)),
                _const_spec((1, CONV_DIM)),
                _const_spec((1, DT_PAD)),
                _const_spec((1, DT_PAD)),
                _const_spec((1, D_INNER)),
                _const_spec((1, D_INNER)),
                _const_spec((D_INNER, D_MODEL)),
                _const_spec((sb, sb)),
                _const_spec((sb, sb)),
                conv_spec, state_spec]
    x_new, conv_new, state_new = pl.pallas_call(
        kern,
        grid=(n_steps + 1,),
        in_specs=in_specs,
        out_specs=(row_spec, conv_spec, state_spec),
        out_shape=out_shape,
        scratch_shapes=[pltpu.VMEM((r, D_MODEL), F32),
                        pltpu.VMEM((r, D_INNER), F32),
                        pltpu.VMEM((r, CONV_DIM), F32),
                        pltpu.VMEM((r, DT_PAD), F32),
                        pltpu.VMEM((r, D_MODEL), F32),
                        pltpu.VMEM((r, D_INNER), F32),
                        pltpu.VMEM((nseq * 8, CONV_DIM), F32),
                        pltpu.VMEM((r, CONV_DIM), F32),
                        pltpu.VMEM((r, D_INNER), F32),
                        pltpu.VMEM((r, D_INNER), F32),
                        pltpu.VMEM((r, D_INNER), BF16),
                        pltpu.VMEM((r, D_MODEL), BF16),
                        pltpu.VMEM((r, D_INNER), BF16)],
        compiler_params=pltpu.CompilerParams(
            dimension_semantics=("arbitrary",), vmem_limit_bytes=VMEM_LIMIT),
        name="ssd_layer",
    )(x, norm_g.reshape(1, D_MODEL), win, conv_w, conv_b.reshape(1, CONV_DIM), dtb, alog, dsk,
      norm_y.reshape(1, D_INNER), w_out.astype(BF16), tri, ones, conv_prev, state)
    return (x_new, conv_new.reshape(1, n_seq, CONV_W - 1, CONV_DIM),
            state_new.reshape(1, n_seq, N_HEADS, HEAD_DIM, D_STATE))


def _sg_kernel(x_ref, g_ref, win_ref, bin_ref, lng_ref, lnb_ref, wmix_ref, bmix_ref, wout_ref,
               xo_ref, *rest, r, want_v):
    v_ref = rest[0] if want_v else None
    hn_scr, uv_scr, vb_scr, um_scr = rest[-4:]
    row_tiles = [slice(a, a + 128) for a in range(0, r, 128)]
    col128 = lambda total: [slice(c, c + 128) for c in range(0, total, 128)]

    for rt in row_tiles:
        ss = None
        for ct in col128(D_MODEL):
            xv = x_ref[rt, ct]
            part = jnp.sum(xv * xv, axis=-1, keepdims=True)
            ss = part if ss is None else ss + part
        scale = lax.rsqrt(ss * (1.0 / D_MODEL) + NORM_EPS)
        for ct in col128(D_MODEL):
            hn_scr[rt, ct] = (x_ref[rt, ct] * scale * g_ref[:, ct]).astype(BF16)

    vcols = [slice(SG_WIDTH + c.start, SG_WIDTH + c.stop) for c in col128(SG_WIDTH)]

    def front(rt):
        def piece(c0):
            def run():
                h = _dot(hn_scr[rt, :], win_ref[:, c0:c0 + PROJ_CHUNK])
                for cc in range(0, PROJ_CHUNK, 128):
                    ct = slice(c0 + cc, c0 + cc + 128)
                    uv_scr[rt, ct] = _gelu_tanh(h[:, cc:cc + 128] + bin_ref[:, ct])
            return run
        return [piece(c0) for c0 in range(0, 2 * SG_WIDTH, PROJ_CHUNK)]

    def back(rt):
        stats = {}

        def ln_stats():
            tot = None
            for ct in vcols:
                part = jnp.sum(uv_scr[rt, ct], axis=-1, keepdims=True)
                tot = part if tot is None else tot + part
            mu = tot * (1.0 / SG_WIDTH)
            ss = None
            for ct in vcols:
                vc = uv_scr[rt, ct] - mu
                part = jnp.sum(vc * vc, axis=-1, keepdims=True)
                ss = part if ss is None else ss + part
            stats["mu"] = mu
            stats["scale"] = lax.rsqrt(ss * (1.0 / SG_WIDTH) + LN_EPS)

        def ln_apply(k0):
            def run():
                for ct, c in list(zip(vcols, col128(SG_WIDTH)))[k0:k0 + 4]:
                    vn = (uv_scr[rt, ct] - stats["mu"]) * stats["scale"] * lng_ref[:, c] + lnb_ref[:, c]
                    if want_v:
                        v_ref[rt, c] = vn
                    vb_scr[rt, c] = vn.astype(BF16)
            return run

        def mix(g):
            def run():
                cols = slice(g * SG_GROUP_DIM, (g + 1) * SG_GROUP_DIM)
                mixed = _dot(wmix_ref[g], vb_scr[rt, cols]) + bmix_ref[:, cols]
                um_scr[rt, cols] = (uv_scr[rt, cols] * mixed).astype(BF16)
            return run

        def out(c0):
            def run():
                ct = slice(c0, c0 + PROJ_CHUNK)
                xo_ref[rt, ct] = x_ref[rt, ct] + _dot(um_scr[rt, :], wout_ref[:, ct])
            return run

        return ([ln_stats] + [ln_apply(k) for k in range(0, len(vcols), 4)]
                + [mix(g) for g in range(SG_GROUPS)]
                + [out(c0) for c0 in range(0, D_MODEL, PROJ_CHUNK)])

    pending = []
    for rt in row_tiles:
        for piece in front(rt):
            piece()
            if pending:
                pending.pop(0)()
        for piece in pending:
            piece()
        pending = back(rt)
    for piece in pending:
        piece()


def _sg_layer(x2d, norm_g, w_in, b_in, ln_g, ln_b, w_s, b_s, w_out, *, seq_len, want_v):
    r = SG_ROWS
    t = x2d.shape[0]
    q = min(seq_len, SG_CHUNK)
    reps = SG_CHUNK // q
    ws = jnp.tril(w_s)[:, :q, :q]
    wmix = jnp.einsum("ab,gts->gatbs", jnp.eye(reps, dtype=F32), ws)
    wmix = wmix.reshape(SG_GROUPS, SG_CHUNK, SG_CHUNK)
    bmix = jnp.tile(jnp.repeat(b_s.T[:q], SG_GROUP_DIM, axis=1), (reps, 1))
    row_spec = pl.BlockSpec((r, D_MODEL), lambda i: (i, 0))
    v_spec = pl.BlockSpec((r, SG_WIDTH), lambda i: (i, 0))
    out_shape = [jax.ShapeDtypeStruct(x2d.shape, F32)]
    out_specs = [row_spec]
    if want_v:
        out_shape.append(jax.ShapeDtypeStruct((t, SG_WIDTH), F32))
        out_specs.append(v_spec)
    outs = pl.pallas_call(
        functools.partial(_sg_kernel, r=r, want_v=want_v),
        grid=(t // r,),
        scratch_shapes=[pltpu.VMEM((r, D_MODEL), BF16),
                        pltpu.VMEM((r, 2 * SG_WIDTH), F32),
                        pltpu.VMEM((r, SG_WIDTH), BF16),
                        pltpu.VMEM((r, SG_WIDTH), BF16)],
        in_specs=[row_spec,
                  _const_spec((1, D_MODEL)),
                  _const_spec((D_MODEL, 2 * SG_WIDTH)),
                  _const_spec((1, 2 * SG_WIDTH)),
                  _const_spec((1, SG_WIDTH)),
                  _const_spec((1, SG_WIDTH)),
                  _const_spec((SG_GROUPS, SG_CHUNK, SG_CHUNK)),
                  _const_spec((SG_CHUNK, SG_WIDTH)),
                  _const_spec((SG_WIDTH, D_MODEL))],
        out_specs=out_specs,
        out_shape=out_shape,
        compiler_params=pltpu.CompilerParams(
            dimension_semantics=("arbitrary",), vmem_limit_bytes=VMEM_LIMIT),
        name="sg_layer",
    )(x2d, norm_g.reshape(1, D_MODEL), w_in.astype(BF16), b_in.reshape(1, 2 * SG_WIDTH),
      ln_g.reshape(1, SG_WIDTH), ln_b.reshape(1, SG_WIDTH), wmix.astype(BF16), bmix,
      w_out.astype(BF16))
    return outs if want_v else (outs[0], None)


ROUTER_ROWS = 64


def _norm_tiles(use_a, xa_ref, xb_ref, g_ref, store):
    def load(rt, ct):
        return jnp.where(use_a, xa_ref[rt, ct], xb_ref[rt, ct])

    for r0 in range(0, xa_ref.shape[0], 128):
        rt = slice(r0, r0 + 128)
        ss = None
        for c0 in range(0, D_MODEL, 256):
            xv = load(rt, slice(c0, c0 + 256))
            part = jnp.sum(xv * xv, axis=-1, keepdims=True)
            ss = part if ss is None else ss + part
        scale = lax.rsqrt(ss * (1.0 / D_MODEL) + NORM_EPS)
        for c0 in range(0, D_MODEL, 256):
            ct = slice(c0, c0 + 256)
            store(rt, ct, load(rt, ct) * scale * g_ref[:, ct])


def _router_t_kernel(xa_ref, xb_ref, g_ref, wt_ref, bt_ref, usl_ref, lsl_ref,
                     meta_ref, metat_ref, pad_ref, h1_scr, h2_scr, *, steps_a):
    wt = wt_ref[...]
    w1 = wt.astype(BF16)
    w2 = (wt - w1.astype(F32)).astype(BF16)
    tt = MOE_TILE
    n = xa_ref.shape[0]

    def split_store(rt, ct, hn):
        hi = hn.astype(BF16)
        h1_scr[rt, ct] = hi
        h2_scr[rt, ct] = (hn - hi.astype(F32)).astype(BF16)

    _norm_tiles(pl.program_id(0) < steps_a, xa_ref, xb_ref, g_ref, split_store)

    h1 = h1_scr[...]
    logits = _dot_nt(w1, h1) + _dot_nt(w2, h1) + _dot_nt(w1, h2_scr[...])
    select = _select_experts(logits, bt_ref[...])
    for k in range(n // tt):
        cols = slice(k * tt, (k + 1) * tt)
        metat, pad = _local_order(select[:, cols], usl_ref[...], lsl_ref[...])
        metat_ref[k * 8:(k + 1) * 8, :] = metat
        full = jnp.concatenate([metat, jnp.zeros((128 - 8, tt), F32)], axis=0)
        meta_ref[cols, :] = full.T
        pad_ref[k * MOE_EXPERTS:(k + 1) * MOE_EXPERTS, :] = pad


def _select_experts(logits, bt):
    n = logits.shape[1]
    reps = n // bt.shape[1]
    logits = logits + (bt if reps == 1 else jnp.concatenate([bt] * reps, axis=1))
    tt = n
    row8 = lax.broadcasted_iota(I32, (8, tt), 0).astype(F32)
    gl = jnp.where(row8 < MOE_GROUPS, logits[0:8, :], -jnp.inf)
    gmax = jnp.max(gl, axis=0, keepdims=True)
    g_top = jnp.min(jnp.where(gl == gmax, row8, 8.0), axis=0, keepdims=True)
    p_g = 1.0 / jnp.sum(jnp.exp(gl - gmax), axis=0, keepdims=True)

    el = logits[8:16, :]
    for grp in range(1, MOE_GROUPS):
        el = jnp.where(g_top == grp, logits[8 + 8 * grp:16 + 8 * grp, :], el)
    emax = jnp.max(el, axis=0, keepdims=True)
    ee = jnp.exp(el - emax)
    prob = ee / jnp.sum(ee, axis=0, keepdims=True)
    p1 = jnp.max(prob, axis=0, keepdims=True)
    i1 = jnp.min(jnp.where(prob == p1, row8, 8.0), axis=0, keepdims=True)
    prob2 = jnp.where(row8 == i1, -1.0, prob)
    p2 = jnp.max(prob2, axis=0, keepdims=True)
    i2 = jnp.min(jnp.where(prob2 == p2, row8, 8.0), axis=0, keepdims=True)
    psum = p1 + p2
    gate1 = p_g * (p1 / psum)
    gate2 = p_g * (p2 / psum)
    e1 = g_top * MOE_EPG + i1
    e2 = g_top * MOE_EPG + i2
    out = jnp.where(row8 == 0, e1, 0.0)
    out = jnp.where(row8 == 1, e2, out)
    out = jnp.where(row8 == 2, gate1, out)
    return jnp.where(row8 == 3, gate2, out)


def _local_order(select, usl, lsl):
    tt = select.shape[1]
    row8 = lax.broadcasted_iota(I32, (8, tt), 0).astype(F32)
    e1, e2 = select[0:1, :], select[1:2, :]
    rowe = lax.broadcasted_iota(I32, (MOE_EXPERTS, tt), 0).astype(F32)
    sel1 = rowe == e1
    sel2 = rowe == e2
    onehot = jnp.where(sel1 | sel2, 1.0, 0.0)
    before = _dot(onehot.astype(BF16), usl)
    cnt = jnp.sum(onehot, axis=1, keepdims=True)
    pad = jnp.floor((cnt + (GRANULE - 1)) * (1.0 / GRANULE)) * GRANULE
    padb = jnp.broadcast_to(pad, (MOE_EXPERTS, tt))
    local = before + _dot(lsl, padb.astype(BF16))
    slot1 = jnp.sum(jnp.where(sel1, local, 0.0), axis=0, keepdims=True)
    slot2 = jnp.sum(jnp.where(sel2, local, 0.0), axis=0, keepdims=True)

    metat = jnp.where(row8 == 4, slot1, select)
    metat = jnp.where(row8 == 5, slot2, metat)
    return metat, padb[:, 0:128]


def _granule(ref, g):
    return ref.at[pl.ds(pl.multiple_of(g * GRANULE, GRANULE), GRANULE), :]


def _dispatch_kernel(ngran_ref, nv_ref, dest_ref, gap_ref, xa_ref, xb_ref, g_ref, meta_ref, *rest,
                     tiles_a, fill):
    xs_hbm, buf, zbuf, hn_scr, sem, zsem, nstart = rest if fill else rest[1:]
    i = pl.program_id(0)
    n_tiles = pl.num_programs(0) - 1
    slot = i % 2
    tt = xa_ref.shape[0]
    bm = zbuf.shape[0]
    n_blocks = xs_hbm.shape[0] // bm

    def out_copy(sl, g, d):
        return pltpu.make_async_copy(_granule(buf.at[sl], g), _granule(xs_hbm, d), sem.at[sl])

    def drain(sl):
        def body(_, carry):
            out_copy(sl, 0, 0).wait()
            return carry
        lax.fori_loop(0, nstart[sl], body, 0)

    def tail_copy(b):
        return pltpu.make_async_copy(
            zbuf, xs_hbm.at[pl.ds(pl.multiple_of(b * bm, bm), bm), :], zsem.at[1])

    @pl.when(i == 0)
    def _():
        nstart[0] = 0
        nstart[1] = 0
        if fill:
            zbuf[...] = jnp.zeros_like(zbuf)

    def norm_store(rt, ct, hn):
        hn_scr[rt, ct] = hn.astype(BF16)

    tail = nv_ref[0] + i

    if fill:
        @pl.when((i >= 1) & (tail - 1 < n_blocks))
        def _():
            tail_copy(tail - 1).wait()

    @pl.when(i < n_tiles)
    def _():
        if fill:
            @pl.when(tail < n_blocks)
            def _():
                tail_copy(tail).start()
        _norm_tiles(i < tiles_a, xa_ref, xb_ref, g_ref, norm_store)
        drain(slot)
        hn = hn_scr[...]
        mt = meta_ref[...]
        for r0 in range(0, LOCAL_ROWS, 256):
            rows = (lax.broadcasted_iota(I32, (256, tt), 0) + r0).astype(F32)
            onehot = jnp.where((rows == mt[4:5, :]) | (rows == mt[5:6, :]), 1.0, 0.0).astype(BF16)
            buf[slot, r0:r0 + 256, :] = _dot(onehot, hn).astype(BF16)
        ng = ngran_ref[i]

        def body(g, carry):
            out_copy(slot, g, dest_ref[0, 0, g]).start()
            return carry
        lax.fori_loop(0, ng, body, 0)
        nstart[slot] = ng

    @pl.when(i == n_tiles)
    def _():
        drain(0)
        drain(1)

        def gap_copy(d):
            return pltpu.make_async_copy(_granule(zbuf, 0), _granule(xs_hbm, d), zsem.at[0])

        def each_gap(fn):
            def body(j, carry):
                d = gap_ref[0, 0, j]

                @pl.when(d >= 0)
                def _():
                    fn(gap_copy(d))
                return carry
            lax.fori_loop(0, gap_ref.shape[2], body, 0)

        def each_tail(fn):
            def body(b, carry):
                fn(tail_copy(b))
                return carry
            lax.fori_loop(nv_ref[0] + n_tiles, n_blocks, body, 0)

        if fill:
            each_gap(lambda cp: cp.start())
            each_tail(lambda cp: cp.start())
            each_gap(lambda cp: cp.wait())
            each_tail(lambda cp: cp.wait())


def _expert_kernel(bexp_ref, nv_ref, xs_ref, wg_ref, wu_ref, wd_ref, ys_ref,
                   wgu_b, wd_b, act_scr):
    b = pl.program_id(0)
    ff = wg_ref.shape[1]

    @pl.when(b < nv_ref[0])
    def _():
        @pl.when((b == 0) | (bexp_ref[b] != bexp_ref[jnp.maximum(b - 1, 0)]))
        def _():
            for k0 in range(0, wg_ref.shape[0], 256):
                wgu_b[k0:k0 + 256, 0:ff] = wg_ref[k0:k0 + 256, :].astype(BF16)
                wgu_b[k0:k0 + 256, ff:2 * ff] = wu_ref[k0:k0 + 256, :].astype(BF16)
            for k0 in range(0, ff, 64):
                wd_b[k0:k0 + 64, :] = wd_ref[k0:k0 + 64, :].astype(BF16)

        bm = xs_ref.shape[0]
        for m0 in range(0, bm, 256):
            h = _dot(xs_ref[m0:m0 + 256, :], wgu_b[...])
            for r0 in range(0, 256, 128):
                for c0 in range(0, ff, 128):
                    hg = h[r0:r0 + 128, c0:c0 + 128]
                    hu = h[r0:r0 + 128, ff + c0:ff + c0 + 128]
                    act_scr[m0 + r0:m0 + r0 + 128, c0:c0 + 128] = (_silu(hg) * hu).astype(BF16)
            ys_ref[m0:m0 + 256, :] = _dot(act_scr[m0:m0 + 256, :], wd_b[...]).astype(BF16)


def _combine_kernel(ngran_ref, src_ref, srcn_ref, xa_ref, xb_ref, meta_ref, gf_ref, ys_hbm,
                    oa_ref, ob_ref, buf, moe_scr, pick_scr, sem, *, final, tiles_a):
    i = pl.program_id(0)
    n = pl.num_programs(0)
    slot = i % 2
    tt = xa_ref.shape[0]

    def in_copy(sl, g, d):
        return pltpu.make_async_copy(_granule(ys_hbm, d), _granule(buf.at[sl], g), sem.at[sl])

    def gather(idx_ref, sl, ng):
        def body(g, carry):
            in_copy(sl, g, idx_ref[0, 0, g]).start()
            return carry
        lax.fori_loop(0, ng, body, 0)

    @pl.when(i == 0)
    def _():
        buf[...] = jnp.zeros_like(buf)
        gather(src_ref, 0, ngran_ref[0])

    @pl.when(i + 1 < n)
    def _():
        gather(srcn_ref, 1 - slot, ngran_ref[i + 1])

    def wait_body(_, carry):
        in_copy(slot, 0, 0).wait()
        return carry
    lax.fori_loop(0, ngran_ref[i], wait_body, 0)

    for r0 in range(0, tt, 128):
        rt = slice(r0, r0 + 128)
        meta = meta_ref[rt, :]
        for c0 in range(0, LOCAL_ROWS, 256):
            cols = (lax.broadcasted_iota(I32, (128, 256), 1) + c0).astype(F32)
            pick_scr[rt, c0:c0 + 256] = (
                jnp.where(cols == meta[:, 4:5], meta[:, 2:3], 0.0)
                + jnp.where(cols == meta[:, 5:6], meta[:, 3:4], 0.0)).astype(BF16)
    is_a = i < tiles_a
    for m0 in range(0, tt, 256):
        moe = _dot(pick_scr[m0:m0 + 256, :], buf[slot])
        for r0 in range(0, 256, 128):
            rt = slice(m0 + r0, m0 + r0 + 128)
            for c0 in range(0, D_MODEL, 256):
                ct = slice(c0, c0 + 256)
                moe_scr[rt, ct] = (jnp.where(is_a, xa_ref[rt, ct], xb_ref[rt, ct])
                                   + moe[r0:r0 + 128, ct])
    for r0 in range(0, tt, 128):
        rt = slice(r0, r0 + 128)
        if final:
            ss = None
            for c0 in range(0, D_MODEL, 256):
                yv = moe_scr[rt, c0:c0 + 256]
                part = jnp.sum(yv * yv, axis=-1, keepdims=True)
                ss = part if ss is None else ss + part
            scale = lax.rsqrt(ss * (1.0 / D_MODEL) + NORM_EPS)
            for c0 in range(0, D_MODEL, 256):
                ct = slice(c0, c0 + 256)
                moe_scr[rt, ct] = moe_scr[rt, ct] * scale * gf_ref[:, ct]

    def emit(o_ref):
        for r0 in range(0, tt, 128):
            for c0 in range(0, D_MODEL, 256):
                o_ref[r0:r0 + 128, c0:c0 + 256] = moe_scr[r0:r0 + 128, c0:c0 + 256]

    @pl.when(is_a)
    def _():
        emit(oa_ref)

    @pl.when(jnp.logical_not(is_a))
    def _():
        emit(ob_ref)


def _moe(xa, xb, norm_g, w_grp, b_grp, w_rt, b_rt, w_gate, w_up, w_down, norm_final, sorted_buf,
         *, layer):
    tt = MOE_TILE
    tiles_a, tiles_b = xa.shape[0] // tt, xb.shape[0] // tt
    steps_a, steps_b = tiles_a // ROUTER_TILES, tiles_b // ROUTER_TILES
    n_tiles = tiles_a + tiles_b
    t = n_tiles * tt
    bm = _expert_block(t)
    n_exp = MOE_EXPERTS
    a_blk = lambda i, n_a: jnp.minimum(i, n_a - 1)
    b_blk = lambda i, n_a, n_b: jnp.clip(i - n_a, 0, n_b - 1)
    zrow = lambda n: jnp.zeros((n, D_MODEL), F32)
    wt = jnp.concatenate([w_grp.T, zrow(8 - MOE_GROUPS), w_rt.T, zrow(ROUTER_ROWS - 8 - n_exp)], axis=0)
    bt = jnp.concatenate([b_grp, jnp.zeros((8 - MOE_GROUPS,), F32), b_rt,
                          jnp.zeros((ROUTER_ROWS - 8 - n_exp,), F32)])
    bt = jnp.broadcast_to(bt[:, None], (ROUTER_ROWS, tt))
    usl = jnp.asarray(np.triu(np.ones((tt, tt)), 1), BF16)
    lsl = jnp.asarray(np.tril(np.ones((n_exp, n_exp)), -1), BF16)
    g2d = norm_g.reshape(1, D_MODEL)
    arb = pltpu.CompilerParams(dimension_semantics=("arbitrary",), vmem_limit_bytes=VMEM_LIMIT)

    meta, metat, pad = pl.pallas_call(
        functools.partial(_router_t_kernel, steps_a=steps_a),
        grid=(steps_a + steps_b,),
        in_specs=[pl.BlockSpec((ROUTER_TILES * tt, D_MODEL), lambda i: (a_blk(i, steps_a), 0)),
                  pl.BlockSpec((ROUTER_TILES * tt, D_MODEL),
                               lambda i: (b_blk(i, steps_a, steps_b), 0)),
                  _const_spec((1, D_MODEL)),
                  _const_spec((ROUTER_ROWS, D_MODEL)),
                  _const_spec((ROUTER_ROWS, tt)),
                  _const_spec((tt, tt)),
                  _const_spec((n_exp, n_exp))],
        out_specs=(pl.BlockSpec((ROUTER_TILES * tt, 128), lambda i: (i, 0)),
                   pl.BlockSpec((ROUTER_TILES * 8, tt), lambda i: (i, 0)),
                   pl.BlockSpec((ROUTER_TILES * n_exp, 128), lambda i: (i, 0))),
        out_shape=(jax.ShapeDtypeStruct((t, 128), F32),
                   jax.ShapeDtypeStruct((n_tiles * 8, tt), F32),
                   jax.ShapeDtypeStruct((n_tiles * n_exp, 128), F32)),
        scratch_shapes=[pltpu.VMEM((ROUTER_TILES * tt, D_MODEL), BF16),
                        pltpu.VMEM((ROUTER_TILES * tt, D_MODEL), BF16)],
        compiler_params=arb,
        name="moe_router",
    )(xa, xb, g2d, wt, bt, usl, lsl)

    runs = pad.reshape(n_tiles, n_exp, 128)[:, :, 0].astype(I32)
    rows_e = jnp.sum(runs, axis=0)
    nblk = (rows_e + bm - 1) // bm
    blk_end = jnp.cumsum(nblk)
    e_start = (blk_end - nblk) * bm
    n_valid = blk_end[-1]
    run_end = jnp.cumsum(runs, axis=1)
    ngran = (run_end[:, -1] // GRANULE).astype(I32)
    shift = e_start[None, :] + (jnp.cumsum(runs, axis=0) - runs) - (run_end - runs)
    g_row = jnp.arange(N_LOCAL_GRAN, dtype=I32) * GRANULE
    e_of_g = jnp.sum((run_end[:, None, :] <= g_row[None, :, None]).astype(I32), axis=-1)
    shift_g = jnp.sum(jnp.where(e_of_g[..., None] == jnp.arange(n_exp, dtype=I32),
                                shift[:, None, :], 0), axis=-1)
    dest = jnp.where(e_of_g < n_exp, (shift_g + g_row[None, :]) // GRANULE, 0)
    dest = dest.astype(I32).reshape(n_tiles, 1, N_LOCAL_GRAN)
    per_blk = bm // GRANULE
    gap = ((e_start + rows_e) // GRANULE)[:, None] + jnp.arange(per_blk, dtype=I32)[None, :]
    gap = jnp.where(gap < ((e_start + nblk * bm) // GRANULE)[:, None], gap, -1)
    gap = gap.astype(I32).reshape(1, 1, n_exp * per_blk)
    n_blocks = (2 * t + n_tiles * n_exp * (GRANULE - 1)) // bm + 1 + n_exp
    blk = jnp.minimum(jnp.arange(n_blocks, dtype=I32), n_valid - 1)
    block_expert = jnp.sum((blk[:, None] >= blk_end[None, :]).astype(I32), axis=1).astype(I32)
    n_valid = n_valid.reshape(1).astype(I32)

    last = n_tiles - 1
    fill = sorted_buf is None
    reuse_spec = [] if fill else [pl.BlockSpec(memory_space=pl.ANY)]
    reuse_arg = [] if fill else [sorted_buf]
    xs = pl.pallas_call(
        functools.partial(_dispatch_kernel, tiles_a=tiles_a, fill=fill),
        grid_spec=pltpu.PrefetchScalarGridSpec(
            num_scalar_prefetch=2,
            grid=(n_tiles + 1,),
            in_specs=[pl.BlockSpec((1, 1, N_LOCAL_GRAN),
                                   lambda i, ng, nv: (jnp.minimum(i, last), 0, 0),
                                   memory_space=pltpu.SMEM),
                      pl.BlockSpec((1, 1, n_exp * per_blk), lambda i, ng, nv: (0, 0, 0),
                                   memory_space=pltpu.SMEM),
                      pl.BlockSpec((tt, D_MODEL), lambda i, ng, nv: (a_blk(i, tiles_a), 0)),
                      pl.BlockSpec((tt, D_MODEL),
                                   lambda i, ng, nv: (b_blk(i, tiles_a, tiles_b), 0)),
                      _const_spec((1, D_MODEL)),
                      pl.BlockSpec((8, tt), lambda i, ng, nv: (jnp.minimum(i, last), 0))]
            + reuse_spec,
            out_specs=pl.BlockSpec(memory_space=pl.ANY),
            scratch_shapes=[pltpu.VMEM((2, LOCAL_ROWS, D_MODEL), BF16),
                            pltpu.VMEM((bm, D_MODEL), BF16),
                            pltpu.VMEM((tt, D_MODEL), BF16),
                            pltpu.SemaphoreType.DMA((2,)),
                            pltpu.SemaphoreType.DMA((2,)),
                            pltpu.SMEM((2,), I32)]),
        out_shape=jax.ShapeDtypeStruct((n_blocks * bm, D_MODEL), BF16),
        input_output_aliases={} if fill else {8: 0},
        compiler_params=arb,
        name="moe_dispatch",
    )(ngran, n_valid, dest, gap, xa, xb, g2d, metat, *reuse_arg)

    w_spec = lambda shape: pl.BlockSpec((None, None) + shape,
                                        lambda b, be, nv: (layer, be[b], 0, 0))
    row_blk = pl.BlockSpec((bm, D_MODEL), lambda b, be, nv: (jnp.minimum(b, nv[0] - 1), 0))
    ys = pl.pallas_call(
        _expert_kernel,
        grid_spec=pltpu.PrefetchScalarGridSpec(
            num_scalar_prefetch=2,
            grid=(n_blocks,),
            in_specs=[row_blk,
                      w_spec((D_MODEL, MOE_D_FF)),
                      w_spec((D_MODEL, MOE_D_FF)),
                      w_spec((MOE_D_FF, D_MODEL))],
            out_specs=row_blk,
            scratch_shapes=[pltpu.VMEM((D_MODEL, 2 * MOE_D_FF), BF16),
                            pltpu.VMEM((MOE_D_FF, D_MODEL), BF16),
                            pltpu.VMEM((bm, MOE_D_FF), BF16)]),
        out_shape=jax.ShapeDtypeStruct((n_blocks * bm, D_MODEL), BF16),
        input_output_aliases={2: 0},
        compiler_params=arb,
        name="moe_experts",
    )(block_expert, n_valid, xs, w_gate, w_up, w_down)

    final = norm_final is not None
    gf = (norm_final if final else jnp.ones((D_MODEL,), F32)).reshape(1, D_MODEL)
    src_spec = lambda nxt: pl.BlockSpec(
        (1, 1, N_LOCAL_GRAN), lambda i, ng: (jnp.minimum(i + nxt, last), 0, 0),
        memory_space=pltpu.SMEM)
    a_spec = pl.BlockSpec((tt, D_MODEL), lambda i, ng: (a_blk(i, tiles_a), 0))
    b_spec = pl.BlockSpec((tt, D_MODEL), lambda i, ng: (b_blk(i, tiles_a, tiles_b), 0))
    out_a, out_b = pl.pallas_call(
        functools.partial(_combine_kernel, final=final, tiles_a=tiles_a),
        grid_spec=pltpu.PrefetchScalarGridSpec(
            num_scalar_prefetch=1,
            grid=(n_tiles,),
            in_specs=[src_spec(0), src_spec(1), a_spec, b_spec,
                      pl.BlockSpec((tt, 128), lambda i, ng: (i, 0)),
                      _const_spec((1, D_MODEL)),
                      pl.BlockSpec(memory_space=pl.ANY)],
            out_specs=(a_spec, b_spec),
            scratch_shapes=[pltpu.VMEM((2, LOCAL_ROWS, D_MODEL), BF16),
                            pltpu.VMEM((tt, D_MODEL), F32),
                            pltpu.VMEM((tt, LOCAL_ROWS), BF16),
                            pltpu.SemaphoreType.DMA((2,))]),
        out_shape=(jax.ShapeDtypeStruct(xa.shape, F32), jax.ShapeDtypeStruct(xb.shape, F32)),
        compiler_params=arb,
        name="moe_combine",
    )(ngran, dest, dest, xa, xb, meta, gf, ys)
    return out_a, out_b, ys


def _ssd(x, conv_prev, ssm_prev, p, *, nsub, ns):
    seq_len = x.shape[1]
    q = seq_len if seq_len < SSD_CHUNK else SSD_CHUNK
    x, conv_new, ssm_new = _ssd_layer(
        x, conv_prev, ssm_prev, p["norm_mix"][0], p["ssd_w_in"][0], p["ssd_conv_w"][0],
        p["ssd_conv_b"][0], p["ssd_dt_bias"][0], p["ssd_a_log"][0], p["ssd_d"][0], p["ssd_norm"][0],
        p["ssd_w_out"][0], nsub=nsub, ns=ns, q=q)
    return x.reshape(-1, D_MODEL), conv_new, ssm_new


def _sg(x2d, p, *, seq_len, want_v):
    return _sg_layer(x2d, p["norm_mix"][1], p["sg_w_in"][0], p["sg_b_in"][0], p["sg_ln_g"][0],
                     p["sg_ln_b"][0], p["sg_w_s"][0], p["sg_b_s"][0], p["sg_w_out"][0],
                     seq_len=seq_len, want_v=want_v)


def _moe_layer(xa, xb, p, layer, norm_final, sorted_buf):
    return _moe(xa, xb, p["norm_ffn"][layer], p["moe_w_group"][layer], p["moe_b_group"][layer],
                p["moe_w_router"][layer], p["moe_b_router"][layer], p["moe_w_gate"], p["moe_w_up"],
                p["moe_w_down"], norm_final, sorted_buf, layer=layer)


def kernel(x_prompt, x_sample, state_ssm, state_conv, norm_mix, norm_ffn, norm_final, ssd_w_in, ssd_conv_w, ssd_conv_b, ssd_dt_bias, ssd_a_log, ssd_d, ssd_norm, ssd_w_out, sg_w_in, sg_b_in, sg_ln_g, sg_ln_b, sg_w_s, sg_b_s, sg_w_out, moe_w_group, moe_b_group, moe_w_router, moe_b_router, moe_w_gate, moe_w_up, moe_w_down):
    p = dict(norm_mix=norm_mix, norm_ffn=norm_ffn, norm_final=norm_final, ssd_w_in=ssd_w_in,
             ssd_conv_w=ssd_conv_w, ssd_conv_b=ssd_conv_b, ssd_dt_bias=ssd_dt_bias,
             ssd_a_log=ssd_a_log, ssd_d=ssd_d, ssd_norm=ssd_norm, ssd_w_out=ssd_w_out,
             sg_w_in=sg_w_in, sg_b_in=sg_b_in, sg_ln_g=sg_ln_g, sg_ln_b=sg_ln_b, sg_w_s=sg_w_s,
             sg_b_s=sg_b_s, sg_w_out=sg_w_out, moe_w_group=moe_w_group, moe_b_group=moe_b_group,
             moe_w_router=moe_w_router, moe_b_router=moe_b_router, moe_w_gate=moe_w_gate,
             moe_w_up=moe_w_up, moe_w_down=moe_w_down)
    nb = x_prompt.shape[0]
    conv0 = jnp.zeros((1, nb, CONV_W - 1, CONV_DIM), F32)
    ssm0 = jnp.zeros((1, nb, N_HEADS, HEAD_DIM, D_STATE), F32)
    xp, conv_p, ssm_p = _ssd(x_prompt, conv0, ssm0, p, nsub=2, ns=1)
    xs, conv_s, ssm_s = _ssd(x_sample, state_conv, state_ssm, p, nsub=1, ns=4)
    xp, xs, sorted_buf = _moe_layer(xp, xs, p, 0, None, None)
    xp, _ = _sg(xp, p, seq_len=x_prompt.shape[1], want_v=False)
    xs, v_s = _sg(xs, p, seq_len=x_sample.shape[1], want_v=True)
    y_p, y_s, _ = _moe_layer(xp, xs, p, 1, norm_final, sorted_buf)
    return (y_p.reshape(x_prompt.shape), y_s.reshape(x_sample.shape), ssm_p, conv_p, ssm_s, conv_s,
            v_s.reshape((1,) + x_sample.shape[:2] + (SG_WIDTH,)))
```

```python
import functools
import math

import jax
import jax.numpy as jnp
import numpy as np
from jax import lax
from jax.experimental import pallas as pl
from jax.experimental.pallas import tpu as pltpu

F32 = jnp.float32
BF16 = jnp.bfloat16
I32 = jnp.int32

D_MODEL = 1024
N_HEADS = 32
HEAD_DIM = 64
N_GROUPS = 4
D_STATE = 128
D_INNER = N_HEADS * HEAD_DIM
GROUP_W = D_INNER // N_GROUPS
CONV_W = 4
CONV_DIM = D_INNER + 2 * N_GROUPS * D_STATE
DT_PAD = 128
SSD_CHUNK = 128
PROJ_CHUNK = 256
SG_WIDTH = 2 * D_MODEL
SG_GROUPS = 8
SG_GROUP_DIM = SG_WIDTH // SG_GROUPS
SG_CHUNK = 128
SG_ROWS = 512
MOE_GROUPS = 4
MOE_EPG = 8
MOE_EXPERTS = MOE_GROUPS * MOE_EPG
MOE_D_FF = 256
NORM_EPS = 1e-6
LN_EPS = 1e-5

MOE_TILE = 512
ROUTER_TILES = 2


def _expert_block(n_tokens):
    return 512 if 2 * n_tokens >= 512 * MOE_EXPERTS else 256


GRANULE = 16
LOCAL_ROWS = 2 * MOE_TILE + MOE_EXPERTS * GRANULE
N_LOCAL_GRAN = LOCAL_ROWS // GRANULE
V7X_VMEM_BYTES = 64 * 1024 * 1024
VMEM_LIMIT = V7X_VMEM_BYTES - 8 * 1024 * 1024


def _sigmoid(x):
    return 0.5 * (jnp.tanh(0.5 * x) + 1.0)


def _silu(x):
    return x * _sigmoid(x)


def _softplus(x):
    return jnp.maximum(x, 0.0) + jnp.log(1.0 + jnp.exp(-jnp.abs(x)))


def _gelu_tanh(x):
    c = math.sqrt(2.0 / math.pi)
    t = jnp.tanh(x * (c + (c * 0.044715) * (x * x)))
    hx = 0.5 * x
    return hx + hx * t


def _rms(x, g):
    return x * lax.rsqrt(jnp.mean(x * x, axis=-1, keepdims=True) + NORM_EPS) * g


def _split3(x):
    a = x.astype(BF16)
    r = x - a.astype(F32)
    b = r.astype(BF16)
    c = (r - b.astype(F32)).astype(BF16)
    return a, b, c


def _dot(a, b):
    return jnp.dot(a, b, preferred_element_type=F32)


def _dot_nt(a, b):
    return lax.dot_general(a, b, (((1,), (1,)), ((), ())), preferred_element_type=F32)


def _dot_tn(a, b):
    return lax.dot_general(a, b, (((0,), (0,)), ((), ())), preferred_element_type=F32)


def _const_spec(shape):
    nd = len(shape)
    return pl.BlockSpec(shape, lambda *_: (0,) * nd)


def _ssd_kernel(x_ref, g_ref, win_ref, cw_ref, cb_ref, dtb_ref, alog_ref, dsk_ref, ng_ref,
                wout_ref, tri_ref, ones_ref, cin_ref, sin_ref,
                xo_ref, cout_ref, sout_ref,
                pend_x, pend_z, pend_xbc, pend_dt, xres_scr, zg_scr,
                prev_scr, xc_scr, yoff_scr, y_scr, xw_scr, hn_scr, yn_scr,
                *, nsub, ns, q, n_chunks):
    i = pl.program_id(0)
    first_chunk = lax.rem(jnp.maximum(i - 1, 0), n_chunks) == 0
    nseq = nsub * ns
    sb = ns * q
    r = nsub * sb

    @pl.when(i == 0)
    def _():
        pend_x[...] = jnp.zeros_like(pend_x)
        pend_z[...] = jnp.zeros_like(pend_z)
        pend_xbc[...] = jnp.zeros_like(pend_xbc)
        pend_dt[...] = jnp.zeros_like(pend_dt)

    tr = min(r, 128)
    row_tiles = [slice(a, a + tr) for a in range(0, r, tr)]

    def col_tiles(total, rows=tr):
        w = min(total, max(128, (32 * 1024) // rows))
        return [slice(c, c + w) for c in range(0, total, w)]

    for rt in row_tiles:
        for ct in col_tiles(D_MODEL):
            xres_scr[rt, ct] = pend_x[rt, ct]

    def gate_tile(rt, ct):
        def run():
            zg_scr[rt, ct] = _silu(pend_z[rt, ct])
        return run
    gate_tiles = [gate_tile(rt, ct) for rt in row_tiles for ct in col_tiles(D_INNER)]

    for s in range(nseq):
        for ct in col_tiles(D_MODEL, q):
            pend_x[s * q:(s + 1) * q, ct] = x_ref[s, :, ct]
    for rt in row_tiles:
        ss = None
        for ct in col_tiles(D_MODEL):
            xv = pend_x[rt, ct]
            part = jnp.sum(xv * xv, axis=-1, keepdims=True)
            ss = part if ss is None else ss + part
        scale = lax.rsqrt(ss * (1.0 / D_MODEL) + NORM_EPS)
        for ct in col_tiles(D_MODEL):
            hn_scr[rt, ct] = (pend_x[rt, ct] * scale * g_ref[:, ct]).astype(BF16)

    def proj_chunk(dst, dst_col, w_col, width):
        def run():
            dst[:, dst_col:dst_col + width] = _dot(hn_scr[...], win_ref[:, w_col:w_col + width])
        return run
    z_chunks = [proj_chunk(pend_z, c, c, PROJ_CHUNK) for c in range(0, D_INNER, PROJ_CHUNK)]
    xbc_chunks = [proj_chunk(pend_xbc, c, D_INNER + c, PROJ_CHUNK)
                  for c in range(0, CONV_DIM, PROJ_CHUNK)]
    dt_chunk = proj_chunk(pend_dt, 0, D_INNER + CONV_DIM, DT_PAD)

    @pl.when(first_chunk)
    def _():
        sout_ref[...] = sin_ref[...]
        prev_scr[...] = jnp.zeros_like(prev_scr)
        for s in range(nseq):
            for k in range(CONV_W - 1):
                prev_scr[pl.ds(s * 8 + 5 + k, 1), :] = cin_ref[s, pl.ds(k, 1), :]

    sub8 = lax.broadcasted_iota(I32, (8, 1), 0)
    for s in range(nseq):
        srows = slice(s * q, (s + 1) * q)
        for ct in col_tiles(CONV_DIM, 4 * q):
            xq = pend_xbc[srows, ct]
            hist = prev_scr[s * 8:(s + 1) * 8, ct]
            acc = cb_ref[:, ct] + cw_ref[pl.ds(CONV_W - 1, 1), ct] * xq
            for j in range(1, CONV_W):
                sh = pltpu.roll(xq, j, 0)
                head = jnp.where(sub8 < j, pltpu.roll(hist, j, 0), sh[0:8, :])
                sh = head if q == 8 else jnp.concatenate([head, sh[8:, :]], axis=0)
                acc = acc + cw_ref[pl.ds(CONV_W - 1 - j, 1), ct] * sh
            xc_scr[srows, ct] = _silu(acc)
            last = xq[q - 8:q, :]
            prev_scr[s * 8:(s + 1) * 8, ct] = last
            for k in range(CONV_W - 1):
                cout_ref[s, pl.ds(k, 1), ct] = last[5 + k:6 + k, :]
    chunks = xbc_chunks + z_chunks + [dt_chunk]
    n_pairs = nsub * (N_HEADS // 2)
    emit_at = {}
    for k, tile in enumerate(gate_tiles):
        emit_at.setdefault(k, []).append(tile)
    for k, ch in enumerate(chunks):
        emit_at.setdefault((k * n_pairs) // len(chunks), []).append(ch)
    first_z = (len(xbc_chunks) * n_pairs) // len(chunks)
    assert len(gate_tiles) <= first_z, "gate tiles must be emitted before pend_z is overwritten"

    tri = tri_ref[...]
    trib = tri.astype(BF16)
    onesb = ones_ref[...].astype(BF16)
    mask = tri > 0.5
    rowseq = lax.shift_right_logical(lax.broadcasted_iota(I32, (sb, 1), 0), int(math.log2(q)))
    lo = lax.broadcasted_iota(I32, (sb, 128), 1) < HEAD_DIM
    neg_a = -jnp.exp(alog_ref[...])

    for u in range(nsub):
        rows = slice(u * sb, (u + 1) * sb)
        xs_ref = xc_scr.at[rows, 0:D_INNER]

        def b_of(g):
            return xc_scr[rows, D_INNER + g * D_STATE:D_INNER + (g + 1) * D_STATE]

        def c_of(g):
            c0 = D_INNER + (N_GROUPS + g) * D_STATE
            return xc_scr[rows, c0:c0 + D_STATE].astype(BF16)

        dt = _softplus(pend_dt[rows, :] + dtb_ref[...])
        d1, d2, d3 = _split3(dt * neg_a)
        cs = _dot(trib, d1) + _dot(trib, d2) + _dot(trib, d3)
        cl = _dot(onesb, d1) + _dot(onesb, d2) + _dot(onesb, d3)
        ecs = jnp.exp(cs)
        wgt = dt * jnp.exp(cl - cs)
        ecl = jnp.exp(cl)
        cs_t = cs.T
        dt_t = dt.T

        for g in range(N_GROUPS):
            cg = c_of(g)
            acc = None
            for s in range(ns):
                st = sout_ref[u * ns + s, g * GROUP_W:(g + 1) * GROUP_W, :].astype(BF16)
                yo = _dot_nt(cg, st)
                if ns > 1:
                    yo = jnp.where(rowseq == s, yo, 0.0)
                acc = yo if acc is None else acc + yo
            yoff_scr[rows, g * GROUP_W:(g + 1) * GROUP_W] = acc

        for g in range(N_GROUPS):
            sc = _dot_nt(c_of(g), b_of(g).astype(BF16))
            for jj in range(N_HEADS // N_GROUPS // 2):
                j = g * (N_HEADS // N_GROUPS // 2) + jj
                ms = []
                for h in (2 * j, 2 * j + 1):
                    diff = cs[:, h:h + 1] - cs_t[h:h + 1, :]
                    dec = jnp.exp(jnp.where(mask, diff, -jnp.inf))
                    ms.append((sc * dec * dt_t[h:h + 1, :]).astype(BF16))
                lhs = jnp.concatenate(ms, axis=1)
                cols = slice(j * 128, (j + 1) * 128)
                xp = xs_ref[:, cols]
                rhs = jnp.concatenate([jnp.where(lo, xp, 0.0).astype(BF16),
                                       jnp.where(lo, 0.0, xp).astype(BF16)], axis=0)
                yd = _dot(lhs, rhs)
                ecs_p = jnp.where(lo, ecs[:, 2 * j:2 * j + 1], ecs[:, 2 * j + 1:2 * j + 2])
                wgt_p = jnp.where(lo, wgt[:, 2 * j:2 * j + 1], wgt[:, 2 * j + 1:2 * j + 2])
                y_scr[rows, cols] = yd + yoff_scr[rows, cols] * ecs_p + xp * dsk_ref[:, cols]
                xw_scr[rows, cols] = (xp * wgt_p).astype(BF16)
                pair = u * (N_HEADS // 2) + j
                for ch in emit_at.get(pair, []):
                    ch()

        for g in range(N_GROUPS):
            xwg = xw_scr[rows, g * GROUP_W:(g + 1) * GROUP_W]
            for s in range(ns):
                bg = b_of(g)
                if ns > 1:
                    bg = jnp.where(rowseq == s, bg, 0.0)
                upd = _dot_tn(xwg, bg.astype(BF16))
                for hh in range(GROUP_W // HEAD_DIM):
                    h = g * (GROUP_W // HEAD_DIM) + hh
                    dec = jnp.broadcast_to(ecl[s * q:s * q + 1, h:h + 1], (HEAD_DIM, D_STATE))
                    hrows = slice(h * HEAD_DIM, (h + 1) * HEAD_DIM)
                    sout_ref[u * ns + s, hrows, :] = (
                        sout_ref[u * ns + s, hrows, :] * dec + upd[hh * HEAD_DIM:(hh + 1) * HEAD_DIM, :])

    for rt in row_tiles:
        for g in range(N_GROUPS):
            cts = [slice(g * GROUP_W + c.start, g * GROUP_W + c.stop) for c in col_tiles(GROUP_W)]
            ss = None
            for ct in cts:
                yz = y_scr[rt, ct] * zg_scr[rt, ct]
                part = jnp.sum(yz * yz, axis=-1, keepdims=True)
                ss = part if ss is None else ss + part
            scale = lax.rsqrt(ss * (1.0 / GROUP_W) + NORM_EPS)
            for ct in cts:
                yn_scr[rt, ct] = (y_scr[rt, ct] * zg_scr[rt, ct] * scale * ng_ref[:, ct]).astype(BF16)
        for c0 in range(0, D_MODEL, PROJ_CHUNK):
            ct = slice(c0, c0 + PROJ_CHUNK)
            o = xres_scr[rt, ct] + _dot(yn_scr[rt, :], wout_ref[:, ct])
            for s in range(rt.start // q, rt.stop // q):
                xo_ref[s, :, ct] = o[s * q - rt.start:(s + 1) * q - rt.start, :]


def _ssd_layer(x, conv_prev, ssm_prev, norm_g, w_in, conv_w, conv_b, dt_bias, a_log, d_skip,
               norm_y, w_out, *, nsub, ns, q):
    n_seq, seq_len, _ = x.shape
    nseq = nsub * ns
    sb = ns * q
    r = nsub * sb
    n_chunks = seq_len // q
    pad = DT_PAD - N_HEADS
    win = jnp.concatenate([w_in, jnp.zeros((D_MODEL, pad), F32)], axis=1).astype(BF16)
    dtb = jnp.pad(dt_bias, (0, pad)).reshape(1, DT_PAD)
    alog = jnp.pad(a_log, (0, pad)).reshape(1, DT_PAD)
    dsk = jnp.repeat(d_skip, HEAD_DIM).reshape(1, D_INNER)
    blk = np.kron(np.eye(ns), np.ones((q, q)))
    tri = jnp.asarray(blk * np.tril(np.ones((sb, sb))), F32)
    ones = jnp.asarray(blk, F32)
    state = ssm_prev.reshape(n_seq, D_INNER, D_STATE)
    conv_prev = conv_prev.reshape(n_seq, CONV_W - 1, CONV_DIM)

    kern = functools.partial(_ssd_kernel, nsub=nsub, ns=ns, q=q, n_chunks=n_chunks)
    out_shape = (jax.ShapeDtypeStruct(x.shape, F32),
                 jax.ShapeDtypeStruct((n_seq, CONV_W - 1, CONV_DIM), F32),
                 jax.ShapeDtypeStruct((n_seq, D_INNER, D_STATE), F32))
    n_steps = (n_seq // nseq) * n_chunks
    nxt = lambda i: jnp.minimum(i, n_steps - 1)
    cur = lambda i: jnp.maximum(i - 1, 0)
    in_row_spec = pl.BlockSpec((nseq, q, D_MODEL),
                               lambda i: (nxt(i) // n_chunks, nxt(i) % n_chunks, 0))
    row_spec = pl.BlockSpec((nseq, q, D_MODEL),
                            lambda i: (cur(i) // n_chunks, cur(i) % n_chunks, 0))
    conv_spec = pl.BlockSpec((nseq, CONV_W - 1, CONV_DIM), lambda i: (cur(i) // n_chunks, 0, 0))
    state_spec = pl.BlockSpec((nseq, D_INNER, D_STATE), lambda i: (cur(i) // n_chunks, 0, 0))
    in_specs = [in_row_spec,
                _const_spec((1, D_MODEL)),
                _const_spec(win.shape),
                _const_spec((CONV_W, CONV_DIM)),
                _const_spec((1, CONV_DIM)),
                _const_spec((1, DT_PAD)),
                _const_spec((1, DT_PAD)),
                _const_spec((1, D_INNER)),
                _const_spec((1, D_INNER)),
                _const_spec((D_INNER, D_MODEL)),
                _const_spec((sb, sb)),
                _const_spec((sb, sb)),
                conv_spec, state_spec]
    x_new, conv_new, state_new = pl.pallas_call(
        kern,
        grid=(n_steps + 1,),
        in_specs=in_specs,
        out_specs=(row_spec, conv_spec, state_spec),
        out_shape=out_shape,
        scratch_shapes=[pltpu.VMEM((r, D_MODEL), F32),
                        pltpu.VMEM((r, D_INNER), F32),
                        pltpu.VMEM((r, CONV_DIM), F32),
                        pltpu.VMEM((r, DT_PAD), F32),
                        pltpu.VMEM((r, D_MODEL), F32),
                        pltpu.VMEM((r, D_INNER), F32),
                        pltpu.VMEM((nseq * 8, CONV_DIM), F32),
                        pltpu.VMEM((r, CONV_DIM), F32),
                        pltpu.VMEM((r, D_INNER), F32),
                        pltpu.VMEM((r, D_INNER), F32),
                        pltpu.VMEM((r, D_INNER), BF16),
                        pltpu.VMEM((r, D_MODEL), BF16),
                        pltpu.VMEM((r, D_INNER), BF16)],
        compiler_params=pltpu.CompilerParams(
            dimension_semantics=("arbitrary",), vmem_limit_bytes=VMEM_LIMIT),
        name="ssd_layer",
    )(x, norm_g.reshape(1, D_MODEL), win, conv_w, conv_b.reshape(1, CONV_DIM), dtb, alog, dsk,
      norm_y.reshape(1, D_INNER), w_out.astype(BF16), tri, ones, conv_prev, state)
    return (x_new, conv_new.reshape(1, n_seq, CONV_W - 1, CONV_DIM),
            state_new.reshape(1, n_seq, N_HEADS, HEAD_DIM, D_STATE))


def _sg_kernel(x_ref, g_ref, win_ref, bin_ref, lng_ref, lnb_ref, wmix_ref, bmix_ref, wout_ref,
               xo_ref, *rest, r, want_v):
    v_ref = rest[0] if want_v else None
    hn_scr, uv_scr, vb_scr, um_scr = rest[-4:]
    row_tiles = [slice(a, a + 128) for a in range(0, r, 128)]
    col128 = lambda total: [slice(c, c + 128) for c in range(0, total, 128)]

    for rt in row_tiles:
        ss = None
        for ct in col128(D_MODEL):
            xv = x_ref[rt, ct]
            part = jnp.sum(xv * xv, axis=-1, keepdims=True)
            ss = part if ss is None else ss + part
        scale = lax.rsqrt(ss * (1.0 / D_MODEL) + NORM_EPS)
        for ct in col128(D_MODEL):
            hn_scr[rt, ct] = (x_ref[rt, ct] * scale * g_ref[:, ct]).astype(BF16)

    vcols = [slice(SG_WIDTH + c.start, SG_WIDTH + c.stop) for c in col128(SG_WIDTH)]

    def front(rt):
        def piece(c0):
            def run():
                h = _dot(hn_scr[rt, :], win_ref[:, c0:c0 + PROJ_CHUNK])
                for cc in range(0, PROJ_CHUNK, 128):
                    ct = slice(c0 + cc, c0 + cc + 128)
                    uv_scr[rt, ct] = _gelu_tanh(h[:, cc:cc + 128] + bin_ref[:, ct])
            return run
        return [piece(c0) for c0 in range(0, 2 * SG_WIDTH, PROJ_CHUNK)]

    def back(rt):
        stats = {}

        def ln_stats():
            tot = None
            for ct in vcols:
                part = jnp.sum(uv_scr[rt, ct], axis=-1, keepdims=True)
                tot = part if tot is None else tot + part
            mu = tot * (1.0 / SG_WIDTH)
            ss = None
            for ct in vcols:
                vc = uv_scr[rt, ct] - mu
                part = jnp.sum(vc * vc, axis=-1, keepdims=True)
                ss = part if ss is None else ss + part
            stats["mu"] = mu
            stats["scale"] = lax.rsqrt(ss * (1.0 / SG_WIDTH) + LN_EPS)

        def ln_apply(k0):
            def run():
                for ct, c in list(zip(vcols, col128(SG_WIDTH)))[k0:k0 + 4]:
                    vn = (uv_scr[rt, ct] - stats["mu"]) * stats["scale"] * lng_ref[:, c] + lnb_ref[:, c]
                    if want_v:
                        v_ref[rt, c] = vn
                    vb_scr[rt, c] = vn.astype(BF16)
            return run

        def mix(g):
            def run():
                cols = slice(g * SG_GROUP_DIM, (g + 1) * SG_GROUP_DIM)
                mixed = _dot(wmix_ref[g], vb_scr[rt, cols]) + bmix_ref[:, cols]
                um_scr[rt, cols] = (uv_scr[rt, cols] * mixed).astype(BF16)
            return run

        def out(c0):
            def run():
                ct = slice(c0, c0 + PROJ_CHUNK)
                xo_ref[rt, ct] = x_ref[rt, ct] + _dot(um_scr[rt, :], wout_ref[:, ct])
            return run

        return ([ln_stats] + [ln_apply(k) for k in range(0, len(vcols), 4)]
                + [mix(g) for g in range(SG_GROUPS)]
                + [out(c0) for c0 in range(0, D_MODEL, PROJ_CHUNK)])

    pending = []
    for rt in row_tiles:
        for piece in front(rt):
            piece()
            if pending:
                pending.pop(0)()
        for piece in pending:
            piece()
        pending = back(rt)
    for piece in pending:
        piece()


def _sg_layer(x2d, norm_g, w_in, b_in, ln_g, ln_b, w_s, b_s, w_out, *, seq_len, want_v):
    r = SG_ROWS
    t = x2d.shape[0]
    q = min(seq_len, SG_CHUNK)
    reps = SG_CHUNK // q
    ws = jnp.tril(w_s)[:, :q, :q]
    wmix = jnp.einsum("ab,gts->gatbs", jnp.eye(reps, dtype=F32), ws)
    wmix = wmix.reshape(SG_GROUPS, SG_CHUNK, SG_CHUNK)
    bmix = jnp.tile(jnp.repeat(b_s.T[:q], SG_GROUP_DIM, axis=1), (reps, 1))
    row_spec = pl.BlockSpec((r, D_MODEL), lambda i: (i, 0))
    v_spec = pl.BlockSpec((r, SG_WIDTH), lambda i: (i, 0))
    out_shape = [jax.ShapeDtypeStruct(x2d.shape, F32)]
    out_specs = [row_spec]
    if want_v:
        out_shape.append(jax.ShapeDtypeStruct((t, SG_WIDTH), F32))
        out_specs.append(v_spec)
    outs = pl.pallas_call(
        functools.partial(_sg_kernel, r=r, want_v=want_v),
        grid=(t // r,),
        scratch_shapes=[pltpu.VMEM((r, D_MODEL), BF16),
                        pltpu.VMEM((r, 2 * SG_WIDTH), F32),
                        pltpu.VMEM((r, SG_WIDTH), BF16),
                        pltpu.VMEM((r, SG_WIDTH), BF16)],
        in_specs=[row_spec,
                  _const_spec((1, D_MODEL)),
                  _const_spec((D_MODEL, 2 * SG_WIDTH)),
                  _const_spec((1, 2 * SG_WIDTH)),
                  _const_spec((1, SG_WIDTH)),
                  _const_spec((1, SG_WIDTH)),
                  _const_spec((SG_GROUPS, SG_CHUNK, SG_CHUNK)),
                  _const_spec((SG_CHUNK, SG_WIDTH)),
                  _const_spec((SG_WIDTH, D_MODEL))],
        out_specs=out_specs,
        out_shape=out_shape,
        compiler_params=pltpu.CompilerParams(
            dimension_semantics=("arbitrary",), vmem_limit_bytes=VMEM_LIMIT),
        name="sg_layer",
    )(x2d, norm_g.reshape(1, D_MODEL), w_in.astype(BF16), b_in.reshape(1, 2 * SG_WIDTH),
      ln_g.reshape(1, SG_WIDTH), ln_b.reshape(1, SG_WIDTH), wmix.astype(BF16), bmix,
      w_out.astype(BF16))
    return outs if want_v else (outs[0], None)


ROUTER_ROWS = 64


def _norm_tiles(use_a, xa_ref, xb_ref, g_ref, store):
    def load(rt, ct):
        return jnp.where(use_a, xa_ref[rt, ct], xb_ref[rt, ct])

    for r0 in range(0, xa_ref.shape[0], 128):
        rt = slice(r0, r0 + 128)
        ss = None
        for c0 in range(0, D_MODEL, 256):
            xv = load(rt, slice(c0, c0 + 256))
            part = jnp.sum(xv * xv, axis=-1, keepdims=True)
            ss = part if ss is None else ss + part
        scale = lax.rsqrt(ss * (1.0 / D_MODEL) + NORM_EPS)
        for c0 in range(0, D_MODEL, 256):
            ct = slice(c0, c0 + 256)
            store(rt, ct, load(rt, ct) * scale * g_ref[:, ct])


def _router_t_kernel(xa_ref, xb_ref, g_ref, wt_ref, bt_ref, usl_ref, lsl_ref,
                     meta_ref, metat_ref, pad_ref, h1_scr, h2_scr, *, steps_a):
    wt = wt_ref[...]
    w1 = wt.astype(BF16)
    w2 = (wt - w1.astype(F32)).astype(BF16)
    tt = MOE_TILE
    n = xa_ref.shape[0]

    def split_store(rt, ct, hn):
        hi = hn.astype(BF16)
        h1_scr[rt, ct] = hi
        h2_scr[rt, ct] = (hn - hi.astype(F32)).astype(BF16)

    _norm_tiles(pl.program_id(0) < steps_a, xa_ref, xb_ref, g_ref, split_store)

    h1 = h1_scr[...]
    logits = _dot_nt(w1, h1) + _dot_nt(w2, h1) + _dot_nt(w1, h2_scr[...])
    select = _select_experts(logits, bt_ref[...])
    for k in range(n // tt):
        cols = slice(k * tt, (k + 1) * tt)
        metat, pad = _local_order(select[:, cols], usl_ref[...], lsl_ref[...])
        metat_ref[k * 8:(k + 1) * 8, :] = metat
        full = jnp.concatenate([metat, jnp.zeros((128 - 8, tt), F32)], axis=0)
        meta_ref[cols, :] = full.T
        pad_ref[k * MOE_EXPERTS:(k + 1) * MOE_EXPERTS, :] = pad


def _select_experts(logits, bt):
    n = logits.shape[1]
    reps = n // bt.shape[1]
    logits = logits + (bt if reps == 1 else jnp.concatenate([bt] * reps, axis=1))
    tt = n
    row8 = lax.broadcasted_iota(I32, (8, tt), 0).astype(F32)
    gl = jnp.where(row8 < MOE_GROUPS, logits[0:8, :], -jnp.inf)
    gmax = jnp.max(gl, axis=0, keepdims=True)
    g_top = jnp.min(jnp.where(gl == gmax, row8, 8.0), axis=0, keepdims=True)
    p_g = 1.0 / jnp.sum(jnp.exp(gl - gmax), axis=0, keepdims=True)

    el = logits[8:16, :]
    for grp in range(1, MOE_GROUPS):
        el = jnp.where(g_top == grp, logits[8 + 8 * grp:16 + 8 * grp, :], el)
    emax = jnp.max(el, axis=0, keepdims=True)
    ee = jnp.exp(el - emax)
    prob = ee / jnp.sum(ee, axis=0, keepdims=True)
    p1 = jnp.max(prob, axis=0, keepdims=True)
    i1 = jnp.min(jnp.where(prob == p1, row8, 8.0), axis=0, keepdims=True)
    prob2 = jnp.where(row8 == i1, -1.0, prob)
    p2 = jnp.max(prob2, axis=0, keepdims=True)
    i2 = jnp.min(jnp.where(prob2 == p2, row8, 8.0), axis=0, keepdims=True)
    psum = p1 + p2
    gate1 = p_g * (p1 / psum)
    gate2 = p_g * (p2 / psum)
    e1 = g_top * MOE_EPG + i1
    e2 = g_top * MOE_EPG + i2
    out = jnp.where(row8 == 0, e1, 0.0)
    out = jnp.where(row8 == 1, e2, out)
    out = jnp.where(row8 == 2, gate1, out)
    return jnp.where(row8 == 3, gate2, out)


def _local_order(select, usl, lsl):
    tt = select.shape[1]
    row8 = lax.broadcasted_iota(I32, (8, tt), 0).astype(F32)
    e1, e2 = select[0:1, :], select[1:2, :]
    rowe = lax.broadcasted_iota(I32, (MOE_EXPERTS, tt), 0).astype(F32)
    sel1 = rowe == e1
    sel2 = rowe == e2
    onehot = jnp.where(sel1 | sel2, 1.0, 0.0)
    before = _dot(onehot.astype(BF16), usl)
    cnt = jnp.sum(onehot, axis=1, keepdims=True)
    pad = jnp.floor((cnt + (GRANULE - 1)) * (1.0 / GRANULE)) * GRANULE
    padb = jnp.broadcast_to(pad, (MOE_EXPERTS, tt))
    local = before + _dot(lsl, padb.astype(BF16))
    slot1 = jnp.sum(jnp.where(sel1, local, 0.0), axis=0, keepdims=True)
    slot2 = jnp.sum(jnp.where(sel2, local, 0.0), axis=0, keepdims=True)

    metat = jnp.where(row8 == 4, slot1, select)
    metat = jnp.where(row8 == 5, slot2, metat)
    return metat, padb[:, 0:128]


def _granule(ref, g):
    return ref.at[pl.ds(pl.multiple_of(g * GRANULE, GRANULE), GRANULE), :]


def _dispatch_kernel(ngran_ref, nv_ref, dest_ref, gap_ref, xa_ref, xb_ref, g_ref, meta_ref, *rest,
                     tiles_a, fill):
    xs_hbm, buf, zbuf, hn_scr, sem, zsem, nstart = rest if fill else rest[1:]
    i = pl.program_id(0)
    n_tiles = pl.num_programs(0) - 1
    slot = i % 2
    tt = xa_ref.shape[0]
    bm = zbuf.shape[0]
    n_blocks = xs_hbm.shape[0] // bm

    def out_copy(sl, g, d):
        return pltpu.make_async_copy(_granule(buf.at[sl], g), _granule(xs_hbm, d), sem.at[sl])

    def drain(sl):
        def body(_, carry):
            out_copy(sl, 0, 0).wait()
            return carry
        lax.fori_loop(0, nstart[sl], body, 0)

    def tail_copy(b):
        return pltpu.make_async_copy(
            zbuf, xs_hbm.at[pl.ds(pl.multiple_of(b * bm, bm), bm), :], zsem.at[1])

    @pl.when(i == 0)
    def _():
        nstart[0] = 0
        nstart[1] = 0
        if fill:
            zbuf[...] = jnp.zeros_like(zbuf)

    def norm_store(rt, ct, hn):
        hn_scr[rt, ct] = hn.astype(BF16)

    tail = nv_ref[0] + i

    if fill:
        @pl.when((i >= 1) & (tail - 1 < n_blocks))
        def _():
            tail_copy(tail - 1).wait()

    @pl.when(i < n_tiles)
    def _():
        if fill:
            @pl.when(tail < n_blocks)
            def _():
                tail_copy(tail).start()
        _norm_tiles(i < tiles_a, xa_ref, xb_ref, g_ref, norm_store)
        drain(slot)
        hn = hn_scr[...]
        mt = meta_ref[...]
        ng = ngran_ref[i]
        for r0 in range(0, LOCAL_ROWS, 256):
            def sort_rows(r0=r0):
                rows = (lax.broadcasted_iota(I32, (256, tt), 0) + r0).astype(F32)
                onehot = jnp.where((rows == mt[4:5, :]) | (rows == mt[5:6, :]), 1.0, 0.0)
                buf[slot, r0:r0 + 256, :] = _dot(onehot.astype(BF16), hn).astype(BF16)
            if r0 < 2 * tt:
                sort_rows()
            else:
                pl.when(r0 < ng * GRANULE)(sort_rows)

        def body(g, carry):
            out_copy(slot, g, dest_ref[0, 0, g]).start()
            return carry
        lax.fori_loop(0, ng, body, 0)
        nstart[slot] = ng

    @pl.when(i == n_tiles)
    def _():
        drain(0)
        drain(1)

        def gap_copy(d):
            return pltpu.make_async_copy(_granule(zbuf, 0), _granule(xs_hbm, d), zsem.at[0])

        def each_gap(fn):
            def body(j, carry):
                d = gap_ref[0, 0, j]

                @pl.when(d >= 0)
                def _():
                    fn(gap_copy(d))
                return carry
            lax.fori_loop(0, gap_ref.shape[2], body, 0)

        def each_tail(fn):
            def body(b, carry):
                fn(tail_copy(b))
                return carry
            lax.fori_loop(nv_ref[0] + n_tiles, n_blocks, body, 0)

        if fill:
            each_gap(lambda cp: cp.start())
            each_tail(lambda cp: cp.start())
            each_gap(lambda cp: cp.wait())
            each_tail(lambda cp: cp.wait())


def _expert_kernel(bexp_ref, nv_ref, xs_ref, wg_ref, wu_ref, wd_ref, ys_ref,
                   wgu_b, wd_b, act_scr):
    b = pl.program_id(0)
    ff = wg_ref.shape[1]

    @pl.when(b < nv_ref[0])
    def _():
        @pl.when((b == 0) | (bexp_ref[b] != bexp_ref[jnp.maximum(b - 1, 0)]))
        def _():
            for k0 in range(0, wg_ref.shape[0], 256):
                wgu_b[k0:k0 + 256, 0:ff] = wg_ref[k0:k0 + 256, :].astype(BF16)
                wgu_b[k0:k0 + 256, ff:2 * ff] = wu_ref[k0:k0 + 256, :].astype(BF16)
            for k0 in range(0, ff, 64):
                wd_b[k0:k0 + 64, :] = wd_ref[k0:k0 + 64, :].astype(BF16)

        bm = xs_ref.shape[0]
        for m0 in range(0, bm, 256):
            h = _dot(xs_ref[m0:m0 + 256, :], wgu_b[...])
            for r0 in range(0, 256, 128):
                for c0 in range(0, ff, 128):
                    hg = h[r0:r0 + 128, c0:c0 + 128]
                    hu = h[r0:r0 + 128, ff + c0:ff + c0 + 128]
                    act_scr[m0 + r0:m0 + r0 + 128, c0:c0 + 128] = (_silu(hg) * hu).astype(BF16)
            ys_ref[m0:m0 + 256, :] = _dot(act_scr[m0:m0 + 256, :], wd_b[...]).astype(BF16)


def _combine_kernel(ngran_ref, src_ref, srcn_ref, xa_ref, xb_ref, meta_ref, gf_ref, ys_hbm,
                    oa_ref, ob_ref, buf, moe_scr, pick_scr, sem, *, final, tiles_a):
    i = pl.program_id(0)
    n = pl.num_programs(0)
    slot = i % 2
    tt = xa_ref.shape[0]

    def in_copy(sl, g, d):
        return pltpu.make_async_copy(_granule(ys_hbm, d), _granule(buf.at[sl], g), sem.at[sl])

    def gather(idx_ref, sl, ng):
        def body(g, carry):
            in_copy(sl, g, idx_ref[0, 0, g]).start()
            return carry
        lax.fori_loop(0, ng, body, 0)

    @pl.when(i == 0)
    def _():
        buf[...] = jnp.zeros_like(buf)
        gather(src_ref, 0, ngran_ref[0])

    @pl.when(i + 1 < n)
    def _():
        gather(srcn_ref, 1 - slot, ngran_ref[i + 1])

    def wait_body(_, carry):
        in_copy(slot, 0, 0).wait()
        return carry
    lax.fori_loop(0, ngran_ref[i], wait_body, 0)

    for r0 in range(0, tt, 128):
        rt = slice(r0, r0 + 128)
        meta = meta_ref[rt, :]
        for c0 in range(0, LOCAL_ROWS, 256):
            cols = (lax.broadcasted_iota(I32, (128, 256), 1) + c0).astype(F32)
            pick_scr[rt, c0:c0 + 256] = (
                jnp.where(cols == meta[:, 4:5], meta[:, 2:3], 0.0)
                + jnp.where(cols == meta[:, 5:6], meta[:, 3:4], 0.0)).astype(BF16)
    is_a = i < tiles_a
    for m0 in range(0, tt, 256):
        moe = _dot(pick_scr[m0:m0 + 256, :], buf[slot])
        for r0 in range(0, 256, 128):
            rt = slice(m0 + r0, m0 + r0 + 128)
            for c0 in range(0, D_MODEL, 256):
                ct = slice(c0, c0 + 256)
                moe_scr[rt, ct] = (jnp.where(is_a, xa_ref[rt, ct], xb_ref[rt, ct])
                                   + moe[r0:r0 + 128, ct])
    for r0 in range(0, tt, 128):
        rt = slice(r0, r0 + 128)
        if final:
            ss = None
            for c0 in range(0, D_MODEL, 256):
                yv = moe_scr[rt, c0:c0 + 256]
                part = jnp.sum(yv * yv, axis=-1, keepdims=True)
                ss = part if ss is None else ss + part
            scale = lax.rsqrt(ss * (1.0 / D_MODEL) + NORM_EPS)
            for c0 in range(0, D_MODEL, 256):
                ct = slice(c0, c0 + 256)
                moe_scr[rt, ct] = moe_scr[rt, ct] * scale * gf_ref[:, ct]

    def emit(o_ref):
        for r0 in range(0, tt, 128):
            for c0 in range(0, D_MODEL, 256):
                o_ref[r0:r0 + 128, c0:c0 + 256] = moe_scr[r0:r0 + 128, c0:c0 + 256]

    @pl.when(is_a)
    def _():
        emit(oa_ref)

    @pl.when(jnp.logical_not(is_a))
    def _():
        emit(ob_ref)


def _moe(xa, xb, norm_g, w_grp, b_grp, w_rt, b_rt, w_gate, w_up, w_down, norm_final, sorted_buf,
         *, layer):
    tt = MOE_TILE
    tiles_a, tiles_b = xa.shape[0] // tt, xb.shape[0] // tt
    steps_a, steps_b = tiles_a // ROUTER_TILES, tiles_b // ROUTER_TILES
    n_tiles = tiles_a + tiles_b
    t = n_tiles * tt
    bm = _expert_block(t)
    n_exp = MOE_EXPERTS
    a_blk = lambda i, n_a: jnp.minimum(i, n_a - 1)
    b_blk = lambda i, n_a, n_b: jnp.clip(i - n_a, 0, n_b - 1)
    zrow = lambda n: jnp.zeros((n, D_MODEL), F32)
    wt = jnp.concatenate([w_grp.T, zrow(8 - MOE_GROUPS), w_rt.T, zrow(ROUTER_ROWS - 8 - n_exp)], axis=0)
    bt = jnp.concatenate([b_grp, jnp.zeros((8 - MOE_GROUPS,), F32), b_rt,
                          jnp.zeros((ROUTER_ROWS - 8 - n_exp,), F32)])
    bt = jnp.broadcast_to(bt[:, None], (ROUTER_ROWS, tt))
    usl = jnp.asarray(np.triu(np.ones((tt, tt)), 1), BF16)
    lsl = jnp.asarray(np.tril(np.ones((n_exp, n_exp)), -1), BF16)
    g2d = norm_g.reshape(1, D_MODEL)
    arb = pltpu.CompilerParams(dimension_semantics=("arbitrary",), vmem_limit_bytes=VMEM_LIMIT)

    meta, metat, pad = pl.pallas_call(
        functools.partial(_router_t_kernel, steps_a=steps_a),
        grid=(steps_a + steps_b,),
        in_specs=[pl.BlockSpec((ROUTER_TILES * tt, D_MODEL), lambda i: (a_blk(i, steps_a), 0)),
                  pl.BlockSpec((ROUTER_TILES * tt, D_MODEL),
                               lambda i: (b_blk(i, steps_a, steps_b), 0)),
                  _const_spec((1, D_MODEL)),
                  _const_spec((ROUTER_ROWS, D_MODEL)),
                  _const_spec((ROUTER_ROWS, tt)),
                  _const_spec((tt, tt)),
                  _const_spec((n_exp, n_exp))],
        out_specs=(pl.BlockSpec((ROUTER_TILES * tt, 128), lambda i: (i, 0)),
                   pl.BlockSpec((ROUTER_TILES * 8, tt), lambda i: (i, 0)),
                   pl.BlockSpec((ROUTER_TILES * n_exp, 128), lambda i: (i, 0))),
        out_shape=(jax.ShapeDtypeStruct((t, 128), F32),
                   jax.ShapeDtypeStruct((n_tiles * 8, tt), F32),
                   jax.ShapeDtypeStruct((n_tiles * n_exp, 128), F32)),
        scratch_shapes=[pltpu.VMEM((ROUTER_TILES * tt, D_MODEL), BF16),
                        pltpu.VMEM((ROUTER_TILES * tt, D_MODEL), BF16)],
        compiler_params=arb,
        name="moe_router",
    )(xa, xb, g2d, wt, bt, usl, lsl)

    runs = pad.reshape(n_tiles, n_exp, 128)[:, :, 0].astype(I32)
    rows_e = jnp.sum(runs, axis=0)
    nblk = (rows_e + bm - 1) // bm
    blk_end = jnp.cumsum(nblk)
    e_start = (blk_end - nblk) * bm
    n_valid = blk_end[-1]
    run_end = jnp.cumsum(runs, axis=1)
    ngran = (run_end[:, -1] // GRANULE).astype(I32)
    shift = e_start[None, :] + (jnp.cumsum(runs, axis=0) - runs) - (run_end - runs)
    g_row = jnp.arange(N_LOCAL_GRAN, dtype=I32) * GRANULE
    e_of_g = jnp.sum((run_end[:, None, :] <= g_row[None, :, None]).astype(I32), axis=-1)
    shift_g = jnp.sum(jnp.where(e_of_g[..., None] == jnp.arange(n_exp, dtype=I32),
                                shift[:, None, :], 0), axis=-1)
    dest = jnp.where(e_of_g < n_exp, (shift_g + g_row[None, :]) // GRANULE, 0)
    dest = dest.astype(I32).reshape(n_tiles, 1, N_LOCAL_GRAN)
    per_blk = bm // GRANULE
    gap = ((e_start + rows_e) // GRANULE)[:, None] + jnp.arange(per_blk, dtype=I32)[None, :]
    gap = jnp.where(gap < ((e_start + nblk * bm) // GRANULE)[:, None], gap, -1)
    gap = gap.astype(I32).reshape(1, 1, n_exp * per_blk)
    n_blocks = (2 * t + n_tiles * n_exp * (GRANULE - 1)) // bm + 1 + n_exp
    blk = jnp.minimum(jnp.arange(n_blocks, dtype=I32), n_valid - 1)
    block_expert = jnp.sum((blk[:, None] >= blk_end[None, :]).astype(I32), axis=1).astype(I32)
    n_valid = n_valid.reshape(1).astype(I32)

    last = n_tiles - 1
    fill = sorted_buf is None
    reuse_spec = [] if fill else [pl.BlockSpec(memory_space=pl.ANY)]
    reuse_arg = [] if fill else [sorted_buf]
    xs = pl.pallas_call(
        functools.partial(_dispatch_kernel, tiles_a=tiles_a, fill=fill),
        grid_spec=pltpu.PrefetchScalarGridSpec(
            num_scalar_prefetch=2,
            grid=(n_tiles + 1,),
            in_specs=[pl.BlockSpec((1, 1, N_LOCAL_GRAN),
                                   lambda i, ng, nv: (jnp.minimum(i, last), 0, 0),
                                   memory_space=pltpu.SMEM),
                      pl.BlockSpec((1, 1, n_exp * per_blk), lambda i, ng, nv: (0, 0, 0),
                                   memory_space=pltpu.SMEM),
                      pl.BlockSpec((tt, D_MODEL), lambda i, ng, nv: (a_blk(i, tiles_a), 0)),
                      pl.BlockSpec((tt, D_MODEL),
                                   lambda i, ng, nv: (b_blk(i, tiles_a, tiles_b), 0)),
                      _const_spec((1, D_MODEL)),
                      pl.BlockSpec((8, tt), lambda i, ng, nv: (jnp.minimum(i, last), 0))]
            + reuse_spec,
            out_specs=pl.BlockSpec(memory_space=pl.ANY),
            scratch_shapes=[pltpu.VMEM((2, LOCAL_ROWS, D_MODEL), BF16),
                            pltpu.VMEM((bm, D_MODEL), BF16),
                            pltpu.VMEM((tt, D_MODEL), BF16),
                            pltpu.SemaphoreType.DMA((2,)),
                            pltpu.SemaphoreType.DMA((2,)),
                            pltpu.SMEM((2,), I32)]),
        out_shape=jax.ShapeDtypeStruct((n_blocks * bm, D_MODEL), BF16),
        input_output_aliases={} if fill else {8: 0},
        compiler_params=arb,
        name="moe_dispatch",
    )(ngran, n_valid, dest, gap, xa, xb, g2d, metat, *reuse_arg)

    w_spec = lambda shape: pl.BlockSpec((None, None) + shape,
                                        lambda b, be, nv: (layer, be[b], 0, 0))
    row_blk = pl.BlockSpec((bm, D_MODEL), lambda b, be, nv: (jnp.minimum(b, nv[0] - 1), 0))
    ys = pl.pallas_call(
        _expert_kernel,
        grid_spec=pltpu.PrefetchScalarGridSpec(
            num_scalar_prefetch=2,
            grid=(n_blocks,),
            in_specs=[row_blk,
                      w_spec((D_MODEL, MOE_D_FF)),
                      w_spec((D_MODEL, MOE_D_FF)),
                      w_spec((MOE_D_FF, D_MODEL))],
            out_specs=row_blk,
            scratch_shapes=[pltpu.VMEM((D_MODEL, 2 * MOE_D_FF), BF16),
                            pltpu.VMEM((MOE_D_FF, D_MODEL), BF16),
                            pltpu.VMEM((bm, MOE_D_FF), BF16)]),
        out_shape=jax.ShapeDtypeStruct((n_blocks * bm, D_MODEL), BF16),
        input_output_aliases={2: 0},
        compiler_params=arb,
        name="moe_experts",
    )(block_expert, n_valid, xs, w_gate, w_up, w_down)

    final = norm_final is not None
    gf = (norm_final if final else jnp.ones((D_MODEL,), F32)).reshape(1, D_MODEL)
    src_spec = lambda nxt: pl.BlockSpec(
        (1, 1, N_LOCAL_GRAN), lambda i, ng: (jnp.minimum(i + nxt, last), 0, 0),
        memory_space=pltpu.SMEM)
    a_spec = pl.BlockSpec((tt, D_MODEL), lambda i, ng: (a_blk(i, tiles_a), 0))
    b_spec = pl.BlockSpec((tt, D_MODEL), lambda i, ng: (b_blk(i, tiles_a, tiles_b), 0))
    out_a, out_b = pl.pallas_call(
        functools.partial(_combine_kernel, final=final, tiles_a=tiles_a),
        grid_spec=pltpu.PrefetchScalarGridSpec(
            num_scalar_prefetch=1,
            grid=(n_tiles,),
            in_specs=[src_spec(0), src_spec(1), a_spec, b_spec,
                      pl.BlockSpec((tt, 128), lambda i, ng: (i, 0)),
                      _const_spec((1, D_MODEL)),
                      pl.BlockSpec(memory_space=pl.ANY)],
            out_specs=(a_spec, b_spec),
            scratch_shapes=[pltpu.VMEM((2, LOCAL_ROWS, D_MODEL), BF16),
                            pltpu.VMEM((tt, D_MODEL), F32),
                            pltpu.VMEM((tt, LOCAL_ROWS), BF16),
                            pltpu.SemaphoreType.DMA((2,))]),
        out_shape=(jax.ShapeDtypeStruct(xa.shape, F32), jax.ShapeDtypeStruct(xb.shape, F32)),
        compiler_params=arb,
        name="moe_combine",
    )(ngran, dest, dest, xa, xb, meta, gf, ys)
    return out_a, out_b, ys


def _ssd(x, conv_prev, ssm_prev, p, *, nsub, ns):
    seq_len = x.shape[1]
    q = seq_len if seq_len < SSD_CHUNK else SSD_CHUNK
    x, conv_new, ssm_new = _ssd_layer(
        x, conv_prev, ssm_prev, p["norm_mix"][0], p["ssd_w_in"][0], p["ssd_conv_w"][0],
        p["ssd_conv_b"][0], p["ssd_dt_bias"][0], p["ssd_a_log"][0], p["ssd_d"][0], p["ssd_norm"][0],
        p["ssd_w_out"][0], nsub=nsub, ns=ns, q=q)
    return x.reshape(-1, D_MODEL), conv_new, ssm_new


def _sg(x2d, p, *, seq_len, want_v):
    return _sg_layer(x2d, p["norm_mix"][1], p["sg_w_in"][0], p["sg_b_in"][0], p["sg_ln_g"][0],
                     p["sg_ln_b"][0], p["sg_w_s"][0], p["sg_b_s"][0], p["sg_w_out"][0],
                     seq_len=seq_len, want_v=want_v)


def _moe_layer(xa, xb, p, layer, norm_final, sorted_buf):
    return _moe(xa, xb, p["norm_ffn"][layer], p["moe_w_group"][layer], p["moe_b_group"][layer],
                p["moe_w_router"][layer], p["moe_b_router"][layer], p["moe_w_gate"], p["moe_w_up"],
                p["moe_w_down"], norm_final, sorted_buf, layer=layer)


def kernel(x_prompt, x_sample, state_ssm, state_conv, norm_mix, norm_ffn, norm_final, ssd_w_in, ssd_conv_w, ssd_conv_b, ssd_dt_bias, ssd_a_log, ssd_d, ssd_norm, ssd_w_out, sg_w_in, sg_b_in, sg_ln_g, sg_ln_b, sg_w_s, sg_b_s, sg_w_out, moe_w_group, moe_b_group, moe_w_router, moe_b_router, moe_w_gate, moe_w_up, moe_w_down):
    p = dict(norm_mix=norm_mix, norm_ffn=norm_ffn, norm_final=norm_final, ssd_w_in=ssd_w_in,
             ssd_conv_w=ssd_conv_w, ssd_conv_b=ssd_conv_b, ssd_dt_bias=ssd_dt_bias,
             ssd_a_log=ssd_a_log, ssd_d=ssd_d, ssd_norm=ssd_norm, ssd_w_out=ssd_w_out,
             sg_w_in=sg_w_in, sg_b_in=sg_b_in, sg_ln_g=sg_ln_g, sg_ln_b=sg_ln_b, sg_w_s=sg_w_s,
             sg_b_s=sg_b_s, sg_w_out=sg_w_out, moe_w_group=moe_w_group, moe_b_group=moe_b_group,
             moe_w_router=moe_w_router, moe_b_router=moe_b_router, moe_w_gate=moe_w_gate,
             moe_w_up=moe_w_up, moe_w_down=moe_w_down)
    nb = x_prompt.shape[0]
    conv0 = jnp.zeros((1, nb, CONV_W - 1, CONV_DIM), F32)
    ssm0 = jnp.zeros((1, nb, N_HEADS, HEAD_DIM, D_STATE), F32)
    xp, conv_p, ssm_p = _ssd(x_prompt, conv0, ssm0, p, nsub=2, ns=1)
    xs, conv_s, ssm_s = _ssd(x_sample, state_conv, state_ssm, p, nsub=1, ns=4)
    xp, xs, sorted_buf = _moe_layer(xp, xs, p, 0, None, None)
    xp, _ = _sg(xp, p, seq_len=x_prompt.shape[1], want_v=False)
    xs, v_s = _sg(xs, p, seq_len=x_sample.shape[1], want_v=True)
    y_p, y_s, _ = _moe_layer(xp, xs, p, 1, norm_final, sorted_buf)
    return (y_p.reshape(x_prompt.shape), y_s.reshape(x_sample.shape), ssm_p, conv_p, ssm_s, conv_s,
            v_s.reshape((1,) + x_sample.shape[:2] + (SG_WIDTH,)))
```
